```python
import math
import jax, jax.numpy as jnp
from jax import lax
import numpy as np

D_MODEL = 1024
BATCH = 8
SEQ = 4096
DEPTH = 1

HEAD_DIM = 64
N_HEADS_A = 8
DILATED_PATTERNS = ((128, 1), (512, 4), (2048, 16))
N_HEADS_B = 4
DIFF_V_DIM = 2 * HEAD_DIM
WIDTH_A = N_HEADS_A * HEAD_DIM
WIDTH_B = N_HEADS_B * DIFF_V_DIM
N_ALIBI_HEADS = N_HEADS_A + N_HEADS_B
ALIBI_IDX_A = (0, 1, 3, 4, 6, 7, 9, 10)
ALIBI_IDX_B = (2, 5, 8, 11)
IN_SIZES = (WIDTH_A, WIDTH_A, WIDTH_A,
            N_HEADS_B * 2 * HEAD_DIM, N_HEADS_B * 2 * HEAD_DIM, WIDTH_B,
            D_MODEL, D_MODEL)
IN_TOTAL = sum(IN_SIZES)
Q_BLOCK = 128
MASK_VALUE = -1e30
N_EXPERTS = 32
TOP_K = 4
D_EXPERT = D_MODEL
SWIGLU_ALPHA = 1.702
SWIGLU_LIMIT = 7.0
MOE_BLOCK = 256
LN_EPS = 1e-5
SUBLN_EPS = 1e-5
DEEPNORM_ALPHA = (2.0 * DEPTH) ** 0.25
DEEPNORM_BETA = (8.0 * DEPTH) ** -0.25

kernel_name = "hybrid_dilated_diffattn_moe_encoder"


def alibi_slopes(n):
    return jnp.asarray(2.0 ** (-8.0 * np.arange(1, n + 1) / n), jnp.float32)


def layer_norm(x, g, b):
    xf = x.astype(jnp.float32)
    mu = jnp.mean(xf, axis=-1, keepdims=True)
    var = jnp.mean(jnp.square(xf - mu), axis=-1, keepdims=True)
    y = (xf - mu) * lax.rsqrt(var + LN_EPS) * g.astype(jnp.float32) + b.astype(jnp.float32)
    return y.astype(x.dtype)


def dilated_window_attention(q, k, v, slopes, window, dilation):
    b, h, s, dh = q.shape
    half = window // (2 * dilation)
    L = s // dilation
    nb = -(-L // Q_BLOCK)
    Lp = nb * Q_BLOCK
    band = Q_BLOCK + 2 * half

    def to_residue(t):
        return t.reshape(b, h, L, dilation, dh).transpose(0, 1, 3, 2, 4)

    qr, kr, vr = to_residue(q), to_residue(k), to_residue(v)
    qb = jnp.pad(qr, ((0, 0), (0, 0), (0, 0), (0, Lp - L), (0, 0))).reshape(b, h, dilation, nb, Q_BLOCK, dh)
    kv_pad = ((0, 0), (0, 0), (0, 0), (half, Lp - L + half), (0, 0))
    kp, vp = jnp.pad(kr, kv_pad), jnp.pad(vr, kv_pad)
    idx = jnp.arange(nb)[:, None] * Q_BLOCK + jnp.arange(band)[None, :]
    kb = kp[:, :, :, idx]
    vb = vp[:, :, :, idx]

    scores = jnp.einsum('bhrnqe,bhrnke->bhrnqk', qb, kb).astype(jnp.float32) * (dh ** -0.5)
    rel = jnp.arange(Q_BLOCK)[:, None] - jnp.arange(band)[None, :] + half
    key_pos = idx - half
    valid = ((jnp.abs(rel) <= half)[None]
             & (key_pos[:, None, :] >= 0) & (key_pos[:, None, :] < L))
    bias = -slopes[:, None, None] * (dilation * jnp.abs(rel)).astype(jnp.float32)
    scores = jnp.where(valid, scores + bias[None, :, None, None], MASK_VALUE)
    m = jnp.max(scores, axis=-1, keepdims=True)
    p = jnp.exp(scores - m)
    z = jnp.sum(p, axis=-1, keepdims=True)
    o = jnp.einsum('bhrnqk,bhrnke->bhrnqe', (p / z).astype(v.dtype), vb)
    lse = (m + jnp.log(z))[..., 0]
    o = o.reshape(b, h, dilation, Lp, dh)[:, :, :, :L].transpose(0, 1, 3, 2, 4).reshape(b, h, s, dh)
    lse = lse.reshape(b, h, dilation, Lp)[..., :L].transpose(0, 1, 3, 2).reshape(b, h, s)
    return o, lse


def differential_attention(q1, q2, k1, k2, v, slopes, lam):
    b, h, s, dh = q1.shape
    nb = s // Q_BLOCK
    scale = dh ** -0.5
    key_pos = jnp.arange(s)

    def blocks(t):
        return t.reshape(b, h, nb, Q_BLOCK, dh).transpose(2, 0, 1, 3, 4)

    def one_block(args):
        q1b, q2b, start = args
        qpos = start + jnp.arange(Q_BLOCK)
        bias = -slopes[:, None, None] * jnp.abs(qpos[:, None] - key_pos[None, :]).astype(jnp.float32)
        a1 = jax.nn.softmax(jnp.einsum('bhqe,bhke->bhqk', q1b, k1).astype(jnp.float32) * scale + bias, axis=-1)
        a2 = jax.nn.softmax(jnp.einsum('bhqe,bhke->bhqk', q2b, k2).astype(jnp.float32) * scale + bias, axis=-1)
        return jnp.einsum('bhqk,bhkv->bhqv', (a1 - lam * a2).astype(v.dtype), v)

    starts = jnp.arange(nb, dtype=jnp.int32) * Q_BLOCK
    o = lax.map(one_block, (blocks(q1), blocks(q2), starts))
    return o.transpose(1, 2, 0, 3, 4).reshape(b, h, s, v.shape[-1])


def token_mixer(x, w_in, b_in, lambda_q1, lambda_k1, lambda_q2, lambda_k2, subln_g,
                w_proj_a, w_proj_b, w_out, b_out, lambda_init):
    bsz, seq, _ = x.shape
    proj = x @ w_in + b_in
    splits = [int(c) for c in np.cumsum(IN_SIZES)[:-1]]
    q_a, k_a, v_a, q_b, k_b, v_b, g_a, g_b = jnp.split(proj, splits, axis=-1)
    slopes = alibi_slopes(N_ALIBI_HEADS)

    def split_heads(t, n, dh):
        return t.reshape(bsz, seq, n, dh).transpose(0, 2, 1, 3)

    qa = split_heads(q_a, N_HEADS_A, HEAD_DIM)
    ka = split_heads(k_a, N_HEADS_A, HEAD_DIM)
    va = split_heads(v_a, N_HEADS_A, HEAD_DIM)
    slopes_a = slopes[np.array(ALIBI_IDX_A)]
    outs, lses = [], []
    for window, dilation in DILATED_PATTERNS:
        o, l = dilated_window_attention(qa, ka, va, slopes_a, window, dilation)
        outs.append(o)
        lses.append(l)
    mix_w = jax.nn.softmax(jnp.stack(lses), axis=0)
    o_a = jnp.sum(mix_w[..., None] * jnp.stack(outs).astype(jnp.float32), axis=0).astype(x.dtype)
    o_a = o_a.transpose(0, 2, 1, 3).reshape(bsz, seq, WIDTH_A)

    def split_pair(t):
        t = t.reshape(bsz, seq, N_HEADS_B, 2, HEAD_DIM).transpose(0, 2, 3, 1, 4)
        return t[:, :, 0], t[:, :, 1]

    q1, q2 = split_pair(q_b)
    k1, k2 = split_pair(k_b)
    vb = split_heads(v_b, N_HEADS_B, DIFF_V_DIM)
    lam = (jnp.exp(jnp.sum(lambda_q1.astype(jnp.float32) * lambda_k1.astype(jnp.float32)))
           - jnp.exp(jnp.sum(lambda_q2.astype(jnp.float32) * lambda_k2.astype(jnp.float32)))
           + lambda_init)
    o_b = differential_attention(q1, q2, k1, k2, vb, slopes[np.array(ALIBI_IDX_B)], lam)
    of = o_b.astype(jnp.float32)
    of = of * lax.rsqrt(jnp.mean(jnp.square(of), axis=-1, keepdims=True) + SUBLN_EPS)
    of = of * subln_g.astype(jnp.float32) * (1.0 - lambda_init)
    o_b = of.astype(x.dtype).transpose(0, 2, 1, 3).reshape(bsz, seq, WIDTH_B)

    merged = jax.nn.sigmoid(g_a) * (o_a @ w_proj_a) + jax.nn.sigmoid(g_b) * (o_b @ w_proj_b)
    return merged @ w_out + b_out


def moe_ffn(x, w_router, b_router, w_up, b_up, w_down, b_down):
    bsz, seq, d = x.shape
    xt = x.reshape(bsz * seq, d)
    n_tok = xt.shape[0]
    logits = (xt @ w_router + b_router).astype(jnp.float32)
    top_vals, top_idx = lax.top_k(logits, TOP_K)
    gates = jax.nn.softmax(top_vals, axis=-1).astype(x.dtype)

    n_assign = n_tok * TOP_K
    flat_e = top_idx.reshape(n_assign)
    flat_tok = jnp.arange(n_assign, dtype=jnp.int32) // TOP_K
    flat_w = gates.reshape(n_assign)
    order = jnp.argsort(flat_e)
    sorted_e = flat_e[order]
    counts = jnp.bincount(flat_e, length=N_EXPERTS)
    start = jnp.cumsum(counts) - counts
    padded_counts = (counts + MOE_BLOCK - 1) // MOE_BLOCK * MOE_BLOCK
    padded_end = jnp.cumsum(padded_counts)
    padded_start = padded_end - padded_counts
    dest = padded_start[sorted_e] + (jnp.arange(n_assign) - start[sorted_e])

    n_blocks = -(-n_assign // MOE_BLOCK) + N_EXPERTS
    n_rows = n_blocks * MOE_BLOCK
    buf_tok = jnp.zeros((n_rows,), jnp.int32).at[dest].set(flat_tok[order])
    buf_w = jnp.zeros((n_rows,), x.dtype).at[dest].set(flat_w[order])
    block_start = jnp.arange(n_blocks) * MOE_BLOCK
    block_e = jnp.minimum(jnp.searchsorted(padded_end, block_start, side='right'), N_EXPERTS - 1)

    def expert_block(args):
        tok, e = args
        xb = xt[tok]
        hcat = xb @ w_up[e] + b_up[e]
        gate = jnp.minimum(hcat[:, :D_EXPERT], SWIGLU_LIMIT)
        up = jnp.clip(hcat[:, D_EXPERT:], -SWIGLU_LIMIT, SWIGLU_LIMIT)
        act = gate * jax.nn.sigmoid(SWIGLU_ALPHA * gate) * (up + 1.0)
        return act @ w_down[e] + b_down[e]

    ys = lax.map(expert_block, (buf_tok.reshape(n_blocks, MOE_BLOCK), block_e))
    ys = ys.reshape(n_rows, d) * buf_w[:, None]
    out = jnp.zeros((n_tok, d), x.dtype).at[buf_tok].add(ys)
    return out.reshape(bsz, seq, d)


def setup_inputs(seed: int = 0) -> dict:
    key = jax.random.key(seed)
    ks = jax.random.split(key, 24)
    f32 = jnp.float32

    def nrm(k, shape, scale):
        return jax.random.normal(k, shape, f32) * scale

    col_scale = np.ones((IN_TOTAL,), np.float32)
    offs = np.concatenate([[0], np.cumsum(IN_SIZES)])
    col_scale[offs[2]:offs[3]] = DEEPNORM_BETA
    col_scale[offs[5]:offs[6]] = DEEPNORM_BETA

    return {
        "x": jax.random.normal(ks[0], (BATCH, SEQ, D_MODEL), f32),
        "w_in": nrm(ks[1], (DEPTH, D_MODEL, IN_TOTAL), D_MODEL ** -0.5) * jnp.asarray(col_scale),
        "b_in": nrm(ks[2], (DEPTH, IN_TOTAL), 0.01),
        "lambda_q1": nrm(ks[3], (DEPTH, HEAD_DIM), 0.1),
        "lambda_k1": nrm(ks[4], (DEPTH, HEAD_DIM), 0.1),
        "lambda_q2": nrm(ks[5], (DEPTH, HEAD_DIM), 0.1),
        "lambda_k2": nrm(ks[6], (DEPTH, HEAD_DIM), 0.1),
        "subln_g": 1.0 + nrm(ks[7], (DEPTH, DIFF_V_DIM), 0.02),
        "w_proj_a": nrm(ks[8], (DEPTH, WIDTH_A, D_MODEL), WIDTH_A ** -0.5),
        "w_proj_b": nrm(ks[9], (DEPTH, WIDTH_B, D_MODEL), WIDTH_B ** -0.5),
        "w_out": nrm(ks[10], (DEPTH, D_MODEL, D_MODEL), D_MODEL ** -0.5 * DEEPNORM_BETA),
        "b_out": nrm(ks[11], (DEPTH, D_MODEL), 0.01),
        "ln1_g": 1.0 + nrm(ks[12], (DEPTH, D_MODEL), 0.02),
        "ln1_b": nrm(ks[13], (DEPTH, D_MODEL), 0.02),
        "w_router": nrm(ks[14], (DEPTH, D_MODEL, N_EXPERTS), D_MODEL ** -0.5),
        "b_router": nrm(ks[15], (DEPTH, N_EXPERTS), 0.01),
        "w_up": nrm(ks[16], (DEPTH, N_EXPERTS, D_MODEL, 2 * D_EXPERT), D_MODEL ** -0.5),
        "b_up": nrm(ks[17], (DEPTH, N_EXPERTS, 2 * D_EXPERT), 0.01),
        "w_down": nrm(ks[18], (DEPTH, N_EXPERTS, D_EXPERT, D_MODEL), D_EXPERT ** -0.5 * DEEPNORM_BETA),
        "b_down": nrm(ks[19], (DEPTH, N_EXPERTS, D_MODEL), 0.01),
        "ln2_g": 1.0 + nrm(ks[20], (DEPTH, D_MODEL), 0.02),
        "ln2_b": nrm(ks[21], (DEPTH, D_MODEL), 0.02),
    }


def reference(x, w_in, b_in, lambda_q1, lambda_k1, lambda_q2, lambda_k2, subln_g,
              w_proj_a, w_proj_b, w_out, b_out, ln1_g, ln1_b,
              w_router, b_router, w_up, b_up, w_down, b_down, ln2_g, ln2_b):
    for layer in range(DEPTH):
        lambda_init = 0.8 - 0.6 * math.exp(-0.3 * layer)
        y = token_mixer(x, w_in[layer], b_in[layer], lambda_q1[layer], lambda_k1[layer],
                        lambda_q2[layer], lambda_k2[layer], subln_g[layer],
                        w_proj_a[layer], w_proj_b[layer], w_out[layer], b_out[layer], lambda_init)
        x = layer_norm(DEEPNORM_ALPHA * x + y, ln1_g[layer], ln1_b[layer])
        y = moe_ffn(x, w_router[layer], b_router[layer], w_up[layer], b_up[layer],
                    w_down[layer], b_down[layer])
        x = layer_norm(DEEPNORM_ALPHA * x + y, ln2_g[layer], ln2_b[layer])
    return x
```

```python
import functools
import math

import numpy as np
import jax
import jax.numpy as jnp
from jax import lax
from jax.experimental import pallas as pl
from jax.experimental.pallas import tpu as pltpu

HEAD_DIM = 64
N_HEADS_A = 8
DILATED_PATTERNS = ((128, 1), (512, 4), (2048, 16))
N_HEADS_B = 4
WIDTH_A = N_HEADS_A * HEAD_DIM
WIDTH_B = N_HEADS_B * 2 * HEAD_DIM
N_ALIBI_HEADS = N_HEADS_A + N_HEADS_B
ALIBI_IDX_A = (0, 1, 3, 4, 6, 7, 9, 10)
ALIBI_IDX_B = (2, 5, 8, 11)
Q_BLOCK = 128
MASK_VALUE = -1e30
N_EXPERTS = 32
TOP_K = 4
SWIGLU_ALPHA = 1.702
SWIGLU_LIMIT = 7.0
LN_EPS = 1e-5
SUBLN_EPS = 1e-5

LANES = 128
V7X_VMEM_LIMIT_BYTES = 56 * 1024 * 1024

PROJ_TM = 1024
PROJ_TN = 1024
DIL_ROWS = 2 * Q_BLOCK
DIL_HALO = 64
DIFF_TQ = 512
DIFF_TK = 512
MERGE_TM = 512
MOE_ROWS = 512
FFN_CHUNK = 512
COMBINE_TM = 256
DISPATCH_TM = 512

_F32 = jnp.float32
_BF16 = jnp.bfloat16


def _params(*sem):
    return pltpu.CompilerParams(dimension_semantics=sem, vmem_limit_bytes=V7X_VMEM_LIMIT_BYTES)


def _alibi_slopes():
    return (2.0 ** (-8.0 * np.arange(1, N_ALIBI_HEADS + 1) / N_ALIBI_HEADS)).astype(np.float32)


def _in_proj_kernel(x_ref, w_ref, b_ref, o_ref):
    x = x_ref[...].astype(_BF16)
    acc = jnp.dot(x, w_ref[...], preferred_element_type=_F32)
    o_ref[...] = (acc + b_ref[...]).astype(o_ref.dtype)


def _in_proj(x2d, w_bf16, b):
    t, dm = x2d.shape
    n = w_bf16.shape[1]
    return pl.pallas_call(
        _in_proj_kernel,
        grid=(t // PROJ_TM, n // PROJ_TN),
        in_specs=[
            pl.BlockSpec((PROJ_TM, dm), lambda i, j: (i, 0)),
            pl.BlockSpec((dm, PROJ_TN), lambda i, j: (0, j)),
            pl.BlockSpec((1, PROJ_TN), lambda i, j: (0, j)),
        ],
        out_specs=pl.BlockSpec((PROJ_TM, PROJ_TN), lambda i, j: (i, j)),
        out_shape=jax.ShapeDtypeStruct((t, n), _BF16),
        compiler_params=_params("parallel", "arbitrary"),
        name="in_proj",
    )(x2d, w_bf16, b.reshape(1, n))


def _dilated_bias_table(dilation):
    slopes = _alibi_slopes()[list(ALIBI_IDX_A)]
    band = Q_BLOCK + 2 * DIL_HALO
    qi = np.arange(Q_BLOCK)[:, None]
    kj = np.arange(band)[None, :]
    rel = qi - kj + DIL_HALO
    in_band = np.abs(rel) <= DIL_HALO
    base = -slopes[:, None, None] * (dilation * np.abs(rel)).astype(np.float32)[None]
    edge = (np.ones_like(kj, bool), kj >= DIL_HALO, kj < band - DIL_HALO)
    out = np.stack([np.where(in_band & e, base, np.float32(MASK_VALUE)) for e in edge])
    return out.astype(np.float32)


def _dilated_kernel(*refs, nblk, has_prev, final):
    q_ref, km_ref, kp_ref, kn_ref, vm_ref, vp_ref, vn_ref, bias_ref = refs[:8]
    pos = 8
    if has_prev:
        po_ref, plse_ref = refs[pos:pos + 2]
        pos += 2
    o_ref = refs[pos]
    pos += 1
    if not final:
        lse_ref = refs[pos]
        pos += 1
    kbuf, vbuf = refs[pos:pos + 2]

    i = pl.program_id(2)
    h0, h1 = DIL_HALO, DIL_HALO + DIL_ROWS
    kbuf[0:h0, :] = kp_ref[0]
    kbuf[h0:h1, :] = km_ref[0]
    kbuf[h1:h1 + DIL_HALO, :] = kn_ref[0]
    vbuf[0:h0, :] = vp_ref[0]
    vbuf[h0:h1, :] = vm_ref[0]
    vbuf[h1:h1 + DIL_HALO, :] = vn_ref[0]

    lane = lax.broadcasted_iota(jnp.int32, (Q_BLOCK, LANES), 1)
    low_half = lane < HEAD_DIM
    scale = HEAD_DIM ** -0.5
    band = Q_BLOCK + 2 * DIL_HALO
    variants = (jnp.where(i == 0, 1, 0), jnp.where(i == nblk - 1, 2, 0))

    for j in range(DIL_ROWS // Q_BLOCK):
        rows = slice(j * Q_BLOCK, (j + 1) * Q_BLOCK)
        krows = slice(j * Q_BLOCK, j * Q_BLOCK + band)
        lse_cols = []
        for hp in range(N_HEADS_A // 2):
            cols = slice(hp * LANES, (hp + 1) * LANES)
            q_pair = q_ref[0, rows, cols]
            k_pair = kbuf[krows, cols]
            v_pair = vbuf[krows, cols]
            halves = []
            for hh in range(2):
                head = 2 * hp + hh
                keep = low_half if hh == 0 else jnp.logical_not(low_half)
                qm = jnp.where(keep, q_pair, jnp.zeros_like(q_pair)) * scale
                s = lax.dot_general(qm, k_pair, (((1,), (1,)), ((), ())), preferred_element_type=_F32)
                s = s + bias_ref[variants[j], head]
                m = jnp.max(s, axis=-1, keepdims=True)
                p = jnp.exp(s - m)
                z = jnp.sum(p, axis=-1, keepdims=True)
                o = jnp.dot(p.astype(_BF16), v_pair, preferred_element_type=_F32) * (1.0 / z)
                lse = m + jnp.log(z)
                if has_prev:
                    lse_p = plse_ref[0, rows, head:head + 1]
                    mx = jnp.maximum(lse_p, lse)
                    a = jnp.exp(lse_p - mx)
                    b = jnp.exp(lse - mx)
                    tot = a + b
                    inv = 1.0 / tot
                    o = o * (b * inv) + po_ref[0, rows, cols] * (a * inv)
                    lse = mx + jnp.log(tot)
                halves.append(o)
                lse_cols.append(lse)
            o_ref[0, rows, cols] = jnp.where(low_half, halves[0], halves[1]).astype(o_ref.dtype)
        if not final:
            tile = jnp.zeros((Q_BLOCK, LANES), _F32)
            for head in range(N_HEADS_A):
                tile = jnp.where(lane == head, lse_cols[head], tile)
            lse_ref[0, rows, :] = tile


def _dilated_pass(proj, dilation, prev, final):
    bsz, seq, width = proj.shape
    sec = width // WIDTH_A
    L = seq // dilation
    assert L % DIL_ROWS == 0 and DIL_ROWS % DIL_HALO == 0
    nblk = L // DIL_ROWS
    per = DIL_ROWS // DIL_HALO
    n_halo = L // DIL_HALO
    pv = proj.reshape(bsz, L, dilation * width)
    bias = jnp.asarray(_dilated_bias_table(dilation))
    has_prev = prev is not None

    def main(s):
        return pl.BlockSpec((1, DIL_ROWS, WIDTH_A), lambda b, r, i: (b, i, r * sec + s))

    def left(s):
        return pl.BlockSpec((1, DIL_HALO, WIDTH_A),
                            lambda b, r, i: (b, jnp.maximum(per * i - 1, 0), r * sec + s))

    def right(s):
        return pl.BlockSpec((1, DIL_HALO, WIDTH_A),
                            lambda b, r, i: (b, jnp.minimum(per * i + per, n_halo - 1), r * sec + s))

    state_o = pl.BlockSpec((1, DIL_ROWS, WIDTH_A), lambda b, r, i: (b, i, r))
    state_l = pl.BlockSpec((1, DIL_ROWS, LANES), lambda b, r, i: (b, i, r))

    in_specs = [main(0), main(1), left(1), right(1), main(2), left(2), right(2),
                pl.BlockSpec(bias.shape, lambda b, r, i: (0, 0, 0, 0))]
    args = [pv, pv, pv, pv, pv, pv, pv, bias]
    if has_prev:
        in_specs += [state_o, state_l]
        args += [prev[0].reshape(bsz, L, dilation * WIDTH_A), prev[1].reshape(bsz, L, dilation * LANES)]
    if final:
        out_specs = state_o
        out_shape = jax.ShapeDtypeStruct((bsz, L, dilation * WIDTH_A), _BF16)
    else:
        out_specs = [state_o, state_l]
        out_shape = [jax.ShapeDtypeStruct((bsz, L, dilation * WIDTH_A), _F32),
                     jax.ShapeDtypeStruct((bsz, L, dilation * LANES), _F32)]
    rows_buf = DIL_ROWS + 2 * DIL_HALO
    out = pl.pallas_call(
        functools.partial(_dilated_kernel, nblk=nblk, has_prev=has_prev, final=final),
        grid=(bsz, dilation, nblk),
        in_specs=in_specs,
        out_specs=out_specs,
        out_shape=out_shape,
        scratch_shapes=[pltpu.VMEM((rows_buf, WIDTH_A), _BF16), pltpu.VMEM((rows_buf, WIDTH_A), _BF16)],
        compiler_params=_params("parallel", "parallel", "arbitrary"),
        name=f"dilated_d{dilation}",
    )(*args)
    if final:
        return out.reshape(bsz, seq, WIDTH_A)
    return out[0].reshape(bsz, seq, WIDTH_A), out[1].reshape(bsz, seq, LANES)


def _diff_kernel(slope_ref, lam_ref, q_ref, k_ref, v_ref, g_ref, o_ref, *, seq, lambda_init):
    head = pl.program_id(1)
    q0 = pl.program_id(2) * DIFF_TQ
    slope = slope_ref[head]
    scale = HEAD_DIM ** -0.5

    q = q_ref[0]
    lane = lax.broadcasted_iota(jnp.int32, q.shape, 1)
    zero = jnp.zeros_like(q)
    q_maps = (jnp.where(lane < HEAD_DIM, q, zero) * scale, jnp.where(lane >= HEAD_DIM, q, zero) * scale)
    rel0 = (lax.broadcasted_iota(jnp.int32, (DIFF_TQ, DIFF_TK), 0)
            - lax.broadcasted_iota(jnp.int32, (DIFF_TQ, DIFF_TK), 1)).astype(_F32)

    def body(kb, carry):
        k0 = kb * DIFF_TK
        kblk = k_ref[0, pl.ds(pl.multiple_of(k0, DIFF_TK), DIFF_TK), :]
        vblk = v_ref[0, pl.ds(pl.multiple_of(k0, DIFF_TK), DIFF_TK), :]
        bias = jnp.abs(rel0 + (q0 - k0).astype(_F32)) * (-slope)
        new = []
        for mp in range(2):
            m_old, l_old, acc = carry[mp]
            s = lax.dot_general(q_maps[mp], kblk, (((1,), (1,)), ((), ())), preferred_element_type=_F32) + bias
            m_new = jnp.maximum(m_old, jnp.max(s, axis=-1, keepdims=True))
            alpha = jnp.exp(m_old - m_new)
            p = jnp.exp(s - m_new)
            l_new = alpha * l_old + jnp.sum(p, axis=-1, keepdims=True)
            acc = alpha * acc + jnp.dot(p.astype(_BF16), vblk, preferred_element_type=_F32)
            new.append((m_new, l_new, acc))
        return tuple(new)

    init = tuple((jnp.full((DIFF_TQ, 1), -jnp.inf, _F32), jnp.zeros((DIFF_TQ, 1), _F32),
                  jnp.zeros((DIFF_TQ, LANES), _F32)) for _ in range(2))
    (_, l1, a1), (_, l2, a2) = lax.fori_loop(0, seq // DIFF_TK, body, init)

    lv = lam_ref[...]
    lam = (jnp.exp(jnp.sum(lv[0:1] * lv[1:2], axis=-1, keepdims=True))
           - jnp.exp(jnp.sum(lv[2:3] * lv[3:4], axis=-1, keepdims=True)) + lambda_init)
    of = a1 * (1.0 / l1) - lam * (a2 * (1.0 / l2))
    of = of * lax.rsqrt(jnp.mean(of * of, axis=-1, keepdims=True) + SUBLN_EPS)
    of = of * g_ref[...] * (1.0 - lambda_init)
    o_ref[0] = of.astype(o_ref.dtype)


def _diff_attention(proj, lam_vecs, subln_g, lambda_init):
    bsz, seq, width = proj.shape
    assert seq % DIFF_TQ == 0 and seq % DIFF_TK == 0
    qcol = 3 * WIDTH_A // LANES
    kcol = qcol + WIDTH_B // LANES
    vcol = kcol + WIDTH_B // LANES
    slopes = jnp.asarray(_alibi_slopes()[list(ALIBI_IDX_B)])
    return pl.pallas_call(
        functools.partial(_diff_kernel, seq=seq, lambda_init=lambda_init),
        grid=(bsz, N_HEADS_B, seq // DIFF_TQ),
        in_specs=[
            pl.BlockSpec(memory_space=pltpu.SMEM),
            pl.BlockSpec((4, HEAD_DIM), lambda b, h, i: (0, 0)),
            pl.BlockSpec((1, DIFF_TQ, LANES), lambda b, h, i: (b, i, qcol + h)),
            pl.BlockSpec((1, seq, LANES), lambda b, h, i: (b, 0, kcol + h)),
            pl.BlockSpec((1, seq, LANES), lambda b, h, i: (b, 0, vcol + h)),
            pl.BlockSpec((1, LANES), lambda b, h, i: (0, 0)),
        ],
        out_specs=pl.BlockSpec((1, DIFF_TQ, LANES), lambda b, h, i: (b, i, h)),
        out_shape=jax.ShapeDtypeStruct((bsz, seq, WIDTH_B), _BF16),
        compiler_params=_params("parallel", "parallel", "arbitrary"),
        name="diff_attn",
    )(slopes, lam_vecs, proj, proj, proj, subln_g.reshape(1, LANES).astype(_F32))


def _layer_norm(z, g, b):
    mu = jnp.mean(z, axis=-1, keepdims=True)
    zc = z - mu
    var = jnp.mean(zc * zc, axis=-1, keepdims=True)
    return zc * lax.rsqrt(var + LN_EPS) * g + b


def _merge_kernel(x_ref, ga_ref, gb_ref, oa_ref, ob_ref, wpa_ref, wpb_ref, wout_ref, bout_ref,
                  g1_ref, b1_ref, wrh_ref, wrl_ref, br_ref,
                  h_ref, idx_ref, gate_ref, rank_ref, cnt_ref, carry_ref, *, alpha):
    step = pl.program_id(0)

    @pl.when(step == 0)
    def _():
        carry_ref[...] = jnp.zeros_like(carry_ref)

    pa = jnp.dot(oa_ref[...], wpa_ref[...], preferred_element_type=_F32)
    pb = jnp.dot(ob_ref[...], wpb_ref[...], preferred_element_type=_F32)
    merged = jax.nn.sigmoid(ga_ref[...].astype(_F32)) * pa + jax.nn.sigmoid(gb_ref[...].astype(_F32)) * pb
    y = jnp.dot(merged.astype(_BF16), wout_ref[...], preferred_element_type=_F32) + bout_ref[...]
    h = _layer_norm(alpha * x_ref[...] + y, g1_ref[...], b1_ref[...])
    h_ref[...] = h

    h_hi = h.astype(_BF16)
    h_lo = (h - h_hi.astype(_F32)).astype(_BF16)
    nt = (((1,), (1,)), ((), ()))
    logits = (lax.dot_general(wrh_ref[...], h_hi, nt, preferred_element_type=_F32)
              + lax.dot_general(wrl_ref[...], h_hi, nt, preferred_element_type=_F32)
              + lax.dot_general(wrh_ref[...], h_lo, nt, preferred_element_type=_F32)
              + br_ref[...])

    tm = logits.shape[1]
    expert = lax.broadcasted_iota(jnp.int32, logits.shape, 0)
    work = logits
    vals, sels, idxs = [], [], []
    for _ in range(TOP_K):
        mx = jnp.max(work, axis=0, keepdims=True)
        idx = jnp.min(jnp.where(work == mx, expert, N_EXPERTS), axis=0, keepdims=True)
        sel = expert == idx
        work = jnp.where(sel, -jnp.inf, work)
        vals.append(mx)
        idxs.append(idx)
        sels.append(sel)
    ex = [jnp.exp(v - vals[0]) for v in vals]
    inv = 1.0 / (ex[0] + ex[1] + ex[2] + ex[3])
    gates = [e * inv for e in ex]

    chosen = sels[0] | sels[1] | sels[2] | sels[3]
    onehot = jnp.where(chosen, 1.0, 0.0).astype(_BF16)
    before = (lax.broadcasted_iota(jnp.int32, (tm, tm), 0)
              < lax.broadcasted_iota(jnp.int32, (tm, tm), 1))
    upper = jnp.where(before, 1.0, 0.0).astype(_BF16)
    prefix = jnp.dot(onehot, upper, preferred_element_type=_F32) + carry_ref[:, 0:1]
    for k in range(TOP_K):
        rank = jnp.sum(jnp.where(sels[k], prefix, 0.0), axis=0, keepdims=True)
        rank_ref[k:k + 1, :] = rank.astype(jnp.int32)
        idx_ref[k:k + 1, :] = idxs[k]
    total = carry_ref[...] + jnp.sum(onehot.astype(_F32), axis=1, keepdims=True)
    carry_ref[...] = total
    cnt_ref[...] = total

    row = lax.broadcasted_iota(jnp.int32, (LANES, tm), 0)
    g_rows = jnp.zeros((LANES, tm), _F32)
    for k in range(TOP_K):
        g_rows = jnp.where(row == k, gates[k], g_rows)
    gate_ref[...] = g_rows.T


def _merge_router(x2d, proj2d, o_a, o_b, wpa, wpb, wout, bout, g1, b1, w_router, b_router, alpha):
    t, dm = x2d.shape
    tm = MERGE_TM
    gcol = (3 * WIDTH_A + 3 * WIDTH_B) // dm
    wr_t = w_router.T.astype(_F32)
    wr_hi = wr_t.astype(_BF16)
    wr_lo = (wr_t - wr_hi.astype(_F32)).astype(_BF16)

    def const(shape):
        return pl.BlockSpec(shape, lambda i: tuple(0 for _ in shape))

    return pl.pallas_call(
        functools.partial(_merge_kernel, alpha=alpha),
        grid=(t // tm,),
        in_specs=[
            pl.BlockSpec((tm, dm), lambda i: (i, 0)),
            pl.BlockSpec((tm, dm), lambda i: (i, gcol)),
            pl.BlockSpec((tm, dm), lambda i: (i, gcol + 1)),
            pl.BlockSpec((tm, WIDTH_A), lambda i: (i, 0)),
            pl.BlockSpec((tm, WIDTH_B), lambda i: (i, 0)),
            const((WIDTH_A, dm)), const((WIDTH_B, dm)), const((dm, dm)), const((1, dm)),
            const((1, dm)), const((1, dm)),
            const((N_EXPERTS, dm)), const((N_EXPERTS, dm)), const((N_EXPERTS, 1)),
        ],
        out_specs=[
            pl.BlockSpec((tm, dm), lambda i: (i, 0)),
            pl.BlockSpec((TOP_K, tm), lambda i: (0, i)),
            pl.BlockSpec((tm, LANES), lambda i: (i, 0)),
            pl.BlockSpec((TOP_K, tm), lambda i: (0, i)),
            pl.BlockSpec((N_EXPERTS, LANES), lambda i: (0, 0)),
        ],
        out_shape=[
            jax.ShapeDtypeStruct((t, dm), _F32),
            jax.ShapeDtypeStruct((TOP_K, t), jnp.int32),
            jax.ShapeDtypeStruct((t, LANES), _F32),
            jax.ShapeDtypeStruct((TOP_K, t), jnp.int32),
            jax.ShapeDtypeStruct((N_EXPERTS, LANES), _F32),
        ],
        scratch_shapes=[pltpu.VMEM((N_EXPERTS, LANES), _F32)],
        compiler_params=_params("arbitrary"),
        name="merge_router",
    )(x2d, proj2d, proj2d, o_a, o_b, wpa, wpb, wout, bout.reshape(1, dm), g1.reshape(1, dm),
      b1.reshape(1, dm), wr_hi, wr_lo, b_router.reshape(N_EXPERTS, 1).astype(_F32))


def _dispatch_kernel(dest_ref, h_hbm, xs_in, xs_hbm, sem, *, tm):
    del xs_in
    base = pl.program_id(0) * tm

    def issue(t, carry):
        for k in range(TOP_K):
            pltpu.make_async_copy(h_hbm.at[pl.ds(base + t, 1)], xs_hbm.at[pl.ds(dest_ref[k, t], 1)], sem).start()
        return carry

    lax.fori_loop(0, tm, issue, 0, unroll=8)
    pltpu.make_async_copy(xs_hbm.at[pl.ds(0, TOP_K * tm)], xs_hbm.at[pl.ds(0, TOP_K * tm)], sem).wait()


def _dispatch(h, dest, n_rows):
    t, dm = h.shape
    tm = DISPATCH_TM
    zeros = jnp.zeros((n_rows, dm), h.dtype)
    return pl.pallas_call(
        functools.partial(_dispatch_kernel, tm=tm),
        grid=(t // tm,),
        in_specs=[
            pl.BlockSpec((TOP_K, tm), lambda i: (0, i), memory_space=pltpu.SMEM),
            pl.BlockSpec(memory_space=pl.ANY),
            pl.BlockSpec(memory_space=pl.ANY),
        ],
        out_specs=pl.BlockSpec(memory_space=pl.ANY),
        out_shape=jax.ShapeDtypeStruct((n_rows, dm), h.dtype),
        scratch_shapes=[pltpu.SemaphoreType.DMA(())],
        input_output_aliases={2: 0},
        compiler_params=_params("arbitrary"),
        name="moe_dispatch",
    )(dest, h, zeros)


def _expert_kernel(blk_e_ref, blk_src_ref, n_used_ref, xs_ref, wup_ref, bup_ref, wdn_ref, bdn_ref, ys_ref,
                   *, d_expert):
    del blk_e_ref, blk_src_ref

    @pl.when(pl.program_id(0) < n_used_ref[0])
    def _():
        x = xs_ref[...].astype(_BF16)
        acc = jnp.zeros(ys_ref.shape, _F32)
        for c in range(d_expert // FFN_CHUNK):
            lo, hi = c * FFN_CHUNK, (c + 1) * FFN_CHUNK
            g = jnp.dot(x, wup_ref[0, :, lo:hi], preferred_element_type=_F32) + bup_ref[0, :, lo:hi]
            u = (jnp.dot(x, wup_ref[0, :, d_expert + lo:d_expert + hi], preferred_element_type=_F32)
                 + bup_ref[0, :, d_expert + lo:d_expert + hi])
            gate = jnp.minimum(g, SWIGLU_LIMIT)
            up = jnp.clip(u, -SWIGLU_LIMIT, SWIGLU_LIMIT)
            act = gate * jax.nn.sigmoid(SWIGLU_ALPHA * gate) * (up + 1.0)
            acc = acc + jnp.dot(act.astype(_BF16), wdn_ref[0, lo:hi, :], preferred_element_type=_F32)
        ys_ref[...] = acc + bdn_ref[0]

    @pl.when(pl.program_id(0) >= n_used_ref[0])
    def _():
        ys_ref[...] = jnp.zeros_like(ys_ref)


def _experts(xs, blk_e, blk_src, n_used, w_up, b_up, w_down, b_down):
    n_rows, dm = xs.shape
    n_blocks = n_rows // MOE_ROWS
    n_exp, _, two_de = w_up.shape
    d_expert = two_de // 2
    grid_spec = pltpu.PrefetchScalarGridSpec(
        num_scalar_prefetch=3,
        grid=(n_blocks,),
        in_specs=[
            pl.BlockSpec((MOE_ROWS, dm), lambda i, be, bs, nu: (bs[i], 0)),
            pl.BlockSpec((1, dm, two_de), lambda i, be, bs, nu: (be[i], 0, 0)),
            pl.BlockSpec((1, 1, two_de), lambda i, be, bs, nu: (be[i], 0, 0)),
            pl.BlockSpec((1, d_expert, dm), lambda i, be, bs, nu: (be[i], 0, 0)),
            pl.BlockSpec((1, 1, dm), lambda i, be, bs, nu: (be[i], 0, 0)),
        ],
        out_specs=pl.BlockSpec((MOE_ROWS, dm), lambda i, be, bs, nu: (i, 0)),
    )
    return pl.pallas_call(
        functools.partial(_expert_kernel, d_expert=d_expert),
        grid_spec=grid_spec,
        out_shape=jax.ShapeDtypeStruct((n_rows, dm), _F32),
        compiler_params=_params("arbitrary"),
        name="moe_experts",
    )(blk_e, blk_src, n_used, xs, w_up, b_up.reshape(n_exp, 1, two_de), w_down, b_down.reshape(n_exp, 1, dm))


def _combine_kernel(dest_ref, h_ref, gate_ref, g2_ref, b2_ref, ys_hbm, o_ref, ybuf, sem, *, tm, alpha):
    def issue(t, carry):
        for k in range(TOP_K):
            pltpu.make_async_copy(ys_hbm.at[pl.ds(dest_ref[k, t], 1)], ybuf.at[k, pl.ds(t, 1)], sem).start()
        return carry

    lax.fori_loop(0, tm, issue, 0, unroll=8)
    for k in range(TOP_K):
        pltpu.make_async_copy(ys_hbm.at[pl.ds(0, tm)], ybuf.at[k], sem).wait()

    gates = gate_ref[...]
    y = gates[:, 0:1] * ybuf[0]
    for k in range(1, TOP_K):
        y = y + gates[:, k:k + 1] * ybuf[k]
    o_ref[...] = _layer_norm(alpha * h_ref[...] + y, g2_ref[...], b2_ref[...])


def _combine(h, gates_col, dest, ys, g2, b2, alpha):
    t, dm = h.shape
    tm = COMBINE_TM
    return pl.pallas_call(
        functools.partial(_combine_kernel, tm=tm, alpha=alpha),
        grid=(t // tm,),
        in_specs=[
            pl.BlockSpec((TOP_K, tm), lambda i: (0, i), memory_space=pltpu.SMEM),
            pl.BlockSpec((tm, dm), lambda i: (i, 0)),
            pl.BlockSpec((tm, LANES), lambda i: (i, 0)),
            pl.BlockSpec((1, dm), lambda i: (0, 0)),
            pl.BlockSpec((1, dm), lambda i: (0, 0)),
            pl.BlockSpec(memory_space=pl.ANY),
        ],
        out_specs=pl.BlockSpec((tm, dm), lambda i: (i, 0)),
        out_shape=jax.ShapeDtypeStruct((t, dm), _F32),
        scratch_shapes=[pltpu.VMEM((TOP_K, tm, dm), _F32), pltpu.SemaphoreType.DMA(())],
        compiler_params=_params("arbitrary"),
        name="moe_combine",
    )(dest, h, gates_col, g2.reshape(1, dm), b2.reshape(1, dm), ys)


def _moe_plan(idx, rank, counts, n_assign):
    counts = counts.astype(jnp.int32)
    padded = (counts + MOE_ROWS - 1) // MOE_ROWS * MOE_ROWS
    pend = jnp.cumsum(padded)
    pstart = pend - padded
    onehot = idx[..., None] == jnp.arange(N_EXPERTS, dtype=jnp.int32)
    dest = rank + jnp.sum(jnp.where(onehot, pstart, 0), axis=-1)
    n_blocks = -(-n_assign // MOE_ROWS) + N_EXPERTS
    blk_start = jnp.arange(n_blocks, dtype=jnp.int32) * MOE_ROWS
    blk_e = jnp.minimum(jnp.searchsorted(pend, blk_start, side='right'), N_EXPERTS - 1).astype(jnp.int32)
    n_used = (pend[-1] // MOE_ROWS).astype(jnp.int32)
    blk_src = jnp.minimum(jnp.arange(n_blocks, dtype=jnp.int32), n_used - 1)
    blk_e = blk_e[blk_src]
    return dest.astype(jnp.int32), blk_e, blk_src, n_used.reshape(1), n_blocks * MOE_ROWS


def kernel(x, w_in, b_in, lambda_q1, lambda_k1, lambda_q2, lambda_k2, subln_g, w_proj_a, w_proj_b, w_out, b_out, ln1_g, ln1_b, w_router, b_router, w_up, b_up, w_down, b_down, ln2_g, ln2_b):
    bsz, seq, dm = x.shape
    depth = w_in.shape[0]
    alpha = (2.0 * depth) ** 0.25
    t = bsz * seq
    for layer in range(depth):
        lambda_init = 0.8 - 0.6 * math.exp(-0.3 * layer)
        x2d = x.reshape(t, dm)
        proj2d = _in_proj(x2d, w_in[layer].astype(_BF16), b_in[layer])
        proj = proj2d.reshape(bsz, seq, -1)

        state = None
        for n, (_, dilation) in enumerate(DILATED_PATTERNS):
            state = _dilated_pass(proj, dilation, state, final=(n == len(DILATED_PATTERNS) - 1))
        o_a = state.reshape(t, WIDTH_A)

        lam_vecs = jnp.stack([lambda_q1[layer], lambda_k1[layer], lambda_q2[layer], lambda_k2[layer]]).astype(_F32)
        o_b = _diff_attention(proj, lam_vecs, subln_g[layer], lambda_init).reshape(t, WIDTH_B)

        h, idx, gates_col, rank, cnt = _merge_router(
            x2d, proj2d, o_a, o_b, w_proj_a[layer].astype(_BF16), w_proj_b[layer].astype(_BF16),
            w_out[layer].astype(_BF16), b_out[layer], ln1_g[layer], ln1_b[layer],
            w_router[layer], b_router[layer], alpha)

        dest, blk_e, blk_src, n_used, n_rows = _moe_plan(idx, rank, cnt[:, 0], t * TOP_K)
        xs = _dispatch(h, dest, n_rows)
        ys = _experts(xs, blk_e, blk_src, n_used, w_up[layer].astype(_BF16), b_up[layer],
                      w_down[layer].astype(_BF16), b_down[layer])
        out = _combine(h, gates_col, dest, ys, ln2_g[layer], ln2_b[layer], alpha)
        x = out.reshape(bsz, seq, dm)
    return x
```

```python
import functools
import math

import numpy as np
import jax
import jax.numpy as jnp
from jax import lax
from jax.experimental import pallas as pl
from jax.experimental.pallas import tpu as pltpu

HEAD_DIM = 64
N_HEADS_A = 8
DILATED_PATTERNS = ((128, 1), (512, 4), (2048, 16))
N_HEADS_B = 4
WIDTH_A = N_HEADS_A * HEAD_DIM
WIDTH_B = N_HEADS_B * 2 * HEAD_DIM
N_ALIBI_HEADS = N_HEADS_A + N_HEADS_B
ALIBI_IDX_A = (0, 1, 3, 4, 6, 7, 9, 10)
ALIBI_IDX_B = (2, 5, 8, 11)
Q_BLOCK = 128
MASK_VALUE = -1e30
N_EXPERTS = 32
TOP_K = 4
SWIGLU_ALPHA = 1.702
SWIGLU_LIMIT = 7.0
LN_EPS = 1e-5
SUBLN_EPS = 1e-5

LANES = 128
V7X_VMEM_LIMIT_BYTES = 56 * 1024 * 1024

PROJ_TM = 1024
PROJ_TN = 1024
DIL_ROWS = 2 * Q_BLOCK
DIL_HALO = 64
DIFF_TQ = 512
DIFF_TK = 512
MERGE_TM = 512
MOE_ROWS = 512
FFN_CHUNK = 512
COMBINE_TM = 256
DISPATCH_TM = 512

_F32 = jnp.float32
_BF16 = jnp.bfloat16


def _params(*sem):
    return pltpu.CompilerParams(dimension_semantics=sem, vmem_limit_bytes=V7X_VMEM_LIMIT_BYTES)


def _alibi_slopes():
    return (2.0 ** (-8.0 * np.arange(1, N_ALIBI_HEADS + 1) / N_ALIBI_HEADS)).astype(np.float32)


def _in_proj_kernel(x_ref, w_ref, b_ref, o_ref):
    x = x_ref[...].astype(_BF16)
    acc = jnp.dot(x, w_ref[...], preferred_element_type=_F32)
    o_ref[...] = (acc + b_ref[...]).astype(o_ref.dtype)


def _in_proj(x2d, w_bf16, b):
    t, dm = x2d.shape
    n = w_bf16.shape[1]
    return pl.pallas_call(
        _in_proj_kernel,
        grid=(t // PROJ_TM, n // PROJ_TN),
        in_specs=[
            pl.BlockSpec((PROJ_TM, dm), lambda i, j: (i, 0)),
            pl.BlockSpec((dm, PROJ_TN), lambda i, j: (0, j)),
            pl.BlockSpec((1, PROJ_TN), lambda i, j: (0, j)),
        ],
        out_specs=pl.BlockSpec((PROJ_TM, PROJ_TN), lambda i, j: (i, j)),
        out_shape=jax.ShapeDtypeStruct((t, n), _BF16),
        compiler_params=_params("parallel", "arbitrary"),
        name="in_proj",
    )(x2d, w_bf16, b.reshape(1, n))


def _dilated_bias_table(dilation):
    slopes = _alibi_slopes()[list(ALIBI_IDX_A)]
    band = Q_BLOCK + 2 * DIL_HALO
    qi = np.arange(Q_BLOCK)[:, None]
    kj = np.arange(band)[None, :]
    rel = qi - kj + DIL_HALO
    in_band = np.abs(rel) <= DIL_HALO
    base = -slopes[:, None, None] * (dilation * np.abs(rel)).astype(np.float32)[None]
    edge = (np.ones_like(kj, bool), kj >= DIL_HALO, kj < band - DIL_HALO)
    out = np.stack([np.where(in_band & e, base, np.float32(MASK_VALUE)) for e in edge])
    return out.astype(np.float32)


def _dilated_kernel(*refs, nblk, has_prev, final):
    q_ref, km_ref, kp_ref, kn_ref, vm_ref, vp_ref, vn_ref, bias_ref = refs[:8]
    pos = 8
    if has_prev:
        po_ref, plse_ref = refs[pos:pos + 2]
        pos += 2
    o_ref = refs[pos]
    pos += 1
    if not final:
        lse_ref = refs[pos]
        pos += 1
    kbuf, vbuf = refs[pos:pos + 2]

    i = pl.program_id(2)
    h0, h1 = DIL_HALO, DIL_HALO + DIL_ROWS
    kbuf[0:h0, :] = kp_ref[0]
    kbuf[h0:h1, :] = km_ref[0]
    kbuf[h1:h1 + DIL_HALO, :] = kn_ref[0]
    vbuf[0:h0, :] = vp_ref[0]
    vbuf[h0:h1, :] = vm_ref[0]
    vbuf[h1:h1 + DIL_HALO, :] = vn_ref[0]

    lane = lax.broadcasted_iota(jnp.int32, (Q_BLOCK, LANES), 1)
    low_half = lane < HEAD_DIM
    scale = HEAD_DIM ** -0.5
    band = Q_BLOCK + 2 * DIL_HALO
    variants = (jnp.where(i == 0, 1, 0), jnp.where(i == nblk - 1, 2, 0))

    for j in range(DIL_ROWS // Q_BLOCK):
        rows = slice(j * Q_BLOCK, (j + 1) * Q_BLOCK)
        krows = slice(j * Q_BLOCK, j * Q_BLOCK + band)
        lse_cols = []
        for hp in range(N_HEADS_A // 2):
            cols = slice(hp * LANES, (hp + 1) * LANES)
            q_pair = q_ref[0, rows, cols]
            k_pair = kbuf[krows, cols]
            v_pair = vbuf[krows, cols]
            halves = []
            for hh in range(2):
                head = 2 * hp + hh
                keep = low_half if hh == 0 else jnp.logical_not(low_half)
                qm = jnp.where(keep, q_pair, jnp.zeros_like(q_pair)) * scale
                s = lax.dot_general(qm, k_pair, (((1,), (1,)), ((), ())), preferred_element_type=_F32)
                s = s + bias_ref[variants[j], head]
                m = jnp.max(s, axis=-1, keepdims=True)
                p = jnp.exp(s - m)
                z = jnp.sum(p, axis=-1, keepdims=True)
                o = jnp.dot(p.astype(_BF16), v_pair, preferred_element_type=_F32) * (1.0 / z)
                lse = m + jnp.log(z)
                if has_prev:
                    lse_p = plse_ref[0, rows, head:head + 1]
                    mx = jnp.maximum(lse_p, lse)
                    a = jnp.exp(lse_p - mx)
                    b = jnp.exp(lse - mx)
                    tot = a + b
                    inv = 1.0 / tot
                    o = o * (b * inv) + po_ref[0, rows, cols] * (a * inv)
                    lse = mx + jnp.log(tot)
                halves.append(o)
                lse_cols.append(lse)
            o_ref[0, rows, cols] = jnp.where(low_half, halves[0], halves[1]).astype(o_ref.dtype)
        if not final:
            tile = jnp.zeros((Q_BLOCK, LANES), _F32)
            for head in range(N_HEADS_A):
                tile = jnp.where(lane == head, lse_cols[head], tile)
            lse_ref[0, rows, :] = tile


def _dilated_pass(proj, dilation, prev, final):
    bsz, seq, width = proj.shape
    sec = width // WIDTH_A
    L = seq // dilation
    assert L % DIL_ROWS == 0 and DIL_ROWS % DIL_HALO == 0
    nblk = L // DIL_ROWS
    per = DIL_ROWS // DIL_HALO
    n_halo = L // DIL_HALO
    pv = proj.reshape(bsz, L, dilation * width)
    bias = jnp.asarray(_dilated_bias_table(dilation))
    has_prev = prev is not None

    def main(s):
        return pl.BlockSpec((1, DIL_ROWS, WIDTH_A), lambda b, r, i: (b, i, r * sec + s))

    def left(s):
        return pl.BlockSpec((1, DIL_HALO, WIDTH_A),
                            lambda b, r, i: (b, jnp.maximum(per * i - 1, 0), r * sec + s))

    def right(s):
        return pl.BlockSpec((1, DIL_HALO, WIDTH_A),
                            lambda b, r, i: (b, jnp.minimum(per * i + per, n_halo - 1), r * sec + s))

    state_o = pl.BlockSpec((1, DIL_ROWS, WIDTH_A), lambda b, r, i: (b, i, r))
    state_l = pl.BlockSpec((1, DIL_ROWS, LANES), lambda b, r, i: (b, i, r))

    in_specs = [main(0), main(1), left(1), right(1), main(2), left(2), right(2),
                pl.BlockSpec(bias.shape, lambda b, r, i: (0, 0, 0, 0))]
    args = [pv, pv, pv, pv, pv, pv, pv, bias]
    if has_prev:
        in_specs += [state_o, state_l]
        args += [prev[0].reshape(bsz, L, dilation * WIDTH_A), prev[1].reshape(bsz, L, dilation * LANES)]
    if final:
        out_specs = state_o
        out_shape = jax.ShapeDtypeStruct((bsz, L, dilation * WIDTH_A), _BF16)
    else:
        out_specs = [state_o, state_l]
        out_shape = [jax.ShapeDtypeStruct((bsz, L, dilation * WIDTH_A), _F32),
                     jax.ShapeDtypeStruct((bsz, L, dilation * LANES), _F32)]
    rows_buf = DIL_ROWS + 2 * DIL_HALO
    out = pl.pallas_call(
        functools.partial(_dilated_kernel, nblk=nblk, has_prev=has_prev, final=final),
        grid=(bsz, dilation, nblk),
        in_specs=in_specs,
        out_specs=out_specs,
        out_shape=out_shape,
        scratch_shapes=[pltpu.VMEM((rows_buf, WIDTH_A), _BF16), pltpu.VMEM((rows_buf, WIDTH_A), _BF16)],
        compiler_params=_params("parallel", "parallel", "arbitrary"),
        name=f"dilated_d{dilation}",
    )(*args)
    if final:
        return out.reshape(bsz, seq, WIDTH_A)
    return out[0].reshape(bsz, seq, WIDTH_A), out[1].reshape(bsz, seq, LANES)


def _diff_kernel(slope_ref, lam_ref, q_ref, k_ref, v_ref, g_ref, o_ref, *, seq, lambda_init):
    head = pl.program_id(1)
    q0 = pl.program_id(2) * DIFF_TQ
    slope = slope_ref[head]
    scale = HEAD_DIM ** -0.5

    q = q_ref[0]
    lane = lax.broadcasted_iota(jnp.int32, q.shape, 1)
    zero = jnp.zeros_like(q)
    q_maps = (jnp.where(lane < HEAD_DIM, q, zero) * scale, jnp.where(lane >= HEAD_DIM, q, zero) * scale)
    rel0 = (lax.broadcasted_iota(jnp.int32, (DIFF_TQ, DIFF_TK), 0)
            - lax.broadcasted_iota(jnp.int32, (DIFF_TQ, DIFF_TK), 1)).astype(_F32)

    def body(kb, carry):
        k0 = kb * DIFF_TK
        kblk = k_ref[0, pl.ds(pl.multiple_of(k0, DIFF_TK), DIFF_TK), :]
        vblk = v_ref[0, pl.ds(pl.multiple_of(k0, DIFF_TK), DIFF_TK), :]
        bias = jnp.abs(rel0 + (q0 - k0).astype(_F32)) * (-slope)
        new = []
        for mp in range(2):
            m_old, l_old, acc = carry[mp]
            s = lax.dot_general(q_maps[mp], kblk, (((1,), (1,)), ((), ())), preferred_element_type=_F32) + bias
            m_new = jnp.maximum(m_old, jnp.max(s, axis=-1, keepdims=True))
            alpha = jnp.exp(m_old - m_new)
            p = jnp.exp(s - m_new)
            l_new = alpha * l_old + jnp.sum(p, axis=-1, keepdims=True)
            acc = alpha * acc + jnp.dot(p.astype(_BF16), vblk, preferred_element_type=_F32)
            new.append((m_new, l_new, acc))
        return tuple(new)

    init = tuple((jnp.full((DIFF_TQ, 1), -jnp.inf, _F32), jnp.zeros((DIFF_TQ, 1), _F32),
                  jnp.zeros((DIFF_TQ, LANES), _F32)) for _ in range(2))
    (_, l1, a1), (_, l2, a2) = lax.fori_loop(0, seq // DIFF_TK, body, init)

    lv = lam_ref[...]
    lam = (jnp.exp(jnp.sum(lv[0:1] * lv[1:2], axis=-1, keepdims=True))
           - jnp.exp(jnp.sum(lv[2:3] * lv[3:4], axis=-1, keepdims=True)) + lambda_init)
    of = a1 * (1.0 / l1) - lam * (a2 * (1.0 / l2))
    of = of * lax.rsqrt(jnp.mean(of * of, axis=-1, keepdims=True) + SUBLN_EPS)
    of = of * g_ref[...] * (1.0 - lambda_init)
    o_ref[0] = of.astype(o_ref.dtype)


def _diff_attention(proj, lam_vecs, subln_g, lambda_init):
    bsz, seq, width = proj.shape
    assert seq % DIFF_TQ == 0 and seq % DIFF_TK == 0
    qcol = 3 * WIDTH_A // LANES
    kcol = qcol + WIDTH_B // LANES
    vcol = kcol + WIDTH_B // LANES
    slopes = jnp.asarray(_alibi_slopes()[list(ALIBI_IDX_B)])
    return pl.pallas_call(
        functools.partial(_diff_kernel, seq=seq, lambda_init=lambda_init),
        grid=(bsz, N_HEADS_B, seq // DIFF_TQ),
        in_specs=[
            pl.BlockSpec(memory_space=pltpu.SMEM),
            pl.BlockSpec((4, HEAD_DIM), lambda b, h, i: (0, 0)),
            pl.BlockSpec((1, DIFF_TQ, LANES), lambda b, h, i: (b, i, qcol + h)),
            pl.BlockSpec((1, seq, LANES), lambda b, h, i: (b, 0, kcol + h)),
            pl.BlockSpec((1, seq, LANES), lambda b, h, i: (b, 0, vcol + h)),
            pl.BlockSpec((1, LANES), lambda b, h, i: (0, 0)),
        ],
        out_specs=pl.BlockSpec((1, DIFF_TQ, LANES), lambda b, h, i: (b, i, h)),
        out_shape=jax.ShapeDtypeStruct((bsz, seq, WIDTH_B), _BF16),
        compiler_params=_params("parallel", "parallel", "arbitrary"),
        name="diff_attn",
    )(slopes, lam_vecs, proj, proj, proj, subln_g.reshape(1, LANES).astype(_F32))


def _layer_norm(z, g, b):
    mu = jnp.mean(z, axis=-1, keepdims=True)
    zc = z - mu
    var = jnp.mean(zc * zc, axis=-1, keepdims=True)
    return zc * lax.rsqrt(var + LN_EPS) * g + b


def _merge_kernel(x_ref, ga_ref, gb_ref, oa_ref, ob_ref, wpa_ref, wpb_ref, wout_ref, bout_ref,
                  g1_ref, b1_ref, wrh_ref, wrl_ref, br_ref,
                  h_ref, idx_ref, gate_ref, rank_ref, cnt_ref, carry_ref, *, alpha):
    step = pl.program_id(0)

    @pl.when(step == 0)
    def _():
        carry_ref[...] = jnp.zeros_like(carry_ref)

    pa = jnp.dot(oa_ref[...], wpa_ref[...], preferred_element_type=_F32)
    pb = jnp.dot(ob_ref[...], wpb_ref[...], preferred_element_type=_F32)
    merged = jax.nn.sigmoid(ga_ref[...].astype(_F32)) * pa + jax.nn.sigmoid(gb_ref[...].astype(_F32)) * pb
    y = jnp.dot(merged.astype(_BF16), wout_ref[...], preferred_element_type=_F32) + bout_ref[...]
    h = _layer_norm(alpha * x_ref[...] + y, g1_ref[...], b1_ref[...])
    h_ref[...] = h

    h_hi = h.astype(_BF16)
    h_lo = (h - h_hi.astype(_F32)).astype(_BF16)
    nt = (((1,), (1,)), ((), ()))
    logits = (lax.dot_general(wrh_ref[...], h_hi, nt, preferred_element_type=_F32)
              + lax.dot_general(wrl_ref[...], h_hi, nt, preferred_element_type=_F32)
              + lax.dot_general(wrh_ref[...], h_lo, nt, preferred_element_type=_F32)
              + br_ref[...])

    tm = logits.shape[1]
    expert = lax.broadcasted_iota(jnp.int32, logits.shape, 0)
    work = logits
    vals, sels, idxs = [], [], []
    for _ in range(TOP_K):
        mx = jnp.max(work, axis=0, keepdims=True)
        idx = jnp.min(jnp.where(work == mx, expert, N_EXPERTS), axis=0, keepdims=True)
        sel = expert == idx
        work = jnp.where(sel, -jnp.inf, work)
        vals.append(mx)
        idxs.append(idx)
        sels.append(sel)
    ex = [jnp.exp(v - vals[0]) for v in vals]
    inv = 1.0 / (ex[0] + ex[1] + ex[2] + ex[3])
    gates = [e * inv for e in ex]

    chosen = sels[0] | sels[1] | sels[2] | sels[3]
    onehot = jnp.where(chosen, 1.0, 0.0).astype(_BF16)
    before = (lax.broadcasted_iota(jnp.int32, (tm, tm), 0)
              < lax.broadcasted_iota(jnp.int32, (tm, tm), 1))
    upper = jnp.where(before, 1.0, 0.0).astype(_BF16)
    prefix = jnp.dot(onehot, upper, preferred_element_type=_F32) + carry_ref[:, 0:1]
    for k in range(TOP_K):
        rank = jnp.sum(jnp.where(sels[k], prefix, 0.0), axis=0, keepdims=True)
        rank_ref[k:k + 1, :] = rank.astype(jnp.int32)
        idx_ref[k:k + 1, :] = idxs[k]
    total = carry_ref[...] + jnp.sum(onehot.astype(_F32), axis=1, keepdims=True)
    carry_ref[...] = total
    cnt_ref[...] = total

    row = lax.broadcasted_iota(jnp.int32, (LANES, tm), 0)
    g_rows = jnp.zeros((LANES, tm), _F32)
    for k in range(TOP_K):
        g_rows = jnp.where(row == k, gates[k], g_rows)
    gate_ref[...] = g_rows.T


def _merge_router(x2d, proj2d, o_a, o_b, wpa, wpb, wout, bout, g1, b1, w_router, b_router, alpha):
    t, dm = x2d.shape
    tm = MERGE_TM
    gcol = (3 * WIDTH_A + 3 * WIDTH_B) // dm
    wr_t = w_router.T.astype(_F32)
    wr_hi = wr_t.astype(_BF16)
    wr_lo = (wr_t - wr_hi.astype(_F32)).astype(_BF16)

    def const(shape):
        return pl.BlockSpec(shape, lambda i: tuple(0 for _ in shape))

    return pl.pallas_call(
        functools.partial(_merge_kernel, alpha=alpha),
        grid=(t // tm,),
        in_specs=[
            pl.BlockSpec((tm, dm), lambda i: (i, 0)),
            pl.BlockSpec((tm, dm), lambda i: (i, gcol)),
            pl.BlockSpec((tm, dm), lambda i: (i, gcol + 1)),
            pl.BlockSpec((tm, WIDTH_A), lambda i: (i, 0)),
            pl.BlockSpec((tm, WIDTH_B), lambda i: (i, 0)),
            const((WIDTH_A, dm)), const((WIDTH_B, dm)), const((dm, dm)), const((1, dm)),
            const((1, dm)), const((1, dm)),
            const((N_EXPERTS, dm)), const((N_EXPERTS, dm)), const((N_EXPERTS, 1)),
        ],
        out_specs=[
            pl.BlockSpec((tm, dm), lambda i: (i, 0)),
            pl.BlockSpec((TOP_K, tm), lambda i: (0, i)),
            pl.BlockSpec((tm, LANES), lambda i: (i, 0)),
            pl.BlockSpec((TOP_K, tm), lambda i: (0, i)),
            pl.BlockSpec((N_EXPERTS, LANES), lambda i: (0, 0)),
        ],
        out_shape=[
            jax.ShapeDtypeStruct((t, dm), _F32),
            jax.ShapeDtypeStruct((TOP_K, t), jnp.int32),
            jax.ShapeDtypeStruct((t, LANES), _F32),
            jax.ShapeDtypeStruct((TOP_K, t), jnp.int32),
            jax.ShapeDtypeStruct((N_EXPERTS, LANES), _F32),
        ],
        scratch_shapes=[pltpu.VMEM((N_EXPERTS, LANES), _F32)],
        compiler_params=_params("arbitrary"),
        name="merge_router",
    )(x2d, proj2d, proj2d, o_a, o_b, wpa, wpb, wout, bout.reshape(1, dm), g1.reshape(1, dm),
      b1.reshape(1, dm), wr_hi, wr_lo, b_router.reshape(N_EXPERTS, 1).astype(_F32))


def _dispatch_kernel(dest_ref, h_ref, xs_in, xs_hbm, sem, *, tm):
    del xs_in

    def issue(t, carry):
        for k in range(TOP_K):
            pltpu.make_async_copy(h_ref.at[pl.ds(t, 1)], xs_hbm.at[pl.ds(dest_ref[k, t], 1)], sem).start(priority=k % 2)
        return carry

    lax.fori_loop(0, tm, issue, 0, unroll=8)
    for _ in range(TOP_K):
        pltpu.make_async_copy(h_ref, xs_hbm.at[pl.ds(0, tm)], sem).wait()


def _dispatch(h, dest, n_rows):
    t, dm = h.shape
    tm = DISPATCH_TM
    zeros = jnp.zeros((n_rows, dm), h.dtype)
    return pl.pallas_call(
        functools.partial(_dispatch_kernel, tm=tm),
        grid=(t // tm,),
        in_specs=[
            pl.BlockSpec((TOP_K, tm), lambda i: (0, i), memory_space=pltpu.SMEM),
            pl.BlockSpec((tm, dm), lambda i: (i, 0)),
            pl.BlockSpec(memory_space=pl.ANY),
        ],
        out_specs=pl.BlockSpec(memory_space=pl.ANY),
        out_shape=jax.ShapeDtypeStruct((n_rows, dm), h.dtype),
        scratch_shapes=[pltpu.SemaphoreType.DMA(())],
        input_output_aliases={2: 0},
        compiler_params=_params("arbitrary"),
        name="moe_dispatch",
    )(dest, h, zeros)


def _expert_kernel(blk_e_ref, blk_src_ref, n_used_ref, xs_ref, wup_ref, bup_ref, wdn_ref, bdn_ref, ys_ref,
                   *, d_expert):
    del blk_e_ref, blk_src_ref

    @pl.when(pl.program_id(0) < n_used_ref[0])
    def _():
        x = xs_ref[...].astype(_BF16)
        acc = jnp.zeros(ys_ref.shape, _F32)
        for c in range(d_expert // FFN_CHUNK):
            lo, hi = c * FFN_CHUNK, (c + 1) * FFN_CHUNK
            g = jnp.dot(x, wup_ref[0, :, lo:hi], preferred_element_type=_F32) + bup_ref[0, :, lo:hi]
            u = (jnp.dot(x, wup_ref[0, :, d_expert + lo:d_expert + hi], preferred_element_type=_F32)
                 + bup_ref[0, :, d_expert + lo:d_expert + hi])
            gate = jnp.minimum(g, SWIGLU_LIMIT)
            up = jnp.clip(u, -SWIGLU_LIMIT, SWIGLU_LIMIT)
            act = gate * jax.nn.sigmoid(SWIGLU_ALPHA * gate) * (up + 1.0)
            acc = acc + jnp.dot(act.astype(_BF16), wdn_ref[0, lo:hi, :], preferred_element_type=_F32)
        ys_ref[...] = acc + bdn_ref[0]

    @pl.when(pl.program_id(0) >= n_used_ref[0])
    def _():
        ys_ref[...] = jnp.zeros_like(ys_ref)


def _experts(xs, blk_e, blk_src, n_used, w_up, b_up, w_down, b_down):
    n_rows, dm = xs.shape
    n_blocks = n_rows // MOE_ROWS
    n_exp, _, two_de = w_up.shape
    d_expert = two_de // 2
    grid_spec = pltpu.PrefetchScalarGridSpec(
        num_scalar_prefetch=3,
        grid=(n_blocks,),
        in_specs=[
            pl.BlockSpec((MOE_ROWS, dm), lambda i, be, bs, nu: (bs[i], 0)),
            pl.BlockSpec((1, dm, two_de), lambda i, be, bs, nu: (be[i], 0, 0)),
            pl.BlockSpec((1, 1, two_de), lambda i, be, bs, nu: (be[i], 0, 0)),
            pl.BlockSpec((1, d_expert, dm), lambda i, be, bs, nu: (be[i], 0, 0)),
            pl.BlockSpec((1, 1, dm), lambda i, be, bs, nu: (be[i], 0, 0)),
        ],
        out_specs=pl.BlockSpec((MOE_ROWS, dm), lambda i, be, bs, nu: (i, 0)),
    )
    return pl.pallas_call(
        functools.partial(_expert_kernel, d_expert=d_expert),
        grid_spec=grid_spec,
        out_shape=jax.ShapeDtypeStruct((n_rows, dm), _F32),
        compiler_params=_params("arbitrary"),
        name="moe_experts",
    )(blk_e, blk_src, n_used, xs, w_up, b_up.reshape(n_exp, 1, two_de), w_down, b_down.reshape(n_exp, 1, dm))


def _combine_kernel(dest_ref, h_ref, gate_ref, g2_ref, b2_ref, ys_hbm, o_ref, ybuf, sem, *, tm, alpha):
    def issue(t, carry):
        for k in range(TOP_K):
            pltpu.make_async_copy(ys_hbm.at[pl.ds(dest_ref[k, t], 1)], ybuf.at[k, pl.ds(t, 1)], sem).start(priority=k % 2)
        return carry

    lax.fori_loop(0, tm, issue, 0, unroll=8)
    for k in range(TOP_K):
        pltpu.make_async_copy(ys_hbm.at[pl.ds(0, tm)], ybuf.at[k], sem).wait()

    gates = gate_ref[...]
    y = gates[:, 0:1] * ybuf[0]
    for k in range(1, TOP_K):
        y = y + gates[:, k:k + 1] * ybuf[k]
    o_ref[...] = _layer_norm(alpha * h_ref[...] + y, g2_ref[...], b2_ref[...])


def _combine(h, gates_col, dest, ys, g2, b2, alpha):
    t, dm = h.shape
    tm = COMBINE_TM
    return pl.pallas_call(
        functools.partial(_combine_kernel, tm=tm, alpha=alpha),
        grid=(t // tm,),
        in_specs=[
            pl.BlockSpec((TOP_K, tm), lambda i: (0, i), memory_space=pltpu.SMEM),
            pl.BlockSpec((tm, dm), lambda i: (i, 0)),
            pl.BlockSpec((tm, LANES), lambda i: (i, 0)),
            pl.BlockSpec((1, dm), lambda i: (0, 0)),
            pl.BlockSpec((1, dm), lambda i: (0, 0)),
            pl.BlockSpec(memory_space=pl.ANY),
        ],
        out_specs=pl.BlockSpec((tm, dm), lambda i: (i, 0)),
        out_shape=jax.ShapeDtypeStruct((t, dm), _F32),
        scratch_shapes=[pltpu.VMEM((TOP_K, tm, dm), _F32), pltpu.SemaphoreType.DMA(())],
        compiler_params=_params("arbitrary"),
        name="moe_combine",
    )(dest, h, gates_col, g2.reshape(1, dm), b2.reshape(1, dm), ys)


def _moe_plan(idx, rank, counts, n_assign):
    counts = counts.astype(jnp.int32)
    padded = (counts + MOE_ROWS - 1) // MOE_ROWS * MOE_ROWS
    pend = jnp.cumsum(padded)
    pstart = pend - padded
    onehot = idx[..., None] == jnp.arange(N_EXPERTS, dtype=jnp.int32)
    dest = rank + jnp.sum(jnp.where(onehot, pstart, 0), axis=-1)
    n_blocks = -(-n_assign // MOE_ROWS) + N_EXPERTS
    blk_start = jnp.arange(n_blocks, dtype=jnp.int32) * MOE_ROWS
    blk_e = jnp.sum((pend[None, :] <= blk_start[:, None]).astype(jnp.int32), axis=1)
    blk_e = jnp.minimum(blk_e, N_EXPERTS - 1)
    n_used = (pend[-1] // MOE_ROWS).astype(jnp.int32)
    blk_src = jnp.minimum(jnp.arange(n_blocks, dtype=jnp.int32), n_used - 1)
    blk_e = blk_e[blk_src]
    return dest.astype(jnp.int32), blk_e, blk_src, n_used.reshape(1), n_blocks * MOE_ROWS


def kernel(x, w_in, b_in, lambda_q1, lambda_k1, lambda_q2, lambda_k2, subln_g, w_proj_a, w_proj_b, w_out, b_out, ln1_g, ln1_b, w_router, b_router, w_up, b_up, w_down, b_down, ln2_g, ln2_b):
    bsz, seq, dm = x.shape
    depth = w_in.shape[0]
    alpha = (2.0 * depth) ** 0.25
    t = bsz * seq
    for layer in range(depth):
        lambda_init = 0.8 - 0.6 * math.exp(-0.3 * layer)
        x2d = x.reshape(t, dm)
        proj2d = _in_proj(x2d, w_in[layer].astype(_BF16), b_in[layer])
        proj = proj2d.reshape(bsz, seq, -1)

        state = None
        for n, (_, dilation) in enumerate(DILATED_PATTERNS):
            state = _dilated_pass(proj, dilation, state, final=(n == len(DILATED_PATTERNS) - 1))
        o_a = state.reshape(t, WIDTH_A)

        lam_vecs = jnp.stack([lambda_q1[layer], lambda_k1[layer], lambda_q2[layer], lambda_k2[layer]]).astype(_F32)
        o_b = _diff_attention(proj, lam_vecs, subln_g[layer], lambda_init).reshape(t, WIDTH_B)

        h, idx, gates_col, rank, cnt = _merge_router(
            x2d, proj2d, o_a, o_b, w_proj_a[layer].astype(_BF16), w_proj_b[layer].astype(_BF16),
            w_out[layer].astype(_BF16), b_out[layer], ln1_g[layer], ln1_b[layer],
            w_router[layer], b_router[layer], alpha)

        dest, blk_e, blk_src, n_used, n_rows = _moe_plan(idx, rank, cnt[:, 0], t * TOP_K)
        xs = _dispatch(h, dest, n_rows)
        ys = _experts(xs, blk_e, blk_src, n_used, w_up[layer].astype(_BF16), b_up[layer],
                      w_down[layer].astype(_BF16), b_down[layer])
        out = _combine(h, gates_col, dest, ys, ln2_g[layer], ln2_b[layer], alpha)
        x = out.reshape(bsz, seq, dm)
    return x
```

```python
import functools
import math

import numpy as np
import jax
import jax.numpy as jnp
from jax import lax
from jax.experimental import pallas as pl
from jax.experimental.pallas import tpu as pltpu

HEAD_DIM = 64
N_HEADS_A = 8
DILATED_PATTERNS = ((128, 1), (512, 4), (2048, 16))
N_HEADS_B = 4
WIDTH_A = N_HEADS_A * HEAD_DIM
WIDTH_B = N_HEADS_B * 2 * HEAD_DIM
N_ALIBI_HEADS = N_HEADS_A + N_HEADS_B
ALIBI_IDX_A = (0, 1, 3, 4, 6, 7, 9, 10)
ALIBI_IDX_B = (2, 5, 8, 11)
Q_BLOCK = 128
MASK_VALUE = -1e30
N_EXPERTS = 32
TOP_K = 4
SWIGLU_ALPHA = 1.702
SWIGLU_LIMIT = 7.0
LN_EPS = 1e-5
SUBLN_EPS = 1e-5
LOG2E = math.log2(math.e)
QUERY_SCALE = HEAD_DIM ** -0.5 * LOG2E

LANES = 128
V7X_VMEM_LIMIT_BYTES = 56 * 1024 * 1024

PROJ_TM = 1024
PROJ_TN = 1024
DIL_ROWS = 2 * Q_BLOCK
DIL_HALO = 64
DIFF_TQ = 512
DIFF_TK = 512
MERGE_TM = 512
MOE_ROWS = 512
FFN_CHUNK = 512
COMBINE_TM = 256
DISPATCH_TM = 512

_F32 = jnp.float32
_BF16 = jnp.bfloat16


def _params(*sem):
    return pltpu.CompilerParams(dimension_semantics=sem, vmem_limit_bytes=V7X_VMEM_LIMIT_BYTES)


def _alibi_slopes():
    return (2.0 ** (-8.0 * np.arange(1, N_ALIBI_HEADS + 1) / N_ALIBI_HEADS)).astype(np.float32)


def _in_proj_kernel(x_ref, w_ref, b_ref, cs_ref, o_ref):
    x = x_ref[...].astype(_BF16)
    acc = jnp.dot(x, w_ref[...], preferred_element_type=_F32)
    o_ref[...] = ((acc + b_ref[...]) * cs_ref[...]).astype(o_ref.dtype)


def _in_proj(x2d, w_bf16, b):
    t, dm = x2d.shape
    n = w_bf16.shape[1]
    colscale = np.ones((1, n), np.float32)
    colscale[:, 0:WIDTH_A] = QUERY_SCALE
    colscale[:, 3 * WIDTH_A:3 * WIDTH_A + WIDTH_B] = QUERY_SCALE
    return pl.pallas_call(
        _in_proj_kernel,
        grid=(t // PROJ_TM, n // PROJ_TN),
        in_specs=[
            pl.BlockSpec((PROJ_TM, dm), lambda i, j: (i, 0)),
            pl.BlockSpec((dm, PROJ_TN), lambda i, j: (0, j)),
            pl.BlockSpec((1, PROJ_TN), lambda i, j: (0, j)),
            pl.BlockSpec((1, PROJ_TN), lambda i, j: (0, j)),
        ],
        out_specs=pl.BlockSpec((PROJ_TM, PROJ_TN), lambda i, j: (i, j)),
        out_shape=jax.ShapeDtypeStruct((t, n), _BF16),
        compiler_params=_params("parallel", "arbitrary"),
        name="in_proj",
    )(x2d, w_bf16, b.reshape(1, n), jnp.asarray(colscale))


def _dilated_bias_table(dilation):
    slopes = _alibi_slopes()[list(ALIBI_IDX_A)]
    band = Q_BLOCK + 2 * DIL_HALO
    qi = np.arange(Q_BLOCK)[:, None]
    kj = np.arange(band)[None, :]
    rel = qi - kj + DIL_HALO
    in_band = np.abs(rel) <= DIL_HALO
    base = -slopes[:, None, None] * (dilation * np.abs(rel)).astype(np.float32)[None]
    base = (base.astype(np.float64) * LOG2E).astype(np.float32)
    edge = (np.ones_like(kj, bool), kj >= DIL_HALO, kj < band - DIL_HALO)
    out = np.stack([np.where(in_band & e, base, np.float32(MASK_VALUE)) for e in edge])
    return out.astype(np.float32)


def _dilated_kernel(*refs, nblk, has_prev, final):
    q_ref, km_ref, kp_ref, kn_ref, vm_ref, vp_ref, vn_ref, bias_ref = refs[:8]
    pos = 8
    if has_prev:
        po_ref, plse_ref = refs[pos:pos + 2]
        pos += 2
    o_ref = refs[pos]
    pos += 1
    if not final:
        lse_ref = refs[pos]
        pos += 1
    kbuf, vbuf = refs[pos:pos + 2]

    i = pl.program_id(2)
    h0, h1 = DIL_HALO, DIL_HALO + DIL_ROWS
    kbuf[0:h0, :] = kp_ref[0]
    kbuf[h0:h1, :] = km_ref[0]
    kbuf[h1:h1 + DIL_HALO, :] = kn_ref[0]
    vbuf[0:h0, :] = vp_ref[0]
    vbuf[h0:h1, :] = vm_ref[0]
    vbuf[h1:h1 + DIL_HALO, :] = vn_ref[0]

    lane = lax.broadcasted_iota(jnp.int32, (Q_BLOCK, LANES), 1)
    low_half = lane < HEAD_DIM
    band = Q_BLOCK + 2 * DIL_HALO
    variants = (jnp.where(i == 0, 1, 0), jnp.where(i == nblk - 1, 2, 0))

    for j in range(DIL_ROWS // Q_BLOCK):
        rows = slice(j * Q_BLOCK, (j + 1) * Q_BLOCK)
        krows = slice(j * Q_BLOCK, j * Q_BLOCK + band)
        lse_cols = []
        for hp in range(N_HEADS_A // 2):
            cols = slice(hp * LANES, (hp + 1) * LANES)
            q_pair = q_ref[0, rows, cols]
            k_pair = kbuf[krows, cols]
            v_pair = vbuf[krows, cols]
            halves = []
            for hh in range(2):
                head = 2 * hp + hh
                keep = low_half if hh == 0 else jnp.logical_not(low_half)
                qm = jnp.where(keep, q_pair, jnp.zeros_like(q_pair))
                s = lax.dot_general(qm, k_pair, (((1,), (1,)), ((), ())), preferred_element_type=_F32)
                s = s + bias_ref[variants[j], head]
                m = jnp.max(s, axis=-1, keepdims=True)
                p = jnp.exp2(s - m)
                z = jnp.sum(p, axis=-1, keepdims=True)
                o = jnp.dot(p.astype(_BF16), v_pair, preferred_element_type=_F32) * (1.0 / z)
                lse = m + jnp.log2(z)
                if has_prev:
                    lse_p = plse_ref[0, rows, head:head + 1]
                    mx = jnp.maximum(lse_p, lse)
                    a = jnp.exp2(lse_p - mx)
                    b = jnp.exp2(lse - mx)
                    tot = a + b
                    inv = 1.0 / tot
                    o = o * (b * inv) + po_ref[0, rows, cols] * (a * inv)
                    lse = mx + jnp.log2(tot)
                halves.append(o)
                lse_cols.append(lse)
            o_ref[0, rows, cols] = jnp.where(low_half, halves[0], halves[1]).astype(o_ref.dtype)
        if not final:
            tile = jnp.zeros((Q_BLOCK, LANES), _F32)
            for head in range(N_HEADS_A):
                tile = jnp.where(lane == head, lse_cols[head], tile)
            lse_ref[0, rows, :] = tile


def _dilated_pass(proj, dilation, prev, final):
    bsz, seq, width = proj.shape
    sec = width // WIDTH_A
    L = seq // dilation
    assert L % DIL_ROWS == 0 and DIL_ROWS % DIL_HALO == 0
    nblk = L // DIL_ROWS
    per = DIL_ROWS // DIL_HALO
    n_halo = L // DIL_HALO
    pv = proj.reshape(bsz, L, dilation * width)
    bias = jnp.asarray(_dilated_bias_table(dilation))
    has_prev = prev is not None

    def main(s):
        return pl.BlockSpec((1, DIL_ROWS, WIDTH_A), lambda b, r, i: (b, i, r * sec + s))

    def left(s):
        return pl.BlockSpec((1, DIL_HALO, WIDTH_A),
                            lambda b, r, i: (b, jnp.maximum(per * i - 1, 0), r * sec + s))

    def right(s):
        return pl.BlockSpec((1, DIL_HALO, WIDTH_A),
                            lambda b, r, i: (b, jnp.minimum(per * i + per, n_halo - 1), r * sec + s))

    state_o = pl.BlockSpec((1, DIL_ROWS, WIDTH_A), lambda b, r, i: (b, i, r))
    state_l = pl.BlockSpec((1, DIL_ROWS, LANES), lambda b, r, i: (b, i, r))

    in_specs = [main(0), main(1), left(1), right(1), main(2), left(2), right(2),
                pl.BlockSpec(bias.shape, lambda b, r, i: (0, 0, 0, 0))]
    args = [pv, pv, pv, pv, pv, pv, pv, bias]
    if has_prev:
        in_specs += [state_o, state_l]
        args += [prev[0].reshape(bsz, L, dilation * WIDTH_A), prev[1].reshape(bsz, L, dilation * LANES)]
    if final:
        out_specs = state_o
        out_shape = jax.ShapeDtypeStruct((bsz, L, dilation * WIDTH_A), _BF16)
    else:
        out_specs = [state_o, state_l]
        out_shape = [jax.ShapeDtypeStruct((bsz, L, dilation * WIDTH_A), _F32),
                     jax.ShapeDtypeStruct((bsz, L, dilation * LANES), _F32)]
    rows_buf = DIL_ROWS + 2 * DIL_HALO
    out = pl.pallas_call(
        functools.partial(_dilated_kernel, nblk=nblk, has_prev=has_prev, final=final),
        grid=(bsz, dilation, nblk),
        in_specs=in_specs,
        out_specs=out_specs,
        out_shape=out_shape,
        scratch_shapes=[pltpu.VMEM((rows_buf, WIDTH_A), _BF16), pltpu.VMEM((rows_buf, WIDTH_A), _BF16)],
        compiler_params=_params("parallel", "parallel", "arbitrary"),
        name=f"dilated_d{dilation}",
    )(*args)
    if final:
        return out.reshape(bsz, seq, WIDTH_A)
    return out[0].reshape(bsz, seq, WIDTH_A), out[1].reshape(bsz, seq, LANES)


def _split_bf16(x):
    hi = x.astype(jnp.bfloat16)
    lo = (x - hi.astype(np.float64)).astype(jnp.bfloat16)
    return hi, lo


def _diff_tables():
    s2 = _alibi_slopes()[list(ALIBI_IDX_B)].astype(np.float64) * LOG2E
    qaug = np.zeros((N_HEADS_B, DIFF_TQ, LANES), jnp.bfloat16)
    kaug = np.zeros((N_HEADS_B, 2, DIFF_TK, LANES), jnp.bfloat16)
    i = np.arange(DIFF_TQ, dtype=np.float64)
    j = np.arange(DIFF_TK, dtype=np.float64)
    for h in range(N_HEADS_B):
        r_hi, r_lo = _split_bf16(-s2[h] * i)
        qaug[h, :, 0] = 1.0
        qaug[h, :, 1] = 1.0
        qaug[h, :, 2] = r_hi
        qaug[h, :, 3] = r_lo
        for side, sign in enumerate((1.0, -1.0)):
            c_hi, c_lo = _split_bf16(sign * s2[h] * j)
            kaug[h, side, :, 0] = c_hi
            kaug[h, side, :, 1] = c_lo
            kaug[h, side, :, 2] = sign
            kaug[h, side, :, 3] = sign
    halves = DIFF_TK // DIFF_TQ
    off = (np.arange(halves) * DIFF_TQ)[:, None, None]
    dist = np.abs(off + i[None, :, None] - j[None, None, :])
    diag = (-s2[:, None, None, None] * dist[None]).astype(np.float32)
    return jnp.asarray(qaug), jnp.asarray(kaug), jnp.asarray(diag), jnp.asarray(s2.astype(np.float32))


def _diff_kernel(slope_ref, lam_ref, q_ref, k_ref, v_ref, qaug_ref, kaug_ref, diag_ref, g_ref, o_ref,
                 *, seq, lambda_init):
    head = pl.program_id(1)
    qb = pl.program_id(2)
    q0 = qb * DIFF_TQ
    halves = DIFF_TK // DIFF_TQ
    kd = qb // halves
    n_kb = seq // DIFF_TK
    slope2 = slope_ref[head]
    nt = (((1,), (1,)), ((), ()))

    q = q_ref[0]
    lane = lax.broadcasted_iota(jnp.int32, q.shape, 1)
    zero = jnp.zeros_like(q)
    qaug = qaug_ref[0]
    q_maps = (jnp.concatenate([jnp.where(lane < HEAD_DIM, q, zero), qaug], axis=1),
              jnp.concatenate([jnp.where(lane >= HEAD_DIM, q, zero), qaug], axis=1))
    zero_aug = jnp.zeros((DIFF_TK, LANES), _BF16)
    vlane = lax.broadcasted_iota(jnp.int32, (DIFF_TK, LANES), 1)
    ones_col = jnp.where(vlane == 0, 1.0, 0.0).astype(_BF16)

    def load(kb):
        k0 = pl.multiple_of(kb * DIFF_TK, DIFF_TK)
        return k_ref[0, pl.ds(k0, DIFF_TK), :], jnp.concatenate([v_ref[0, pl.ds(k0, DIFF_TK), :], ones_col], axis=1)

    def split(pv):
        return pv[:, LANES:LANES + 1], pv[:, 0:LANES]

    def update(s, const, vblk, carry):
        m_old, l_old, acc = carry
        m_new = jnp.maximum(m_old, jnp.max(s, axis=-1, keepdims=True) + const)
        alpha = jnp.exp2(m_old - m_new)
        p = jnp.exp2(s - (m_new - const))
        l_blk, o_blk = split(jnp.dot(p.astype(_BF16), vblk, preferred_element_type=_F32))
        return m_new, alpha * l_old + l_blk, alpha * acc + o_blk

    kblk, vblk = load(kd)
    k_all = jnp.concatenate([kblk, zero_aug], axis=1)
    bias = diag_ref[0, qb % halves]
    carries = []
    for mp in range(2):
        s = lax.dot_general(q_maps[mp], k_all, nt, preferred_element_type=_F32) + bias
        m = jnp.max(s, axis=-1, keepdims=True)
        p = jnp.exp2(s - m)
        carries.append((m,) + split(jnp.dot(p.astype(_BF16), vblk, preferred_element_type=_F32)))

    for t in range(n_kb - 1):
        kb = jnp.where(t >= kd, t + 1, t)
        side = jnp.where(t >= kd, 1, 0)
        kblk, vblk = load(kb)
        k_all = jnp.concatenate([kblk, kaug_ref[0, side]], axis=1)
        const = -slope2 * jnp.abs(q0 - kb * DIFF_TK).astype(_F32)
        for mp in range(2):
            s = lax.dot_general(q_maps[mp], k_all, nt, preferred_element_type=_F32)
            carries[mp] = update(s, const, vblk, carries[mp])

    lv = lam_ref[...]
    lam = (jnp.exp(jnp.sum(lv[0:1] * lv[1:2], axis=-1, keepdims=True))
           - jnp.exp(jnp.sum(lv[2:3] * lv[3:4], axis=-1, keepdims=True)) + lambda_init)
    (_, l1, a1), (_, l2, a2) = carries
    of = a1 * (1.0 / l1) - lam * (a2 * (1.0 / l2))
    of = of * lax.rsqrt(jnp.mean(of * of, axis=-1, keepdims=True) + SUBLN_EPS)
    of = of * g_ref[...] * (1.0 - lambda_init)
    o_ref[0] = of.astype(o_ref.dtype)


def _diff_attention(proj, lam_vecs, subln_g, lambda_init):
    bsz, seq, width = proj.shape
    assert seq % DIFF_TQ == 0 and seq % DIFF_TK == 0
    qcol = 3 * WIDTH_A // LANES
    kcol = qcol + WIDTH_B // LANES
    vcol = kcol + WIDTH_B // LANES
    assert DIFF_TK % DIFF_TQ == 0
    halves = DIFF_TK // DIFF_TQ
    qaug, kaug, diag, slopes2 = _diff_tables()
    return pl.pallas_call(
        functools.partial(_diff_kernel, seq=seq, lambda_init=lambda_init),
        grid=(bsz, N_HEADS_B, seq // DIFF_TQ),
        in_specs=[
            pl.BlockSpec(memory_space=pltpu.SMEM),
            pl.BlockSpec((4, HEAD_DIM), lambda b, h, i: (0, 0)),
            pl.BlockSpec((1, DIFF_TQ, LANES), lambda b, h, i: (b, i, qcol + h)),
            pl.BlockSpec((1, seq, LANES), lambda b, h, i: (b, 0, kcol + h)),
            pl.BlockSpec((1, seq, LANES), lambda b, h, i: (b, 0, vcol + h)),
            pl.BlockSpec((1, DIFF_TQ, LANES), lambda b, h, i: (h, 0, 0)),
            pl.BlockSpec((1, 2, DIFF_TK, LANES), lambda b, h, i: (h, 0, 0, 0)),
            pl.BlockSpec((1, halves, DIFF_TQ, DIFF_TK), lambda b, h, i: (h, 0, 0, 0)),
            pl.BlockSpec((1, LANES), lambda b, h, i: (0, 0)),
        ],
        out_specs=pl.BlockSpec((1, DIFF_TQ, LANES), lambda b, h, i: (b, i, h)),
        out_shape=jax.ShapeDtypeStruct((bsz, seq, WIDTH_B), _BF16),
        compiler_params=_params("parallel", "parallel", "arbitrary"),
        name="diff_attn",
    )(slopes2, lam_vecs, proj, proj, proj, qaug, kaug, diag, subln_g.reshape(1, LANES).astype(_F32))


def _layer_norm(z, g, b):
    mu = jnp.mean(z, axis=-1, keepdims=True)
    zc = z - mu
    var = jnp.mean(zc * zc, axis=-1, keepdims=True)
    return zc * lax.rsqrt(var + LN_EPS) * g + b


def _merge_kernel(x_ref, ga_ref, gb_ref, oa_ref, ob_ref, wpa_ref, wpb_ref, wout_ref, bout_ref,
                  g1_ref, b1_ref, wrh_ref, wrl_ref, br_ref,
                  h_ref, idx_ref, gate_ref, rank_ref, cnt_ref, carry_ref, *, alpha):
    step = pl.program_id(0)

    @pl.when(step == 0)
    def _():
        carry_ref[...] = jnp.zeros_like(carry_ref)

    pa = jnp.dot(oa_ref[...], wpa_ref[...], preferred_element_type=_F32)
    pb = jnp.dot(ob_ref[...], wpb_ref[...], preferred_element_type=_F32)
    merged = jax.nn.sigmoid(ga_ref[...].astype(_F32)) * pa + jax.nn.sigmoid(gb_ref[...].astype(_F32)) * pb
    y = jnp.dot(merged.astype(_BF16), wout_ref[...], preferred_element_type=_F32) + bout_ref[...]
    h = _layer_norm(alpha * x_ref[...] + y, g1_ref[...], b1_ref[...])
    h_ref[...] = h

    h_hi = h.astype(_BF16)
    h_lo = (h - h_hi.astype(_F32)).astype(_BF16)
    nt = (((1,), (1,)), ((), ()))
    logits = (lax.dot_general(wrh_ref[...], h_hi, nt, preferred_element_type=_F32)
              + lax.dot_general(wrl_ref[...], h_hi, nt, preferred_element_type=_F32)
              + lax.dot_general(wrh_ref[...], h_lo, nt, preferred_element_type=_F32)
              + br_ref[...])

    tm = logits.shape[1]
    expert = lax.broadcasted_iota(jnp.int32, logits.shape, 0)
    work = logits
    vals, sels, idxs = [], [], []
    for _ in range(TOP_K):
        mx = jnp.max(work, axis=0, keepdims=True)
        idx = jnp.min(jnp.where(work == mx, expert, N_EXPERTS), axis=0, keepdims=True)
        sel = expert == idx
        work = jnp.where(sel, -jnp.inf, work)
        vals.append(mx)
        idxs.append(idx)
        sels.append(sel)
    ex = [jnp.exp(v - vals[0]) for v in vals]
    inv = 1.0 / (ex[0] + ex[1] + ex[2] + ex[3])
    gates = [e * inv for e in ex]

    chosen = sels[0] | sels[1] | sels[2] | sels[3]
    onehot = jnp.where(chosen, 1.0, 0.0).astype(_BF16)
    before = (lax.broadcasted_iota(jnp.int32, (tm, tm), 0)
              < lax.broadcasted_iota(jnp.int32, (tm, tm), 1))
    upper = jnp.where(before, 1.0, 0.0).astype(_BF16)
    prefix = jnp.dot(onehot, upper, preferred_element_type=_F32) + carry_ref[:, 0:1]
    for k in range(TOP_K):
        rank = jnp.sum(jnp.where(sels[k], prefix, 0.0), axis=0, keepdims=True)
        rank_ref[k:k + 1, :] = rank.astype(jnp.int32)
        idx_ref[k:k + 1, :] = idxs[k]
    total = carry_ref[...] + jnp.sum(onehot.astype(_F32), axis=1, keepdims=True)
    carry_ref[...] = total
    cnt_ref[...] = total

    row = lax.broadcasted_iota(jnp.int32, (LANES, tm), 0)
    g_rows = jnp.zeros((LANES, tm), _F32)
    for k in range(TOP_K):
        g_rows = jnp.where(row == k, gates[k], g_rows)
    gate_ref[...] = g_rows.T


def _merge_router(x2d, proj2d, o_a, o_b, wpa, wpb, wout, bout, g1, b1, w_router, b_router, alpha):
    t, dm = x2d.shape
    tm = MERGE_TM
    gcol = (3 * WIDTH_A + 3 * WIDTH_B) // dm
    wr_t = w_router.T.astype(_F32)
    wr_hi = wr_t.astype(_BF16)
    wr_lo = (wr_t - wr_hi.astype(_F32)).astype(_BF16)

    def const(shape):
        return pl.BlockSpec(shape, lambda i: tuple(0 for _ in shape))

    return pl.pallas_call(
        functools.partial(_merge_kernel, alpha=alpha),
        grid=(t // tm,),
        in_specs=[
            pl.BlockSpec((tm, dm), lambda i: (i, 0)),
            pl.BlockSpec((tm, dm), lambda i: (i, gcol)),
            pl.BlockSpec((tm, dm), lambda i: (i, gcol + 1)),
            pl.BlockSpec((tm, WIDTH_A), lambda i: (i, 0)),
            pl.BlockSpec((tm, WIDTH_B), lambda i: (i, 0)),
            const((WIDTH_A, dm)), const((WIDTH_B, dm)), const((dm, dm)), const((1, dm)),
            const((1, dm)), const((1, dm)),
            const((N_EXPERTS, dm)), const((N_EXPERTS, dm)), const((N_EXPERTS, 1)),
        ],
        out_specs=[
            pl.BlockSpec((tm, dm), lambda i: (i, 0)),
            pl.BlockSpec((TOP_K, tm), lambda i: (0, i)),
            pl.BlockSpec((tm, LANES), lambda i: (i, 0)),
            pl.BlockSpec((TOP_K, tm), lambda i: (0, i)),
            pl.BlockSpec((N_EXPERTS, LANES), lambda i: (0, 0)),
        ],
        out_shape=[
            jax.ShapeDtypeStruct((t, dm), _F32),
            jax.ShapeDtypeStruct((TOP_K, t), jnp.int32),
            jax.ShapeDtypeStruct((t, LANES), _F32),
            jax.ShapeDtypeStruct((TOP_K, t), jnp.int32),
            jax.ShapeDtypeStruct((N_EXPERTS, LANES), _F32),
        ],
        scratch_shapes=[pltpu.VMEM((N_EXPERTS, LANES), _F32)],
        compiler_params=_params("arbitrary"),
        name="merge_router",
    )(x2d, proj2d, proj2d, o_a, o_b, wpa, wpb, wout, bout.reshape(1, dm), g1.reshape(1, dm),
      b1.reshape(1, dm), wr_hi, wr_lo, b_router.reshape(N_EXPERTS, 1).astype(_F32))


def _dispatch_kernel(dest_ref, h_ref, xs_in, xs_hbm, sem, *, tm):
    del xs_in

    def issue(t, carry):
        for k in range(TOP_K):
            pltpu.make_async_copy(h_ref.at[pl.ds(t, 1)], xs_hbm.at[pl.ds(dest_ref[k, t], 1)], sem).start(priority=k % 2)
        return carry

    lax.fori_loop(0, tm, issue, 0, unroll=8)
    for _ in range(TOP_K):
        pltpu.make_async_copy(h_ref, xs_hbm.at[pl.ds(0, tm)], sem).wait()


def _dispatch(h, dest, n_rows):
    t, dm = h.shape
    tm = DISPATCH_TM
    zeros = jnp.zeros((n_rows, dm), h.dtype)
    return pl.pallas_call(
        functools.partial(_dispatch_kernel, tm=tm),
        grid=(t // tm,),
        in_specs=[
            pl.BlockSpec((TOP_K, tm), lambda i: (0, i), memory_space=pltpu.SMEM),
            pl.BlockSpec((tm, dm), lambda i: (i, 0)),
            pl.BlockSpec(memory_space=pl.ANY),
        ],
        out_specs=pl.BlockSpec(memory_space=pl.ANY),
        out_shape=jax.ShapeDtypeStruct((n_rows, dm), h.dtype),
        scratch_shapes=[pltpu.SemaphoreType.DMA(())],
        input_output_aliases={2: 0},
        compiler_params=_params("arbitrary"),
        name="moe_dispatch",
    )(dest, h, zeros)


def _expert_kernel(blk_e_ref, blk_src_ref, n_used_ref, xs_ref, wup_ref, bup_ref, wdn_ref, bdn_ref, ys_ref,
                   *, d_expert):
    del blk_e_ref, blk_src_ref

    @pl.when(pl.program_id(0) < n_used_ref[0])
    def _():
        x = xs_ref[...].astype(_BF16)
        acc = jnp.zeros(ys_ref.shape, _F32)
        for c in range(d_expert // FFN_CHUNK):
            lo, hi = c * FFN_CHUNK, (c + 1) * FFN_CHUNK
            g = jnp.dot(x, wup_ref[0, :, lo:hi], preferred_element_type=_F32) + bup_ref[0, :, lo:hi]
            u = (jnp.dot(x, wup_ref[0, :, d_expert + lo:d_expert + hi], preferred_element_type=_F32)
                 + bup_ref[0, :, d_expert + lo:d_expert + hi])
            gate = jnp.minimum(g, SWIGLU_LIMIT)
            up = jnp.clip(u, -SWIGLU_LIMIT, SWIGLU_LIMIT)
            act = gate * jax.nn.sigmoid(SWIGLU_ALPHA * gate) * (up + 1.0)
            acc = acc + jnp.dot(act.astype(_BF16), wdn_ref[0, lo:hi, :], preferred_element_type=_F32)
        ys_ref[...] = acc + bdn_ref[0]

    @pl.when(pl.program_id(0) >= n_used_ref[0])
    def _():
        ys_ref[...] = jnp.zeros_like(ys_ref)


def _experts(xs, blk_e, blk_src, n_used, w_up, b_up, w_down, b_down):
    n_rows, dm = xs.shape
    n_blocks = n_rows // MOE_ROWS
    n_exp, _, two_de = w_up.shape
    d_expert = two_de // 2
    grid_spec = pltpu.PrefetchScalarGridSpec(
        num_scalar_prefetch=3,
        grid=(n_blocks,),
        in_specs=[
            pl.BlockSpec((MOE_ROWS, dm), lambda i, be, bs, nu: (bs[i], 0)),
            pl.BlockSpec((1, dm, two_de), lambda i, be, bs, nu: (be[i], 0, 0)),
            pl.BlockSpec((1, 1, two_de), lambda i, be, bs, nu: (be[i], 0, 0)),
            pl.BlockSpec((1, d_expert, dm), lambda i, be, bs, nu: (be[i], 0, 0)),
            pl.BlockSpec((1, 1, dm), lambda i, be, bs, nu: (be[i], 0, 0)),
        ],
        out_specs=pl.BlockSpec((MOE_ROWS, dm), lambda i, be, bs, nu: (i, 0)),
    )
    return pl.pallas_call(
        functools.partial(_expert_kernel, d_expert=d_expert),
        grid_spec=grid_spec,
        out_shape=jax.ShapeDtypeStruct((n_rows, dm), _F32),
        compiler_params=_params("arbitrary"),
        name="moe_experts",
    )(blk_e, blk_src, n_used, xs, w_up, b_up.reshape(n_exp, 1, two_de), w_down, b_down.reshape(n_exp, 1, dm))


def _combine_kernel(dest_ref, h_ref, gate_ref, g2_ref, b2_ref, ys_hbm, o_ref, ybuf, sem, *, tm, alpha):
    def issue(t, carry):
        for k in range(TOP_K):
            pltpu.make_async_copy(ys_hbm.at[pl.ds(dest_ref[k, t], 1)], ybuf.at[k, pl.ds(t, 1)], sem).start(priority=k % 2)
        return carry

    lax.fori_loop(0, tm, issue, 0, unroll=8)
    for k in range(TOP_K):
        pltpu.make_async_copy(ys_hbm.at[pl.ds(0, tm)], ybuf.at[k], sem).wait()

    gates = gate_ref[...]
    y = gates[:, 0:1] * ybuf[0]
    for k in range(1, TOP_K):
        y = y + gates[:, k:k + 1] * ybuf[k]
    o_ref[...] = _layer_norm(alpha * h_ref[...] + y, g2_ref[...], b2_ref[...])


def _combine(h, gates_col, dest, ys, g2, b2, alpha):
    t, dm = h.shape
    tm = COMBINE_TM
    return pl.pallas_call(
        functools.partial(_combine_kernel, tm=tm, alpha=alpha),
        grid=(t // tm,),
        in_specs=[
            pl.BlockSpec((TOP_K, tm), lambda i: (0, i), memory_space=pltpu.SMEM),
            pl.BlockSpec((tm, dm), lambda i: (i, 0)),
            pl.BlockSpec((tm, LANES), lambda i: (i, 0)),
            pl.BlockSpec((1, dm), lambda i: (0, 0)),
            pl.BlockSpec((1, dm), lambda i: (0, 0)),
            pl.BlockSpec(memory_space=pl.ANY),
        ],
        out_specs=pl.BlockSpec((tm, dm), lambda i: (i, 0)),
        out_shape=jax.ShapeDtypeStruct((t, dm), _F32),
        scratch_shapes=[pltpu.VMEM((TOP_K, tm, dm), _F32), pltpu.SemaphoreType.DMA(())],
        compiler_params=_params("arbitrary"),
        name="moe_combine",
    )(dest, h, gates_col, g2.reshape(1, dm), b2.reshape(1, dm), ys)


def _moe_plan(idx, rank, counts, n_assign):
    counts = counts.astype(jnp.int32)
    padded = (counts + MOE_ROWS - 1) // MOE_ROWS * MOE_ROWS
    pend = jnp.cumsum(padded)
    pstart = pend - padded
    onehot = idx[..., None] == jnp.arange(N_EXPERTS, dtype=jnp.int32)
    dest = rank + jnp.sum(jnp.where(onehot, pstart, 0), axis=-1)
    n_blocks = -(-n_assign // MOE_ROWS) + N_EXPERTS
    blk_start = jnp.arange(n_blocks, dtype=jnp.int32) * MOE_ROWS
    blk_e = jnp.sum((pend[None, :] <= blk_start[:, None]).astype(jnp.int32), axis=1)
    blk_e = jnp.minimum(blk_e, N_EXPERTS - 1)
    n_used = (pend[-1] // MOE_ROWS).astype(jnp.int32)
    blk_src = jnp.minimum(jnp.arange(n_blocks, dtype=jnp.int32), n_used - 1)
    blk_e = blk_e[blk_src]
    return dest.astype(jnp.int32), blk_e, blk_src, n_used.reshape(1), n_blocks * MOE_ROWS


def kernel(x, w_in, b_in, lambda_q1, lambda_k1, lambda_q2, lambda_k2, subln_g, w_proj_a, w_proj_b, w_out, b_out, ln1_g, ln1_b, w_router, b_router, w_up, b_up, w_down, b_down, ln2_g, ln2_b):
    bsz, seq, dm = x.shape
    depth = w_in.shape[0]
    alpha = (2.0 * depth) ** 0.25
    t = bsz * seq
    for layer in range(depth):
        lambda_init = 0.8 - 0.6 * math.exp(-0.3 * layer)
        x2d = x.reshape(t, dm)
        proj2d = _in_proj(x2d, w_in[layer].astype(_BF16), b_in[layer])
        proj = proj2d.reshape(bsz, seq, -1)

        state = None
        for n, (_, dilation) in enumerate(DILATED_PATTERNS):
            state = _dilated_pass(proj, dilation, state, final=(n == len(DILATED_PATTERNS) - 1))
        o_a = state.reshape(t, WIDTH_A)

        lam_vecs = jnp.stack([lambda_q1[layer], lambda_k1[layer], lambda_q2[layer], lambda_k2[layer]]).astype(_F32)
        o_b = _diff_attention(proj, lam_vecs, subln_g[layer], lambda_init).reshape(t, WIDTH_B)

        h, idx, gates_col, rank, cnt = _merge_router(
            x2d, proj2d, o_a, o_b, w_proj_a[layer].astype(_BF16), w_proj_b[layer].astype(_BF16),
            w_out[layer].astype(_BF16), b_out[layer], ln1_g[layer], ln1_b[layer],
            w_router[layer], b_router[layer], alpha)

        dest, blk_e, blk_src, n_used, n_rows = _moe_plan(idx, rank, cnt[:, 0], t * TOP_K)
        xs = _dispatch(h, dest, n_rows)
        ys = _experts(xs, blk_e, blk_src, n_used, w_up[layer].astype(_BF16), b_up[layer],
                      w_down[layer].astype(_BF16), b_down[layer])
        out = _combine(h, gates_col, dest, ys, ln2_g[layer], ln2_b[layer], alpha)
        x = out.reshape(bsz, seq, dm)
    return x
```

```python
import functools
import math

import numpy as np
import jax
import jax.numpy as jnp
from jax import lax
from jax.experimental import pallas as pl
from jax.experimental.pallas import tpu as pltpu

HEAD_DIM = 64
N_HEADS_A = 8
DILATED_PATTERNS = ((128, 1), (512, 4), (2048, 16))
N_HEADS_B = 4
WIDTH_A = N_HEADS_A * HEAD_DIM
WIDTH_B = N_HEADS_B * 2 * HEAD_DIM
N_ALIBI_HEADS = N_HEADS_A + N_HEADS_B
ALIBI_IDX_A = (0, 1, 3, 4, 6, 7, 9, 10)
ALIBI_IDX_B = (2, 5, 8, 11)
Q_BLOCK = 128
MASK_VALUE = -1e30
N_EXPERTS = 32
TOP_K = 4
SWIGLU_ALPHA = 1.702
SWIGLU_LIMIT = 7.0
LN_EPS = 1e-5
SUBLN_EPS = 1e-5
LOG2E = math.log2(math.e)
QUERY_SCALE = HEAD_DIM ** -0.5 * LOG2E

LANES = 128
V7X_VMEM_LIMIT_BYTES = 56 * 1024 * 1024

PROJ_TM = 1024
PROJ_A_TM = 512
DIL_ROWS = 2 * Q_BLOCK
DIL_HALO = 64
DIFF_TQ = 512
DIFF_TK = 512
MERGE_TM = 512
MOE_ROWS = 512
FFN_CHUNK = 512
COMBINE_TM = 256
DISPATCH_TM = 512

_F32 = jnp.float32
_BF16 = jnp.bfloat16


def _params(*sem):
    return pltpu.CompilerParams(dimension_semantics=sem, vmem_limit_bytes=V7X_VMEM_LIMIT_BYTES)


def _alibi_slopes():
    return (2.0 ** (-8.0 * np.arange(1, N_ALIBI_HEADS + 1) / N_ALIBI_HEADS)).astype(np.float32)


def _in_proj_kernel(x_ref, w_ref, b_ref, cs_ref, o_ref):
    x = x_ref[...].astype(_BF16)
    acc = jnp.dot(x, w_ref[...], preferred_element_type=_F32)
    o_ref[...] = ((acc + b_ref[...]) * cs_ref[...]).astype(o_ref.dtype)


def _in_proj(x2d, w_bf16, b, colscale):
    t, dm = x2d.shape
    n = w_bf16.shape[1]
    tn = n // 2
    assert n % 2 == 0 and tn % LANES == 0
    return pl.pallas_call(
        _in_proj_kernel,
        grid=(t // PROJ_TM, n // tn),
        in_specs=[
            pl.BlockSpec((PROJ_TM, dm), lambda i, j: (i, 0)),
            pl.BlockSpec((dm, tn), lambda i, j: (0, j)),
            pl.BlockSpec((1, tn), lambda i, j: (0, j)),
            pl.BlockSpec((1, tn), lambda i, j: (0, j)),
        ],
        out_specs=pl.BlockSpec((PROJ_TM, tn), lambda i, j: (i, j)),
        out_shape=jax.ShapeDtypeStruct((t, n), _BF16),
        compiler_params=_params("parallel", "arbitrary"),
        name="in_proj",
    )(x2d, w_bf16, b.reshape(1, n), colscale.reshape(1, n))


def _in_proj_a_kernel(x_ref, w_ref, b_ref, cs_ref, *refs):
    out_refs, acc_ref = refs[:-1], refs[-1]
    x = x_ref[0].astype(_BF16)
    acc = (jnp.dot(x, w_ref[...], preferred_element_type=_F32) + b_ref[...]) * cs_ref[...]
    n_chunks, tm, _ = acc_ref.shape
    for c in range(n_chunks):
        acc_ref[c] = acc[:, c * LANES:(c + 1) * LANES]
    for o_ref, (_, d) in zip(out_refs, DILATED_PATTERNS):
        if d == 1:
            o_ref[0, 0] = acc.astype(o_ref.dtype)
            continue
        for r in range(d):
            for c in range(n_chunks):
                rows = acc_ref[c, pl.ds(r, tm // d, stride=d), :]
                o_ref[0, r, :, c * LANES:(c + 1) * LANES] = rows.astype(o_ref.dtype)


def _in_proj_a(x, w_bf16, b, colscale):
    bsz, seq, dm = x.shape
    n = w_bf16.shape[1]
    tm = PROJ_A_TM
    per_b = seq // tm
    dils = [d for _, d in DILATED_PATTERNS]
    assert seq % tm == 0 and all(tm % d == 0 and (tm // d) % 16 == 0 for d in dils)
    return pl.pallas_call(
        _in_proj_a_kernel,
        grid=(bsz * per_b,),
        in_specs=[
            pl.BlockSpec((1, tm, dm), lambda i: (i // per_b, i % per_b, 0)),
            pl.BlockSpec((dm, n), lambda i: (0, 0)),
            pl.BlockSpec((1, n), lambda i: (0, 0)),
            pl.BlockSpec((1, n), lambda i: (0, 0)),
        ],
        out_specs=[pl.BlockSpec((1, d, tm // d, n), lambda i: (i // per_b, 0, i % per_b, 0)) for d in dils],
        out_shape=[jax.ShapeDtypeStruct((bsz, d, seq // d, n), _BF16) for d in dils],
        scratch_shapes=[pltpu.VMEM((n // LANES, tm, LANES), _F32)],
        compiler_params=_params("parallel"),
        name="in_proj_a",
    )(x, w_bf16, b.reshape(1, n), colscale.reshape(1, n))


def _dilated_bias_table(dilation):
    slopes = _alibi_slopes()[list(ALIBI_IDX_A)]
    band = Q_BLOCK + 2 * DIL_HALO
    qi = np.arange(Q_BLOCK)[:, None]
    kj = np.arange(band)[None, :]
    rel = qi - kj + DIL_HALO
    in_band = np.abs(rel) <= DIL_HALO
    base = -slopes[:, None, None] * (dilation * np.abs(rel)).astype(np.float32)[None]
    base = (base.astype(np.float64) * LOG2E).astype(np.float32)
    edge = (np.ones_like(kj, bool), kj >= DIL_HALO, kj < band - DIL_HALO)
    out = np.stack([np.where(in_band & e, base, np.float32(MASK_VALUE)) for e in edge])
    return out.astype(np.float32)


def _dilated_kernel(q_ref, km_ref, kp_ref, kn_ref, vm_ref, vp_ref, vn_ref, bias_ref, o_ref, lse_ref,
                    kbuf, vbuf, *, nblk):
    i = pl.program_id(2)
    h0, h1 = DIL_HALO, DIL_HALO + DIL_ROWS
    kbuf[0:h0, :] = kp_ref[0, 0]
    kbuf[h0:h1, :] = km_ref[0, 0]
    kbuf[h1:h1 + DIL_HALO, :] = kn_ref[0, 0]
    vbuf[0:h0, :] = vp_ref[0, 0]
    vbuf[h0:h1, :] = vm_ref[0, 0]
    vbuf[h1:h1 + DIL_HALO, :] = vn_ref[0, 0]

    lane = lax.broadcasted_iota(jnp.int32, (Q_BLOCK, LANES), 1)
    low_half = lane < HEAD_DIM
    band = Q_BLOCK + 2 * DIL_HALO
    variants = (jnp.where(i == 0, 1, 0), jnp.where(i == nblk - 1, 2, 0))

    for j in range(DIL_ROWS // Q_BLOCK):
        rows = slice(j * Q_BLOCK, (j + 1) * Q_BLOCK)
        krows = slice(j * Q_BLOCK, j * Q_BLOCK + band)
        lse_cols = []
        for hp in range(N_HEADS_A // 2):
            cols = slice(hp * LANES, (hp + 1) * LANES)
            q_pair = q_ref[0, 0, rows, cols]
            k_pair = kbuf[krows, cols]
            v_pair = vbuf[krows, cols]
            halves = []
            for hh in range(2):
                head = 2 * hp + hh
                keep = low_half if hh == 0 else jnp.logical_not(low_half)
                qm = jnp.where(keep, q_pair, jnp.zeros_like(q_pair))
                s = lax.dot_general(qm, k_pair, (((1,), (1,)), ((), ())), preferred_element_type=_F32)
                s = s + bias_ref[variants[j], head]
                m = jnp.max(s, axis=-1, keepdims=True)
                p = jnp.exp2(s - m)
                z = jnp.sum(p, axis=-1, keepdims=True)
                halves.append(jnp.dot(p.astype(_BF16), v_pair, preferred_element_type=_F32) * (1.0 / z))
                lse_cols.append(m + jnp.log2(z))
            o_ref[0, 0, rows, cols] = jnp.where(low_half, halves[0], halves[1])
        tile = jnp.zeros((Q_BLOCK, LANES), _F32)
        for head in range(N_HEADS_A):
            tile = jnp.where(lane == head, lse_cols[head], tile)
        lse_ref[0, 0, rows, :] = tile


def _dilated_pass(qkv, dilation):
    bsz, d, L, _ = qkv.shape
    assert d == dilation and L % DIL_ROWS == 0 and DIL_ROWS % DIL_HALO == 0
    nblk = L // DIL_ROWS
    per = DIL_ROWS // DIL_HALO
    n_halo = L // DIL_HALO
    bias = jnp.asarray(_dilated_bias_table(dilation))

    def main(s):
        return pl.BlockSpec((1, 1, DIL_ROWS, WIDTH_A), lambda b, r, i: (b, r, i, s))

    def left(s):
        return pl.BlockSpec((1, 1, DIL_HALO, WIDTH_A), lambda b, r, i: (b, r, jnp.maximum(per * i - 1, 0), s))

    def right(s):
        return pl.BlockSpec((1, 1, DIL_HALO, WIDTH_A),
                            lambda b, r, i: (b, r, jnp.minimum(per * i + per, n_halo - 1), s))

    rows_buf = DIL_ROWS + 2 * DIL_HALO
    return pl.pallas_call(
        functools.partial(_dilated_kernel, nblk=nblk),
        grid=(bsz, dilation, nblk),
        in_specs=[main(0), main(1), left(1), right(1), main(2), left(2), right(2),
                  pl.BlockSpec(bias.shape, lambda b, r, i: (0, 0, 0, 0))],
        out_specs=[pl.BlockSpec((1, 1, DIL_ROWS, WIDTH_A), lambda b, r, i: (b, r, i, 0)),
                   pl.BlockSpec((1, 1, DIL_ROWS, LANES), lambda b, r, i: (b, r, i, 0))],
        out_shape=[jax.ShapeDtypeStruct((bsz, dilation, L, WIDTH_A), _F32),
                   jax.ShapeDtypeStruct((bsz, dilation, L, LANES), _F32)],
        scratch_shapes=[pltpu.VMEM((rows_buf, WIDTH_A), _BF16), pltpu.VMEM((rows_buf, WIDTH_A), _BF16)],
        compiler_params=_params("parallel", "parallel", "arbitrary"),
        name=f"dilated_d{dilation}",
    )(qkv, qkv, qkv, qkv, qkv, qkv, qkv, bias)


def _split_bf16(x):
    hi = x.astype(jnp.bfloat16)
    lo = (x - hi.astype(np.float64)).astype(jnp.bfloat16)
    return hi, lo


def _diff_tables():
    s2 = _alibi_slopes()[list(ALIBI_IDX_B)].astype(np.float64) * LOG2E
    qaug = np.zeros((N_HEADS_B, DIFF_TQ, LANES), jnp.bfloat16)
    kaug = np.zeros((N_HEADS_B, 2, DIFF_TK, LANES), jnp.bfloat16)
    i = np.arange(DIFF_TQ, dtype=np.float64)
    j = np.arange(DIFF_TK, dtype=np.float64)
    for h in range(N_HEADS_B):
        r_hi, r_lo = _split_bf16(-s2[h] * i)
        qaug[h, :, 0] = 1.0
        qaug[h, :, 1] = 1.0
        qaug[h, :, 2] = r_hi
        qaug[h, :, 3] = r_lo
        for side, sign in enumerate((1.0, -1.0)):
            c_hi, c_lo = _split_bf16(sign * s2[h] * j)
            kaug[h, side, :, 0] = c_hi
            kaug[h, side, :, 1] = c_lo
            kaug[h, side, :, 2] = sign
            kaug[h, side, :, 3] = sign
    halves = DIFF_TK // DIFF_TQ
    off = (np.arange(halves) * DIFF_TQ)[:, None, None]
    dist = np.abs(off + i[None, :, None] - j[None, None, :])
    diag = (-s2[:, None, None, None] * dist[None]).astype(np.float32)
    return jnp.asarray(qaug), jnp.asarray(kaug), jnp.asarray(diag), jnp.asarray(s2.astype(np.float32))


def _diff_kernel(slope_ref, lam_ref, q_ref, k_ref, v_ref, qaug_ref, kaug_ref, diag_ref, g_ref, o_ref,
                 *, seq, lambda_init):
    head = pl.program_id(1)
    qb = pl.program_id(2)
    q0 = qb * DIFF_TQ
    halves = DIFF_TK // DIFF_TQ
    kd = qb // halves
    n_kb = seq // DIFF_TK
    slope2 = slope_ref[head]
    nt = (((1,), (1,)), ((), ()))

    q = q_ref[0]
    lane = lax.broadcasted_iota(jnp.int32, q.shape, 1)
    zero = jnp.zeros_like(q)
    qaug = qaug_ref[0]
    q_maps = (jnp.concatenate([jnp.where(lane < HEAD_DIM, q, zero), qaug], axis=1),
              jnp.concatenate([jnp.where(lane >= HEAD_DIM, q, zero), qaug], axis=1))
    zero_aug = jnp.zeros((DIFF_TK, LANES), _BF16)
    vlane = lax.broadcasted_iota(jnp.int32, (DIFF_TK, LANES), 1)
    ones_col = jnp.where(vlane == 0, 1.0, 0.0).astype(_BF16)

    def load(kb):
        k0 = pl.multiple_of(kb * DIFF_TK, DIFF_TK)
        return k_ref[0, pl.ds(k0, DIFF_TK), :], jnp.concatenate([v_ref[0, pl.ds(k0, DIFF_TK), :], ones_col], axis=1)

    def split(pv):
        return pv[:, LANES:LANES + 1], pv[:, 0:LANES]

    def update(s, const, vblk, carry):
        m_old, l_old, acc = carry
        m_new = jnp.maximum(m_old, jnp.max(s, axis=-1, keepdims=True) + const)
        alpha = jnp.exp2(m_old - m_new)
        p = jnp.exp2(s - (m_new - const))
        l_blk, o_blk = split(jnp.dot(p.astype(_BF16), vblk, preferred_element_type=_F32))
        return m_new, alpha * l_old + l_blk, alpha * acc + o_blk

    kblk, vblk = load(kd)
    k_all = jnp.concatenate([kblk, zero_aug], axis=1)
    bias = diag_ref[0, qb % halves]
    carries = []
    for mp in range(2):
        s = lax.dot_general(q_maps[mp], k_all, nt, preferred_element_type=_F32) + bias
        m = jnp.max(s, axis=-1, keepdims=True)
        p = jnp.exp2(s - m)
        carries.append((m,) + split(jnp.dot(p.astype(_BF16), vblk, preferred_element_type=_F32)))

    for t in range(n_kb - 1):
        kb = jnp.where(t >= kd, t + 1, t)
        side = jnp.where(t >= kd, 1, 0)
        kblk, vblk = load(kb)
        k_all = jnp.concatenate([kblk, kaug_ref[0, side]], axis=1)
        const = -slope2 * jnp.abs(q0 - kb * DIFF_TK).astype(_F32)
        for mp in range(2):
            s = lax.dot_general(q_maps[mp], k_all, nt, preferred_element_type=_F32)
            carries[mp] = update(s, const, vblk, carries[mp])

    lv = lam_ref[...]
    lam = (jnp.exp(jnp.sum(lv[0:1] * lv[1:2], axis=-1, keepdims=True))
           - jnp.exp(jnp.sum(lv[2:3] * lv[3:4], axis=-1, keepdims=True)) + lambda_init)
    (_, l1, a1), (_, l2, a2) = carries
    of = a1 * (1.0 / l1) - lam * (a2 * (1.0 / l2))
    of = of * lax.rsqrt(jnp.mean(of * of, axis=-1, keepdims=True) + SUBLN_EPS)
    of = of * g_ref[...] * (1.0 - lambda_init)
    o_ref[0] = of.astype(o_ref.dtype)


def _diff_attention(proj, q_offset, lam_vecs, subln_g, lambda_init):
    bsz, seq, _ = proj.shape
    assert seq % DIFF_TQ == 0 and seq % DIFF_TK == 0 and q_offset % LANES == 0
    qcol = q_offset // LANES
    kcol = qcol + WIDTH_B // LANES
    vcol = kcol + WIDTH_B // LANES
    assert DIFF_TK % DIFF_TQ == 0
    halves = DIFF_TK // DIFF_TQ
    qaug, kaug, diag, slopes2 = _diff_tables()
    return pl.pallas_call(
        functools.partial(_diff_kernel, seq=seq, lambda_init=lambda_init),
        grid=(bsz, N_HEADS_B, seq // DIFF_TQ),
        in_specs=[
            pl.BlockSpec(memory_space=pltpu.SMEM),
            pl.BlockSpec((4, HEAD_DIM), lambda b, h, i: (0, 0)),
            pl.BlockSpec((1, DIFF_TQ, LANES), lambda b, h, i: (b, i, qcol + h)),
            pl.BlockSpec((1, seq, LANES), lambda b, h, i: (b, 0, kcol + h)),
            pl.BlockSpec((1, seq, LANES), lambda b, h, i: (b, 0, vcol + h)),
            pl.BlockSpec((1, DIFF_TQ, LANES), lambda b, h, i: (h, 0, 0)),
            pl.BlockSpec((1, 2, DIFF_TK, LANES), lambda b, h, i: (h, 0, 0, 0)),
            pl.BlockSpec((1, halves, DIFF_TQ, DIFF_TK), lambda b, h, i: (h, 0, 0, 0)),
            pl.BlockSpec((1, LANES), lambda b, h, i: (0, 0)),
        ],
        out_specs=pl.BlockSpec((1, DIFF_TQ, LANES), lambda b, h, i: (b, i, h)),
        out_shape=jax.ShapeDtypeStruct((bsz, seq, WIDTH_B), _BF16),
        compiler_params=_params("parallel", "parallel", "arbitrary"),
        name="diff_attn",
    )(slopes2, lam_vecs, proj, proj, proj, qaug, kaug, diag, subln_g.reshape(1, LANES).astype(_F32))


def _layer_norm(z, g, b):
    mu = jnp.mean(z, axis=-1, keepdims=True)
    zc = z - mu
    var = jnp.mean(zc * zc, axis=-1, keepdims=True)
    return zc * lax.rsqrt(var + LN_EPS) * g + b


def _mix_patterns(o_refs, lse_refs, expand_ref, o_nat, lse_nat):
    n_chunks, tm = o_nat.shape[1], o_nat.shape[2]
    outs, lses = [], []
    for p, (_, d) in enumerate(DILATED_PATTERNS):
        if d == 1:
            outs.append(o_refs[p][0, 0])
            lses.append(lse_refs[p][0, 0])
            continue
        for r in range(d):
            for c in range(n_chunks):
                o_nat[p, c, pl.ds(r, tm // d, stride=d), :] = o_refs[p][0, r, :, c * LANES:(c + 1) * LANES]
            lse_nat[p, pl.ds(r, tm // d, stride=d), :] = lse_refs[p][0, r]
        outs.append(jnp.concatenate([o_nat[p, c] for c in range(n_chunks)], axis=1))
        lses.append(lse_nat[p])
    top = functools.reduce(jnp.maximum, lses)
    es = [jnp.exp2(l - top) for l in lses]
    inv = 1.0 / functools.reduce(lambda a, b: a + b, es)
    mixed = None
    for o, e in zip(outs, es):
        w = e * inv
        w_hi = w.astype(_BF16)
        w_lo = (w - w_hi.astype(_F32)).astype(_BF16)
        w_full = (jnp.dot(w_hi, expand_ref[...], preferred_element_type=_F32)
                  + jnp.dot(w_lo, expand_ref[...], preferred_element_type=_F32))
        mixed = w_full * o if mixed is None else mixed + w_full * o
    return mixed


def _merge_kernel(x_ref, ga_ref, gb_ref, o1_ref, o2_ref, o3_ref, l1_ref, l2_ref, l3_ref, expand_ref, ob_ref,
                  wpa_ref, wpb_ref, wout_ref, bout_ref, g1_ref, b1_ref, wrh_ref, wrl_ref, br_ref,
                  h_ref, idx_ref, gate_ref, rank_ref, cnt_ref, carry_ref, o_nat, lse_nat, *, alpha):
    step = pl.program_id(0)

    @pl.when(step == 0)
    def _():
        carry_ref[...] = jnp.zeros_like(carry_ref)

    o_a = _mix_patterns((o1_ref, o2_ref, o3_ref), (l1_ref, l2_ref, l3_ref), expand_ref, o_nat, lse_nat)
    pa = jnp.dot(o_a.astype(_BF16), wpa_ref[...], preferred_element_type=_F32)
    pb = jnp.dot(ob_ref[...], wpb_ref[...], preferred_element_type=_F32)
    merged = jax.nn.sigmoid(ga_ref[...].astype(_F32)) * pa + jax.nn.sigmoid(gb_ref[...].astype(_F32)) * pb
    y = jnp.dot(merged.astype(_BF16), wout_ref[...], preferred_element_type=_F32) + bout_ref[...]
    h = _layer_norm(alpha * x_ref[...] + y, g1_ref[...], b1_ref[...])
    h_ref[...] = h

    h_hi = h.astype(_BF16)
    h_lo = (h - h_hi.astype(_F32)).astype(_BF16)
    nt = (((1,), (1,)), ((), ()))
    logits = (lax.dot_general(wrh_ref[...], h_hi, nt, preferred_element_type=_F32)
              + lax.dot_general(wrl_ref[...], h_hi, nt, preferred_element_type=_F32)
              + lax.dot_general(wrh_ref[...], h_lo, nt, preferred_element_type=_F32)
              + br_ref[...])

    tm = logits.shape[1]
    expert = lax.broadcasted_iota(jnp.int32, logits.shape, 0)
    work = logits
    vals, sels, idxs = [], [], []
    for _ in range(TOP_K):
        mx = jnp.max(work, axis=0, keepdims=True)
        idx = jnp.min(jnp.where(work == mx, expert, N_EXPERTS), axis=0, keepdims=True)
        sel = expert == idx
        work = jnp.where(sel, -jnp.inf, work)
        vals.append(mx)
        idxs.append(idx)
        sels.append(sel)
    ex = [jnp.exp(v - vals[0]) for v in vals]
    inv = 1.0 / (ex[0] + ex[1] + ex[2] + ex[3])
    gates = [e * inv for e in ex]

    chosen = sels[0] | sels[1] | sels[2] | sels[3]
    onehot = jnp.where(chosen, 1.0, 0.0).astype(_BF16)
    before = (lax.broadcasted_iota(jnp.int32, (tm, tm), 0)
              < lax.broadcasted_iota(jnp.int32, (tm, tm), 1))
    upper = jnp.where(before, 1.0, 0.0).astype(_BF16)
    prefix = jnp.dot(onehot, upper, preferred_element_type=_F32) + carry_ref[:, 0:1]
    for k in range(TOP_K):
        rank = jnp.sum(jnp.where(sels[k], prefix, 0.0), axis=0, keepdims=True)
        rank_ref[k:k + 1, :] = rank.astype(jnp.int32)
        idx_ref[k:k + 1, :] = idxs[k]
    total = carry_ref[...] + jnp.sum(onehot.astype(_F32), axis=1, keepdims=True)
    carry_ref[...] = total
    cnt_ref[...] = total

    row = lax.broadcasted_iota(jnp.int32, (LANES, tm), 0)
    g_rows = jnp.zeros((LANES, tm), _F32)
    for k in range(TOP_K):
        g_rows = jnp.where(row == k, gates[k], g_rows)
    gate_ref[...] = g_rows.T


def _merge_router(x2d, proj2d, gate_offset, dil_outs, dil_lses, o_b, wpa, wpb, wout, bout, g1, b1,
                  w_router, b_router, alpha):
    t, dm = x2d.shape
    tm = MERGE_TM
    bsz, _, seq_over_d0, _ = dil_outs[0].shape
    seq = seq_over_d0 * DILATED_PATTERNS[0][1]
    per_b = seq // tm
    dils = [d for _, d in DILATED_PATTERNS]
    assert gate_offset % dm == 0 and seq % tm == 0 and all(tm % d == 0 for d in dils)
    gcol = gate_offset // dm
    wr_t = w_router.T.astype(_F32)
    wr_hi = wr_t.astype(_BF16)
    wr_lo = (wr_t - wr_hi.astype(_F32)).astype(_BF16)
    expand = np.zeros((LANES, WIDTH_A), np.float32)
    for head in range(N_HEADS_A):
        expand[head, head * HEAD_DIM:(head + 1) * HEAD_DIM] = 1.0
    expand = jnp.asarray(expand, _BF16)

    def const(shape):
        return pl.BlockSpec(shape, lambda i: tuple(0 for _ in shape))

    def residue_major(d, width):
        return pl.BlockSpec((1, d, tm // d, width), lambda i: (i // per_b, 0, i % per_b, 0))

    return pl.pallas_call(
        functools.partial(_merge_kernel, alpha=alpha),
        grid=(t // tm,),
        in_specs=[
            pl.BlockSpec((tm, dm), lambda i: (i, 0)),
            pl.BlockSpec((tm, dm), lambda i: (i, gcol)),
            pl.BlockSpec((tm, dm), lambda i: (i, gcol + 1)),
            *[residue_major(d, WIDTH_A) for d in dils],
            *[residue_major(d, LANES) for d in dils],
            const((LANES, WIDTH_A)),
            pl.BlockSpec((tm, WIDTH_B), lambda i: (i, 0)),
            const((WIDTH_A, dm)), const((WIDTH_B, dm)), const((dm, dm)), const((1, dm)),
            const((1, dm)), const((1, dm)),
            const((N_EXPERTS, dm)), const((N_EXPERTS, dm)), const((N_EXPERTS, 1)),
        ],
        out_specs=[
            pl.BlockSpec((tm, dm), lambda i: (i, 0)),
            pl.BlockSpec((TOP_K, tm), lambda i: (0, i)),
            pl.BlockSpec((tm, LANES), lambda i: (i, 0)),
            pl.BlockSpec((TOP_K, tm), lambda i: (0, i)),
            pl.BlockSpec((N_EXPERTS, LANES), lambda i: (0, 0)),
        ],
        out_shape=[
            jax.ShapeDtypeStruct((t, dm), _F32),
            jax.ShapeDtypeStruct((TOP_K, t), jnp.int32),
            jax.ShapeDtypeStruct((t, LANES), _F32),
            jax.ShapeDtypeStruct((TOP_K, t), jnp.int32),
            jax.ShapeDtypeStruct((N_EXPERTS, LANES), _F32),
        ],
        scratch_shapes=[pltpu.VMEM((N_EXPERTS, LANES), _F32),
                        pltpu.VMEM((len(dils), WIDTH_A // LANES, tm, LANES), _F32),
                        pltpu.VMEM((len(dils), tm, LANES), _F32)],
        compiler_params=_params("arbitrary"),
        name="merge_router",
    )(x2d, proj2d, proj2d, *dil_outs, *dil_lses, expand, o_b, wpa, wpb, wout, bout.reshape(1, dm),
      g1.reshape(1, dm), b1.reshape(1, dm), wr_hi, wr_lo, b_router.reshape(N_EXPERTS, 1).astype(_F32))


def _dispatch_kernel(dest_ref, h_ref, xs_in, xs_hbm, sem, *, tm):
    del xs_in

    def issue(t, carry):
        for k in range(TOP_K):
            pltpu.make_async_copy(h_ref.at[pl.ds(t, 1)], xs_hbm.at[pl.ds(dest_ref[k, t], 1)], sem).start(priority=k % 2)
        return carry

    lax.fori_loop(0, tm, issue, 0, unroll=8)
    for _ in range(TOP_K):
        pltpu.make_async_copy(h_ref, xs_hbm.at[pl.ds(0, tm)], sem).wait()


def _dispatch(h, dest, n_rows):
    t, dm = h.shape
    tm = DISPATCH_TM
    zeros = jnp.zeros((n_rows, dm), h.dtype)
    return pl.pallas_call(
        functools.partial(_dispatch_kernel, tm=tm),
        grid=(t // tm,),
        in_specs=[
            pl.BlockSpec((TOP_K, tm), lambda i: (0, i), memory_space=pltpu.SMEM),
            pl.BlockSpec((tm, dm), lambda i: (i, 0)),
            pl.BlockSpec(memory_space=pl.ANY),
        ],
        out_specs=pl.BlockSpec(memory_space=pl.ANY),
        out_shape=jax.ShapeDtypeStruct((n_rows, dm), h.dtype),
        scratch_shapes=[pltpu.SemaphoreType.DMA(())],
        input_output_aliases={2: 0},
        compiler_params=_params("arbitrary"),
        name="moe_dispatch",
    )(dest, h, zeros)


def _expert_kernel(blk_e_ref, blk_src_ref, n_used_ref, xs_ref, wup_ref, bup_ref, wdn_ref, bdn_ref, ys_ref,
                   *, d_expert):
    del blk_e_ref, blk_src_ref

    @pl.when(pl.program_id(0) < n_used_ref[0])
    def _():
        x = xs_ref[...].astype(_BF16)
        acc = jnp.zeros(ys_ref.shape, _F32)
        for c in range(d_expert // FFN_CHUNK):
            lo, hi = c * FFN_CHUNK, (c + 1) * FFN_CHUNK
            g = jnp.dot(x, wup_ref[0, :, lo:hi], preferred_element_type=_F32) + bup_ref[0, :, lo:hi]
            u = (jnp.dot(x, wup_ref[0, :, d_expert + lo:d_expert + hi], preferred_element_type=_F32)
                 + bup_ref[0, :, d_expert + lo:d_expert + hi])
            gate = jnp.minimum(g, SWIGLU_LIMIT)
            up = jnp.clip(u, -SWIGLU_LIMIT, SWIGLU_LIMIT)
            act = gate * jax.nn.sigmoid(SWIGLU_ALPHA * gate) * (up + 1.0)
            acc = acc + jnp.dot(act.astype(_BF16), wdn_ref[0, lo:hi, :], preferred_element_type=_F32)
        ys_ref[...] = acc + bdn_ref[0]

    @pl.when(pl.program_id(0) >= n_used_ref[0])
    def _():
        ys_ref[...] = jnp.zeros_like(ys_ref)


def _experts(xs, blk_e, blk_src, n_used, w_up, b_up, w_down, b_down):
    n_rows, dm = xs.shape
    n_blocks = n_rows // MOE_ROWS
    n_exp, _, two_de = w_up.shape
    d_expert = two_de // 2
    grid_spec = pltpu.PrefetchScalarGridSpec(
        num_scalar_prefetch=3,
        grid=(n_blocks,),
        in_specs=[
            pl.BlockSpec((MOE_ROWS, dm), lambda i, be, bs, nu: (bs[i], 0)),
            pl.BlockSpec((1, dm, two_de), lambda i, be, bs, nu: (be[i], 0, 0)),
            pl.BlockSpec((1, 1, two_de), lambda i, be, bs, nu: (be[i], 0, 0)),
            pl.BlockSpec((1, d_expert, dm), lambda i, be, bs, nu: (be[i], 0, 0)),
            pl.BlockSpec((1, 1, dm), lambda i, be, bs, nu: (be[i], 0, 0)),
        ],
        out_specs=pl.BlockSpec((MOE_ROWS, dm), lambda i, be, bs, nu: (i, 0)),
    )
    return pl.pallas_call(
        functools.partial(_expert_kernel, d_expert=d_expert),
        grid_spec=grid_spec,
        out_shape=jax.ShapeDtypeStruct((n_rows, dm), _F32),
        compiler_params=_params("arbitrary"),
        name="moe_experts",
    )(blk_e, blk_src, n_used, xs, w_up, b_up.reshape(n_exp, 1, two_de), w_down, b_down.reshape(n_exp, 1, dm))


def _combine_kernel(dest_ref, h_ref, gate_ref, g2_ref, b2_ref, ys_hbm, o_ref, ybuf, sem, *, tm, alpha):
    def issue(t, carry):
        for k in range(TOP_K):
            pltpu.make_async_copy(ys_hbm.at[pl.ds(dest_ref[k, t], 1)], ybuf.at[k, pl.ds(t, 1)], sem).start(priority=k % 2)
        return carry

    lax.fori_loop(0, tm, issue, 0, unroll=8)
    for k in range(TOP_K):
        pltpu.make_async_copy(ys_hbm.at[pl.ds(0, tm)], ybuf.at[k], sem).wait()

    gates = gate_ref[...]
    y = gates[:, 0:1] * ybuf[0]
    for k in range(1, TOP_K):
        y = y + gates[:, k:k + 1] * ybuf[k]
    o_ref[...] = _layer_norm(alpha * h_ref[...] + y, g2_ref[...], b2_ref[...])


def _combine(h, gates_col, dest, ys, g2, b2, alpha):
    t, dm = h.shape
    tm = COMBINE_TM
    return pl.pallas_call(
        functools.partial(_combine_kernel, tm=tm, alpha=alpha),
        grid=(t // tm,),
        in_specs=[
            pl.BlockSpec((TOP_K, tm), lambda i: (0, i), memory_space=pltpu.SMEM),
            pl.BlockSpec((tm, dm), lambda i: (i, 0)),
            pl.BlockSpec((tm, LANES), lambda i: (i, 0)),
            pl.BlockSpec((1, dm), lambda i: (0, 0)),
            pl.BlockSpec((1, dm), lambda i: (0, 0)),
            pl.BlockSpec(memory_space=pl.ANY),
        ],
        out_specs=pl.BlockSpec((tm, dm), lambda i: (i, 0)),
        out_shape=jax.ShapeDtypeStruct((t, dm), _F32),
        scratch_shapes=[pltpu.VMEM((TOP_K, tm, dm), _F32), pltpu.SemaphoreType.DMA(())],
        compiler_params=_params("arbitrary"),
        name="moe_combine",
    )(dest, h, gates_col, g2.reshape(1, dm), b2.reshape(1, dm), ys)


def _moe_plan(idx, rank, counts, n_assign):
    counts = counts.astype(jnp.int32)
    padded = (counts + MOE_ROWS - 1) // MOE_ROWS * MOE_ROWS
    pend = jnp.cumsum(padded)
    pstart = pend - padded
    onehot = idx[..., None] == jnp.arange(N_EXPERTS, dtype=jnp.int32)
    dest = rank + jnp.sum(jnp.where(onehot, pstart, 0), axis=-1)
    n_blocks = -(-n_assign // MOE_ROWS) + N_EXPERTS
    blk_start = jnp.arange(n_blocks, dtype=jnp.int32) * MOE_ROWS
    blk_e = jnp.sum((pend[None, :] <= blk_start[:, None]).astype(jnp.int32), axis=1)
    blk_e = jnp.minimum(blk_e, N_EXPERTS - 1)
    n_used = (pend[-1] // MOE_ROWS).astype(jnp.int32)
    blk_src = jnp.minimum(jnp.arange(n_blocks, dtype=jnp.int32), n_used - 1)
    blk_e = blk_e[blk_src]
    return dest.astype(jnp.int32), blk_e, blk_src, n_used.reshape(1), n_blocks * MOE_ROWS


def kernel(x, w_in, b_in, lambda_q1, lambda_k1, lambda_q2, lambda_k2, subln_g, w_proj_a, w_proj_b, w_out, b_out, ln1_g, ln1_b, w_router, b_router, w_up, b_up, w_down, b_down, ln2_g, ln2_b):
    bsz, seq, dm = x.shape
    depth = w_in.shape[0]
    alpha = (2.0 * depth) ** 0.25
    t = bsz * seq
    for layer in range(depth):
        lambda_init = 0.8 - 0.6 * math.exp(-0.3 * layer)
        x2d = x.reshape(t, dm)
        n_a, n_b = 3 * WIDTH_A, 3 * WIDTH_B
        w_l, b_l = w_in[layer], b_in[layer]
        query_scale = jnp.full((WIDTH_A,), QUERY_SCALE, _F32)
        ones = functools.partial(jnp.ones, dtype=_F32)
        qkv_a = _in_proj_a(x, w_l[:, :n_a].astype(_BF16), b_l[:n_a],
                           jnp.concatenate([query_scale, ones((2 * WIDTH_A,))]))
        w_rest = jnp.concatenate([w_l[:, n_a + n_b:], w_l[:, n_a:n_a + n_b]], axis=1).astype(_BF16)
        b_rest = jnp.concatenate([b_l[n_a + n_b:], b_l[n_a:n_a + n_b]])
        scale_rest = jnp.concatenate([ones((2 * dm,)), query_scale, ones((2 * WIDTH_B,))])
        proj2d = _in_proj(x2d, w_rest, b_rest, scale_rest)

        dil = [_dilated_pass(a, d) for a, (_, d) in zip(qkv_a, DILATED_PATTERNS)]

        lam_vecs = jnp.stack([lambda_q1[layer], lambda_k1[layer], lambda_q2[layer], lambda_k2[layer]]).astype(_F32)
        o_b = _diff_attention(proj2d.reshape(bsz, seq, -1), 2 * dm, lam_vecs, subln_g[layer],
                              lambda_init).reshape(t, WIDTH_B)

        h, idx, gates_col, rank, cnt = _merge_router(
            x2d, proj2d, 0, [o for o, _ in dil], [l for _, l in dil], o_b,
            w_proj_a[layer].astype(_BF16), w_proj_b[layer].astype(_BF16),
            w_out[layer].astype(_BF16), b_out[layer], ln1_g[layer], ln1_b[layer],
            w_router[layer], b_router[layer], alpha)

        dest, blk_e, blk_src, n_used, n_rows = _moe_plan(idx, rank, cnt[:, 0], t * TOP_K)
        xs = _dispatch(h, dest, n_rows)
        ys = _experts(xs, blk_e, blk_src, n_used, w_up[layer].astype(_BF16), b_up[layer],
                      w_down[layer].astype(_BF16), b_down[layer])
        out = _combine(h, gates_col, dest, ys, ln2_g[layer], ln2_b[layer], alpha)
        x = out.reshape(bsz, seq, dm)
    return x
```

```python
import functools
import math

import numpy as np
import jax
import jax.numpy as jnp
from jax import lax
from jax.experimental import pallas as pl
from jax.experimental.pallas import tpu as pltpu

HEAD_DIM = 64
N_HEADS_A = 8
DILATED_PATTERNS = ((128, 1), (512, 4), (2048, 16))
N_HEADS_B = 4
WIDTH_A = N_HEADS_A * HEAD_DIM
WIDTH_B = N_HEADS_B * 2 * HEAD_DIM
N_ALIBI_HEADS = N_HEADS_A + N_HEADS_B
ALIBI_IDX_A = (0, 1, 3, 4, 6, 7, 9, 10)
ALIBI_IDX_B = (2, 5, 8, 11)
Q_BLOCK = 128
MASK_VALUE = -1e30
N_EXPERTS = 32
TOP_K = 4
SWIGLU_ALPHA = 1.702
SWIGLU_LIMIT = 7.0
LN_EPS = 1e-5
SUBLN_EPS = 1e-5
LOG2E = math.log2(math.e)
QUERY_SCALE = HEAD_DIM ** -0.5 * LOG2E

LANES = 128
V7X_VMEM_LIMIT_BYTES = 56 * 1024 * 1024

PROJ_TM = 1024
PROJ_A_TM = 512
DIL_ROWS = 2 * Q_BLOCK
DIL_HALO = 64
DIFF_TQ = 512
DIFF_TK = 512
MERGE_TM = 512
MOE_ROWS = 512
FFN_CHUNK = 512
ROW_ALIGN = 8
DISP_CHUNK = 64
COMB_CHUNK = 32
DISP_LOCAL_ROWS = -(-(TOP_K * MERGE_TM + N_EXPERTS * (ROW_ALIGN - 1) + DISP_CHUNK) // LANES) * LANES
COMB_LOCAL_ROWS = -(-(TOP_K * MERGE_TM + N_EXPERTS * (COMB_CHUNK - 1)) // LANES) * LANES

_F32 = jnp.float32
_BF16 = jnp.bfloat16


def _params(*sem):
    return pltpu.CompilerParams(dimension_semantics=sem, vmem_limit_bytes=V7X_VMEM_LIMIT_BYTES)


def _alibi_slopes():
    return (2.0 ** (-8.0 * np.arange(1, N_ALIBI_HEADS + 1) / N_ALIBI_HEADS)).astype(np.float32)


def _in_proj_kernel(x_ref, w_ref, b_ref, cs_ref, o_ref):
    x = x_ref[...].astype(_BF16)
    acc = jnp.dot(x, w_ref[...], preferred_element_type=_F32)
    o_ref[...] = ((acc + b_ref[...]) * cs_ref[...]).astype(o_ref.dtype)


def _in_proj(x2d, w_bf16, b, colscale):
    t, dm = x2d.shape
    n = w_bf16.shape[1]
    tn = n // 2
    assert n % 2 == 0 and tn % LANES == 0
    return pl.pallas_call(
        _in_proj_kernel,
        grid=(t // PROJ_TM, n // tn),
        in_specs=[
            pl.BlockSpec((PROJ_TM, dm), lambda i, j: (i, 0)),
            pl.BlockSpec((dm, tn), lambda i, j: (0, j)),
            pl.BlockSpec((1, tn), lambda i, j: (0, j)),
            pl.BlockSpec((1, tn), lambda i, j: (0, j)),
        ],
        out_specs=pl.BlockSpec((PROJ_TM, tn), lambda i, j: (i, j)),
        out_shape=jax.ShapeDtypeStruct((t, n), _BF16),
        compiler_params=_params("parallel", "arbitrary"),
        name="in_proj",
    )(x2d, w_bf16, b.reshape(1, n), colscale.reshape(1, n))


def _in_proj_a_kernel(x_ref, w_ref, b_ref, cs_ref, *refs):
    out_refs, acc_ref = refs[:-1], refs[-1]
    x = x_ref[0].astype(_BF16)
    acc = (jnp.dot(x, w_ref[...], preferred_element_type=_F32) + b_ref[...]) * cs_ref[...]
    n_chunks, tm, _ = acc_ref.shape
    for c in range(n_chunks):
        acc_ref[c] = acc[:, c * LANES:(c + 1) * LANES]
    for o_ref, (_, d) in zip(out_refs, DILATED_PATTERNS):
        if d == 1:
            o_ref[0, 0] = acc.astype(o_ref.dtype)
            continue
        for r in range(d):
            for c in range(n_chunks):
                rows = acc_ref[c, pl.ds(r, tm // d, stride=d), :]
                o_ref[0, r, :, c * LANES:(c + 1) * LANES] = rows.astype(o_ref.dtype)


def _in_proj_a(x, w_bf16, b, colscale):
    bsz, seq, dm = x.shape
    n = w_bf16.shape[1]
    tm = PROJ_A_TM
    per_b = seq // tm
    dils = [d for _, d in DILATED_PATTERNS]
    assert seq % tm == 0 and all(tm % d == 0 and (tm // d) % 16 == 0 for d in dils)
    return pl.pallas_call(
        _in_proj_a_kernel,
        grid=(bsz * per_b,),
        in_specs=[
            pl.BlockSpec((1, tm, dm), lambda i: (i // per_b, i % per_b, 0)),
            pl.BlockSpec((dm, n), lambda i: (0, 0)),
            pl.BlockSpec((1, n), lambda i: (0, 0)),
            pl.BlockSpec((1, n), lambda i: (0, 0)),
        ],
        out_specs=[pl.BlockSpec((1, d, tm // d, n), lambda i: (i // per_b, 0, i % per_b, 0)) for d in dils],
        out_shape=[jax.ShapeDtypeStruct((bsz, d, seq // d, n), _BF16) for d in dils],
        scratch_shapes=[pltpu.VMEM((n // LANES, tm, LANES), _F32)],
        compiler_params=_params("parallel"),
        name="in_proj_a",
    )(x, w_bf16, b.reshape(1, n), colscale.reshape(1, n))


def _dilated_bias_table(dilation):
    slopes = _alibi_slopes()[list(ALIBI_IDX_A)]
    band = Q_BLOCK + 2 * DIL_HALO
    qi = np.arange(Q_BLOCK)[:, None]
    kj = np.arange(band)[None, :]
    rel = qi - kj + DIL_HALO
    in_band = np.abs(rel) <= DIL_HALO
    base = -slopes[:, None, None] * (dilation * np.abs(rel)).astype(np.float32)[None]
    base = (base.astype(np.float64) * LOG2E).astype(np.float32)
    edge = (np.ones_like(kj, bool), kj >= DIL_HALO, kj < band - DIL_HALO)
    out = np.stack([np.where(in_band & e, base, np.float32(MASK_VALUE)) for e in edge])
    return out.astype(np.float32)


def _dilated_kernel(q_ref, km_ref, kp_ref, kn_ref, vm_ref, vp_ref, vn_ref, bias_ref, o_ref, lse_ref,
                    kbuf, vbuf, *, nblk):
    i = pl.program_id(2)
    h0, h1 = DIL_HALO, DIL_HALO + DIL_ROWS
    kbuf[0:h0, :] = kp_ref[0, 0]
    kbuf[h0:h1, :] = km_ref[0, 0]
    kbuf[h1:h1 + DIL_HALO, :] = kn_ref[0, 0]
    vbuf[0:h0, :] = vp_ref[0, 0]
    vbuf[h0:h1, :] = vm_ref[0, 0]
    vbuf[h1:h1 + DIL_HALO, :] = vn_ref[0, 0]

    lane = lax.broadcasted_iota(jnp.int32, (Q_BLOCK, LANES), 1)
    low_half = lane < HEAD_DIM
    band = Q_BLOCK + 2 * DIL_HALO
    variants = (jnp.where(i == 0, 1, 0), jnp.where(i == nblk - 1, 2, 0))

    for j in range(DIL_ROWS // Q_BLOCK):
        rows = slice(j * Q_BLOCK, (j + 1) * Q_BLOCK)
        krows = slice(j * Q_BLOCK, j * Q_BLOCK + band)
        lse_cols = []
        for hp in range(N_HEADS_A // 2):
            cols = slice(hp * LANES, (hp + 1) * LANES)
            q_pair = q_ref[0, 0, rows, cols]
            k_pair = kbuf[krows, cols]
            v_pair = vbuf[krows, cols]
            halves = []
            for hh in range(2):
                head = 2 * hp + hh
                keep = low_half if hh == 0 else jnp.logical_not(low_half)
                qm = jnp.where(keep, q_pair, jnp.zeros_like(q_pair))
                s = lax.dot_general(qm, k_pair, (((1,), (1,)), ((), ())), preferred_element_type=_F32)
                s = s + bias_ref[variants[j], head]
                m = jnp.max(s, axis=-1, keepdims=True)
                p = jnp.exp2(s - m)
                z = jnp.sum(p, axis=-1, keepdims=True)
                halves.append(jnp.dot(p.astype(_BF16), v_pair, preferred_element_type=_F32) * (1.0 / z))
                lse_cols.append(m + jnp.log2(z))
            o_ref[0, 0, rows, cols] = jnp.where(low_half, halves[0], halves[1])
        tile = jnp.zeros((Q_BLOCK, LANES), _F32)
        for head in range(N_HEADS_A):
            tile = jnp.where(lane == head, lse_cols[head], tile)
        lse_ref[0, 0, rows, :] = tile


def _dilated_pass(qkv, dilation):
    bsz, d, L, _ = qkv.shape
    assert d == dilation and L % DIL_ROWS == 0 and DIL_ROWS % DIL_HALO == 0
    nblk = L // DIL_ROWS
    per = DIL_ROWS // DIL_HALO
    n_halo = L // DIL_HALO
    bias = jnp.asarray(_dilated_bias_table(dilation))

    def main(s):
        return pl.BlockSpec((1, 1, DIL_ROWS, WIDTH_A), lambda b, r, i: (b, r, i, s))

    def left(s):
        return pl.BlockSpec((1, 1, DIL_HALO, WIDTH_A), lambda b, r, i: (b, r, jnp.maximum(per * i - 1, 0), s))

    def right(s):
        return pl.BlockSpec((1, 1, DIL_HALO, WIDTH_A),
                            lambda b, r, i: (b, r, jnp.minimum(per * i + per, n_halo - 1), s))

    rows_buf = DIL_ROWS + 2 * DIL_HALO
    return pl.pallas_call(
        functools.partial(_dilated_kernel, nblk=nblk),
        grid=(bsz, dilation, nblk),
        in_specs=[main(0), main(1), left(1), right(1), main(2), left(2), right(2),
                  pl.BlockSpec(bias.shape, lambda b, r, i: (0, 0, 0, 0))],
        out_specs=[pl.BlockSpec((1, 1, DIL_ROWS, WIDTH_A), lambda b, r, i: (b, r, i, 0)),
                   pl.BlockSpec((1, 1, DIL_ROWS, LANES), lambda b, r, i: (b, r, i, 0))],
        out_shape=[jax.ShapeDtypeStruct((bsz, dilation, L, WIDTH_A), _F32),
                   jax.ShapeDtypeStruct((bsz, dilation, L, LANES), _F32)],
        scratch_shapes=[pltpu.VMEM((rows_buf, WIDTH_A), _BF16), pltpu.VMEM((rows_buf, WIDTH_A), _BF16)],
        compiler_params=_params("parallel", "parallel", "arbitrary"),
        name=f"dilated_d{dilation}",
    )(qkv, qkv, qkv, qkv, qkv, qkv, qkv, bias)


def _split_bf16(x):
    hi = x.astype(jnp.bfloat16)
    lo = (x - hi.astype(np.float64)).astype(jnp.bfloat16)
    return hi, lo


def _diff_tables():
    s2 = _alibi_slopes()[list(ALIBI_IDX_B)].astype(np.float64) * LOG2E
    qaug = np.zeros((N_HEADS_B, DIFF_TQ, LANES), jnp.bfloat16)
    kaug = np.zeros((N_HEADS_B, 2, DIFF_TK, LANES), jnp.bfloat16)
    i = np.arange(DIFF_TQ, dtype=np.float64)
    j = np.arange(DIFF_TK, dtype=np.float64)
    for h in range(N_HEADS_B):
        r_hi, r_lo = _split_bf16(-s2[h] * i)
        qaug[h, :, 0] = 1.0
        qaug[h, :, 1] = 1.0
        qaug[h, :, 2] = r_hi
        qaug[h, :, 3] = r_lo
        for side, sign in enumerate((1.0, -1.0)):
            c_hi, c_lo = _split_bf16(sign * s2[h] * j)
            kaug[h, side, :, 0] = c_hi
            kaug[h, side, :, 1] = c_lo
            kaug[h, side, :, 2] = sign
            kaug[h, side, :, 3] = sign
    halves = DIFF_TK // DIFF_TQ
    off = (np.arange(halves) * DIFF_TQ)[:, None, None]
    dist = np.abs(off + i[None, :, None] - j[None, None, :])
    diag = (-s2[:, None, None, None] * dist[None]).astype(np.float32)
    return jnp.asarray(qaug), jnp.asarray(kaug), jnp.asarray(diag), jnp.asarray(s2.astype(np.float32))


def _diff_kernel(slope_ref, lam_ref, q_ref, k_ref, v_ref, qaug_ref, kaug_ref, diag_ref, g_ref, o_ref,
                 *, seq, lambda_init):
    head = pl.program_id(1)
    qb = pl.program_id(2)
    q0 = qb * DIFF_TQ
    halves = DIFF_TK // DIFF_TQ
    kd = qb // halves
    n_kb = seq // DIFF_TK
    slope2 = slope_ref[head]
    nt = (((1,), (1,)), ((), ()))

    q = q_ref[0]
    lane = lax.broadcasted_iota(jnp.int32, q.shape, 1)
    zero = jnp.zeros_like(q)
    qaug = qaug_ref[0]
    q_maps = (jnp.concatenate([jnp.where(lane < HEAD_DIM, q, zero), qaug], axis=1),
              jnp.concatenate([jnp.where(lane >= HEAD_DIM, q, zero), qaug], axis=1))
    zero_aug = jnp.zeros((DIFF_TK, LANES), _BF16)
    vlane = lax.broadcasted_iota(jnp.int32, (DIFF_TK, LANES), 1)
    ones_col = jnp.where(vlane == 0, 1.0, 0.0).astype(_BF16)

    def load(kb):
        k0 = pl.multiple_of(kb * DIFF_TK, DIFF_TK)
        return k_ref[0, pl.ds(k0, DIFF_TK), :], jnp.concatenate([v_ref[0, pl.ds(k0, DIFF_TK), :], ones_col], axis=1)

    def split(pv):
        return pv[:, LANES:LANES + 1], pv[:, 0:LANES]

    def update(s, const, vblk, carry):
        m_old, l_old, acc = carry
        m_new = jnp.maximum(m_old, jnp.max(s, axis=-1, keepdims=True) + const)
        alpha = jnp.exp2(m_old - m_new)
        p = jnp.exp2(s - (m_new - const))
        l_blk, o_blk = split(jnp.dot(p.astype(_BF16), vblk, preferred_element_type=_F32))
        return m_new, alpha * l_old + l_blk, alpha * acc + o_blk

    kblk, vblk = load(kd)
    k_all = jnp.concatenate([kblk, zero_aug], axis=1)
    bias = diag_ref[0, qb % halves]
    carries = []
    for mp in range(2):
        s = lax.dot_general(q_maps[mp], k_all, nt, preferred_element_type=_F32) + bias
        m = jnp.max(s, axis=-1, keepdims=True)
        p = jnp.exp2(s - m)
        carries.append((m,) + split(jnp.dot(p.astype(_BF16), vblk, preferred_element_type=_F32)))

    for t in range(n_kb - 1):
        kb = jnp.where(t >= kd, t + 1, t)
        side = jnp.where(t >= kd, 1, 0)
        kblk, vblk = load(kb)
        k_all = jnp.concatenate([kblk, kaug_ref[0, side]], axis=1)
        const = -slope2 * jnp.abs(q0 - kb * DIFF_TK).astype(_F32)
        for mp in range(2):
            s = lax.dot_general(q_maps[mp], k_all, nt, preferred_element_type=_F32)
            carries[mp] = update(s, const, vblk, carries[mp])

    lv = lam_ref[...]
    lam = (jnp.exp(jnp.sum(lv[0:1] * lv[1:2], axis=-1, keepdims=True))
           - jnp.exp(jnp.sum(lv[2:3] * lv[3:4], axis=-1, keepdims=True)) + lambda_init)
    (_, l1, a1), (_, l2, a2) = carries
    of = a1 * (1.0 / l1) - lam * (a2 * (1.0 / l2))
    of = of * lax.rsqrt(jnp.mean(of * of, axis=-1, keepdims=True) + SUBLN_EPS)
    of = of * g_ref[...] * (1.0 - lambda_init)
    o_ref[0] = of.astype(o_ref.dtype)


def _diff_attention(proj, q_offset, lam_vecs, subln_g, lambda_init):
    bsz, seq, _ = proj.shape
    assert seq % DIFF_TQ == 0 and seq % DIFF_TK == 0 and q_offset % LANES == 0
    qcol = q_offset // LANES
    kcol = qcol + WIDTH_B // LANES
    vcol = kcol + WIDTH_B // LANES
    assert DIFF_TK % DIFF_TQ == 0
    halves = DIFF_TK // DIFF_TQ
    qaug, kaug, diag, slopes2 = _diff_tables()
    return pl.pallas_call(
        functools.partial(_diff_kernel, seq=seq, lambda_init=lambda_init),
        grid=(bsz, N_HEADS_B, seq // DIFF_TQ),
        in_specs=[
            pl.BlockSpec(memory_space=pltpu.SMEM),
            pl.BlockSpec((4, HEAD_DIM), lambda b, h, i: (0, 0)),
            pl.BlockSpec((1, DIFF_TQ, LANES), lambda b, h, i: (b, i, qcol + h)),
            pl.BlockSpec((1, seq, LANES), lambda b, h, i: (b, 0, kcol + h)),
            pl.BlockSpec((1, seq, LANES), lambda b, h, i: (b, 0, vcol + h)),
            pl.BlockSpec((1, DIFF_TQ, LANES), lambda b, h, i: (h, 0, 0)),
            pl.BlockSpec((1, 2, DIFF_TK, LANES), lambda b, h, i: (h, 0, 0, 0)),
            pl.BlockSpec((1, halves, DIFF_TQ, DIFF_TK), lambda b, h, i: (h, 0, 0, 0)),
            pl.BlockSpec((1, LANES), lambda b, h, i: (0, 0)),
        ],
        out_specs=pl.BlockSpec((1, DIFF_TQ, LANES), lambda b, h, i: (b, i, h)),
        out_shape=jax.ShapeDtypeStruct((bsz, seq, WIDTH_B), _BF16),
        compiler_params=_params("parallel", "parallel", "arbitrary"),
        name="diff_attn",
    )(slopes2, lam_vecs, proj, proj, proj, qaug, kaug, diag, subln_g.reshape(1, LANES).astype(_F32))


def _layer_norm(z, g, b):
    mu = jnp.mean(z, axis=-1, keepdims=True)
    zc = z - mu
    var = jnp.mean(zc * zc, axis=-1, keepdims=True)
    return zc * lax.rsqrt(var + LN_EPS) * g + b


def _mix_patterns(o_refs, lse_refs, expand_ref, o_nat, lse_nat):
    n_chunks, tm = o_nat.shape[1], o_nat.shape[2]
    outs, lses = [], []
    for p, (_, d) in enumerate(DILATED_PATTERNS):
        if d == 1:
            outs.append(o_refs[p][0, 0])
            lses.append(lse_refs[p][0, 0])
            continue
        for r in range(d):
            for c in range(n_chunks):
                o_nat[p, c, pl.ds(r, tm // d, stride=d), :] = o_refs[p][0, r, :, c * LANES:(c + 1) * LANES]
            lse_nat[p, pl.ds(r, tm // d, stride=d), :] = lse_refs[p][0, r]
        outs.append(jnp.concatenate([o_nat[p, c] for c in range(n_chunks)], axis=1))
        lses.append(lse_nat[p])
    top = functools.reduce(jnp.maximum, lses)
    es = [jnp.exp2(l - top) for l in lses]
    inv = 1.0 / functools.reduce(lambda a, b: a + b, es)
    mixed = None
    for o, e in zip(outs, es):
        w = e * inv
        w_hi = w.astype(_BF16)
        w_lo = (w - w_hi.astype(_F32)).astype(_BF16)
        w_full = (jnp.dot(w_hi, expand_ref[...], preferred_element_type=_F32)
                  + jnp.dot(w_lo, expand_ref[...], preferred_element_type=_F32))
        mixed = w_full * o if mixed is None else mixed + w_full * o
    return mixed


def _merge_kernel(x_ref, ga_ref, gb_ref, o1_ref, o2_ref, o3_ref, l1_ref, l2_ref, l3_ref, expand_ref, ob_ref,
                  wpa_ref, wpb_ref, wout_ref, bout_ref, g1_ref, b1_ref, wrh_ref, wrl_ref, br_ref,
                  h_ref, idx_ref, gate_ref, rank_ref, cnt_ref, o_nat, lse_nat, *, alpha):
    o_a = _mix_patterns((o1_ref, o2_ref, o3_ref), (l1_ref, l2_ref, l3_ref), expand_ref, o_nat, lse_nat)
    pa = jnp.dot(o_a.astype(_BF16), wpa_ref[...], preferred_element_type=_F32)
    pb = jnp.dot(ob_ref[...], wpb_ref[...], preferred_element_type=_F32)
    merged = jax.nn.sigmoid(ga_ref[...].astype(_F32)) * pa + jax.nn.sigmoid(gb_ref[...].astype(_F32)) * pb
    y = jnp.dot(merged.astype(_BF16), wout_ref[...], preferred_element_type=_F32) + bout_ref[...]
    h = _layer_norm(alpha * x_ref[...] + y, g1_ref[...], b1_ref[...])
    h_ref[...] = h

    h_hi = h.astype(_BF16)
    h_lo = (h - h_hi.astype(_F32)).astype(_BF16)
    nt = (((1,), (1,)), ((), ()))
    logits = (lax.dot_general(wrh_ref[...], h_hi, nt, preferred_element_type=_F32)
              + lax.dot_general(wrl_ref[...], h_hi, nt, preferred_element_type=_F32)
              + lax.dot_general(wrh_ref[...], h_lo, nt, preferred_element_type=_F32)
              + br_ref[...])

    tm = logits.shape[1]
    expert = lax.broadcasted_iota(jnp.int32, logits.shape, 0)
    work = logits
    vals, sels, idxs = [], [], []
    for _ in range(TOP_K):
        mx = jnp.max(work, axis=0, keepdims=True)
        idx = jnp.min(jnp.where(work == mx, expert, N_EXPERTS), axis=0, keepdims=True)
        sel = expert == idx
        work = jnp.where(sel, -jnp.inf, work)
        vals.append(mx)
        idxs.append(idx)
        sels.append(sel)
    ex = [jnp.exp(v - vals[0]) for v in vals]
    inv = 1.0 / (ex[0] + ex[1] + ex[2] + ex[3])
    gates = [e * inv for e in ex]

    chosen = sels[0] | sels[1] | sels[2] | sels[3]
    onehot = jnp.where(chosen, 1.0, 0.0).astype(_BF16)
    before = (lax.broadcasted_iota(jnp.int32, (tm, tm), 0)
              < lax.broadcasted_iota(jnp.int32, (tm, tm), 1))
    upper = jnp.where(before, 1.0, 0.0).astype(_BF16)
    prefix = jnp.dot(onehot, upper, preferred_element_type=_F32)
    for k in range(TOP_K):
        rank = jnp.sum(jnp.where(sels[k], prefix, 0.0), axis=0, keepdims=True)
        rank_ref[k:k + 1, :] = rank.astype(jnp.int32)
        idx_ref[k:k + 1, :] = idxs[k]
        gate_ref[k:k + 1, :] = gates[k]
    cnt_ref[0] = jnp.broadcast_to(jnp.sum(onehot.astype(_F32), axis=1, keepdims=True), cnt_ref.shape[1:])


def _merge_router(x2d, proj2d, gate_offset, dil_outs, dil_lses, o_b, wpa, wpb, wout, bout, g1, b1,
                  w_router, b_router, alpha):
    t, dm = x2d.shape
    tm = MERGE_TM
    bsz, _, seq_over_d0, _ = dil_outs[0].shape
    seq = seq_over_d0 * DILATED_PATTERNS[0][1]
    per_b = seq // tm
    dils = [d for _, d in DILATED_PATTERNS]
    assert gate_offset % dm == 0 and seq % tm == 0 and all(tm % d == 0 for d in dils)
    gcol = gate_offset // dm
    wr_t = w_router.T.astype(_F32)
    wr_hi = wr_t.astype(_BF16)
    wr_lo = (wr_t - wr_hi.astype(_F32)).astype(_BF16)
    expand = np.zeros((LANES, WIDTH_A), np.float32)
    for head in range(N_HEADS_A):
        expand[head, head * HEAD_DIM:(head + 1) * HEAD_DIM] = 1.0
    expand = jnp.asarray(expand, _BF16)

    def const(shape):
        return pl.BlockSpec(shape, lambda i: tuple(0 for _ in shape))

    def residue_major(d, width):
        return pl.BlockSpec((1, d, tm // d, width), lambda i: (i // per_b, 0, i % per_b, 0))

    return pl.pallas_call(
        functools.partial(_merge_kernel, alpha=alpha),
        grid=(t // tm,),
        in_specs=[
            pl.BlockSpec((tm, dm), lambda i: (i, 0)),
            pl.BlockSpec((tm, dm), lambda i: (i, gcol)),
            pl.BlockSpec((tm, dm), lambda i: (i, gcol + 1)),
            *[residue_major(d, WIDTH_A) for d in dils],
            *[residue_major(d, LANES) for d in dils],
            const((LANES, WIDTH_A)),
            pl.BlockSpec((tm, WIDTH_B), lambda i: (i, 0)),
            const((WIDTH_A, dm)), const((WIDTH_B, dm)), const((dm, dm)), const((1, dm)),
            const((1, dm)), const((1, dm)),
            const((N_EXPERTS, dm)), const((N_EXPERTS, dm)), const((N_EXPERTS, 1)),
        ],
        out_specs=[
            pl.BlockSpec((tm, dm), lambda i: (i, 0)),
            pl.BlockSpec((TOP_K, tm), lambda i: (0, i)),
            pl.BlockSpec((TOP_K, tm), lambda i: (0, i)),
            pl.BlockSpec((TOP_K, tm), lambda i: (0, i)),
            pl.BlockSpec((1, N_EXPERTS, LANES), lambda i: (i, 0, 0)),
        ],
        out_shape=[
            jax.ShapeDtypeStruct((t, dm), _F32),
            jax.ShapeDtypeStruct((TOP_K, t), jnp.int32),
            jax.ShapeDtypeStruct((TOP_K, t), _F32),
            jax.ShapeDtypeStruct((TOP_K, t), jnp.int32),
            jax.ShapeDtypeStruct((t // tm, N_EXPERTS, LANES), _F32),
        ],
        scratch_shapes=[pltpu.VMEM((len(dils), WIDTH_A // LANES, tm, LANES), _F32),
                        pltpu.VMEM((len(dils), tm, LANES), _F32)],
        compiler_params=_params("parallel"),
        name="merge_router",
    )(x2d, proj2d, proj2d, *dil_outs, *dil_lses, expand, o_b, wpa, wpb, wout, bout.reshape(1, dm),
      g1.reshape(1, dm), b1.reshape(1, dm), wr_hi, wr_lo, b_router.reshape(N_EXPERTS, 1).astype(_F32))


def _chunk_count(n, chunk):
    return lax.shift_right_logical(n + (chunk - 1), int(math.log2(chunk)))


def _dispatch_kernel(cnt_ref, base_ref, off_ref, tail_ref, end_ref, lpos_ref, gate_ref, h_ref, xs_hbm,
                     local, zeros, sem, *, tm, dm):
    i = pl.program_id(0)
    rows = local.shape[0]
    slot = lax.broadcasted_iota(jnp.int32, (rows, tm), 0)
    perm = jnp.zeros((rows, tm), _F32)
    gsel = jnp.zeros((rows, tm), _F32)
    for k in range(TOP_K):
        hit = slot == lpos_ref[k:k + 1, :]
        perm = perm + jnp.where(hit, 1.0, 0.0)
        gsel = gsel + jnp.where(hit, gate_ref[k:k + 1, :], 0.0)
    local[:, 0:dm] = jnp.dot(perm.astype(_BF16), h_ref[...].astype(_BF16), preferred_element_type=_F32)
    local[:, dm:dm + LANES] = jnp.broadcast_to(jnp.sum(gsel, axis=1, keepdims=True), (rows, LANES))

    def chunk_copy(src_row, dst_row):
        return pltpu.make_async_copy(local.at[pl.ds(pl.multiple_of(src_row, ROW_ALIGN), DISP_CHUNK)],
                                     xs_hbm.at[pl.ds(pl.multiple_of(dst_row, ROW_ALIGN), DISP_CHUNK)], sem)

    def per_expert(e, total):
        j = i * N_EXPERTS + e
        n_chunks = _chunk_count(cnt_ref[j], DISP_CHUNK)

        def issue(c, carry):
            chunk_copy(off_ref[j] + c * DISP_CHUNK, base_ref[j] + c * DISP_CHUNK).start()
            return carry

        lax.fori_loop(0, n_chunks, issue, 0)
        return total + n_chunks

    total = lax.fori_loop(0, N_EXPERTS, per_expert, 0)

    def drain(c, carry):
        chunk_copy(0, 0).wait()
        return carry

    lax.fori_loop(0, total, drain, 0)

    @pl.when(i == pl.num_programs(0) - 1)
    def _():
        zeros[...] = jnp.zeros_like(zeros)

        def fill(e, carry):
            gap = end_ref[e] - tail_ref[e]
            pos = tail_ref[e]
            size = MOE_ROWS
            while size >= ROW_ALIGN:
                take = (gap & size) != 0
                cp = pltpu.make_async_copy(zeros.at[pl.ds(0, size)],
                                           xs_hbm.at[pl.ds(pl.multiple_of(pos, ROW_ALIGN), size)], sem)

                @pl.when(take)
                def _():
                    cp.start()
                    cp.wait()

                pos = pos + jnp.where(take, size, 0)
                size //= 2
            return carry

        lax.fori_loop(0, N_EXPERTS, fill, 0)

        def fill_block(blk, carry):
            cp = pltpu.make_async_copy(
                zeros, xs_hbm.at[pl.ds(pl.multiple_of(blk * MOE_ROWS, MOE_ROWS), MOE_ROWS)], sem)
            cp.start()
            cp.wait()
            return carry

        first_unused = lax.shift_right_logical(end_ref[N_EXPERTS - 1], int(math.log2(MOE_ROWS)))
        lax.fori_loop(first_unused, xs_hbm.shape[0] // MOE_ROWS, fill_block, 0)


def _dispatch(h, gates, plan):
    t, dm = h.shape
    tm = MERGE_TM
    width = dm + LANES
    grid_spec = pltpu.PrefetchScalarGridSpec(
        num_scalar_prefetch=5,
        grid=(t // tm,),
        in_specs=[
            pl.BlockSpec((TOP_K, tm), lambda i, *_: (0, i)),
            pl.BlockSpec((TOP_K, tm), lambda i, *_: (0, i)),
            pl.BlockSpec((tm, dm), lambda i, *_: (i, 0)),
        ],
        out_specs=pl.BlockSpec(memory_space=pl.ANY),
        scratch_shapes=[pltpu.VMEM((DISP_LOCAL_ROWS, width), _F32), pltpu.VMEM((MOE_ROWS, width), _F32),
                        pltpu.SemaphoreType.DMA(())],
    )
    return pl.pallas_call(
        functools.partial(_dispatch_kernel, tm=tm, dm=dm),
        grid_spec=grid_spec,
        out_shape=jax.ShapeDtypeStruct((plan["n_rows"], width), _F32),
        compiler_params=_params("arbitrary"),
        name="moe_dispatch",
    )(plan["cnt"], plan["base"], plan["off_d"], plan["tail"], plan["end"], plan["lpos_d"], gates, h)


def _expert_kernel(blk_e_ref, blk_src_ref, n_used_ref, xs_ref, wup_ref, bup_ref, wdn_ref, bdn_ref, ys_ref,
                   *, d_expert):
    del blk_e_ref, blk_src_ref

    dm = ys_ref.shape[1]

    @pl.when(pl.program_id(0) < n_used_ref[0])
    def _():
        x = xs_ref[:, 0:dm].astype(_BF16)
        row_gate = xs_ref[:, dm:dm + 1]
        acc = jnp.zeros(ys_ref.shape, _F32)
        for c in range(d_expert // FFN_CHUNK):
            lo, hi = c * FFN_CHUNK, (c + 1) * FFN_CHUNK
            g = jnp.dot(x, wup_ref[0, :, lo:hi], preferred_element_type=_F32) + bup_ref[0, :, lo:hi]
            u = (jnp.dot(x, wup_ref[0, :, d_expert + lo:d_expert + hi], preferred_element_type=_F32)
                 + bup_ref[0, :, d_expert + lo:d_expert + hi])
            gate = jnp.minimum(g, SWIGLU_LIMIT)
            up = jnp.clip(u, -SWIGLU_LIMIT, SWIGLU_LIMIT)
            act = gate * jax.nn.sigmoid(SWIGLU_ALPHA * gate) * (up + 1.0)
            acc = acc + jnp.dot(act.astype(_BF16), wdn_ref[0, lo:hi, :], preferred_element_type=_F32)
        ys_ref[...] = ((acc + bdn_ref[0]) * row_gate).astype(_BF16).astype(_F32)

    @pl.when(pl.program_id(0) >= n_used_ref[0])
    def _():
        ys_ref[...] = jnp.zeros_like(ys_ref)


def _experts(xs, blk_e, blk_src, n_used, w_up, b_up, w_down, b_down):
    n_rows, width = xs.shape
    dm = width - LANES
    n_blocks = n_rows // MOE_ROWS
    n_exp, _, two_de = w_up.shape
    d_expert = two_de // 2
    grid_spec = pltpu.PrefetchScalarGridSpec(
        num_scalar_prefetch=3,
        grid=(n_blocks,),
        in_specs=[
            pl.BlockSpec((MOE_ROWS, width), lambda i, be, bs, nu: (bs[i], 0)),
            pl.BlockSpec((1, dm, two_de), lambda i, be, bs, nu: (be[i], 0, 0)),
            pl.BlockSpec((1, 1, two_de), lambda i, be, bs, nu: (be[i], 0, 0)),
            pl.BlockSpec((1, d_expert, dm), lambda i, be, bs, nu: (be[i], 0, 0)),
            pl.BlockSpec((1, 1, dm), lambda i, be, bs, nu: (be[i], 0, 0)),
        ],
        out_specs=pl.BlockSpec((MOE_ROWS, dm), lambda i, be, bs, nu: (i, 0)),
    )
    return pl.pallas_call(
        functools.partial(_expert_kernel, d_expert=d_expert),
        grid_spec=grid_spec,
        out_shape=jax.ShapeDtypeStruct((n_rows, dm), _F32),
        compiler_params=_params("arbitrary"),
        name="moe_experts",
    )(blk_e, blk_src, n_used, xs, w_up, b_up.reshape(n_exp, 1, two_de), w_down, b_down.reshape(n_exp, 1, dm))


def _combine_kernel(cnt_ref, base_ref, off_ref, lpos_ref, h_ref, g2_ref, b2_ref, ys_hbm, o_ref, local, sem,
                    *, tm, alpha):
    i = pl.program_id(0)
    rows = local.shape[0]

    @pl.when(i == 0)
    def _():
        local[...] = jnp.zeros_like(local)

    def chunk_copy(src_row, dst_row):
        return pltpu.make_async_copy(ys_hbm.at[pl.ds(pl.multiple_of(src_row, ROW_ALIGN), COMB_CHUNK)],
                                     local.at[pl.ds(pl.multiple_of(dst_row, ROW_ALIGN), COMB_CHUNK)], sem)

    def per_expert(e, total):
        j = i * N_EXPERTS + e
        n_chunks = _chunk_count(cnt_ref[j], COMB_CHUNK)

        def issue(c, carry):
            chunk_copy(base_ref[j] + c * COMB_CHUNK, off_ref[j] + c * COMB_CHUNK).start()
            return carry

        lax.fori_loop(0, n_chunks, issue, 0)
        return total + n_chunks

    total = lax.fori_loop(0, N_EXPERTS, per_expert, 0)

    slot = lax.broadcasted_iota(jnp.int32, (tm, rows), 1)
    pick = jnp.zeros((tm, rows), _F32)
    for k in range(TOP_K):
        pick = pick + jnp.where(slot == lpos_ref[:, k:k + 1], 1.0, 0.0)

    def drain(c, carry):
        chunk_copy(0, 0).wait()
        return carry

    lax.fori_loop(0, total, drain, 0)
    y = jnp.dot(pick.astype(_BF16), local[...].astype(_BF16), preferred_element_type=_F32)
    o_ref[...] = _layer_norm(alpha * h_ref[...] + y, g2_ref[...], b2_ref[...])


def _combine(h, ys, plan, g2, b2, alpha):
    t, dm = h.shape
    tm = MERGE_TM
    grid_spec = pltpu.PrefetchScalarGridSpec(
        num_scalar_prefetch=3,
        grid=(t // tm,),
        in_specs=[
            pl.BlockSpec((tm, TOP_K), lambda i, *_: (i, 0)),
            pl.BlockSpec((tm, dm), lambda i, *_: (i, 0)),
            pl.BlockSpec((1, dm), lambda i, *_: (0, 0)),
            pl.BlockSpec((1, dm), lambda i, *_: (0, 0)),
            pl.BlockSpec(memory_space=pl.ANY),
        ],
        out_specs=pl.BlockSpec((tm, dm), lambda i, *_: (i, 0)),
        scratch_shapes=[pltpu.VMEM((COMB_LOCAL_ROWS, dm), _F32), pltpu.SemaphoreType.DMA(())],
    )
    return pl.pallas_call(
        functools.partial(_combine_kernel, tm=tm, alpha=alpha),
        grid_spec=grid_spec,
        out_shape=jax.ShapeDtypeStruct((t, dm), _F32),
        compiler_params=_params("arbitrary"),
        name="moe_combine",
    )(plan["cnt"], plan["base"], plan["off_c"], plan["lpos_c"], h, g2.reshape(1, dm), b2.reshape(1, dm), ys)


def _round_up(x, m):
    return (x + m - 1) // m * m


def _moe_plan(idx, rank, tile_cnt, tm):
    n_tiles, n_exp = tile_cnt.shape
    t = idx.shape[1]
    cnt = tile_cnt.astype(jnp.int32)
    grp = _round_up(cnt, ROW_ALIGN)
    tot = jnp.sum(grp, axis=0)
    padded = jnp.where(tot > 0, _round_up(tot + DISP_CHUNK, MOE_ROWS), 0)
    pend = jnp.cumsum(padded)
    pstart = pend - padded
    base = pstart[None, :] + jnp.cumsum(grp, axis=0) - grp
    off_d = jnp.cumsum(grp, axis=1) - grp
    n_chunk_c = _round_up(cnt, COMB_CHUNK) // COMB_CHUNK
    off_c = (jnp.cumsum(n_chunk_c, axis=1) - n_chunk_c) * COMB_CHUNK
    cover = jnp.where(cnt > 0, base + _round_up(cnt, DISP_CHUNK), 0)
    tail = jnp.maximum(jnp.max(cover, axis=0), pstart)

    onehot = idx[..., None] == jnp.arange(n_exp, dtype=jnp.int32)
    tile_of = jnp.arange(t, dtype=jnp.int32) // tm

    def per_assignment(table):
        return jnp.sum(jnp.where(onehot, table[tile_of][None], 0), axis=-1)

    n_rows = _round_up(t * idx.shape[0] + n_tiles * n_exp * (ROW_ALIGN - 1)
                       + n_exp * (DISP_CHUNK + MOE_ROWS - 1), MOE_ROWS)
    n_blocks = n_rows // MOE_ROWS
    blk_start = jnp.arange(n_blocks, dtype=jnp.int32) * MOE_ROWS
    blk_e = jnp.sum((pend[None, :] <= blk_start[:, None]).astype(jnp.int32), axis=1)
    blk_e = jnp.minimum(blk_e, n_exp - 1)
    n_used = (pend[-1] // MOE_ROWS).astype(jnp.int32)
    blk_src = jnp.minimum(jnp.arange(n_blocks, dtype=jnp.int32), n_used - 1)
    return dict(
        cnt=cnt.reshape(-1), base=base.reshape(-1).astype(jnp.int32),
        off_d=off_d.reshape(-1).astype(jnp.int32), off_c=off_c.reshape(-1).astype(jnp.int32),
        tail=tail.astype(jnp.int32), end=pend.astype(jnp.int32),
        lpos_d=(per_assignment(off_d) + rank).astype(jnp.int32),
        lpos_c=(per_assignment(off_c) + rank).astype(jnp.int32).T,
        blk_e=blk_e[blk_src], blk_src=blk_src, n_used=n_used.reshape(1), n_rows=n_rows)


def kernel(x, w_in, b_in, lambda_q1, lambda_k1, lambda_q2, lambda_k2, subln_g, w_proj_a, w_proj_b, w_out, b_out, ln1_g, ln1_b, w_router, b_router, w_up, b_up, w_down, b_down, ln2_g, ln2_b):
    bsz, seq, dm = x.shape
    depth = w_in.shape[0]
    alpha = (2.0 * depth) ** 0.25
    t = bsz * seq
    for layer in range(depth):
        lambda_init = 0.8 - 0.6 * math.exp(-0.3 * layer)
        x2d = x.reshape(t, dm)
        n_a, n_b = 3 * WIDTH_A, 3 * WIDTH_B
        w_l, b_l = w_in[layer], b_in[layer]
        query_scale = jnp.full((WIDTH_A,), QUERY_SCALE, _F32)
        ones = functools.partial(jnp.ones, dtype=_F32)
        qkv_a = _in_proj_a(x, w_l[:, :n_a].astype(_BF16), b_l[:n_a],
                           jnp.concatenate([query_scale, ones((2 * WIDTH_A,))]))
        w_rest = jnp.concatenate([w_l[:, n_a + n_b:], w_l[:, n_a:n_a + n_b]], axis=1).astype(_BF16)
        b_rest = jnp.concatenate([b_l[n_a + n_b:], b_l[n_a:n_a + n_b]])
        scale_rest = jnp.concatenate([ones((2 * dm,)), query_scale, ones((2 * WIDTH_B,))])
        proj2d = _in_proj(x2d, w_rest, b_rest, scale_rest)

        dil = [_dilated_pass(a, d) for a, (_, d) in zip(qkv_a, DILATED_PATTERNS)]

        lam_vecs = jnp.stack([lambda_q1[layer], lambda_k1[layer], lambda_q2[layer], lambda_k2[layer]]).astype(_F32)
        o_b = _diff_attention(proj2d.reshape(bsz, seq, -1), 2 * dm, lam_vecs, subln_g[layer],
                              lambda_init).reshape(t, WIDTH_B)

        h, idx, gates, rank, cnt = _merge_router(
            x2d, proj2d, 0, [o for o, _ in dil], [l for _, l in dil], o_b,
            w_proj_a[layer].astype(_BF16), w_proj_b[layer].astype(_BF16),
            w_out[layer].astype(_BF16), b_out[layer], ln1_g[layer], ln1_b[layer],
            w_router[layer], b_router[layer], alpha)

        plan = _moe_plan(idx, rank, cnt[:, :, 0], MERGE_TM)
        xs = _dispatch(h, gates, plan)
        ys = _experts(xs, plan["blk_e"], plan["blk_src"], plan["n_used"], w_up[layer].astype(_BF16),
                      b_up[layer], w_down[layer].astype(_BF16), b_down[layer])
        out = _combine(h, ys, plan, ln2_g[layer], ln2_b[layer], alpha)
        x = out.reshape(bsz, seq, dm)
    return x
```

```python
import functools
import math

import numpy as np
import jax
import jax.numpy as jnp
from jax import lax
from jax.experimental import pallas as pl
from jax.experimental.pallas import tpu as pltpu

HEAD_DIM = 64
N_HEADS_A = 8
DILATED_PATTERNS = ((128, 1), (512, 4), (2048, 16))
N_HEADS_B = 4
WIDTH_A = N_HEADS_A * HEAD_DIM
WIDTH_B = N_HEADS_B * 2 * HEAD_DIM
N_ALIBI_HEADS = N_HEADS_A + N_HEADS_B
ALIBI_IDX_A = (0, 1, 3, 4, 6, 7, 9, 10)
ALIBI_IDX_B = (2, 5, 8, 11)
Q_BLOCK = 128
MASK_VALUE = -1e30
N_EXPERTS = 32
TOP_K = 4
SWIGLU_ALPHA = 1.702
SWIGLU_LIMIT = 7.0
LN_EPS = 1e-5
SUBLN_EPS = 1e-5
LOG2E = math.log2(math.e)
QUERY_SCALE = HEAD_DIM ** -0.5 * LOG2E

LANES = 128
V7X_VMEM_LIMIT_BYTES = 56 * 1024 * 1024

PROJ_TM = 1024
PROJ_A_TM = 512
DIL_ROWS = 2 * Q_BLOCK
DIL_HALO = 64
DIFF_TQ = 512
DIFF_TK = 512
MERGE_TM = 512
MOE_ROWS = 512
FFN_CHUNK = 512
ROW_ALIGN = 8
DISP_CHUNK = 64
COMB_CHUNK = 32
DISP_LOCAL_ROWS = -(-(TOP_K * MERGE_TM + N_EXPERTS * (ROW_ALIGN - 1) + DISP_CHUNK) // LANES) * LANES
COMB_LOCAL_ROWS = -(-(TOP_K * MERGE_TM + N_EXPERTS * (COMB_CHUNK - 1)) // LANES) * LANES

_F32 = jnp.float32
_BF16 = jnp.bfloat16


def _params(*sem):
    return pltpu.CompilerParams(dimension_semantics=sem, vmem_limit_bytes=V7X_VMEM_LIMIT_BYTES)


def _alibi_slopes():
    return (2.0 ** (-8.0 * np.arange(1, N_ALIBI_HEADS + 1) / N_ALIBI_HEADS)).astype(np.float32)


def _in_proj_kernel(x_ref, w_ref, b_ref, cs_ref, o_ref):
    x = x_ref[...].astype(_BF16)
    acc = jnp.dot(x, w_ref[...], preferred_element_type=_F32)
    o_ref[...] = ((acc + b_ref[...]) * cs_ref[...]).astype(o_ref.dtype)


def _in_proj(x2d, w_bf16, b, colscale):
    t, dm = x2d.shape
    n = w_bf16.shape[1]
    tn = n // 2
    assert n % 2 == 0 and tn % LANES == 0
    return pl.pallas_call(
        _in_proj_kernel,
        grid=(t // PROJ_TM, n // tn),
        in_specs=[
            pl.BlockSpec((PROJ_TM, dm), lambda i, j: (i, 0)),
            pl.BlockSpec((dm, tn), lambda i, j: (0, j)),
            pl.BlockSpec((1, tn), lambda i, j: (0, j)),
            pl.BlockSpec((1, tn), lambda i, j: (0, j)),
        ],
        out_specs=pl.BlockSpec((PROJ_TM, tn), lambda i, j: (i, j)),
        out_shape=jax.ShapeDtypeStruct((t, n), _BF16),
        compiler_params=_params("parallel", "arbitrary"),
        name="in_proj",
    )(x2d, w_bf16, b.reshape(1, n), colscale.reshape(1, n))


def _in_proj_a_kernel(x_ref, w_ref, b_ref, cs_ref, *refs):
    out_refs, acc_ref = refs[:-1], refs[-1]
    x = x_ref[0].astype(_BF16)
    acc = (jnp.dot(x, w_ref[...], preferred_element_type=_F32) + b_ref[...]) * cs_ref[...]
    n_chunks, tm, _ = acc_ref.shape
    for c in range(n_chunks):
        acc_ref[c] = acc[:, c * LANES:(c + 1) * LANES]
    for o_ref, (_, d) in zip(out_refs, DILATED_PATTERNS):
        if d == 1:
            o_ref[0, 0] = acc.astype(o_ref.dtype)
            continue
        for r in range(d):
            for c in range(n_chunks):
                rows = acc_ref[c, pl.ds(r, tm // d, stride=d), :]
                o_ref[0, r, :, c * LANES:(c + 1) * LANES] = rows.astype(o_ref.dtype)


def _in_proj_a(x, w_bf16, b, colscale):
    bsz, seq, dm = x.shape
    n = w_bf16.shape[1]
    tm = PROJ_A_TM
    per_b = seq // tm
    dils = [d for _, d in DILATED_PATTERNS]
    assert seq % tm == 0 and all(tm % d == 0 and (tm // d) % 16 == 0 for d in dils)
    return pl.pallas_call(
        _in_proj_a_kernel,
        grid=(bsz * per_b,),
        in_specs=[
            pl.BlockSpec((1, tm, dm), lambda i: (i // per_b, i % per_b, 0)),
            pl.BlockSpec((dm, n), lambda i: (0, 0)),
            pl.BlockSpec((1, n), lambda i: (0, 0)),
            pl.BlockSpec((1, n), lambda i: (0, 0)),
        ],
        out_specs=[pl.BlockSpec((1, d, tm // d, n), lambda i: (i // per_b, 0, i % per_b, 0)) for d in dils],
        out_shape=[jax.ShapeDtypeStruct((bsz, d, seq // d, n), _BF16) for d in dils],
        scratch_shapes=[pltpu.VMEM((n // LANES, tm, LANES), _F32)],
        compiler_params=_params("parallel"),
        name="in_proj_a",
    )(x, w_bf16, b.reshape(1, n), colscale.reshape(1, n))


def _dilated_bias_table(dilation):
    slopes = _alibi_slopes()[list(ALIBI_IDX_A)]
    band = Q_BLOCK + 2 * DIL_HALO
    qi = np.arange(Q_BLOCK)[:, None]
    kj = np.arange(band)[None, :]
    rel = qi - kj + DIL_HALO
    in_band = np.abs(rel) <= DIL_HALO
    base = -slopes[:, None, None] * (dilation * np.abs(rel)).astype(np.float32)[None]
    base = (base.astype(np.float64) * LOG2E).astype(np.float32)
    edge = (np.ones_like(kj, bool), kj >= DIL_HALO, kj < band - DIL_HALO)
    out = np.stack([np.where(in_band & e, base, np.float32(MASK_VALUE)) for e in edge])
    return out.astype(np.float32)


def _dilated_kernel(q_ref, km_ref, kp_ref, kn_ref, vm_ref, vp_ref, vn_ref, bias_ref, o_ref, lse_ref,
                    kbuf, vbuf, *, nblk):
    i = pl.program_id(2)
    h0, h1 = DIL_HALO, DIL_HALO + DIL_ROWS
    kbuf[0:h0, :] = kp_ref[0, 0]
    kbuf[h0:h1, :] = km_ref[0, 0]
    kbuf[h1:h1 + DIL_HALO, :] = kn_ref[0, 0]
    vbuf[0:h0, :] = vp_ref[0, 0]
    vbuf[h0:h1, :] = vm_ref[0, 0]
    vbuf[h1:h1 + DIL_HALO, :] = vn_ref[0, 0]

    lane = lax.broadcasted_iota(jnp.int32, (Q_BLOCK, LANES), 1)
    low_half = lane < HEAD_DIM
    band = Q_BLOCK + 2 * DIL_HALO
    variants = (jnp.where(i == 0, 1, 0), jnp.where(i == nblk - 1, 2, 0))

    for j in range(DIL_ROWS // Q_BLOCK):
        rows = slice(j * Q_BLOCK, (j + 1) * Q_BLOCK)
        krows = slice(j * Q_BLOCK, j * Q_BLOCK + band)
        lse_cols = []
        for hp in range(N_HEADS_A // 2):
            cols = slice(hp * LANES, (hp + 1) * LANES)
            q_pair = q_ref[0, 0, rows, cols]
            k_pair = kbuf[krows, cols]
            v_pair = vbuf[krows, cols]
            halves = []
            for hh in range(2):
                head = 2 * hp + hh
                keep = low_half if hh == 0 else jnp.logical_not(low_half)
                qm = jnp.where(keep, q_pair, jnp.zeros_like(q_pair))
                s = lax.dot_general(qm, k_pair, (((1,), (1,)), ((), ())), preferred_element_type=_F32)
                s = s + bias_ref[variants[j], head]
                m = jnp.max(s, axis=-1, keepdims=True)
                p = jnp.exp2(s - m)
                z = jnp.sum(p, axis=-1, keepdims=True)
                halves.append(jnp.dot(p.astype(_BF16), v_pair, preferred_element_type=_F32) * (1.0 / z))
                lse_cols.append(m + jnp.log2(z))
            o_ref[0, 0, rows, cols] = jnp.where(low_half, halves[0], halves[1])
        tile = jnp.zeros((Q_BLOCK, LANES), _F32)
        for head in range(N_HEADS_A):
            tile = jnp.where(lane == head, lse_cols[head], tile)
        lse_ref[0, 0, rows, :] = tile


def _dilated_pass(qkv, dilation):
    bsz, d, L, _ = qkv.shape
    assert d == dilation and L % DIL_ROWS == 0 and DIL_ROWS % DIL_HALO == 0
    nblk = L // DIL_ROWS
    per = DIL_ROWS // DIL_HALO
    n_halo = L // DIL_HALO
    bias = jnp.asarray(_dilated_bias_table(dilation))

    def main(s):
        return pl.BlockSpec((1, 1, DIL_ROWS, WIDTH_A), lambda b, r, i: (b, r, i, s))

    def left(s):
        return pl.BlockSpec((1, 1, DIL_HALO, WIDTH_A), lambda b, r, i: (b, r, jnp.maximum(per * i - 1, 0), s))

    def right(s):
        return pl.BlockSpec((1, 1, DIL_HALO, WIDTH_A),
                            lambda b, r, i: (b, r, jnp.minimum(per * i + per, n_halo - 1), s))

    rows_buf = DIL_ROWS + 2 * DIL_HALO
    return pl.pallas_call(
        functools.partial(_dilated_kernel, nblk=nblk),
        grid=(bsz, dilation, nblk),
        in_specs=[main(0), main(1), left(1), right(1), main(2), left(2), right(2),
                  pl.BlockSpec(bias.shape, lambda b, r, i: (0, 0, 0, 0))],
        out_specs=[pl.BlockSpec((1, 1, DIL_ROWS, WIDTH_A), lambda b, r, i: (b, r, i, 0)),
                   pl.BlockSpec((1, 1, DIL_ROWS, LANES), lambda b, r, i: (b, r, i, 0))],
        out_shape=[jax.ShapeDtypeStruct((bsz, dilation, L, WIDTH_A), _F32),
                   jax.ShapeDtypeStruct((bsz, dilation, L, LANES), _F32)],
        scratch_shapes=[pltpu.VMEM((rows_buf, WIDTH_A), _BF16), pltpu.VMEM((rows_buf, WIDTH_A), _BF16)],
        compiler_params=_params("parallel", "parallel", "arbitrary"),
        name=f"dilated_d{dilation}",
    )(qkv, qkv, qkv, qkv, qkv, qkv, qkv, bias)


def _split_bf16(x):
    hi = x.astype(jnp.bfloat16)
    lo = (x - hi.astype(np.float64)).astype(jnp.bfloat16)
    return hi, lo


def _diff_tables():
    s2 = _alibi_slopes()[list(ALIBI_IDX_B)].astype(np.float64) * LOG2E
    qaug = np.zeros((N_HEADS_B, DIFF_TQ, LANES), jnp.bfloat16)
    kaug = np.zeros((N_HEADS_B, 2, DIFF_TK, LANES), jnp.bfloat16)
    i = np.arange(DIFF_TQ, dtype=np.float64)
    j = np.arange(DIFF_TK, dtype=np.float64)
    for h in range(N_HEADS_B):
        r_hi, r_lo = _split_bf16(-s2[h] * i)
        qaug[h, :, 0] = 1.0
        qaug[h, :, 1] = 1.0
        qaug[h, :, 2] = r_hi
        qaug[h, :, 3] = r_lo
        for side, sign in enumerate((1.0, -1.0)):
            c_hi, c_lo = _split_bf16(sign * s2[h] * j)
            kaug[h, side, :, 0] = c_hi
            kaug[h, side, :, 1] = c_lo
            kaug[h, side, :, 2] = sign
            kaug[h, side, :, 3] = sign
    halves = DIFF_TK // DIFF_TQ
    off = (np.arange(halves) * DIFF_TQ)[:, None, None]
    dist = np.abs(off + i[None, :, None] - j[None, None, :])
    diag = (-s2[:, None, None, None] * dist[None]).astype(np.float32)
    return jnp.asarray(qaug), jnp.asarray(kaug), jnp.asarray(diag), jnp.asarray(s2.astype(np.float32))


def _diff_kernel(slope_ref, lam_ref, q_ref, k_ref, v_ref, qaug_ref, kaug_ref, diag_ref, g_ref, o_ref,
                 *, seq, lambda_init):
    head = pl.program_id(1)
    qb = pl.program_id(2)
    q0 = qb * DIFF_TQ
    halves = DIFF_TK // DIFF_TQ
    kd = qb // halves
    n_kb = seq // DIFF_TK
    slope2 = slope_ref[head]
    nt = (((1,), (1,)), ((), ()))

    q = q_ref[0]
    lane = lax.broadcasted_iota(jnp.int32, q.shape, 1)
    zero = jnp.zeros_like(q)
    qaug = qaug_ref[0]
    q_maps = (jnp.concatenate([jnp.where(lane < HEAD_DIM, q, zero), qaug], axis=1),
              jnp.concatenate([jnp.where(lane >= HEAD_DIM, q, zero), qaug], axis=1))
    zero_aug = jnp.zeros((DIFF_TK, LANES), _BF16)
    vlane = lax.broadcasted_iota(jnp.int32, (DIFF_TK, LANES), 1)
    ones_col = jnp.where(vlane == 0, 1.0, 0.0).astype(_BF16)

    def load(kb):
        k0 = pl.multiple_of(kb * DIFF_TK, DIFF_TK)
        return k_ref[0, pl.ds(k0, DIFF_TK), :], jnp.concatenate([v_ref[0, pl.ds(k0, DIFF_TK), :], ones_col], axis=1)

    def split(pv):
        return pv[:, LANES:LANES + 1], pv[:, 0:LANES]

    def update(s, const, vblk, carry):
        m_old, l_old, acc = carry
        m_new = jnp.maximum(m_old, jnp.max(s, axis=-1, keepdims=True) + const)
        alpha = jnp.exp2(m_old - m_new)
        p = jnp.exp2(s - (m_new - const))
        l_blk, o_blk = split(jnp.dot(p.astype(_BF16), vblk, preferred_element_type=_F32))
        return m_new, alpha * l_old + l_blk, alpha * acc + o_blk

    kblk, vblk = load(kd)
    k_all = jnp.concatenate([kblk, zero_aug], axis=1)
    bias = diag_ref[0, qb % halves]
    carries = []
    for mp in range(2):
        s = lax.dot_general(q_maps[mp], k_all, nt, preferred_element_type=_F32) + bias
        m = jnp.max(s, axis=-1, keepdims=True)
        p = jnp.exp2(s - m)
        carries.append((m,) + split(jnp.dot(p.astype(_BF16), vblk, preferred_element_type=_F32)))

    for t in range(n_kb - 1):
        kb = jnp.where(t >= kd, t + 1, t)
        side = jnp.where(t >= kd, 1, 0)
        kblk, vblk = load(kb)
        k_all = jnp.concatenate([kblk, kaug_ref[0, side]], axis=1)
        const = -slope2 * jnp.abs(q0 - kb * DIFF_TK).astype(_F32)
        for mp in range(2):
            s = lax.dot_general(q_maps[mp], k_all, nt, preferred_element_type=_F32)
            carries[mp] = update(s, const, vblk, carries[mp])

    lv = lam_ref[...]
    lam = (jnp.exp(jnp.sum(lv[0:1] * lv[1:2], axis=-1, keepdims=True))
           - jnp.exp(jnp.sum(lv[2:3] * lv[3:4], axis=-1, keepdims=True)) + lambda_init)
    (_, l1, a1), (_, l2, a2) = carries
    of = a1 * (1.0 / l1) - lam * (a2 * (1.0 / l2))
    of = of * lax.rsqrt(jnp.mean(of * of, axis=-1, keepdims=True) + SUBLN_EPS)
    of = of * g_ref[...] * (1.0 - lambda_init)
    o_ref[0] = of.astype(o_ref.dtype)


def _diff_attention(proj, q_offset, lam_vecs, subln_g, lambda_init):
    bsz, seq, _ = proj.shape
    assert seq % DIFF_TQ == 0 and seq % DIFF_TK == 0 and q_offset % LANES == 0
    qcol = q_offset // LANES
    kcol = qcol + WIDTH_B // LANES
    vcol = kcol + WIDTH_B // LANES
    assert DIFF_TK % DIFF_TQ == 0
    halves = DIFF_TK // DIFF_TQ
    qaug, kaug, diag, slopes2 = _diff_tables()
    return pl.pallas_call(
        functools.partial(_diff_kernel, seq=seq, lambda_init=lambda_init),
        grid=(bsz, N_HEADS_B, seq // DIFF_TQ),
        in_specs=[
            pl.BlockSpec(memory_space=pltpu.SMEM),
            pl.BlockSpec((4, HEAD_DIM), lambda b, h, i: (0, 0)),
            pl.BlockSpec((1, DIFF_TQ, LANES), lambda b, h, i: (b, i, qcol + h)),
            pl.BlockSpec((1, seq, LANES), lambda b, h, i: (b, 0, kcol + h)),
            pl.BlockSpec((1, seq, LANES), lambda b, h, i: (b, 0, vcol + h)),
            pl.BlockSpec((1, DIFF_TQ, LANES), lambda b, h, i: (h, 0, 0)),
            pl.BlockSpec((1, 2, DIFF_TK, LANES), lambda b, h, i: (h, 0, 0, 0)),
            pl.BlockSpec((1, halves, DIFF_TQ, DIFF_TK), lambda b, h, i: (h, 0, 0, 0)),
            pl.BlockSpec((1, LANES), lambda b, h, i: (0, 0)),
        ],
        out_specs=pl.BlockSpec((1, DIFF_TQ, LANES), lambda b, h, i: (b, i, h)),
        out_shape=jax.ShapeDtypeStruct((bsz, seq, WIDTH_B), _BF16),
        compiler_params=_params("parallel", "parallel", "arbitrary"),
        name="diff_attn",
    )(slopes2, lam_vecs, proj, proj, proj, qaug, kaug, diag, subln_g.reshape(1, LANES).astype(_F32))


def _layer_norm(z, g, b):
    mu = jnp.mean(z, axis=-1, keepdims=True)
    zc = z - mu
    var = jnp.mean(zc * zc, axis=-1, keepdims=True)
    return zc * lax.rsqrt(var + LN_EPS) * g + b


def _mix_patterns(o_refs, lse_refs, expand_ref, o_nat, lse_nat):
    n_chunks, tm = o_nat.shape[1], o_nat.shape[2]
    outs, lses = [], []
    for p, (_, d) in enumerate(DILATED_PATTERNS):
        if d == 1:
            outs.append(o_refs[p][0, 0])
            lses.append(lse_refs[p][0, 0])
            continue
        for r in range(d):
            for c in range(n_chunks):
                o_nat[p, c, pl.ds(r, tm // d, stride=d), :] = o_refs[p][0, r, :, c * LANES:(c + 1) * LANES]
            lse_nat[p, pl.ds(r, tm // d, stride=d), :] = lse_refs[p][0, r]
        outs.append(jnp.concatenate([o_nat[p, c] for c in range(n_chunks)], axis=1))
        lses.append(lse_nat[p])
    top = functools.reduce(jnp.maximum, lses)
    es = [jnp.exp2(l - top) for l in lses]
    inv = 1.0 / functools.reduce(lambda a, b: a + b, es)
    mixed = None
    for o, e in zip(outs, es):
        w = e * inv
        w_hi = w.astype(_BF16)
        w_lo = (w - w_hi.astype(_F32)).astype(_BF16)
        w_full = (jnp.dot(w_hi, expand_ref[...], preferred_element_type=_F32)
                  + jnp.dot(w_lo, expand_ref[...], preferred_element_type=_F32))
        mixed = w_full * o if mixed is None else mixed + w_full * o
    return mixed


def _merge_kernel(x_ref, ga_ref, gb_ref, o1_ref, o2_ref, o3_ref, l1_ref, l2_ref, l3_ref, expand_ref, ob_ref,
                  wpa_ref, wpb_ref, wout_ref, bout_ref, g1_ref, b1_ref, wrh_ref, wrl_ref, br_ref,
                  h_ref, idx_ref, gate_ref, rank_ref, cnt_ref, o_nat, lse_nat, *, alpha):
    o_a = _mix_patterns((o1_ref, o2_ref, o3_ref), (l1_ref, l2_ref, l3_ref), expand_ref, o_nat, lse_nat)
    pa = jnp.dot(o_a.astype(_BF16), wpa_ref[...], preferred_element_type=_F32)
    pb = jnp.dot(ob_ref[...], wpb_ref[...], preferred_element_type=_F32)
    merged = jax.nn.sigmoid(ga_ref[...].astype(_F32)) * pa + jax.nn.sigmoid(gb_ref[...].astype(_F32)) * pb
    y = jnp.dot(merged.astype(_BF16), wout_ref[...], preferred_element_type=_F32) + bout_ref[...]
    h = _layer_norm(alpha * x_ref[...] + y, g1_ref[...], b1_ref[...])
    h_ref[...] = h

    h_hi = h.astype(_BF16)
    h_lo = (h - h_hi.astype(_F32)).astype(_BF16)
    nt = (((1,), (1,)), ((), ()))
    logits = (lax.dot_general(wrh_ref[...], h_hi, nt, preferred_element_type=_F32)
              + lax.dot_general(wrl_ref[...], h_hi, nt, preferred_element_type=_F32)
              + lax.dot_general(wrh_ref[...], h_lo, nt, preferred_element_type=_F32)
              + br_ref[...])

    tm = logits.shape[1]
    expert = lax.broadcasted_iota(jnp.int32, logits.shape, 0)
    work = logits
    vals, sels, idxs = [], [], []
    for _ in range(TOP_K):
        mx = jnp.max(work, axis=0, keepdims=True)
        idx = jnp.min(jnp.where(work == mx, expert, N_EXPERTS), axis=0, keepdims=True)
        sel = expert == idx
        work = jnp.where(sel, -jnp.inf, work)
        vals.append(mx)
        idxs.append(idx)
        sels.append(sel)
    ex = [jnp.exp(v - vals[0]) for v in vals]
    inv = 1.0 / (ex[0] + ex[1] + ex[2] + ex[3])
    gates = [e * inv for e in ex]

    chosen = sels[0] | sels[1] | sels[2] | sels[3]
    onehot = jnp.where(chosen, 1.0, 0.0).astype(_BF16)
    before = (lax.broadcasted_iota(jnp.int32, (tm, tm), 0)
              < lax.broadcasted_iota(jnp.int32, (tm, tm), 1))
    upper = jnp.where(before, 1.0, 0.0).astype(_BF16)
    prefix = jnp.dot(onehot, upper, preferred_element_type=_F32)
    for k in range(TOP_K):
        rank = jnp.sum(jnp.where(sels[k], prefix, 0.0), axis=0, keepdims=True)
        rank_ref[k:k + 1, :] = rank.astype(jnp.int32)
        idx_ref[k:k + 1, :] = idxs[k]
        gate_ref[k:k + 1, :] = gates[k]
    cnt_ref[0] = jnp.broadcast_to(jnp.sum(onehot.astype(_F32), axis=1, keepdims=True), cnt_ref.shape[1:])


def _merge_router(x2d, proj2d, gate_offset, dil_outs, dil_lses, o_b, wpa, wpb, wout, bout, g1, b1,
                  w_router, b_router, alpha):
    t, dm = x2d.shape
    tm = MERGE_TM
    bsz, _, seq_over_d0, _ = dil_outs[0].shape
    seq = seq_over_d0 * DILATED_PATTERNS[0][1]
    per_b = seq // tm
    dils = [d for _, d in DILATED_PATTERNS]
    assert gate_offset % dm == 0 and seq % tm == 0 and all(tm % d == 0 for d in dils)
    gcol = gate_offset // dm
    wr_t = w_router.T.astype(_F32)
    wr_hi = wr_t.astype(_BF16)
    wr_lo = (wr_t - wr_hi.astype(_F32)).astype(_BF16)
    expand = np.zeros((LANES, WIDTH_A), np.float32)
    for head in range(N_HEADS_A):
        expand[head, head * HEAD_DIM:(head + 1) * HEAD_DIM] = 1.0
    expand = jnp.asarray(expand, _BF16)

    def const(shape):
        return pl.BlockSpec(shape, lambda i: tuple(0 for _ in shape))

    def residue_major(d, width):
        return pl.BlockSpec((1, d, tm // d, width), lambda i: (i // per_b, 0, i % per_b, 0))

    return pl.pallas_call(
        functools.partial(_merge_kernel, alpha=alpha),
        grid=(t // tm,),
        in_specs=[
            pl.BlockSpec((tm, dm), lambda i: (i, 0)),
            pl.BlockSpec((tm, dm), lambda i: (i, gcol)),
            pl.BlockSpec((tm, dm), lambda i: (i, gcol + 1)),
            *[residue_major(d, WIDTH_A) for d in dils],
            *[residue_major(d, LANES) for d in dils],
            const((LANES, WIDTH_A)),
            pl.BlockSpec((tm, WIDTH_B), lambda i: (i, 0)),
            const((WIDTH_A, dm)), const((WIDTH_B, dm)), const((dm, dm)), const((1, dm)),
            const((1, dm)), const((1, dm)),
            const((N_EXPERTS, dm)), const((N_EXPERTS, dm)), const((N_EXPERTS, 1)),
        ],
        out_specs=[
            pl.BlockSpec((tm, dm), lambda i: (i, 0)),
            pl.BlockSpec((TOP_K, tm), lambda i: (0, i)),
            pl.BlockSpec((TOP_K, tm), lambda i: (0, i)),
            pl.BlockSpec((TOP_K, tm), lambda i: (0, i)),
            pl.BlockSpec((1, N_EXPERTS, LANES), lambda i: (i, 0, 0)),
        ],
        out_shape=[
            jax.ShapeDtypeStruct((t, dm), _F32),
            jax.ShapeDtypeStruct((TOP_K, t), jnp.int32),
            jax.ShapeDtypeStruct((TOP_K, t), _F32),
            jax.ShapeDtypeStruct((TOP_K, t), jnp.int32),
            jax.ShapeDtypeStruct((t // tm, N_EXPERTS, LANES), _F32),
        ],
        scratch_shapes=[pltpu.VMEM((len(dils), WIDTH_A // LANES, tm, LANES), _F32),
                        pltpu.VMEM((len(dils), tm, LANES), _F32)],
        compiler_params=_params("parallel"),
        name="merge_router",
    )(x2d, proj2d, proj2d, *dil_outs, *dil_lses, expand, o_b, wpa, wpb, wout, bout.reshape(1, dm),
      g1.reshape(1, dm), b1.reshape(1, dm), wr_hi, wr_lo, b_router.reshape(N_EXPERTS, 1).astype(_F32))


def _chunk_count(n, chunk):
    return lax.shift_right_logical(n + (chunk - 1), int(math.log2(chunk)))


def _dispatch_kernel(cnt_ref, base_ref, off_ref, tail_ref, end_ref, lpos_ref, gate_ref, h_ref, xs_hbm,
                     local, zeros, pending, sem, *, tm, dm):
    i = pl.program_id(0)
    rows = local.shape[1]
    buf = i % 2

    @pl.when(i == 0)
    def _():
        pending[0] = 0
    slot = lax.broadcasted_iota(jnp.int32, (rows, tm), 0)
    perm = jnp.zeros((rows, tm), _F32)
    gsel = jnp.zeros((rows, tm), _F32)
    for k in range(TOP_K):
        hit = slot == lpos_ref[k:k + 1, :]
        perm = perm + jnp.where(hit, 1.0, 0.0)
        gsel = gsel + jnp.where(hit, gate_ref[k:k + 1, :], 0.0)
    local[buf, :, 0:dm] = jnp.dot(perm.astype(_BF16), h_ref[...].astype(_BF16), preferred_element_type=_F32)
    local[buf, :, dm:dm + LANES] = jnp.broadcast_to(jnp.sum(gsel, axis=1, keepdims=True), (rows, LANES))

    def chunk_copy(src_row, dst_row):
        return pltpu.make_async_copy(local.at[buf, pl.ds(pl.multiple_of(src_row, ROW_ALIGN), DISP_CHUNK)],
                                     xs_hbm.at[pl.ds(pl.multiple_of(dst_row, ROW_ALIGN), DISP_CHUNK)], sem)

    def drain(c, carry):
        chunk_copy(0, 0).wait()
        return carry

    lax.fori_loop(0, pending[0], drain, 0)

    def per_expert(e, total):
        j = i * N_EXPERTS + e
        n_chunks = _chunk_count(cnt_ref[j], DISP_CHUNK)

        def issue(c, carry):
            chunk_copy(off_ref[j] + c * DISP_CHUNK, base_ref[j] + c * DISP_CHUNK).start()
            return carry

        lax.fori_loop(0, n_chunks, issue, 0)
        return total + n_chunks

    pending[0] = lax.fori_loop(0, N_EXPERTS, per_expert, 0)

    @pl.when(i == pl.num_programs(0) - 1)
    def _():
        lax.fori_loop(0, pending[0], drain, 0)
        zeros[...] = jnp.zeros_like(zeros)

        def fill(e, carry):
            gap = end_ref[e] - tail_ref[e]
            pos = tail_ref[e]
            size = MOE_ROWS
            while size >= ROW_ALIGN:
                take = (gap & size) != 0
                cp = pltpu.make_async_copy(zeros.at[pl.ds(0, size)],
                                           xs_hbm.at[pl.ds(pl.multiple_of(pos, ROW_ALIGN), size)], sem)

                @pl.when(take)
                def _():
                    cp.start()
                    cp.wait()

                pos = pos + jnp.where(take, size, 0)
                size //= 2
            return carry

        lax.fori_loop(0, N_EXPERTS, fill, 0)

        def fill_block(blk, carry):
            cp = pltpu.make_async_copy(
                zeros, xs_hbm.at[pl.ds(pl.multiple_of(blk * MOE_ROWS, MOE_ROWS), MOE_ROWS)], sem)
            cp.start()
            cp.wait()
            return carry

        first_unused = lax.shift_right_logical(end_ref[N_EXPERTS - 1], int(math.log2(MOE_ROWS)))
        lax.fori_loop(first_unused, xs_hbm.shape[0] // MOE_ROWS, fill_block, 0)


def _dispatch(h, gates, plan):
    t, dm = h.shape
    tm = MERGE_TM
    width = dm + LANES
    grid_spec = pltpu.PrefetchScalarGridSpec(
        num_scalar_prefetch=5,
        grid=(t // tm,),
        in_specs=[
            pl.BlockSpec((TOP_K, tm), lambda i, *_: (0, i)),
            pl.BlockSpec((TOP_K, tm), lambda i, *_: (0, i)),
            pl.BlockSpec((tm, dm), lambda i, *_: (i, 0)),
        ],
        out_specs=pl.BlockSpec(memory_space=pl.ANY),
        scratch_shapes=[pltpu.VMEM((2, DISP_LOCAL_ROWS, width), _F32), pltpu.VMEM((MOE_ROWS, width), _F32),
                        pltpu.SMEM((1,), jnp.int32), pltpu.SemaphoreType.DMA(())],
    )
    return pl.pallas_call(
        functools.partial(_dispatch_kernel, tm=tm, dm=dm),
        grid_spec=grid_spec,
        out_shape=jax.ShapeDtypeStruct((plan["n_rows"], width), _F32),
        compiler_params=_params("arbitrary"),
        name="moe_dispatch",
    )(plan["cnt"], plan["base"], plan["off_d"], plan["tail"], plan["end"], plan["lpos_d"], gates, h)


def _expert_kernel(blk_e_ref, blk_src_ref, n_used_ref, xs_ref, wup_ref, bup_ref, wdn_ref, bdn_ref, ys_ref,
                   *, d_expert):
    del blk_e_ref, blk_src_ref

    dm = ys_ref.shape[1]

    @pl.when(pl.program_id(0) < n_used_ref[0])
    def _():
        x = xs_ref[:, 0:dm].astype(_BF16)
        row_gate = xs_ref[:, dm:dm + 1]
        acc = jnp.zeros(ys_ref.shape, _F32)
        for c in range(d_expert // FFN_CHUNK):
            lo, hi = c * FFN_CHUNK, (c + 1) * FFN_CHUNK
            g = jnp.dot(x, wup_ref[0, :, lo:hi], preferred_element_type=_F32) + bup_ref[0, :, lo:hi]
            u = (jnp.dot(x, wup_ref[0, :, d_expert + lo:d_expert + hi], preferred_element_type=_F32)
                 + bup_ref[0, :, d_expert + lo:d_expert + hi])
            gate = jnp.minimum(g, SWIGLU_LIMIT)
            up = jnp.clip(u, -SWIGLU_LIMIT, SWIGLU_LIMIT)
            act = gate * jax.nn.sigmoid(SWIGLU_ALPHA * gate) * (up + 1.0)
            acc = acc + jnp.dot(act.astype(_BF16), wdn_ref[0, lo:hi, :], preferred_element_type=_F32)
        ys_ref[...] = ((acc + bdn_ref[0]) * row_gate).astype(_BF16).astype(_F32)

    @pl.when(pl.program_id(0) >= n_used_ref[0])
    def _():
        ys_ref[...] = jnp.zeros_like(ys_ref)


def _experts(xs, blk_e, blk_src, n_used, w_up, b_up, w_down, b_down):
    n_rows, width = xs.shape
    dm = width - LANES
    n_blocks = n_rows // MOE_ROWS
    n_exp, _, two_de = w_up.shape
    d_expert = two_de // 2
    grid_spec = pltpu.PrefetchScalarGridSpec(
        num_scalar_prefetch=3,
        grid=(n_blocks,),
        in_specs=[
            pl.BlockSpec((MOE_ROWS, width), lambda i, be, bs, nu: (bs[i], 0)),
            pl.BlockSpec((1, dm, two_de), lambda i, be, bs, nu: (be[i], 0, 0)),
            pl.BlockSpec((1, 1, two_de), lambda i, be, bs, nu: (be[i], 0, 0)),
            pl.BlockSpec((1, d_expert, dm), lambda i, be, bs, nu: (be[i], 0, 0)),
            pl.BlockSpec((1, 1, dm), lambda i, be, bs, nu: (be[i], 0, 0)),
        ],
        out_specs=pl.BlockSpec((MOE_ROWS, dm), lambda i, be, bs, nu: (i, 0)),
    )
    return pl.pallas_call(
        functools.partial(_expert_kernel, d_expert=d_expert),
        grid_spec=grid_spec,
        out_shape=jax.ShapeDtypeStruct((n_rows, dm), _F32),
        compiler_params=_params("arbitrary"),
        name="moe_experts",
    )(blk_e, blk_src, n_used, xs, w_up, b_up.reshape(n_exp, 1, two_de), w_down, b_down.reshape(n_exp, 1, dm))


def _combine_kernel(cnt_ref, base_ref, off_ref, lpos_ref, h_ref, g2_ref, b2_ref, ys_hbm, o_ref, local, sems,
                    *, tm, alpha):
    i = pl.program_id(0)
    n_tiles = pl.num_programs(0)
    rows = local.shape[1]

    def chunk_copy(buf, src_row, dst_row):
        return pltpu.make_async_copy(ys_hbm.at[pl.ds(pl.multiple_of(src_row, ROW_ALIGN), COMB_CHUNK)],
                                     local.at[buf, pl.ds(pl.multiple_of(dst_row, ROW_ALIGN), COMB_CHUNK)],
                                     sems.at[buf])

    def for_each_chunk(tile, fn):
        def per_expert(e, carry):
            j = tile * N_EXPERTS + e

            def per_chunk(c, carry):
                fn(base_ref[j] + c * COMB_CHUNK, off_ref[j] + c * COMB_CHUNK)
                return carry

            return lax.fori_loop(0, _chunk_count(cnt_ref[j], COMB_CHUNK), per_chunk, carry)

        lax.fori_loop(0, N_EXPERTS, per_expert, 0)

    def fetch(tile):
        for_each_chunk(tile, lambda src, dst: chunk_copy(tile % 2, src, dst).start())

    @pl.when(i == 0)
    def _():
        local[...] = jnp.zeros_like(local)
        fetch(i)

    @pl.when(i + 1 < n_tiles)
    def _():
        fetch(i + 1)

    slot = lax.broadcasted_iota(jnp.int32, (tm, rows), 1)
    pick = jnp.zeros((tm, rows), _F32)
    for k in range(TOP_K):
        pick = pick + jnp.where(slot == lpos_ref[:, k:k + 1], 1.0, 0.0)

    for_each_chunk(i, lambda src, dst: chunk_copy(i % 2, 0, 0).wait())
    y = jnp.dot(pick.astype(_BF16), local[i % 2].astype(_BF16), preferred_element_type=_F32)
    o_ref[...] = _layer_norm(alpha * h_ref[...] + y, g2_ref[...], b2_ref[...])


def _combine(h, ys, plan, g2, b2, alpha):
    t, dm = h.shape
    tm = MERGE_TM
    grid_spec = pltpu.PrefetchScalarGridSpec(
        num_scalar_prefetch=3,
        grid=(t // tm,),
        in_specs=[
            pl.BlockSpec((tm, TOP_K), lambda i, *_: (i, 0)),
            pl.BlockSpec((tm, dm), lambda i, *_: (i, 0)),
            pl.BlockSpec((1, dm), lambda i, *_: (0, 0)),
            pl.BlockSpec((1, dm), lambda i, *_: (0, 0)),
            pl.BlockSpec(memory_space=pl.ANY),
        ],
        out_specs=pl.BlockSpec((tm, dm), lambda i, *_: (i, 0)),
        scratch_shapes=[pltpu.VMEM((2, COMB_LOCAL_ROWS, dm), _F32), pltpu.SemaphoreType.DMA((2,))],
    )
    return pl.pallas_call(
        functools.partial(_combine_kernel, tm=tm, alpha=alpha),
        grid_spec=grid_spec,
        out_shape=jax.ShapeDtypeStruct((t, dm), _F32),
        compiler_params=_params("arbitrary"),
        name="moe_combine",
    )(plan["cnt"], plan["base"], plan["off_c"], plan["lpos_c"], h, g2.reshape(1, dm), b2.reshape(1, dm), ys)


def _round_up(x, m):
    return (x + m - 1) // m * m


def _moe_plan(idx, rank, tile_cnt, tm):
    n_tiles, n_exp = tile_cnt.shape
    t = idx.shape[1]
    cnt = tile_cnt.astype(jnp.int32)
    grp = _round_up(cnt, ROW_ALIGN)
    tot = jnp.sum(grp, axis=0)
    padded = jnp.where(tot > 0, _round_up(tot + DISP_CHUNK, MOE_ROWS), 0)
    pend = jnp.cumsum(padded)
    pstart = pend - padded
    base = pstart[None, :] + jnp.cumsum(grp, axis=0) - grp
    off_d = jnp.cumsum(grp, axis=1) - grp
    n_chunk_c = _round_up(cnt, COMB_CHUNK) // COMB_CHUNK
    off_c = (jnp.cumsum(n_chunk_c, axis=1) - n_chunk_c) * COMB_CHUNK
    cover = jnp.where(cnt > 0, base + _round_up(cnt, DISP_CHUNK), 0)
    tail = jnp.maximum(jnp.max(cover, axis=0), pstart)

    onehot = idx[..., None] == jnp.arange(n_exp, dtype=jnp.int32)

    def per_assignment(table):
        per_token = jnp.broadcast_to(table[:, None, :], (n_tiles, tm, n_exp)).reshape(t, n_exp)
        return jnp.sum(jnp.where(onehot, per_token[None], 0), axis=-1)

    n_rows = _round_up(t * idx.shape[0] + n_tiles * n_exp * (ROW_ALIGN - 1)
                       + n_exp * (DISP_CHUNK + MOE_ROWS - 1), MOE_ROWS)
    n_blocks = n_rows // MOE_ROWS
    blk_start = jnp.arange(n_blocks, dtype=jnp.int32) * MOE_ROWS
    blk_e = jnp.sum((pend[None, :] <= blk_start[:, None]).astype(jnp.int32), axis=1)
    blk_e = jnp.minimum(blk_e, n_exp - 1)
    n_used = (pend[-1] // MOE_ROWS).astype(jnp.int32)
    blk_src = jnp.minimum(jnp.arange(n_blocks, dtype=jnp.int32), n_used - 1)
    return dict(
        cnt=cnt.reshape(-1), base=base.reshape(-1).astype(jnp.int32),
        off_d=off_d.reshape(-1).astype(jnp.int32), off_c=off_c.reshape(-1).astype(jnp.int32),
        tail=tail.astype(jnp.int32), end=pend.astype(jnp.int32),
        lpos_d=(per_assignment(off_d) + rank).astype(jnp.int32),
        lpos_c=(per_assignment(off_c) + rank).astype(jnp.int32).T,
        blk_e=blk_e[blk_src], blk_src=blk_src, n_used=n_used.reshape(1), n_rows=n_rows)


def kernel(x, w_in, b_in, lambda_q1, lambda_k1, lambda_q2, lambda_k2, subln_g, w_proj_a, w_proj_b, w_out, b_out, ln1_g, ln1_b, w_router, b_router, w_up, b_up, w_down, b_down, ln2_g, ln2_b):
    bsz, seq, dm = x.shape
    depth = w_in.shape[0]
    alpha = (2.0 * depth) ** 0.25
    t = bsz * seq
    for layer in range(depth):
        lambda_init = 0.8 - 0.6 * math.exp(-0.3 * layer)
        x2d = x.reshape(t, dm)
        n_a, n_b = 3 * WIDTH_A, 3 * WIDTH_B
        w_l, b_l = w_in[layer], b_in[layer]
        query_scale = jnp.full((WIDTH_A,), QUERY_SCALE, _F32)
        ones = functools.partial(jnp.ones, dtype=_F32)
        qkv_a = _in_proj_a(x, w_l[:, :n_a].astype(_BF16), b_l[:n_a],
                           jnp.concatenate([query_scale, ones((2 * WIDTH_A,))]))
        w_rest = jnp.concatenate([w_l[:, n_a + n_b:], w_l[:, n_a:n_a + n_b]], axis=1).astype(_BF16)
        b_rest = jnp.concatenate([b_l[n_a + n_b:], b_l[n_a:n_a + n_b]])
        scale_rest = jnp.concatenate([ones((2 * dm,)), query_scale, ones((2 * WIDTH_B,))])
        proj2d = _in_proj(x2d, w_rest, b_rest, scale_rest)

        dil = [_dilated_pass(a, d) for a, (_, d) in zip(qkv_a, DILATED_PATTERNS)]

        lam_vecs = jnp.stack([lambda_q1[layer], lambda_k1[layer], lambda_q2[layer], lambda_k2[layer]]).astype(_F32)
        o_b = _diff_attention(proj2d.reshape(bsz, seq, -1), 2 * dm, lam_vecs, subln_g[layer],
                              lambda_init).reshape(t, WIDTH_B)

        h, idx, gates, rank, cnt = _merge_router(
            x2d, proj2d, 0, [o for o, _ in dil], [l for _, l in dil], o_b,
            w_proj_a[layer].astype(_BF16), w_proj_b[layer].astype(_BF16),
            w_out[layer].astype(_BF16), b_out[layer], ln1_g[layer], ln1_b[layer],
            w_router[layer], b_router[layer], alpha)

        plan = _moe_plan(idx, rank, cnt[:, :, 0], MERGE_TM)
        xs = _dispatch(h, gates, plan)
        ys = _experts(xs, plan["blk_e"], plan["blk_src"], plan["n_used"], w_up[layer].astype(_BF16),
                      b_up[layer], w_down[layer].astype(_BF16), b_down[layer])
        out = _combine(h, ys, plan, ln2_g[layer], ln2_b[layer], alpha)
        x = out.reshape(bsz, seq, dm)
    return x
```

```python
import functools
import math

import numpy as np
import jax
import jax.numpy as jnp
from jax import lax
from jax.experimental import pallas as pl
from jax.experimental.pallas import tpu as pltpu

HEAD_DIM = 64
N_HEADS_A = 8
DILATED_PATTERNS = ((128, 1), (512, 4), (2048, 16))
N_HEADS_B = 4
WIDTH_A = N_HEADS_A * HEAD_DIM
WIDTH_B = N_HEADS_B * 2 * HEAD_DIM
N_ALIBI_HEADS = N_HEADS_A + N_HEADS_B
ALIBI_IDX_A = (0, 1, 3, 4, 6, 7, 9, 10)
ALIBI_IDX_B = (2, 5, 8, 11)
Q_BLOCK = 128
MASK_VALUE = -1e30
N_EXPERTS = 32
TOP_K = 4
SWIGLU_ALPHA = 1.702
SWIGLU_LIMIT = 7.0
LN_EPS = 1e-5
SUBLN_EPS = 1e-5
LOG2E = math.log2(math.e)
QUERY_SCALE = HEAD_DIM ** -0.5 * LOG2E

LANES = 128
V7X_VMEM_LIMIT_BYTES = 56 * 1024 * 1024

PROJ_TM = 1024
PROJ_A_TM = 512
DIL_ROWS = 2 * Q_BLOCK
DIL_HALO = 64
DIFF_TQ = 512
DIFF_TK = 512
MERGE_TM = 512
MOE_ROWS = 512
FFN_CHUNK = 512
ROW_ALIGN = 8
DISP_CHUNK = 64
COMB_CHUNK = 32
DISP_LOCAL_ROWS = -(-(TOP_K * MERGE_TM + N_EXPERTS * (ROW_ALIGN - 1) + DISP_CHUNK) // LANES) * LANES
COMB_LOCAL_ROWS = -(-(TOP_K * MERGE_TM + N_EXPERTS * (COMB_CHUNK - 1)) // LANES) * LANES

_F32 = jnp.float32
_BF16 = jnp.bfloat16


def _params(*sem):
    return pltpu.CompilerParams(dimension_semantics=sem, vmem_limit_bytes=V7X_VMEM_LIMIT_BYTES)


def _alibi_slopes():
    return (2.0 ** (-8.0 * np.arange(1, N_ALIBI_HEADS + 1) / N_ALIBI_HEADS)).astype(np.float32)


def _in_proj_kernel(x_ref, w_ref, b_ref, cs_ref, o_ref):
    x = x_ref[...].astype(_BF16)
    acc = jnp.dot(x, w_ref[...], preferred_element_type=_F32)
    o_ref[...] = ((acc + b_ref[...]) * cs_ref[...]).astype(o_ref.dtype)


def _in_proj(x2d, w_bf16, b, colscale):
    t, dm = x2d.shape
    n = w_bf16.shape[1]
    tn = n // 2
    assert n % 2 == 0 and tn % LANES == 0
    return pl.pallas_call(
        _in_proj_kernel,
        grid=(t // PROJ_TM, n // tn),
        in_specs=[
            pl.BlockSpec((PROJ_TM, dm), lambda i, j: (i, 0)),
            pl.BlockSpec((dm, tn), lambda i, j: (0, j)),
            pl.BlockSpec((1, tn), lambda i, j: (0, j)),
            pl.BlockSpec((1, tn), lambda i, j: (0, j)),
        ],
        out_specs=pl.BlockSpec((PROJ_TM, tn), lambda i, j: (i, j)),
        out_shape=jax.ShapeDtypeStruct((t, n), _BF16),
        compiler_params=_params("parallel", "arbitrary"),
        name="in_proj",
    )(x2d, w_bf16, b.reshape(1, n), colscale.reshape(1, n))


def _in_proj_a_kernel(x_ref, w_ref, b_ref, cs_ref, *refs):
    out_refs, acc_ref = refs[:-1], refs[-1]
    x = x_ref[0].astype(_BF16)
    acc = (jnp.dot(x, w_ref[...], preferred_element_type=_F32) + b_ref[...]) * cs_ref[...]
    n_chunks, tm, _ = acc_ref.shape
    for c in range(n_chunks):
        acc_ref[c] = acc[:, c * LANES:(c + 1) * LANES]
    for o_ref, (_, d) in zip(out_refs, DILATED_PATTERNS):
        if d == 1:
            o_ref[0, 0] = acc.astype(o_ref.dtype)
            continue
        for r in range(d):
            for c in range(n_chunks):
                rows = acc_ref[c, pl.ds(r, tm // d, stride=d), :]
                o_ref[0, r, :, c * LANES:(c + 1) * LANES] = rows.astype(o_ref.dtype)


def _in_proj_a(x, w_bf16, b, colscale):
    bsz, seq, dm = x.shape
    n = w_bf16.shape[1]
    tm = PROJ_A_TM
    per_b = seq // tm
    dils = [d for _, d in DILATED_PATTERNS]
    assert seq % tm == 0 and all(tm % d == 0 and (tm // d) % 16 == 0 for d in dils)
    return pl.pallas_call(
        _in_proj_a_kernel,
        grid=(bsz * per_b,),
        in_specs=[
            pl.BlockSpec((1, tm, dm), lambda i: (i // per_b, i % per_b, 0)),
            pl.BlockSpec((dm, n), lambda i: (0, 0)),
            pl.BlockSpec((1, n), lambda i: (0, 0)),
            pl.BlockSpec((1, n), lambda i: (0, 0)),
        ],
        out_specs=[pl.BlockSpec((1, d, tm // d, n), lambda i: (i // per_b, 0, i % per_b, 0)) for d in dils],
        out_shape=[jax.ShapeDtypeStruct((bsz, d, seq // d, n), _BF16) for d in dils],
        scratch_shapes=[pltpu.VMEM((n // LANES, tm, LANES), _F32)],
        compiler_params=_params("parallel"),
        name="in_proj_a",
    )(x, w_bf16, b.reshape(1, n), colscale.reshape(1, n))


def _dilated_bias_table(dilation):
    slopes = _alibi_slopes()[list(ALIBI_IDX_A)]
    band = Q_BLOCK + 2 * DIL_HALO
    qi = np.arange(Q_BLOCK)[:, None]
    kj = np.arange(band)[None, :]
    rel = qi - kj + DIL_HALO
    in_band = np.abs(rel) <= DIL_HALO
    base = -slopes[:, None, None] * (dilation * np.abs(rel)).astype(np.float32)[None]
    base = (base.astype(np.float64) * LOG2E).astype(np.float32)
    edge = (np.ones_like(kj, bool), kj >= DIL_HALO, kj < band - DIL_HALO)
    out = np.stack([np.where(in_band & e, base, np.float32(MASK_VALUE)) for e in edge])
    return out.astype(np.float32)


def _dilated_kernel(q_ref, km_ref, kp_ref, kn_ref, vm_ref, vp_ref, vn_ref, bias_ref, o_ref, lse_ref,
                    kbuf, vbuf, *, nblk):
    i = pl.program_id(2)
    h0, h1 = DIL_HALO, DIL_HALO + DIL_ROWS
    kbuf[0:h0, :] = kp_ref[0, 0]
    kbuf[h0:h1, :] = km_ref[0, 0]
    kbuf[h1:h1 + DIL_HALO, :] = kn_ref[0, 0]
    vbuf[0:h0, :] = vp_ref[0, 0]
    vbuf[h0:h1, :] = vm_ref[0, 0]
    vbuf[h1:h1 + DIL_HALO, :] = vn_ref[0, 0]

    lane = lax.broadcasted_iota(jnp.int32, (Q_BLOCK, LANES), 1)
    low_half = lane < HEAD_DIM
    band = Q_BLOCK + 2 * DIL_HALO
    variants = (jnp.where(i == 0, 1, 0), jnp.where(i == nblk - 1, 2, 0))

    for j in range(DIL_ROWS // Q_BLOCK):
        rows = slice(j * Q_BLOCK, (j + 1) * Q_BLOCK)
        krows = slice(j * Q_BLOCK, j * Q_BLOCK + band)
        lse_cols = []
        for hp in range(N_HEADS_A // 2):
            cols = slice(hp * LANES, (hp + 1) * LANES)
            q_pair = q_ref[0, 0, rows, cols]
            k_pair = kbuf[krows, cols]
            v_pair = vbuf[krows, cols]
            halves = []
            for hh in range(2):
                head = 2 * hp + hh
                keep = low_half if hh == 0 else jnp.logical_not(low_half)
                qm = jnp.where(keep, q_pair, jnp.zeros_like(q_pair))
                s = lax.dot_general(qm, k_pair, (((1,), (1,)), ((), ())), preferred_element_type=_F32)
                s = s + bias_ref[variants[j], head]
                m = jnp.max(s, axis=-1, keepdims=True)
                p = jnp.exp2(s - m)
                z = jnp.sum(p, axis=-1, keepdims=True)
                halves.append(jnp.dot(p.astype(_BF16), v_pair, preferred_element_type=_F32) * (1.0 / z))
                lse_cols.append(m + jnp.log2(z))
            o_ref[0, 0, rows, cols] = jnp.where(low_half, halves[0], halves[1])
        tile = jnp.zeros((Q_BLOCK, LANES), _F32)
        for head in range(N_HEADS_A):
            tile = jnp.where(lane == head, lse_cols[head], tile)
        lse_ref[0, 0, rows, :] = tile


def _dilated_pass(qkv, dilation):
    bsz, d, L, _ = qkv.shape
    assert d == dilation and L % DIL_ROWS == 0 and DIL_ROWS % DIL_HALO == 0
    nblk = L // DIL_ROWS
    per = DIL_ROWS // DIL_HALO
    n_halo = L // DIL_HALO
    bias = jnp.asarray(_dilated_bias_table(dilation))

    def main(s):
        return pl.BlockSpec((1, 1, DIL_ROWS, WIDTH_A), lambda b, r, i: (b, r, i, s))

    def left(s):
        return pl.BlockSpec((1, 1, DIL_HALO, WIDTH_A), lambda b, r, i: (b, r, jnp.maximum(per * i - 1, 0), s))

    def right(s):
        return pl.BlockSpec((1, 1, DIL_HALO, WIDTH_A),
                            lambda b, r, i: (b, r, jnp.minimum(per * i + per, n_halo - 1), s))

    rows_buf = DIL_ROWS + 2 * DIL_HALO
    return pl.pallas_call(
        functools.partial(_dilated_kernel, nblk=nblk),
        grid=(bsz, dilation, nblk),
        in_specs=[main(0), main(1), left(1), right(1), main(2), left(2), right(2),
                  pl.BlockSpec(bias.shape, lambda b, r, i: (0, 0, 0, 0))],
        out_specs=[pl.BlockSpec((1, 1, DIL_ROWS, WIDTH_A), lambda b, r, i: (b, r, i, 0)),
                   pl.BlockSpec((1, 1, DIL_ROWS, LANES), lambda b, r, i: (b, r, i, 0))],
        out_shape=[jax.ShapeDtypeStruct((bsz, dilation, L, WIDTH_A), _F32),
                   jax.ShapeDtypeStruct((bsz, dilation, L, LANES), _F32)],
        scratch_shapes=[pltpu.VMEM((rows_buf, WIDTH_A), _BF16), pltpu.VMEM((rows_buf, WIDTH_A), _BF16)],
        compiler_params=_params("parallel", "parallel", "arbitrary"),
        name=f"dilated_d{dilation}",
    )(qkv, qkv, qkv, qkv, qkv, qkv, qkv, bias)


def _split_bf16(x):
    hi = x.astype(jnp.bfloat16)
    lo = (x - hi.astype(np.float64)).astype(jnp.bfloat16)
    return hi, lo


def _diff_tables():
    s2 = _alibi_slopes()[list(ALIBI_IDX_B)].astype(np.float64) * LOG2E
    qaug = np.zeros((N_HEADS_B, DIFF_TQ, LANES), jnp.bfloat16)
    kaug = np.zeros((N_HEADS_B, 2, DIFF_TK, LANES), jnp.bfloat16)
    i = np.arange(DIFF_TQ, dtype=np.float64)
    j = np.arange(DIFF_TK, dtype=np.float64)
    for h in range(N_HEADS_B):
        r_hi, r_lo = _split_bf16(-s2[h] * i)
        qaug[h, :, 0] = 1.0
        qaug[h, :, 1] = 1.0
        qaug[h, :, 2] = r_hi
        qaug[h, :, 3] = r_lo
        for side, sign in enumerate((1.0, -1.0)):
            c_hi, c_lo = _split_bf16(sign * s2[h] * j)
            kaug[h, side, :, 0] = c_hi
            kaug[h, side, :, 1] = c_lo
            kaug[h, side, :, 2] = sign
            kaug[h, side, :, 3] = sign
    halves = DIFF_TK // DIFF_TQ
    off = (np.arange(halves) * DIFF_TQ)[:, None, None]
    dist = np.abs(off + i[None, :, None] - j[None, None, :])
    diag = (-s2[:, None, None, None] * dist[None]).astype(np.float32)
    return jnp.asarray(qaug), jnp.asarray(kaug), jnp.asarray(diag), jnp.asarray(s2.astype(np.float32))


def _diff_kernel(slope_ref, lam_ref, q_ref, k_ref, v_ref, qaug_ref, kaug_ref, diag_ref, g_ref, o_ref,
                 *, seq, lambda_init):
    head = pl.program_id(1)
    qb = pl.program_id(2)
    q0 = qb * DIFF_TQ
    halves = DIFF_TK // DIFF_TQ
    kd = qb // halves
    n_kb = seq // DIFF_TK
    slope2 = slope_ref[head]
    nt = (((1,), (1,)), ((), ()))

    q = q_ref[0]
    lane = lax.broadcasted_iota(jnp.int32, q.shape, 1)
    zero = jnp.zeros_like(q)
    qaug = qaug_ref[0]
    q_maps = (jnp.concatenate([jnp.where(lane < HEAD_DIM, q, zero), qaug], axis=1),
              jnp.concatenate([jnp.where(lane >= HEAD_DIM, q, zero), qaug], axis=1))
    zero_aug = jnp.zeros((DIFF_TK, LANES), _BF16)
    vlane = lax.broadcasted_iota(jnp.int32, (DIFF_TK, LANES), 1)
    ones_col = jnp.where(vlane == 0, 1.0, 0.0).astype(_BF16)

    def load(kb):
        k0 = pl.multiple_of(kb * DIFF_TK, DIFF_TK)
        return k_ref[0, pl.ds(k0, DIFF_TK), :], jnp.concatenate([v_ref[0, pl.ds(k0, DIFF_TK), :], ones_col], axis=1)

    def split(pv):
        return pv[:, LANES:LANES + 1], pv[:, 0:LANES]

    def update(s, const, vblk, carry):
        m_old, l_old, acc = carry
        m_new = jnp.maximum(m_old, jnp.max(s, axis=-1, keepdims=True) + const)
        alpha = jnp.exp2(m_old - m_new)
        p = jnp.exp2(s - (m_new - const))
        l_blk, o_blk = split(jnp.dot(p.astype(_BF16), vblk, preferred_element_type=_F32))
        return m_new, alpha * l_old + l_blk, alpha * acc + o_blk

    kblk, vblk = load(kd)
    k_all = jnp.concatenate([kblk, zero_aug], axis=1)
    bias = diag_ref[0, qb % halves]
    carries = []
    for mp in range(2):
        s = lax.dot_general(q_maps[mp], k_all, nt, preferred_element_type=_F32) + bias
        m = jnp.max(s, axis=-1, keepdims=True)
        p = jnp.exp2(s - m)
        carries.append((m,) + split(jnp.dot(p.astype(_BF16), vblk, preferred_element_type=_F32)))

    for t in range(n_kb - 1):
        kb = jnp.where(t >= kd, t + 1, t)
        side = jnp.where(t >= kd, 1, 0)
        kblk, vblk = load(kb)
        k_all = jnp.concatenate([kblk, kaug_ref[0, side]], axis=1)
        const = -slope2 * jnp.abs(q0 - kb * DIFF_TK).astype(_F32)
        for mp in range(2):
            s = lax.dot_general(q_maps[mp], k_all, nt, preferred_element_type=_F32)
            carries[mp] = update(s, const, vblk, carries[mp])

    lv = lam_ref[...]
    lam = (jnp.exp(jnp.sum(lv[0:1] * lv[1:2], axis=-1, keepdims=True))
           - jnp.exp(jnp.sum(lv[2:3] * lv[3:4], axis=-1, keepdims=True)) + lambda_init)
    (_, l1, a1), (_, l2, a2) = carries
    of = a1 * (1.0 / l1) - lam * (a2 * (1.0 / l2))
    of = of * lax.rsqrt(jnp.mean(of * of, axis=-1, keepdims=True) + SUBLN_EPS)
    of = of * g_ref[...] * (1.0 - lambda_init)
    o_ref[0] = of.astype(o_ref.dtype)


def _diff_attention(proj, q_offset, lam_vecs, subln_g, lambda_init):
    bsz, seq, _ = proj.shape
    assert seq % DIFF_TQ == 0 and seq % DIFF_TK == 0 and q_offset % LANES == 0
    qcol = q_offset // LANES
    kcol = qcol + WIDTH_B // LANES
    vcol = kcol + WIDTH_B // LANES
    assert DIFF_TK % DIFF_TQ == 0
    halves = DIFF_TK // DIFF_TQ
    qaug, kaug, diag, slopes2 = _diff_tables()
    return pl.pallas_call(
        functools.partial(_diff_kernel, seq=seq, lambda_init=lambda_init),
        grid=(bsz, N_HEADS_B, seq // DIFF_TQ),
        in_specs=[
            pl.BlockSpec(memory_space=pltpu.SMEM),
            pl.BlockSpec((4, HEAD_DIM), lambda b, h, i: (0, 0)),
            pl.BlockSpec((1, DIFF_TQ, LANES), lambda b, h, i: (b, i, qcol + h)),
            pl.BlockSpec((1, seq, LANES), lambda b, h, i: (b, 0, kcol + h)),
            pl.BlockSpec((1, seq, LANES), lambda b, h, i: (b, 0, vcol + h)),
            pl.BlockSpec((1, DIFF_TQ, LANES), lambda b, h, i: (h, 0, 0)),
            pl.BlockSpec((1, 2, DIFF_TK, LANES), lambda b, h, i: (h, 0, 0, 0)),
            pl.BlockSpec((1, halves, DIFF_TQ, DIFF_TK), lambda b, h, i: (h, 0, 0, 0)),
            pl.BlockSpec((1, LANES), lambda b, h, i: (0, 0)),
        ],
        out_specs=pl.BlockSpec((1, DIFF_TQ, LANES), lambda b, h, i: (b, i, h)),
        out_shape=jax.ShapeDtypeStruct((bsz, seq, WIDTH_B), _BF16),
        compiler_params=_params("parallel", "parallel", "arbitrary"),
        name="diff_attn",
    )(slopes2, lam_vecs, proj, proj, proj, qaug, kaug, diag, subln_g.reshape(1, LANES).astype(_F32))


def _layer_norm(z, g, b):
    mu = jnp.mean(z, axis=-1, keepdims=True)
    zc = z - mu
    var = jnp.mean(zc * zc, axis=-1, keepdims=True)
    return zc * lax.rsqrt(var + LN_EPS) * g + b


def _mix_patterns(o_refs, lse_refs, expand_ref, o_nat, lse_nat):
    n_chunks, tm = o_nat.shape[1], o_nat.shape[2]
    outs, lses = [], []
    for p, (_, d) in enumerate(DILATED_PATTERNS):
        if d == 1:
            outs.append(o_refs[p][0, 0])
            lses.append(lse_refs[p][0, 0])
            continue
        for r in range(d):
            for c in range(n_chunks):
                o_nat[p, c, pl.ds(r, tm // d, stride=d), :] = o_refs[p][0, r, :, c * LANES:(c + 1) * LANES]
            lse_nat[p, pl.ds(r, tm // d, stride=d), :] = lse_refs[p][0, r]
        outs.append(jnp.concatenate([o_nat[p, c] for c in range(n_chunks)], axis=1))
        lses.append(lse_nat[p])
    top = functools.reduce(jnp.maximum, lses)
    es = [jnp.exp2(l - top) for l in lses]
    inv = 1.0 / functools.reduce(lambda a, b: a + b, es)
    mixed = None
    for o, e in zip(outs, es):
        w = e * inv
        w_hi = w.astype(_BF16)
        w_lo = (w - w_hi.astype(_F32)).astype(_BF16)
        w_full = (jnp.dot(w_hi, expand_ref[...], preferred_element_type=_F32)
                  + jnp.dot(w_lo, expand_ref[...], preferred_element_type=_F32))
        mixed = w_full * o if mixed is None else mixed + w_full * o
    return mixed


def _merge_kernel(x_ref, ga_ref, gb_ref, o1_ref, o2_ref, o3_ref, l1_ref, l2_ref, l3_ref, expand_ref, ob_ref,
                  wpa_ref, wpb_ref, wout_ref, bout_ref, g1_ref, b1_ref, wrh_ref, wrl_ref, br_ref,
                  h_ref, idx_ref, gate_ref, rank_ref, cnt_ref, o_nat, lse_nat, *, alpha):
    o_a = _mix_patterns((o1_ref, o2_ref, o3_ref), (l1_ref, l2_ref, l3_ref), expand_ref, o_nat, lse_nat)
    pa = jnp.dot(o_a.astype(_BF16), wpa_ref[...], preferred_element_type=_F32)
    pb = jnp.dot(ob_ref[...], wpb_ref[...], preferred_element_type=_F32)
    merged = jax.nn.sigmoid(ga_ref[...].astype(_F32)) * pa + jax.nn.sigmoid(gb_ref[...].astype(_F32)) * pb
    y = jnp.dot(merged.astype(_BF16), wout_ref[...], preferred_element_type=_F32) + bout_ref[...]
    h = _layer_norm(alpha * x_ref[...] + y, g1_ref[...], b1_ref[...])
    h_ref[...] = h

    h_hi = h.astype(_BF16)
    h_lo = (h - h_hi.astype(_F32)).astype(_BF16)
    nt = (((1,), (1,)), ((), ()))
    logits = (lax.dot_general(wrh_ref[...], h_hi, nt, preferred_element_type=_F32)
              + lax.dot_general(wrl_ref[...], h_hi, nt, preferred_element_type=_F32)
              + lax.dot_general(wrh_ref[...], h_lo, nt, preferred_element_type=_F32)
              + br_ref[...])

    tm = logits.shape[1]
    expert = lax.broadcasted_iota(jnp.int32, logits.shape, 0)
    work = logits
    vals, sels, idxs = [], [], []
    for _ in range(TOP_K):
        mx = jnp.max(work, axis=0, keepdims=True)
        idx = jnp.min(jnp.where(work == mx, expert, N_EXPERTS), axis=0, keepdims=True)
        sel = expert == idx
        work = jnp.where(sel, -jnp.inf, work)
        vals.append(mx)
        idxs.append(idx)
        sels.append(sel)
    ex = [jnp.exp(v - vals[0]) for v in vals]
    inv = 1.0 / (ex[0] + ex[1] + ex[2] + ex[3])
    gates = [e * inv for e in ex]

    chosen = sels[0] | sels[1] | sels[2] | sels[3]
    onehot = jnp.where(chosen, 1.0, 0.0).astype(_BF16)
    before = (lax.broadcasted_iota(jnp.int32, (tm, tm), 0)
              < lax.broadcasted_iota(jnp.int32, (tm, tm), 1))
    upper = jnp.where(before, 1.0, 0.0).astype(_BF16)
    prefix = jnp.dot(onehot, upper, preferred_element_type=_F32)
    for k in range(TOP_K):
        rank = jnp.sum(jnp.where(sels[k], prefix, 0.0), axis=0, keepdims=True)
        rank_ref[k:k + 1, :] = rank.astype(jnp.int32)
        idx_ref[k:k + 1, :] = idxs[k]
        gate_ref[k:k + 1, :] = gates[k]
    cnt_ref[0] = jnp.broadcast_to(jnp.sum(onehot.astype(_F32), axis=1, keepdims=True), cnt_ref.shape[1:])


def _merge_router(x2d, proj2d, gate_offset, dil_outs, dil_lses, o_b, wpa, wpb, wout, bout, g1, b1,
                  w_router, b_router, alpha):
    t, dm = x2d.shape
    tm = MERGE_TM
    bsz, _, seq_over_d0, _ = dil_outs[0].shape
    seq = seq_over_d0 * DILATED_PATTERNS[0][1]
    per_b = seq // tm
    dils = [d for _, d in DILATED_PATTERNS]
    assert gate_offset % dm == 0 and seq % tm == 0 and all(tm % d == 0 for d in dils)
    gcol = gate_offset // dm
    wr_t = w_router.T.astype(_F32)
    wr_hi = wr_t.astype(_BF16)
    wr_lo = (wr_t - wr_hi.astype(_F32)).astype(_BF16)
    expand = np.zeros((LANES, WIDTH_A), np.float32)
    for head in range(N_HEADS_A):
        expand[head, head * HEAD_DIM:(head + 1) * HEAD_DIM] = 1.0
    expand = jnp.asarray(expand, _BF16)

    def const(shape):
        return pl.BlockSpec(shape, lambda i: tuple(0 for _ in shape))

    def residue_major(d, width):
        return pl.BlockSpec((1, d, tm // d, width), lambda i: (i // per_b, 0, i % per_b, 0))

    return pl.pallas_call(
        functools.partial(_merge_kernel, alpha=alpha),
        grid=(t // tm,),
        in_specs=[
            pl.BlockSpec((tm, dm), lambda i: (i, 0)),
            pl.BlockSpec((tm, dm), lambda i: (i, gcol)),
            pl.BlockSpec((tm, dm), lambda i: (i, gcol + 1)),
            *[residue_major(d, WIDTH_A) for d in dils],
            *[residue_major(d, LANES) for d in dils],
            const((LANES, WIDTH_A)),
            pl.BlockSpec((tm, WIDTH_B), lambda i: (i, 0)),
            const((WIDTH_A, dm)), const((WIDTH_B, dm)), const((dm, dm)), const((1, dm)),
            const((1, dm)), const((1, dm)),
            const((N_EXPERTS, dm)), const((N_EXPERTS, dm)), const((N_EXPERTS, 1)),
        ],
        out_specs=[
            pl.BlockSpec((tm, dm), lambda i: (i, 0)),
            pl.BlockSpec((TOP_K, tm), lambda i: (0, i)),
            pl.BlockSpec((TOP_K, tm), lambda i: (0, i)),
            pl.BlockSpec((TOP_K, tm), lambda i: (0, i)),
            pl.BlockSpec((1, N_EXPERTS, LANES), lambda i: (i, 0, 0)),
        ],
        out_shape=[
            jax.ShapeDtypeStruct((t, dm), _F32),
            jax.ShapeDtypeStruct((TOP_K, t), jnp.int32),
            jax.ShapeDtypeStruct((TOP_K, t), _F32),
            jax.ShapeDtypeStruct((TOP_K, t), jnp.int32),
            jax.ShapeDtypeStruct((t // tm, N_EXPERTS, LANES), _F32),
        ],
        scratch_shapes=[pltpu.VMEM((len(dils), WIDTH_A // LANES, tm, LANES), _F32),
                        pltpu.VMEM((len(dils), tm, LANES), _F32)],
        compiler_params=_params("parallel"),
        name="merge_router",
    )(x2d, proj2d, proj2d, *dil_outs, *dil_lses, expand, o_b, wpa, wpb, wout, bout.reshape(1, dm),
      g1.reshape(1, dm), b1.reshape(1, dm), wr_hi, wr_lo, b_router.reshape(N_EXPERTS, 1).astype(_F32))


def _chunk_count(n, chunk):
    return lax.shift_right_logical(n + (chunk - 1), int(math.log2(chunk)))


def _dispatch_kernel(cnt_ref, base_ref, off_ref, tail_ref, end_ref, lpos_ref, gate_ref, h_ref, xs_hbm,
                     local, zeros, pending, sem, *, tm, dm):
    i = pl.program_id(0)
    rows = local.shape[1]
    buf = i % 2

    @pl.when(i == 0)
    def _():
        pending[0] = 0
    slot = lax.broadcasted_iota(jnp.int32, (rows, tm), 0)
    perm = jnp.zeros((rows, tm), _F32)
    gsel = jnp.zeros((rows, tm), _F32)
    for k in range(TOP_K):
        hit = slot == lpos_ref[k:k + 1, :]
        perm = perm + jnp.where(hit, 1.0, 0.0)
        gsel = gsel + jnp.where(hit, gate_ref[k:k + 1, :], 0.0)
    local[buf, :, 0:dm] = jnp.dot(perm.astype(_BF16), h_ref[...].astype(_BF16), preferred_element_type=_F32)
    local[buf, :, dm:dm + LANES] = jnp.broadcast_to(jnp.sum(gsel, axis=1, keepdims=True), (rows, LANES))

    def chunk_copy(src_row, dst_row):
        return pltpu.make_async_copy(local.at[buf, pl.ds(pl.multiple_of(src_row, ROW_ALIGN), DISP_CHUNK)],
                                     xs_hbm.at[pl.ds(pl.multiple_of(dst_row, ROW_ALIGN), DISP_CHUNK)], sem)

    def drain(c, carry):
        chunk_copy(0, 0).wait()
        return carry

    lax.fori_loop(0, pending[0], drain, 0)

    def per_expert(e, total):
        j = i * N_EXPERTS + e
        n_chunks = _chunk_count(cnt_ref[j], DISP_CHUNK)

        def issue(c, carry):
            chunk_copy(off_ref[j] + c * DISP_CHUNK, base_ref[j] + c * DISP_CHUNK).start()
            return carry

        lax.fori_loop(0, n_chunks, issue, 0)
        return total + n_chunks

    pending[0] = lax.fori_loop(0, N_EXPERTS, per_expert, 0)

    @pl.when(i == pl.num_programs(0) - 1)
    def _():
        lax.fori_loop(0, pending[0], drain, 0)
        zeros[...] = jnp.zeros_like(zeros)

        def fill(e, carry):
            gap = end_ref[e] - tail_ref[e]
            pos = tail_ref[e]
            size = MOE_ROWS
            while size >= ROW_ALIGN:
                take = (gap & size) != 0
                cp = pltpu.make_async_copy(zeros.at[pl.ds(0, size)],
                                           xs_hbm.at[pl.ds(pl.multiple_of(pos, ROW_ALIGN), size)], sem)

                @pl.when(take)
                def _():
                    cp.start()
                    cp.wait()

                pos = pos + jnp.where(take, size, 0)
                size //= 2
            return carry

        lax.fori_loop(0, N_EXPERTS, fill, 0)

        def fill_block(blk, carry):
            cp = pltpu.make_async_copy(
                zeros, xs_hbm.at[pl.ds(pl.multiple_of(blk * MOE_ROWS, MOE_ROWS), MOE_ROWS)], sem)
            cp.start()
            cp.wait()
            return carry

        first_unused = lax.shift_right_logical(end_ref[N_EXPERTS - 1], int(math.log2(MOE_ROWS)))
        lax.fori_loop(first_unused, xs_hbm.shape[0] // MOE_ROWS, fill_block, 0)


def _dispatch(h, gates, plan):
    t, dm = h.shape
    tm = MERGE_TM
    width = dm + LANES
    grid_spec = pltpu.PrefetchScalarGridSpec(
        num_scalar_prefetch=5,
        grid=(t // tm,),
        in_specs=[
            pl.BlockSpec((TOP_K, tm), lambda i, *_: (0, i)),
            pl.BlockSpec((TOP_K, tm), lambda i, *_: (0, i)),
            pl.BlockSpec((tm, dm), lambda i, *_: (i, 0)),
        ],
        out_specs=pl.BlockSpec(memory_space=pl.ANY),
        scratch_shapes=[pltpu.VMEM((2, DISP_LOCAL_ROWS, width), _F32), pltpu.VMEM((MOE_ROWS, width), _F32),
                        pltpu.SMEM((1,), jnp.int32), pltpu.SemaphoreType.DMA(())],
    )
    return pl.pallas_call(
        functools.partial(_dispatch_kernel, tm=tm, dm=dm),
        grid_spec=grid_spec,
        out_shape=jax.ShapeDtypeStruct((plan["n_rows"], width), _F32),
        compiler_params=_params("arbitrary"),
        name="moe_dispatch",
    )(plan["cnt"], plan["base"], plan["off_d"], plan["tail"], plan["end"], plan["lpos_d"], gates, h)


def _expert_kernel(blk_e_ref, blk_src_ref, n_used_ref, xs_ref, wup_ref, bup_ref, wdn_ref, bdn_ref, ys_ref,
                   wup_bf, wdn_bf, *, d_expert):
    del blk_src_ref
    i = pl.program_id(0)
    dm = ys_ref.shape[1]
    active = i < n_used_ref[0]

    @pl.when(active & ((i == 0) | (blk_e_ref[i] != blk_e_ref[jnp.maximum(i - 1, 0)])))
    def _():
        wup_bf[...] = wup_ref[0].astype(_BF16)
        wdn_bf[...] = wdn_ref[0].astype(_BF16)

    @pl.when(active)
    def _():
        x = xs_ref[:, 0:dm].astype(_BF16)
        row_gate = xs_ref[:, dm:dm + 1]
        acc = jnp.zeros(ys_ref.shape, _F32)
        for c in range(d_expert // FFN_CHUNK):
            lo, hi = c * FFN_CHUNK, (c + 1) * FFN_CHUNK
            g = jnp.dot(x, wup_bf[:, lo:hi], preferred_element_type=_F32) + bup_ref[0, :, lo:hi]
            u = (jnp.dot(x, wup_bf[:, d_expert + lo:d_expert + hi], preferred_element_type=_F32)
                 + bup_ref[0, :, d_expert + lo:d_expert + hi])
            gate = jnp.minimum(g, SWIGLU_LIMIT)
            up = jnp.clip(u, -SWIGLU_LIMIT, SWIGLU_LIMIT)
            act = gate * jax.nn.sigmoid(SWIGLU_ALPHA * gate) * (up + 1.0)
            acc = acc + jnp.dot(act.astype(_BF16), wdn_bf[lo:hi, :], preferred_element_type=_F32)
        ys_ref[...] = ((acc + bdn_ref[0]) * row_gate).astype(_BF16).astype(_F32)

    @pl.when(jnp.logical_not(active))
    def _():
        ys_ref[...] = jnp.zeros_like(ys_ref)


def _experts(xs, blk_e, blk_src, n_used, w_up, b_up, w_down, b_down):
    n_rows, width = xs.shape
    dm = width - LANES
    n_blocks = n_rows // MOE_ROWS
    n_exp, _, two_de = w_up.shape
    d_expert = two_de // 2
    grid_spec = pltpu.PrefetchScalarGridSpec(
        num_scalar_prefetch=3,
        grid=(n_blocks,),
        in_specs=[
            pl.BlockSpec((MOE_ROWS, width), lambda i, be, bs, nu: (bs[i], 0)),
            pl.BlockSpec((1, dm, two_de), lambda i, be, bs, nu: (be[i], 0, 0)),
            pl.BlockSpec((1, 1, two_de), lambda i, be, bs, nu: (be[i], 0, 0)),
            pl.BlockSpec((1, d_expert, dm), lambda i, be, bs, nu: (be[i], 0, 0)),
            pl.BlockSpec((1, 1, dm), lambda i, be, bs, nu: (be[i], 0, 0)),
        ],
        out_specs=pl.BlockSpec((MOE_ROWS, dm), lambda i, be, bs, nu: (i, 0)),
        scratch_shapes=[pltpu.VMEM((dm, two_de), _BF16), pltpu.VMEM((d_expert, dm), _BF16)],
    )
    return pl.pallas_call(
        functools.partial(_expert_kernel, d_expert=d_expert),
        grid_spec=grid_spec,
        out_shape=jax.ShapeDtypeStruct((n_rows, dm), _F32),
        compiler_params=_params("arbitrary"),
        name="moe_experts",
    )(blk_e, blk_src, n_used, xs, w_up, b_up.reshape(n_exp, 1, two_de), w_down, b_down.reshape(n_exp, 1, dm))


def _combine_kernel(cnt_ref, base_ref, off_ref, lpos_ref, h_ref, g2_ref, b2_ref, ys_hbm, o_ref, local, sems,
                    *, tm, alpha):
    i = pl.program_id(0)
    n_tiles = pl.num_programs(0)
    rows = local.shape[1]

    def chunk_copy(buf, src_row, dst_row):
        return pltpu.make_async_copy(ys_hbm.at[pl.ds(pl.multiple_of(src_row, ROW_ALIGN), COMB_CHUNK)],
                                     local.at[buf, pl.ds(pl.multiple_of(dst_row, ROW_ALIGN), COMB_CHUNK)],
                                     sems.at[buf])

    def for_each_chunk(tile, fn):
        def per_expert(e, carry):
            j = tile * N_EXPERTS + e

            def per_chunk(c, carry):
                fn(base_ref[j] + c * COMB_CHUNK, off_ref[j] + c * COMB_CHUNK)
                return carry

            return lax.fori_loop(0, _chunk_count(cnt_ref[j], COMB_CHUNK), per_chunk, carry)

        lax.fori_loop(0, N_EXPERTS, per_expert, 0)

    def fetch(tile):
        for_each_chunk(tile, lambda src, dst: chunk_copy(tile % 2, src, dst).start())

    @pl.when(i == 0)
    def _():
        local[...] = jnp.zeros_like(local)
        fetch(i)

    @pl.when(i + 1 < n_tiles)
    def _():
        fetch(i + 1)

    slot = lax.broadcasted_iota(jnp.int32, (tm, rows), 1)
    pick = jnp.zeros((tm, rows), _F32)
    for k in range(TOP_K):
        pick = pick + jnp.where(slot == lpos_ref[:, k:k + 1], 1.0, 0.0)

    for_each_chunk(i, lambda src, dst: chunk_copy(i % 2, 0, 0).wait())
    y = jnp.dot(pick.astype(_BF16), local[i % 2].astype(_BF16), preferred_element_type=_F32)
    o_ref[...] = _layer_norm(alpha * h_ref[...] + y, g2_ref[...], b2_ref[...])


def _combine(h, ys, plan, g2, b2, alpha):
    t, dm = h.shape
    tm = MERGE_TM
    grid_spec = pltpu.PrefetchScalarGridSpec(
        num_scalar_prefetch=3,
        grid=(t // tm,),
        in_specs=[
            pl.BlockSpec((tm, TOP_K), lambda i, *_: (i, 0)),
            pl.BlockSpec((tm, dm), lambda i, *_: (i, 0)),
            pl.BlockSpec((1, dm), lambda i, *_: (0, 0)),
            pl.BlockSpec((1, dm), lambda i, *_: (0, 0)),
            pl.BlockSpec(memory_space=pl.ANY),
        ],
        out_specs=pl.BlockSpec((tm, dm), lambda i, *_: (i, 0)),
        scratch_shapes=[pltpu.VMEM((2, COMB_LOCAL_ROWS, dm), _F32), pltpu.SemaphoreType.DMA((2,))],
    )
    return pl.pallas_call(
        functools.partial(_combine_kernel, tm=tm, alpha=alpha),
        grid_spec=grid_spec,
        out_shape=jax.ShapeDtypeStruct((t, dm), _F32),
        compiler_params=_params("arbitrary"),
        name="moe_combine",
    )(plan["cnt"], plan["base"], plan["off_c"], plan["lpos_c"], h, g2.reshape(1, dm), b2.reshape(1, dm), ys)


def _round_up(x, m):
    return (x + m - 1) // m * m


def _moe_plan(idx, rank, tile_cnt, tm):
    n_tiles, n_exp = tile_cnt.shape
    t = idx.shape[1]
    cnt = tile_cnt.astype(jnp.int32)
    grp = _round_up(cnt, ROW_ALIGN)
    tot = jnp.sum(grp, axis=0)
    padded = jnp.where(tot > 0, _round_up(tot + DISP_CHUNK, MOE_ROWS), 0)
    pend = jnp.cumsum(padded)
    pstart = pend - padded
    base = pstart[None, :] + jnp.cumsum(grp, axis=0) - grp
    off_d = jnp.cumsum(grp, axis=1) - grp
    n_chunk_c = _round_up(cnt, COMB_CHUNK) // COMB_CHUNK
    off_c = (jnp.cumsum(n_chunk_c, axis=1) - n_chunk_c) * COMB_CHUNK
    cover = jnp.where(cnt > 0, base + _round_up(cnt, DISP_CHUNK), 0)
    tail = jnp.maximum(jnp.max(cover, axis=0), pstart)

    onehot = idx[..., None] == jnp.arange(n_exp, dtype=jnp.int32)

    def per_assignment(table):
        per_token = jnp.broadcast_to(table[:, None, :], (n_tiles, tm, n_exp)).reshape(t, n_exp)
        return jnp.sum(jnp.where(onehot, per_token[None], 0), axis=-1)

    n_rows = _round_up(t * idx.shape[0] + n_tiles * n_exp * (ROW_ALIGN - 1)
                       + n_exp * (DISP_CHUNK + MOE_ROWS - 1), MOE_ROWS)
    n_blocks = n_rows // MOE_ROWS
    blk_start = jnp.arange(n_blocks, dtype=jnp.int32) * MOE_ROWS
    blk_e = jnp.sum((pend[None, :] <= blk_start[:, None]).astype(jnp.int32), axis=1)
    blk_e = jnp.minimum(blk_e, n_exp - 1)
    n_used = (pend[-1] // MOE_ROWS).astype(jnp.int32)
    blk_src = jnp.minimum(jnp.arange(n_blocks, dtype=jnp.int32), n_used - 1)
    return dict(
        cnt=cnt.reshape(-1), base=base.reshape(-1).astype(jnp.int32),
        off_d=off_d.reshape(-1).astype(jnp.int32), off_c=off_c.reshape(-1).astype(jnp.int32),
        tail=tail.astype(jnp.int32), end=pend.astype(jnp.int32),
        lpos_d=(per_assignment(off_d) + rank).astype(jnp.int32),
        lpos_c=(per_assignment(off_c) + rank).astype(jnp.int32).T,
        blk_e=blk_e[blk_src], blk_src=blk_src, n_used=n_used.reshape(1), n_rows=n_rows)


def kernel(x, w_in, b_in, lambda_q1, lambda_k1, lambda_q2, lambda_k2, subln_g, w_proj_a, w_proj_b, w_out, b_out, ln1_g, ln1_b, w_router, b_router, w_up, b_up, w_down, b_down, ln2_g, ln2_b):
    bsz, seq, dm = x.shape
    depth = w_in.shape[0]
    alpha = (2.0 * depth) ** 0.25
    t = bsz * seq
    for layer in range(depth):
        lambda_init = 0.8 - 0.6 * math.exp(-0.3 * layer)
        x2d = x.reshape(t, dm)
        n_a, n_b = 3 * WIDTH_A, 3 * WIDTH_B
        w_l, b_l = w_in[layer], b_in[layer]
        query_scale = jnp.full((WIDTH_A,), QUERY_SCALE, _F32)
        ones = functools.partial(jnp.ones, dtype=_F32)
        qkv_a = _in_proj_a(x, w_l[:, :n_a].astype(_BF16), b_l[:n_a],
                           jnp.concatenate([query_scale, ones((2 * WIDTH_A,))]))
        w_rest = jnp.concatenate([w_l[:, n_a + n_b:], w_l[:, n_a:n_a + n_b]], axis=1).astype(_BF16)
        b_rest = jnp.concatenate([b_l[n_a + n_b:], b_l[n_a:n_a + n_b]])
        scale_rest = jnp.concatenate([ones((2 * dm,)), query_scale, ones((2 * WIDTH_B,))])
        proj2d = _in_proj(x2d, w_rest, b_rest, scale_rest)

        dil = [_dilated_pass(a, d) for a, (_, d) in zip(qkv_a, DILATED_PATTERNS)]

        lam_vecs = jnp.stack([lambda_q1[layer], lambda_k1[layer], lambda_q2[layer], lambda_k2[layer]]).astype(_F32)
        o_b = _diff_attention(proj2d.reshape(bsz, seq, -1), 2 * dm, lam_vecs, subln_g[layer],
                              lambda_init).reshape(t, WIDTH_B)

        h, idx, gates, rank, cnt = _merge_router(
            x2d, proj2d, 0, [o for o, _ in dil], [l for _, l in dil], o_b,
            w_proj_a[layer].astype(_BF16), w_proj_b[layer].astype(_BF16),
            w_out[layer].astype(_BF16), b_out[layer], ln1_g[layer], ln1_b[layer],
            w_router[layer], b_router[layer], alpha)

        plan = _moe_plan(idx, rank, cnt[:, :, 0], MERGE_TM)
        xs = _dispatch(h, gates, plan)
        ys = _experts(xs, plan["blk_e"], plan["blk_src"], plan["n_used"], w_up[layer], b_up[layer],
                      w_down[layer], b_down[layer])
        out = _combine(h, ys, plan, ln2_g[layer], ln2_b[layer], alpha)
        x = out.reshape(bsz, seq, dm)
    return x
```

```python
import functools
import math

import numpy as np
import jax
import jax.numpy as jnp
from jax import lax
from jax.experimental import pallas as pl
from jax.experimental.pallas import tpu as pltpu

HEAD_DIM = 64
N_HEADS_A = 8
DILATED_PATTERNS = ((128, 1), (512, 4), (2048, 16))
N_HEADS_B = 4
WIDTH_A = N_HEADS_A * HEAD_DIM
WIDTH_B = N_HEADS_B * 2 * HEAD_DIM
N_ALIBI_HEADS = N_HEADS_A + N_HEADS_B
ALIBI_IDX_A = (0, 1, 3, 4, 6, 7, 9, 10)
ALIBI_IDX_B = (2, 5, 8, 11)
Q_BLOCK = 128
MASK_VALUE = -1e30
N_EXPERTS = 32
TOP_K = 4
SWIGLU_ALPHA = 1.702
SWIGLU_LIMIT = 7.0
LN_EPS = 1e-5
SUBLN_EPS = 1e-5
LOG2E = math.log2(math.e)
QUERY_SCALE = HEAD_DIM ** -0.5 * LOG2E

LANES = 128
V7X_VMEM_LIMIT_BYTES = 56 * 1024 * 1024

PROJ_TM = 1024
PROJ_A_TM = 512
DIL_ROWS = 2 * Q_BLOCK
DIL_HALO = 64
DIFF_TQ = 512
DIFF_TK = 512
MERGE_TM = 512
MOE_ROWS = 512
FFN_CHUNK = 512
MOE_TM = 256
ROW_ALIGN = 8
LOCAL_ROWS = -(-(TOP_K * MOE_TM + N_EXPERTS * (ROW_ALIGN - 1)) // LANES) * LANES

_F32 = jnp.float32
_BF16 = jnp.bfloat16


def _params(*sem):
    return pltpu.CompilerParams(dimension_semantics=sem, vmem_limit_bytes=V7X_VMEM_LIMIT_BYTES)


def _alibi_slopes():
    return (2.0 ** (-8.0 * np.arange(1, N_ALIBI_HEADS + 1) / N_ALIBI_HEADS)).astype(np.float32)


def _in_proj_kernel(x_ref, w_ref, b_ref, cs_ref, o_ref):
    x = x_ref[...].astype(_BF16)
    acc = jnp.dot(x, w_ref[...], preferred_element_type=_F32)
    o_ref[...] = ((acc + b_ref[...]) * cs_ref[...]).astype(o_ref.dtype)


def _in_proj(x2d, w_bf16, b, colscale):
    t, dm = x2d.shape
    n = w_bf16.shape[1]
    tn = n // 2
    assert n % 2 == 0 and tn % LANES == 0
    return pl.pallas_call(
        _in_proj_kernel,
        grid=(t // PROJ_TM, n // tn),
        in_specs=[
            pl.BlockSpec((PROJ_TM, dm), lambda i, j: (i, 0)),
            pl.BlockSpec((dm, tn), lambda i, j: (0, j)),
            pl.BlockSpec((1, tn), lambda i, j: (0, j)),
            pl.BlockSpec((1, tn), lambda i, j: (0, j)),
        ],
        out_specs=pl.BlockSpec((PROJ_TM, tn), lambda i, j: (i, j)),
        out_shape=jax.ShapeDtypeStruct((t, n), _BF16),
        compiler_params=_params("parallel", "arbitrary"),
        name="in_proj",
    )(x2d, w_bf16, b.reshape(1, n), colscale.reshape(1, n))


def _in_proj_a_kernel(x_ref, w_ref, b_ref, cs_ref, *refs):
    out_refs, acc_ref = refs[:-1], refs[-1]
    x = x_ref[0].astype(_BF16)
    acc = (jnp.dot(x, w_ref[...], preferred_element_type=_F32) + b_ref[...]) * cs_ref[...]
    n_chunks, tm, _ = acc_ref.shape
    for c in range(n_chunks):
        acc_ref[c] = acc[:, c * LANES:(c + 1) * LANES]
    for o_ref, (_, d) in zip(out_refs, DILATED_PATTERNS):
        if d == 1:
            o_ref[0, 0] = acc.astype(o_ref.dtype)
            continue
        for r in range(d):
            for c in range(n_chunks):
                rows = acc_ref[c, pl.ds(r, tm // d, stride=d), :]
                o_ref[0, r, :, c * LANES:(c + 1) * LANES] = rows.astype(o_ref.dtype)


def _in_proj_a(x, w_bf16, b, colscale):
    bsz, seq, dm = x.shape
    n = w_bf16.shape[1]
    tm = PROJ_A_TM
    per_b = seq // tm
    dils = [d for _, d in DILATED_PATTERNS]
    assert seq % tm == 0 and all(tm % d == 0 and (tm // d) % 16 == 0 for d in dils)
    return pl.pallas_call(
        _in_proj_a_kernel,
        grid=(bsz * per_b,),
        in_specs=[
            pl.BlockSpec((1, tm, dm), lambda i: (i // per_b, i % per_b, 0)),
            pl.BlockSpec((dm, n), lambda i: (0, 0)),
            pl.BlockSpec((1, n), lambda i: (0, 0)),
            pl.BlockSpec((1, n), lambda i: (0, 0)),
        ],
        out_specs=[pl.BlockSpec((1, d, tm // d, n), lambda i: (i // per_b, 0, i % per_b, 0)) for d in dils],
        out_shape=[jax.ShapeDtypeStruct((bsz, d, seq // d, n), _BF16) for d in dils],
        scratch_shapes=[pltpu.VMEM((n // LANES, tm, LANES), _F32)],
        compiler_params=_params("parallel"),
        name="in_proj_a",
    )(x, w_bf16, b.reshape(1, n), colscale.reshape(1, n))


def _dilated_bias_table(dilation):
    slopes = _alibi_slopes()[list(ALIBI_IDX_A)]
    band = Q_BLOCK + 2 * DIL_HALO
    qi = np.arange(Q_BLOCK)[:, None]
    kj = np.arange(band)[None, :]
    rel = qi - kj + DIL_HALO
    in_band = np.abs(rel) <= DIL_HALO
    base = -slopes[:, None, None] * (dilation * np.abs(rel)).astype(np.float32)[None]
    base = (base.astype(np.float64) * LOG2E).astype(np.float32)
    edge = (np.ones_like(kj, bool), kj >= DIL_HALO, kj < band - DIL_HALO)
    out = np.stack([np.where(in_band & e, base, np.float32(MASK_VALUE)) for e in edge])
    return out.astype(np.float32)


def _dilated_kernel(q_ref, km_ref, kp_ref, kn_ref, vm_ref, vp_ref, vn_ref, bias_ref, o_ref, lse_ref,
                    kbuf, vbuf, *, nblk):
    i = pl.program_id(2)
    h0, h1 = DIL_HALO, DIL_HALO + DIL_ROWS
    kbuf[0:h0, :] = kp_ref[0, 0]
    kbuf[h0:h1, :] = km_ref[0, 0]
    kbuf[h1:h1 + DIL_HALO, :] = kn_ref[0, 0]
    vbuf[0:h0, :] = vp_ref[0, 0]
    vbuf[h0:h1, :] = vm_ref[0, 0]
    vbuf[h1:h1 + DIL_HALO, :] = vn_ref[0, 0]

    lane = lax.broadcasted_iota(jnp.int32, (Q_BLOCK, LANES), 1)
    low_half = lane < HEAD_DIM
    band = Q_BLOCK + 2 * DIL_HALO
    variants = (jnp.where(i == 0, 1, 0), jnp.where(i == nblk - 1, 2, 0))

    for j in range(DIL_ROWS // Q_BLOCK):
        rows = slice(j * Q_BLOCK, (j + 1) * Q_BLOCK)
        krows = slice(j * Q_BLOCK, j * Q_BLOCK + band)
        lse_cols = []
        for hp in range(N_HEADS_A // 2):
            cols = slice(hp * LANES, (hp + 1) * LANES)
            q_pair = q_ref[0, 0, rows, cols]
            k_pair = kbuf[krows, cols]
            v_pair = vbuf[krows, cols]
            halves = []
            for hh in range(2):
                head = 2 * hp + hh
                keep = low_half if hh == 0 else jnp.logical_not(low_half)
                qm = jnp.where(keep, q_pair, jnp.zeros_like(q_pair))
                s = lax.dot_general(qm, k_pair, (((1,), (1,)), ((), ())), preferred_element_type=_F32)
                s = s + bias_ref[variants[j], head]
                m = jnp.max(s, axis=-1, keepdims=True)
                p = jnp.exp2(s - m)
                z = jnp.sum(p, axis=-1, keepdims=True)
                halves.append(jnp.dot(p.astype(_BF16), v_pair, preferred_element_type=_F32) * (1.0 / z))
                lse_cols.append(m + jnp.log2(z))
            o_ref[0, 0, rows, cols] = jnp.where(low_half, halves[0], halves[1])
        tile = jnp.zeros((Q_BLOCK, LANES), _F32)
        for head in range(N_HEADS_A):
            tile = jnp.where(lane == head, lse_cols[head], tile)
        lse_ref[0, 0, rows, :] = tile


def _dilated_pass(qkv, dilation):
    bsz, d, L, _ = qkv.shape
    assert d == dilation and L % DIL_ROWS == 0 and DIL_ROWS % DIL_HALO == 0
    nblk = L // DIL_ROWS
    per = DIL_ROWS // DIL_HALO
    n_halo = L // DIL_HALO
    bias = jnp.asarray(_dilated_bias_table(dilation))

    def main(s):
        return pl.BlockSpec((1, 1, DIL_ROWS, WIDTH_A), lambda b, r, i: (b, r, i, s))

    def left(s):
        return pl.BlockSpec((1, 1, DIL_HALO, WIDTH_A), lambda b, r, i: (b, r, jnp.maximum(per * i - 1, 0), s))

    def right(s):
        return pl.BlockSpec((1, 1, DIL_HALO, WIDTH_A),
                            lambda b, r, i: (b, r, jnp.minimum(per * i + per, n_halo - 1), s))

    rows_buf = DIL_ROWS + 2 * DIL_HALO
    return pl.pallas_call(
        functools.partial(_dilated_kernel, nblk=nblk),
        grid=(bsz, dilation, nblk),
        in_specs=[main(0), main(1), left(1), right(1), main(2), left(2), right(2),
                  pl.BlockSpec(bias.shape, lambda b, r, i: (0, 0, 0, 0))],
        out_specs=[pl.BlockSpec((1, 1, DIL_ROWS, WIDTH_A), lambda b, r, i: (b, r, i, 0)),
                   pl.BlockSpec((1, 1, DIL_ROWS, LANES), lambda b, r, i: (b, r, i, 0))],
        out_shape=[jax.ShapeDtypeStruct((bsz, dilation, L, WIDTH_A), _F32),
                   jax.ShapeDtypeStruct((bsz, dilation, L, LANES), _F32)],
        scratch_shapes=[pltpu.VMEM((rows_buf, WIDTH_A), _BF16), pltpu.VMEM((rows_buf, WIDTH_A), _BF16)],
        compiler_params=_params("parallel", "parallel", "arbitrary"),
        name=f"dilated_d{dilation}",
    )(qkv, qkv, qkv, qkv, qkv, qkv, qkv, bias)


def _split_bf16(x):
    hi = x.astype(jnp.bfloat16)
    lo = (x - hi.astype(np.float64)).astype(jnp.bfloat16)
    return hi, lo


def _diff_tables():
    s2 = _alibi_slopes()[list(ALIBI_IDX_B)].astype(np.float64) * LOG2E
    qaug = np.zeros((N_HEADS_B, DIFF_TQ, LANES), jnp.bfloat16)
    kaug = np.zeros((N_HEADS_B, 2, DIFF_TK, LANES), jnp.bfloat16)
    i = np.arange(DIFF_TQ, dtype=np.float64)
    j = np.arange(DIFF_TK, dtype=np.float64)
    for h in range(N_HEADS_B):
        r_hi, r_lo = _split_bf16(-s2[h] * i)
        qaug[h, :, 0] = 1.0
        qaug[h, :, 1] = 1.0
        qaug[h, :, 2] = r_hi
        qaug[h, :, 3] = r_lo
        for side, sign in enumerate((1.0, -1.0)):
            c_hi, c_lo = _split_bf16(sign * s2[h] * j)
            kaug[h, side, :, 0] = c_hi
            kaug[h, side, :, 1] = c_lo
            kaug[h, side, :, 2] = sign
            kaug[h, side, :, 3] = sign
    halves = DIFF_TK // DIFF_TQ
    off = (np.arange(halves) * DIFF_TQ)[:, None, None]
    dist = np.abs(off + i[None, :, None] - j[None, None, :])
    diag = (-s2[:, None, None, None] * dist[None]).astype(np.float32)
    return jnp.asarray(qaug), jnp.asarray(kaug), jnp.asarray(diag), jnp.asarray(s2.astype(np.float32))


def _diff_kernel(slope_ref, lam_ref, q_ref, k_ref, v_ref, qaug_ref, kaug_ref, diag_ref, g_ref, o_ref,
                 *, seq, lambda_init):
    head = pl.program_id(1)
    qb = pl.program_id(2)
    q0 = qb * DIFF_TQ
    halves = DIFF_TK // DIFF_TQ
    kd = qb // halves
    n_kb = seq // DIFF_TK
    slope2 = slope_ref[head]
    nt = (((1,), (1,)), ((), ()))

    q = q_ref[0]
    lane = lax.broadcasted_iota(jnp.int32, q.shape, 1)
    zero = jnp.zeros_like(q)
    qaug = qaug_ref[0]
    q_maps = (jnp.concatenate([jnp.where(lane < HEAD_DIM, q, zero), qaug], axis=1),
              jnp.concatenate([jnp.where(lane >= HEAD_DIM, q, zero), qaug], axis=1))
    zero_aug = jnp.zeros((DIFF_TK, LANES), _BF16)
    vlane = lax.broadcasted_iota(jnp.int32, (DIFF_TK, LANES), 1)
    ones_col = jnp.where(vlane == 0, 1.0, 0.0).astype(_BF16)

    def load(kb):
        k0 = pl.multiple_of(kb * DIFF_TK, DIFF_TK)
        return k_ref[0, pl.ds(k0, DIFF_TK), :], jnp.concatenate([v_ref[0, pl.ds(k0, DIFF_TK), :], ones_col], axis=1)

    def split(pv):
        return pv[:, LANES:LANES + 1], pv[:, 0:LANES]

    def update(s, const, vblk, carry):
        m_old, l_old, acc = carry
        m_new = jnp.maximum(m_old, jnp.max(s, axis=-1, keepdims=True) + const)
        alpha = jnp.exp2(m_old - m_new)
        p = jnp.exp2(s - (m_new - const))
        l_blk, o_blk = split(jnp.dot(p.astype(_BF16), vblk, preferred_element_type=_F32))
        return m_new, alpha * l_old + l_blk, alpha * acc + o_blk

    kblk, vblk = load(kd)
    k_all = jnp.concatenate([kblk, zero_aug], axis=1)
    bias = diag_ref[0, qb % halves]
    carries = []
    for mp in range(2):
        s = lax.dot_general(q_maps[mp], k_all, nt, preferred_element_type=_F32) + bias
        m = jnp.max(s, axis=-1, keepdims=True)
        p = jnp.exp2(s - m)
        carries.append((m,) + split(jnp.dot(p.astype(_BF16), vblk, preferred_element_type=_F32)))

    for t in range(n_kb - 1):
        kb = jnp.where(t >= kd, t + 1, t)
        side = jnp.where(t >= kd, 1, 0)
        kblk, vblk = load(kb)
        k_all = jnp.concatenate([kblk, kaug_ref[0, side]], axis=1)
        const = -slope2 * jnp.abs(q0 - kb * DIFF_TK).astype(_F32)
        for mp in range(2):
            s = lax.dot_general(q_maps[mp], k_all, nt, preferred_element_type=_F32)
            carries[mp] = update(s, const, vblk, carries[mp])

    lv = lam_ref[...]
    lam = (jnp.exp(jnp.sum(lv[0:1] * lv[1:2], axis=-1, keepdims=True))
           - jnp.exp(jnp.sum(lv[2:3] * lv[3:4], axis=-1, keepdims=True)) + lambda_init)
    (_, l1, a1), (_, l2, a2) = carries
    of = a1 * (1.0 / l1) - lam * (a2 * (1.0 / l2))
    of = of * lax.rsqrt(jnp.mean(of * of, axis=-1, keepdims=True) + SUBLN_EPS)
    of = of * g_ref[...] * (1.0 - lambda_init)
    o_ref[0] = of.astype(o_ref.dtype)


def _diff_attention(proj, q_offset, lam_vecs, subln_g, lambda_init):
    bsz, seq, _ = proj.shape
    assert seq % DIFF_TQ == 0 and seq % DIFF_TK == 0 and q_offset % LANES == 0
    qcol = q_offset // LANES
    kcol = qcol + WIDTH_B // LANES
    vcol = kcol + WIDTH_B // LANES
    assert DIFF_TK % DIFF_TQ == 0
    halves = DIFF_TK // DIFF_TQ
    qaug, kaug, diag, slopes2 = _diff_tables()
    return pl.pallas_call(
        functools.partial(_diff_kernel, seq=seq, lambda_init=lambda_init),
        grid=(bsz, N_HEADS_B, seq // DIFF_TQ),
        in_specs=[
            pl.BlockSpec(memory_space=pltpu.SMEM),
            pl.BlockSpec((4, HEAD_DIM), lambda b, h, i: (0, 0)),
            pl.BlockSpec((1, DIFF_TQ, LANES), lambda b, h, i: (b, i, qcol + h)),
            pl.BlockSpec((1, seq, LANES), lambda b, h, i: (b, 0, kcol + h)),
            pl.BlockSpec((1, seq, LANES), lambda b, h, i: (b, 0, vcol + h)),
            pl.BlockSpec((1, DIFF_TQ, LANES), lambda b, h, i: (h, 0, 0)),
            pl.BlockSpec((1, 2, DIFF_TK, LANES), lambda b, h, i: (h, 0, 0, 0)),
            pl.BlockSpec((1, halves, DIFF_TQ, DIFF_TK), lambda b, h, i: (h, 0, 0, 0)),
            pl.BlockSpec((1, LANES), lambda b, h, i: (0, 0)),
        ],
        out_specs=pl.BlockSpec((1, DIFF_TQ, LANES), lambda b, h, i: (b, i, h)),
        out_shape=jax.ShapeDtypeStruct((bsz, seq, WIDTH_B), _BF16),
        compiler_params=_params("parallel", "parallel", "arbitrary"),
        name="diff_attn",
    )(slopes2, lam_vecs, proj, proj, proj, qaug, kaug, diag, subln_g.reshape(1, LANES).astype(_F32))


def _layer_norm(z, g, b):
    mu = jnp.mean(z, axis=-1, keepdims=True)
    zc = z - mu
    var = jnp.mean(zc * zc, axis=-1, keepdims=True)
    return zc * lax.rsqrt(var + LN_EPS) * g + b


def _mix_patterns(o_refs, lse_refs, expand_ref, o_nat, lse_nat):
    n_chunks, tm = o_nat.shape[1], o_nat.shape[2]
    outs, lses = [], []
    for p, (_, d) in enumerate(DILATED_PATTERNS):
        if d == 1:
            outs.append(o_refs[p][0, 0])
            lses.append(lse_refs[p][0, 0])
            continue
        for r in range(d):
            for c in range(n_chunks):
                o_nat[p, c, pl.ds(r, tm // d, stride=d), :] = o_refs[p][0, r, :, c * LANES:(c + 1) * LANES]
            lse_nat[p, pl.ds(r, tm // d, stride=d), :] = lse_refs[p][0, r]
        outs.append(jnp.concatenate([o_nat[p, c] for c in range(n_chunks)], axis=1))
        lses.append(lse_nat[p])
    top = functools.reduce(jnp.maximum, lses)
    es = [jnp.exp2(l - top) for l in lses]
    inv = 1.0 / functools.reduce(lambda a, b: a + b, es)
    mixed = None
    for o, e in zip(outs, es):
        w = e * inv
        w_hi = w.astype(_BF16)
        w_lo = (w - w_hi.astype(_F32)).astype(_BF16)
        w_full = (jnp.dot(w_hi, expand_ref[...], preferred_element_type=_F32)
                  + jnp.dot(w_lo, expand_ref[...], preferred_element_type=_F32))
        mixed = w_full * o if mixed is None else mixed + w_full * o
    return mixed


def _merge_kernel(x_ref, ga_ref, gb_ref, o1_ref, o2_ref, o3_ref, l1_ref, l2_ref, l3_ref, expand_ref, ob_ref,
                  wpa_ref, wpb_ref, wout_ref, bout_ref, g1_ref, b1_ref, wrh_ref, wrl_ref, br_ref,
                  h_ref, idx_ref, gate_ref, rank_ref, cnt_ref, o_nat, lse_nat, *, alpha):
    o_a = _mix_patterns((o1_ref, o2_ref, o3_ref), (l1_ref, l2_ref, l3_ref), expand_ref, o_nat, lse_nat)
    pa = jnp.dot(o_a.astype(_BF16), wpa_ref[...], preferred_element_type=_F32)
    pb = jnp.dot(ob_ref[...], wpb_ref[...], preferred_element_type=_F32)
    merged = jax.nn.sigmoid(ga_ref[...].astype(_F32)) * pa + jax.nn.sigmoid(gb_ref[...].astype(_F32)) * pb
    y = jnp.dot(merged.astype(_BF16), wout_ref[...], preferred_element_type=_F32) + bout_ref[...]
    h = _layer_norm(alpha * x_ref[...] + y, g1_ref[...], b1_ref[...])
    h_ref[...] = h

    h_hi = h.astype(_BF16)
    h_lo = (h - h_hi.astype(_F32)).astype(_BF16)
    nt = (((1,), (1,)), ((), ()))
    logits = (lax.dot_general(wrh_ref[...], h_hi, nt, preferred_element_type=_F32)
              + lax.dot_general(wrl_ref[...], h_hi, nt, preferred_element_type=_F32)
              + lax.dot_general(wrh_ref[...], h_lo, nt, preferred_element_type=_F32)
              + br_ref[...])

    tm = logits.shape[1]
    expert = lax.broadcasted_iota(jnp.int32, logits.shape, 0)
    work = logits
    vals, sels, idxs = [], [], []
    for _ in range(TOP_K):
        mx = jnp.max(work, axis=0, keepdims=True)
        idx = jnp.min(jnp.where(work == mx, expert, N_EXPERTS), axis=0, keepdims=True)
        sel = expert == idx
        work = jnp.where(sel, -jnp.inf, work)
        vals.append(mx)
        idxs.append(idx)
        sels.append(sel)
    ex = [jnp.exp(v - vals[0]) for v in vals]
    inv = 1.0 / (ex[0] + ex[1] + ex[2] + ex[3])
    gates = [e * inv for e in ex]

    chosen = sels[0] | sels[1] | sels[2] | sels[3]
    onehot = jnp.where(chosen, 1.0, 0.0).astype(_BF16)
    earlier = lax.broadcasted_iota(jnp.int32, (tm, tm), 0)
    later = lax.broadcasted_iota(jnp.int32, (tm, tm), 1)
    sub_shift = int(math.log2(MOE_TM))
    before = (earlier < later) & (lax.shift_right_logical(earlier, sub_shift)
                                  == lax.shift_right_logical(later, sub_shift))
    upper = jnp.where(before, 1.0, 0.0).astype(_BF16)
    prefix = jnp.dot(onehot, upper, preferred_element_type=_F32)
    for k in range(TOP_K):
        rank = jnp.sum(jnp.where(sels[k], prefix, 0.0), axis=0, keepdims=True)
        rank_ref[k:k + 1, :] = rank.astype(jnp.int32)
        idx_ref[k:k + 1, :] = idxs[k]
        gate_ref[k:k + 1, :] = gates[k]
    for sub in range(tm // MOE_TM):
        part = onehot[:, sub * MOE_TM:(sub + 1) * MOE_TM].astype(_F32)
        cnt_ref[sub] = jnp.broadcast_to(jnp.sum(part, axis=1, keepdims=True), cnt_ref.shape[1:])


def _merge_router(x2d, proj2d, gate_offset, dil_outs, dil_lses, o_b, wpa, wpb, wout, bout, g1, b1,
                  w_router, b_router, alpha):
    t, dm = x2d.shape
    tm = MERGE_TM
    bsz, _, seq_over_d0, _ = dil_outs[0].shape
    seq = seq_over_d0 * DILATED_PATTERNS[0][1]
    per_b = seq // tm
    dils = [d for _, d in DILATED_PATTERNS]
    assert gate_offset % dm == 0 and seq % tm == 0 and all(tm % d == 0 for d in dils)
    gcol = gate_offset // dm
    wr_t = w_router.T.astype(_F32)
    wr_hi = wr_t.astype(_BF16)
    wr_lo = (wr_t - wr_hi.astype(_F32)).astype(_BF16)
    expand = np.zeros((LANES, WIDTH_A), np.float32)
    for head in range(N_HEADS_A):
        expand[head, head * HEAD_DIM:(head + 1) * HEAD_DIM] = 1.0
    expand = jnp.asarray(expand, _BF16)

    def const(shape):
        return pl.BlockSpec(shape, lambda i: tuple(0 for _ in shape))

    def residue_major(d, width):
        return pl.BlockSpec((1, d, tm // d, width), lambda i: (i // per_b, 0, i % per_b, 0))

    return pl.pallas_call(
        functools.partial(_merge_kernel, alpha=alpha),
        grid=(t // tm,),
        in_specs=[
            pl.BlockSpec((tm, dm), lambda i: (i, 0)),
            pl.BlockSpec((tm, dm), lambda i: (i, gcol)),
            pl.BlockSpec((tm, dm), lambda i: (i, gcol + 1)),
            *[residue_major(d, WIDTH_A) for d in dils],
            *[residue_major(d, LANES) for d in dils],
            const((LANES, WIDTH_A)),
            pl.BlockSpec((tm, WIDTH_B), lambda i: (i, 0)),
            const((WIDTH_A, dm)), const((WIDTH_B, dm)), const((dm, dm)), const((1, dm)),
            const((1, dm)), const((1, dm)),
            const((N_EXPERTS, dm)), const((N_EXPERTS, dm)), const((N_EXPERTS, 1)),
        ],
        out_specs=[
            pl.BlockSpec((tm, dm), lambda i: (i, 0)),
            pl.BlockSpec((TOP_K, tm), lambda i: (0, i)),
            pl.BlockSpec((TOP_K, tm), lambda i: (0, i)),
            pl.BlockSpec((TOP_K, tm), lambda i: (0, i)),
            pl.BlockSpec((tm // MOE_TM, N_EXPERTS, LANES), lambda i: (i, 0, 0)),
        ],
        out_shape=[
            jax.ShapeDtypeStruct((t, dm), _F32),
            jax.ShapeDtypeStruct((TOP_K, t), jnp.int32),
            jax.ShapeDtypeStruct((TOP_K, t), _F32),
            jax.ShapeDtypeStruct((TOP_K, t), jnp.int32),
            jax.ShapeDtypeStruct((t // MOE_TM, N_EXPERTS, LANES), _F32),
        ],
        scratch_shapes=[pltpu.VMEM((len(dils), WIDTH_A // LANES, tm, LANES), _F32),
                        pltpu.VMEM((len(dils), tm, LANES), _F32)],
        compiler_params=_params("parallel"),
        name="merge_router",
    )(x2d, proj2d, proj2d, *dil_outs, *dil_lses, expand, o_b, wpa, wpb, wout, bout.reshape(1, dm),
      g1.reshape(1, dm), b1.reshape(1, dm), wr_hi, wr_lo, b_router.reshape(N_EXPERTS, 1).astype(_F32))


def _chunk_count(n):
    return lax.shift_right_logical(n + (ROW_ALIGN - 1), int(math.log2(ROW_ALIGN)))


def _dispatch_kernel(cnt_ref, base_ref, off_ref, tail_ref, end_ref, lpos_ref, gate_ref, h_ref, xs_hbm,
                     local, zeros, pending, sem, *, tm, dm):
    i = pl.program_id(0)
    rows = local.shape[1]
    buf = i % 2

    @pl.when(i == 0)
    def _():
        pending[0] = 0
    slot = lax.broadcasted_iota(jnp.int32, (rows, tm), 0)
    perm = jnp.zeros((rows, tm), _F32)
    gsel = jnp.zeros((rows, tm), _F32)
    for k in range(TOP_K):
        hit = slot == lpos_ref[k:k + 1, :]
        perm = perm + jnp.where(hit, 1.0, 0.0)
        gsel = gsel + jnp.where(hit, gate_ref[k:k + 1, :], 0.0)
    local[buf, :, 0:dm] = jnp.dot(perm.astype(_BF16), h_ref[...].astype(_BF16), preferred_element_type=_F32)
    local[buf, :, dm:dm + LANES] = jnp.broadcast_to(jnp.sum(gsel, axis=1, keepdims=True), (rows, LANES))

    def chunk_copy(src_row, dst_row):
        return pltpu.make_async_copy(local.at[buf, pl.ds(pl.multiple_of(src_row, ROW_ALIGN), ROW_ALIGN)],
                                     xs_hbm.at[pl.ds(pl.multiple_of(dst_row, ROW_ALIGN), ROW_ALIGN)], sem)

    def drain(c, carry):
        chunk_copy(0, 0).wait()
        return carry

    lax.fori_loop(0, pending[0], drain, 0)

    def per_expert(e, total):
        j = i * N_EXPERTS + e
        n_chunks = _chunk_count(cnt_ref[j])

        def issue(c, carry):
            chunk_copy(off_ref[j] + c * ROW_ALIGN, base_ref[j] + c * ROW_ALIGN).start()
            return carry

        lax.fori_loop(0, n_chunks, issue, 0)
        return total + n_chunks

    pending[0] = lax.fori_loop(0, N_EXPERTS, per_expert, 0)

    @pl.when(i == pl.num_programs(0) - 1)
    def _():
        lax.fori_loop(0, pending[0], drain, 0)
        zeros[...] = jnp.zeros_like(zeros)

        def fill(e, carry):
            gap = end_ref[e] - tail_ref[e]
            pos = tail_ref[e]
            size = MOE_ROWS
            while size >= ROW_ALIGN:
                take = (gap & size) != 0
                cp = pltpu.make_async_copy(zeros.at[pl.ds(0, size)],
                                           xs_hbm.at[pl.ds(pl.multiple_of(pos, ROW_ALIGN), size)], sem)

                @pl.when(take)
                def _():
                    cp.start()
                    cp.wait()

                pos = pos + jnp.where(take, size, 0)
                size //= 2
            return carry

        lax.fori_loop(0, N_EXPERTS, fill, 0)

        def fill_block(blk, carry):
            cp = pltpu.make_async_copy(
                zeros, xs_hbm.at[pl.ds(pl.multiple_of(blk * MOE_ROWS, MOE_ROWS), MOE_ROWS)], sem)
            cp.start()
            cp.wait()
            return carry

        first_unused = lax.shift_right_logical(end_ref[N_EXPERTS - 1], int(math.log2(MOE_ROWS)))
        lax.fori_loop(first_unused, xs_hbm.shape[0] // MOE_ROWS, fill_block, 0)


def _dispatch(h, gates, plan):
    t, dm = h.shape
    tm = MOE_TM
    width = dm + LANES
    grid_spec = pltpu.PrefetchScalarGridSpec(
        num_scalar_prefetch=5,
        grid=(t // tm,),
        in_specs=[
            pl.BlockSpec((TOP_K, tm), lambda i, *_: (0, i)),
            pl.BlockSpec((TOP_K, tm), lambda i, *_: (0, i)),
            pl.BlockSpec((tm, dm), lambda i, *_: (i, 0)),
        ],
        out_specs=pl.BlockSpec(memory_space=pl.ANY),
        scratch_shapes=[pltpu.VMEM((2, LOCAL_ROWS, width), _F32), pltpu.VMEM((MOE_ROWS, width), _F32),
                        pltpu.SMEM((1,), jnp.int32), pltpu.SemaphoreType.DMA(())],
    )
    return pl.pallas_call(
        functools.partial(_dispatch_kernel, tm=tm, dm=dm),
        grid_spec=grid_spec,
        out_shape=jax.ShapeDtypeStruct((plan["n_rows"], width), _F32),
        compiler_params=_params("arbitrary"),
        name="moe_dispatch",
    )(plan["cnt"], plan["base"], plan["off"], plan["tail"], plan["end"], plan["lpos"], gates, h)


def _expert_kernel(blk_e_ref, blk_src_ref, n_used_ref, xs_ref, wup_ref, bup_ref, wdn_ref, bdn_ref, ys_ref,
                   wup_bf, wdn_bf, *, d_expert):
    del blk_src_ref
    i = pl.program_id(0)
    dm = ys_ref.shape[1]
    active = i < n_used_ref[0]

    @pl.when(active & ((i == 0) | (blk_e_ref[i] != blk_e_ref[jnp.maximum(i - 1, 0)])))
    def _():
        wup_bf[...] = wup_ref[0].astype(_BF16)
        wdn_bf[...] = wdn_ref[0].astype(_BF16)

    @pl.when(active)
    def _():
        x = xs_ref[:, 0:dm].astype(_BF16)
        row_gate = xs_ref[:, dm:dm + 1]
        acc = jnp.zeros(ys_ref.shape, _F32)
        for c in range(d_expert // FFN_CHUNK):
            lo, hi = c * FFN_CHUNK, (c + 1) * FFN_CHUNK
            g = jnp.dot(x, wup_bf[:, lo:hi], preferred_element_type=_F32) + bup_ref[0, :, lo:hi]
            u = (jnp.dot(x, wup_bf[:, d_expert + lo:d_expert + hi], preferred_element_type=_F32)
                 + bup_ref[0, :, d_expert + lo:d_expert + hi])
            gate = jnp.minimum(g, SWIGLU_LIMIT)
            up = jnp.clip(u, -SWIGLU_LIMIT, SWIGLU_LIMIT)
            act = gate * jax.nn.sigmoid(SWIGLU_ALPHA * gate) * (up + 1.0)
            acc = acc + jnp.dot(act.astype(_BF16), wdn_bf[lo:hi, :], preferred_element_type=_F32)
        ys_ref[...] = ((acc + bdn_ref[0]) * row_gate).astype(_BF16).astype(_F32)

    @pl.when(jnp.logical_not(active))
    def _():
        ys_ref[...] = jnp.zeros_like(ys_ref)


def _experts(xs, blk_e, blk_src, n_used, w_up, b_up, w_down, b_down):
    n_rows, width = xs.shape
    dm = width - LANES
    n_blocks = n_rows // MOE_ROWS
    n_exp, _, two_de = w_up.shape
    d_expert = two_de // 2
    grid_spec = pltpu.PrefetchScalarGridSpec(
        num_scalar_prefetch=3,
        grid=(n_blocks,),
        in_specs=[
            pl.BlockSpec((MOE_ROWS, width), lambda i, be, bs, nu: (bs[i], 0)),
            pl.BlockSpec((1, dm, two_de), lambda i, be, bs, nu: (be[i], 0, 0)),
            pl.BlockSpec((1, 1, two_de), lambda i, be, bs, nu: (be[i], 0, 0)),
            pl.BlockSpec((1, d_expert, dm), lambda i, be, bs, nu: (be[i], 0, 0)),
            pl.BlockSpec((1, 1, dm), lambda i, be, bs, nu: (be[i], 0, 0)),
        ],
        out_specs=pl.BlockSpec((MOE_ROWS, dm), lambda i, be, bs, nu: (i, 0)),
        scratch_shapes=[pltpu.VMEM((dm, two_de), _BF16), pltpu.VMEM((d_expert, dm), _BF16)],
    )
    return pl.pallas_call(
        functools.partial(_expert_kernel, d_expert=d_expert),
        grid_spec=grid_spec,
        out_shape=jax.ShapeDtypeStruct((n_rows, dm), _F32),
        compiler_params=_params("arbitrary"),
        name="moe_experts",
    )(blk_e, blk_src, n_used, xs, w_up, b_up.reshape(n_exp, 1, two_de), w_down, b_down.reshape(n_exp, 1, dm))


def _combine_kernel(cnt_ref, base_ref, off_ref, lpos_ref, h_ref, g2_ref, b2_ref, ys_hbm, o_ref, local, sems,
                    *, tm, alpha):
    i = pl.program_id(0)
    n_tiles = pl.num_programs(0)
    rows = local.shape[1]

    def chunk_copy(buf, src_row, dst_row):
        return pltpu.make_async_copy(ys_hbm.at[pl.ds(pl.multiple_of(src_row, ROW_ALIGN), ROW_ALIGN)],
                                     local.at[buf, pl.ds(pl.multiple_of(dst_row, ROW_ALIGN), ROW_ALIGN)],
                                     sems.at[buf])

    def for_each_chunk(tile, fn):
        def per_expert(e, carry):
            j = tile * N_EXPERTS + e

            def per_chunk(c, carry):
                fn(base_ref[j] + c * ROW_ALIGN, off_ref[j] + c * ROW_ALIGN)
                return carry

            return lax.fori_loop(0, _chunk_count(cnt_ref[j]), per_chunk, carry)

        lax.fori_loop(0, N_EXPERTS, per_expert, 0)

    def fetch(tile):
        for_each_chunk(tile, lambda src, dst: chunk_copy(tile % 2, src, dst).start())

    @pl.when(i == 0)
    def _():
        local[...] = jnp.zeros_like(local)
        fetch(i)

    @pl.when(i + 1 < n_tiles)
    def _():
        fetch(i + 1)

    slot = lax.broadcasted_iota(jnp.int32, (tm, rows), 1)
    pick = jnp.zeros((tm, rows), _F32)
    for k in range(TOP_K):
        pick = pick + jnp.where(slot == lpos_ref[:, k:k + 1], 1.0, 0.0)

    for_each_chunk(i, lambda src, dst: chunk_copy(i % 2, 0, 0).wait())
    y = jnp.dot(pick.astype(_BF16), local[i % 2].astype(_BF16), preferred_element_type=_F32)
    o_ref[...] = _layer_norm(alpha * h_ref[...] + y, g2_ref[...], b2_ref[...])


def _combine(h, ys, plan, g2, b2, alpha):
    t, dm = h.shape
    tm = MOE_TM
    grid_spec = pltpu.PrefetchScalarGridSpec(
        num_scalar_prefetch=3,
        grid=(t // tm,),
        in_specs=[
            pl.BlockSpec((tm, TOP_K), lambda i, *_: (i, 0)),
            pl.BlockSpec((tm, dm), lambda i, *_: (i, 0)),
            pl.BlockSpec((1, dm), lambda i, *_: (0, 0)),
            pl.BlockSpec((1, dm), lambda i, *_: (0, 0)),
            pl.BlockSpec(memory_space=pl.ANY),
        ],
        out_specs=pl.BlockSpec((tm, dm), lambda i, *_: (i, 0)),
        scratch_shapes=[pltpu.VMEM((2, LOCAL_ROWS, dm), _F32), pltpu.SemaphoreType.DMA((2,))],
    )
    return pl.pallas_call(
        functools.partial(_combine_kernel, tm=tm, alpha=alpha),
        grid_spec=grid_spec,
        out_shape=jax.ShapeDtypeStruct((t, dm), _F32),
        compiler_params=_params("arbitrary"),
        name="moe_combine",
    )(plan["cnt"], plan["base"], plan["off"], plan["lpos"].T, h, g2.reshape(1, dm), b2.reshape(1, dm), ys)


def _round_up(x, m):
    return (x + m - 1) // m * m


def _moe_plan(idx, rank, tile_cnt, tm):
    n_tiles, n_exp = tile_cnt.shape
    t = idx.shape[1]
    cnt = tile_cnt.astype(jnp.int32)
    grp = _round_up(cnt, ROW_ALIGN)
    tot = jnp.sum(grp, axis=0)
    padded = _round_up(tot, MOE_ROWS)
    pend = jnp.cumsum(padded)
    pstart = pend - padded
    base = pstart[None, :] + jnp.cumsum(grp, axis=0) - grp
    off = jnp.cumsum(grp, axis=1) - grp
    onehot = idx[..., None] == jnp.arange(n_exp, dtype=jnp.int32)
    off_tok = jnp.broadcast_to(off[:, None, :], (n_tiles, tm, n_exp)).reshape(t, n_exp)
    lpos = jnp.sum(jnp.where(onehot, off_tok[None], 0), axis=-1) + rank

    n_rows = _round_up(t * idx.shape[0] + n_tiles * n_exp * (ROW_ALIGN - 1) + n_exp * (MOE_ROWS - 1), MOE_ROWS)
    n_blocks = n_rows // MOE_ROWS
    blk_start = jnp.arange(n_blocks, dtype=jnp.int32) * MOE_ROWS
    blk_e = jnp.sum((pend[None, :] <= blk_start[:, None]).astype(jnp.int32), axis=1)
    blk_e = jnp.minimum(blk_e, n_exp - 1)
    n_used = (pend[-1] // MOE_ROWS).astype(jnp.int32)
    blk_src = jnp.minimum(jnp.arange(n_blocks, dtype=jnp.int32), n_used - 1)
    return dict(
        cnt=cnt.reshape(-1), base=base.reshape(-1).astype(jnp.int32), off=off.reshape(-1).astype(jnp.int32),
        tail=(pstart + tot).astype(jnp.int32), end=pend.astype(jnp.int32), lpos=lpos.astype(jnp.int32),
        blk_e=blk_e[blk_src], blk_src=blk_src, n_used=n_used.reshape(1), n_rows=n_rows)


def kernel(x, w_in, b_in, lambda_q1, lambda_k1, lambda_q2, lambda_k2, subln_g, w_proj_a, w_proj_b, w_out, b_out, ln1_g, ln1_b, w_router, b_router, w_up, b_up, w_down, b_down, ln2_g, ln2_b):
    bsz, seq, dm = x.shape
    depth = w_in.shape[0]
    alpha = (2.0 * depth) ** 0.25
    t = bsz * seq
    for layer in range(depth):
        lambda_init = 0.8 - 0.6 * math.exp(-0.3 * layer)
        x2d = x.reshape(t, dm)
        n_a, n_b = 3 * WIDTH_A, 3 * WIDTH_B
        w_l, b_l = w_in[layer], b_in[layer]
        query_scale = jnp.full((WIDTH_A,), QUERY_SCALE, _F32)
        ones = functools.partial(jnp.ones, dtype=_F32)
        qkv_a = _in_proj_a(x, w_l[:, :n_a].astype(_BF16), b_l[:n_a],
                           jnp.concatenate([query_scale, ones((2 * WIDTH_A,))]))
        w_rest = jnp.concatenate([w_l[:, n_a + n_b:], w_l[:, n_a:n_a + n_b]], axis=1).astype(_BF16)
        b_rest = jnp.concatenate([b_l[n_a + n_b:], b_l[n_a:n_a + n_b]])
        scale_rest = jnp.concatenate([ones((2 * dm,)), query_scale, ones((2 * WIDTH_B,))])
        proj2d = _in_proj(x2d, w_rest, b_rest, scale_rest)

        dil = [_dilated_pass(a, d) for a, (_, d) in zip(qkv_a, DILATED_PATTERNS)]

        lam_vecs = jnp.stack([lambda_q1[layer], lambda_k1[layer], lambda_q2[layer], lambda_k2[layer]]).astype(_F32)
        o_b = _diff_attention(proj2d.reshape(bsz, seq, -1), 2 * dm, lam_vecs, subln_g[layer],
                              lambda_init).reshape(t, WIDTH_B)

        h, idx, gates, rank, cnt = _merge_router(
            x2d, proj2d, 0, [o for o, _ in dil], [l for _, l in dil], o_b,
            w_proj_a[layer].astype(_BF16), w_proj_b[layer].astype(_BF16),
            w_out[layer].astype(_BF16), b_out[layer], ln1_g[layer], ln1_b[layer],
            w_router[layer], b_router[layer], alpha)

        plan = _moe_plan(idx, rank, cnt[:, :, 0], MOE_TM)
        xs = _dispatch(h, gates, plan)
        ys = _experts(xs, plan["blk_e"], plan["blk_src"], plan["n_used"], w_up[layer], b_up[layer],
                      w_down[layer], b_down[layer])
        out = _combine(h, ys, plan, ln2_g[layer], ln2_b[layer], alpha)
        x = out.reshape(bsz, seq, dm)
    return x
```

```python
import functools
import math

import numpy as np
import jax
import jax.numpy as jnp
from jax import lax
from jax.experimental import pallas as pl
from jax.experimental.pallas import tpu as pltpu

HEAD_DIM = 64
N_HEADS_A = 8
DILATED_PATTERNS = ((128, 1), (512, 4), (2048, 16))
N_HEADS_B = 4
WIDTH_A = N_HEADS_A * HEAD_DIM
WIDTH_B = N_HEADS_B * 2 * HEAD_DIM
N_ALIBI_HEADS = N_HEADS_A + N_HEADS_B
ALIBI_IDX_A = (0, 1, 3, 4, 6, 7, 9, 10)
ALIBI_IDX_B = (2, 5, 8, 11)
Q_BLOCK = 128
MASK_VALUE = -1e30
N_EXPERTS = 32
TOP_K = 4
SWIGLU_ALPHA = 1.702
SWIGLU_LIMIT = 7.0
LN_EPS = 1e-5
SUBLN_EPS = 1e-5
LOG2E = math.log2(math.e)
QUERY_SCALE = HEAD_DIM ** -0.5 * LOG2E

LANES = 128
V7X_VMEM_LIMIT_BYTES = 56 * 1024 * 1024

PROJ_TM = 1024
PROJ_A_TM = 512
DIL_ROWS = 2 * Q_BLOCK
DIL_HALO = 64
DIFF_TQ = 512
DIFF_TK = 512
MERGE_TM = 512
MOE_ROWS = 512
FFN_CHUNK = 512
MOE_TM = 256
ROW_ALIGN = 8
LOCAL_ROWS = -(-(TOP_K * MOE_TM + N_EXPERTS * (ROW_ALIGN - 1)) // LANES) * LANES

_F32 = jnp.float32
_BF16 = jnp.bfloat16


def _params(*sem):
    return pltpu.CompilerParams(dimension_semantics=sem, vmem_limit_bytes=V7X_VMEM_LIMIT_BYTES)


def _alibi_slopes():
    return (2.0 ** (-8.0 * np.arange(1, N_ALIBI_HEADS + 1) / N_ALIBI_HEADS)).astype(np.float32)


def _in_proj_kernel(x_ref, w_ref, b_ref, cs_ref, o_ref):
    x = x_ref[...].astype(_BF16)
    acc = jnp.dot(x, w_ref[...], preferred_element_type=_F32)
    o_ref[...] = ((acc + b_ref[...]) * cs_ref[...]).astype(o_ref.dtype)


def _in_proj(x2d, w_bf16, b, colscale):
    t, dm = x2d.shape
    n = w_bf16.shape[1]
    tn = n // 2
    assert n % 2 == 0 and tn % LANES == 0
    return pl.pallas_call(
        _in_proj_kernel,
        grid=(t // PROJ_TM, n // tn),
        in_specs=[
            pl.BlockSpec((PROJ_TM, dm), lambda i, j: (i, 0)),
            pl.BlockSpec((dm, tn), lambda i, j: (0, j)),
            pl.BlockSpec((1, tn), lambda i, j: (0, j)),
            pl.BlockSpec((1, tn), lambda i, j: (0, j)),
        ],
        out_specs=pl.BlockSpec((PROJ_TM, tn), lambda i, j: (i, j)),
        out_shape=jax.ShapeDtypeStruct((t, n), _BF16),
        compiler_params=_params("parallel", "arbitrary"),
        name="in_proj",
    )(x2d, w_bf16, b.reshape(1, n), colscale.reshape(1, n))


def _in_proj_a_kernel(x_ref, w_ref, b_ref, cs_ref, *refs):
    out_refs, acc_ref = refs[:-1], refs[-1]
    x = x_ref[0].astype(_BF16)
    acc = (jnp.dot(x, w_ref[...], preferred_element_type=_F32) + b_ref[...]) * cs_ref[...]
    n_chunks, tm, _ = acc_ref.shape
    for c in range(n_chunks):
        acc_ref[c] = acc[:, c * LANES:(c + 1) * LANES]
    for o_ref, (_, d) in zip(out_refs, DILATED_PATTERNS):
        if d == 1:
            o_ref[0, 0] = acc.astype(o_ref.dtype)
            continue
        for r in range(d):
            for c in range(n_chunks):
                rows = acc_ref[c, pl.ds(r, tm // d, stride=d), :]
                o_ref[0, r, :, c * LANES:(c + 1) * LANES] = rows.astype(o_ref.dtype)


def _in_proj_a(x, w_bf16, b, colscale):
    bsz, seq, dm = x.shape
    n = w_bf16.shape[1]
    tm = PROJ_A_TM
    per_b = seq // tm
    dils = [d for _, d in DILATED_PATTERNS]
    assert seq % tm == 0 and all(tm % d == 0 and (tm // d) % 16 == 0 for d in dils)
    return pl.pallas_call(
        _in_proj_a_kernel,
        grid=(bsz * per_b,),
        in_specs=[
            pl.BlockSpec((1, tm, dm), lambda i: (i // per_b, i % per_b, 0)),
            pl.BlockSpec((dm, n), lambda i: (0, 0)),
            pl.BlockSpec((1, n), lambda i: (0, 0)),
            pl.BlockSpec((1, n), lambda i: (0, 0)),
        ],
        out_specs=[pl.BlockSpec((1, d, tm // d, n), lambda i: (i // per_b, 0, i % per_b, 0)) for d in dils],
        out_shape=[jax.ShapeDtypeStruct((bsz, d, seq // d, n), _BF16) for d in dils],
        scratch_shapes=[pltpu.VMEM((n // LANES, tm, LANES), _F32)],
        compiler_params=_params("parallel"),
        name="in_proj_a",
    )(x, w_bf16, b.reshape(1, n), colscale.reshape(1, n))


def _dilated_bias_table(dilation):
    slopes = _alibi_slopes()[list(ALIBI_IDX_A)]
    band = Q_BLOCK + 2 * DIL_HALO
    qi = np.arange(Q_BLOCK)[:, None]
    kj = np.arange(band)[None, :]
    rel = qi - kj + DIL_HALO
    in_band = np.abs(rel) <= DIL_HALO
    base = -slopes[:, None, None] * (dilation * np.abs(rel)).astype(np.float32)[None]
    base = (base.astype(np.float64) * LOG2E).astype(np.float32)
    edge = (np.ones_like(kj, bool), kj >= DIL_HALO, kj < band - DIL_HALO)
    out = np.stack([np.where(in_band & e, base, np.float32(MASK_VALUE)) for e in edge])
    return out.astype(np.float32)


def _dilated_kernel(q_ref, km_ref, kp_ref, kn_ref, vm_ref, vp_ref, vn_ref, bias_ref, o_ref, lse_ref,
                    kbuf, vbuf, *, nblk):
    i = pl.program_id(2)
    h0, h1 = DIL_HALO, DIL_HALO + DIL_ROWS
    kbuf[0:h0, :] = kp_ref[0, 0]
    kbuf[h0:h1, :] = km_ref[0, 0]
    kbuf[h1:h1 + DIL_HALO, :] = kn_ref[0, 0]
    vbuf[0:h0, :] = vp_ref[0, 0]
    vbuf[h0:h1, :] = vm_ref[0, 0]
    vbuf[h1:h1 + DIL_HALO, :] = vn_ref[0, 0]

    lane = lax.broadcasted_iota(jnp.int32, (Q_BLOCK, LANES), 1)
    low_half = lane < HEAD_DIM
    band = Q_BLOCK + 2 * DIL_HALO
    variants = (jnp.where(i == 0, 1, 0), jnp.where(i == nblk - 1, 2, 0))

    for j in range(DIL_ROWS // Q_BLOCK):
        rows = slice(j * Q_BLOCK, (j + 1) * Q_BLOCK)
        krows = slice(j * Q_BLOCK, j * Q_BLOCK + band)
        lse_cols = []
        for hp in range(N_HEADS_A // 2):
            cols = slice(hp * LANES, (hp + 1) * LANES)
            q_pair = q_ref[0, 0, rows, cols]
            k_pair = kbuf[krows, cols]
            v_pair = vbuf[krows, cols]
            halves = []
            for hh in range(2):
                head = 2 * hp + hh
                keep = low_half if hh == 0 else jnp.logical_not(low_half)
                qm = jnp.where(keep, q_pair, jnp.zeros_like(q_pair))
                s = lax.dot_general(qm, k_pair, (((1,), (1,)), ((), ())), preferred_element_type=_F32)
                s = s + bias_ref[variants[j], head]
                m = jnp.max(s, axis=-1, keepdims=True)
                p = jnp.exp2(s - m)
                z = jnp.sum(p, axis=-1, keepdims=True)
                halves.append(jnp.dot(p.astype(_BF16), v_pair, preferred_element_type=_F32) * (1.0 / z))
                lse_cols.append(m + jnp.log2(z))
            o_ref[0, 0, rows, cols] = jnp.where(low_half, halves[0], halves[1])
        tile = jnp.zeros((Q_BLOCK, LANES), _F32)
        for head in range(N_HEADS_A):
            tile = jnp.where(lane == head, lse_cols[head], tile)
        lse_ref[0, 0, rows, :] = tile


def _dilated_pass(qkv, dilation):
    bsz, d, L, _ = qkv.shape
    assert d == dilation and L % DIL_ROWS == 0 and DIL_ROWS % DIL_HALO == 0
    nblk = L // DIL_ROWS
    per = DIL_ROWS // DIL_HALO
    n_halo = L // DIL_HALO
    bias = jnp.asarray(_dilated_bias_table(dilation))

    def main(s):
        return pl.BlockSpec((1, 1, DIL_ROWS, WIDTH_A), lambda b, r, i: (b, r, i, s))

    def left(s):
        return pl.BlockSpec((1, 1, DIL_HALO, WIDTH_A), lambda b, r, i: (b, r, jnp.maximum(per * i - 1, 0), s))

    def right(s):
        return pl.BlockSpec((1, 1, DIL_HALO, WIDTH_A),
                            lambda b, r, i: (b, r, jnp.minimum(per * i + per, n_halo - 1), s))

    rows_buf = DIL_ROWS + 2 * DIL_HALO
    return pl.pallas_call(
        functools.partial(_dilated_kernel, nblk=nblk),
        grid=(bsz, dilation, nblk),
        in_specs=[main(0), main(1), left(1), right(1), main(2), left(2), right(2),
                  pl.BlockSpec(bias.shape, lambda b, r, i: (0, 0, 0, 0))],
        out_specs=[pl.BlockSpec((1, 1, DIL_ROWS, WIDTH_A), lambda b, r, i: (b, r, i, 0)),
                   pl.BlockSpec((1, 1, DIL_ROWS, LANES), lambda b, r, i: (b, r, i, 0))],
        out_shape=[jax.ShapeDtypeStruct((bsz, dilation, L, WIDTH_A), _F32),
                   jax.ShapeDtypeStruct((bsz, dilation, L, LANES), _F32)],
        scratch_shapes=[pltpu.VMEM((rows_buf, WIDTH_A), _BF16), pltpu.VMEM((rows_buf, WIDTH_A), _BF16)],
        compiler_params=_params("parallel", "parallel", "arbitrary"),
        name=f"dilated_d{dilation}",
    )(qkv, qkv, qkv, qkv, qkv, qkv, qkv, bias)


def _split_bf16(x):
    hi = x.astype(jnp.bfloat16)
    lo = (x - hi.astype(np.float64)).astype(jnp.bfloat16)
    return hi, lo


def _diff_tables():
    s2 = _alibi_slopes()[list(ALIBI_IDX_B)].astype(np.float64) * LOG2E
    qaug = np.zeros((N_HEADS_B, DIFF_TQ, LANES), jnp.bfloat16)
    kaug = np.zeros((N_HEADS_B, 2, DIFF_TK, LANES), jnp.bfloat16)
    i = np.arange(DIFF_TQ, dtype=np.float64)
    j = np.arange(DIFF_TK, dtype=np.float64)
    for h in range(N_HEADS_B):
        r_hi, r_lo = _split_bf16(-s2[h] * i)
        qaug[h, :, 0] = 1.0
        qaug[h, :, 1] = 1.0
        qaug[h, :, 2] = r_hi
        qaug[h, :, 3] = r_lo
        for side, sign in enumerate((1.0, -1.0)):
            c_hi, c_lo = _split_bf16(sign * s2[h] * j)
            kaug[h, side, :, 0] = c_hi
            kaug[h, side, :, 1] = c_lo
            kaug[h, side, :, 2] = sign
            kaug[h, side, :, 3] = sign
    halves = DIFF_TK // DIFF_TQ
    off = (np.arange(halves) * DIFF_TQ)[:, None, None]
    dist = np.abs(off + i[None, :, None] - j[None, None, :])
    diag = (-s2[:, None, None, None] * dist[None]).astype(np.float32)
    return jnp.asarray(qaug), jnp.asarray(kaug), jnp.asarray(diag), jnp.asarray(s2.astype(np.float32))


def _diff_kernel(slope_ref, lam_ref, q_ref, k_ref, v_ref, qaug_ref, kaug_ref, diag_ref, g_ref, o_ref,
                 *, seq, lambda_init):
    head = pl.program_id(1)
    qb = pl.program_id(2)
    q0 = qb * DIFF_TQ
    halves = DIFF_TK // DIFF_TQ
    kd = qb // halves
    n_kb = seq // DIFF_TK
    slope2 = slope_ref[head]
    nt = (((1,), (1,)), ((), ()))

    q = q_ref[0]
    lane = lax.broadcasted_iota(jnp.int32, q.shape, 1)
    zero = jnp.zeros_like(q)
    qaug = qaug_ref[0]
    q_maps = (jnp.concatenate([jnp.where(lane < HEAD_DIM, q, zero), qaug], axis=1),
              jnp.concatenate([jnp.where(lane >= HEAD_DIM, q, zero), qaug], axis=1))
    zero_aug = jnp.zeros((DIFF_TK, LANES), _BF16)
    vlane = lax.broadcasted_iota(jnp.int32, (DIFF_TK, LANES), 1)
    ones_col = jnp.where(vlane == 0, 1.0, 0.0).astype(_BF16)

    def load(kb):
        k0 = pl.multiple_of(kb * DIFF_TK, DIFF_TK)
        return k_ref[0, pl.ds(k0, DIFF_TK), :], jnp.concatenate([v_ref[0, pl.ds(k0, DIFF_TK), :], ones_col], axis=1)

    def split(pv):
        return pv[:, LANES:LANES + 1], pv[:, 0:LANES]

    def update(s, const, vblk, carry):
        m_old, l_old, acc = carry
        m_new = jnp.maximum(m_old, jnp.max(s, axis=-1, keepdims=True) + const)
        alpha = jnp.exp2(m_old - m_new)
        p = jnp.exp2(s - (m_new - const))
        l_blk, o_blk = split(jnp.dot(p.astype(_BF16), vblk, preferred_element_type=_F32))
        return m_new, alpha * l_old + l_blk, alpha * acc + o_blk

    kblk, vblk = load(kd)
    k_all = jnp.concatenate([kblk, zero_aug], axis=1)
    bias = diag_ref[0, qb % halves]
    carries = []
    for mp in range(2):
        s = lax.dot_general(q_maps[mp], k_all, nt, preferred_element_type=_F32) + bias
        m = jnp.max(s, axis=-1, keepdims=True)
        p = jnp.exp2(s - m)
        carries.append((m,) + split(jnp.dot(p.astype(_BF16), vblk, preferred_element_type=_F32)))

    for t in range(n_kb - 1):
        kb = jnp.where(t >= kd, t + 1, t)
        side = jnp.where(t >= kd, 1, 0)
        kblk, vblk = load(kb)
        k_all = jnp.concatenate([kblk, kaug_ref[0, side]], axis=1)
        const = -slope2 * jnp.abs(q0 - kb * DIFF_TK).astype(_F32)
        for mp in range(2):
            s = lax.dot_general(q_maps[mp], k_all, nt, preferred_element_type=_F32)
            carries[mp] = update(s, const, vblk, carries[mp])

    lv = lam_ref[...]
    lam = (jnp.exp(jnp.sum(lv[0:1] * lv[1:2], axis=-1, keepdims=True))
           - jnp.exp(jnp.sum(lv[2:3] * lv[3:4], axis=-1, keepdims=True)) + lambda_init)
    (_, l1, a1), (_, l2, a2) = carries
    of = a1 * (1.0 / l1) - lam * (a2 * (1.0 / l2))
    of = of * lax.rsqrt(jnp.mean(of * of, axis=-1, keepdims=True) + SUBLN_EPS)
    of = of * g_ref[...] * (1.0 - lambda_init)
    o_ref[0] = of.astype(o_ref.dtype)


def _diff_attention(proj, q_offset, lam_vecs, subln_g, lambda_init):
    bsz, seq, _ = proj.shape
    assert seq % DIFF_TQ == 0 and seq % DIFF_TK == 0 and q_offset % LANES == 0
    qcol = q_offset // LANES
    kcol = qcol + WIDTH_B // LANES
    vcol = kcol + WIDTH_B // LANES
    assert DIFF_TK % DIFF_TQ == 0
    halves = DIFF_TK // DIFF_TQ
    qaug, kaug, diag, slopes2 = _diff_tables()
    return pl.pallas_call(
        functools.partial(_diff_kernel, seq=seq, lambda_init=lambda_init),
        grid=(bsz, N_HEADS_B, seq // DIFF_TQ),
        in_specs=[
            pl.BlockSpec(memory_space=pltpu.SMEM),
            pl.BlockSpec((4, HEAD_DIM), lambda b, h, i: (0, 0)),
            pl.BlockSpec((1, DIFF_TQ, LANES), lambda b, h, i: (b, i, qcol + h)),
            pl.BlockSpec((1, seq, LANES), lambda b, h, i: (b, 0, kcol + h)),
            pl.BlockSpec((1, seq, LANES), lambda b, h, i: (b, 0, vcol + h)),
            pl.BlockSpec((1, DIFF_TQ, LANES), lambda b, h, i: (h, 0, 0)),
            pl.BlockSpec((1, 2, DIFF_TK, LANES), lambda b, h, i: (h, 0, 0, 0)),
            pl.BlockSpec((1, halves, DIFF_TQ, DIFF_TK), lambda b, h, i: (h, 0, 0, 0)),
            pl.BlockSpec((1, LANES), lambda b, h, i: (0, 0)),
        ],
        out_specs=pl.BlockSpec((1, DIFF_TQ, LANES), lambda b, h, i: (b, i, h)),
        out_shape=jax.ShapeDtypeStruct((bsz, seq, WIDTH_B), _BF16),
        compiler_params=_params("parallel", "parallel", "arbitrary"),
        name="diff_attn",
    )(slopes2, lam_vecs, proj, proj, proj, qaug, kaug, diag, subln_g.reshape(1, LANES).astype(_F32))


def _layer_norm(z, g, b):
    mu = jnp.mean(z, axis=-1, keepdims=True)
    zc = z - mu
    var = jnp.mean(zc * zc, axis=-1, keepdims=True)
    return zc * lax.rsqrt(var + LN_EPS) * g + b


def _mix_patterns(o_refs, lse_refs, expand_ref, o_nat, lse_nat):
    n_chunks, tm = o_nat.shape[1], o_nat.shape[2]
    outs, lses = [], []
    for p, (_, d) in enumerate(DILATED_PATTERNS):
        if d == 1:
            outs.append(o_refs[p][0, 0])
            lses.append(lse_refs[p][0, 0])
            continue
        for r in range(d):
            for c in range(n_chunks):
                o_nat[p, c, pl.ds(r, tm // d, stride=d), :] = o_refs[p][0, r, :, c * LANES:(c + 1) * LANES]
            lse_nat[p, pl.ds(r, tm // d, stride=d), :] = lse_refs[p][0, r]
        outs.append(jnp.concatenate([o_nat[p, c] for c in range(n_chunks)], axis=1))
        lses.append(lse_nat[p])
    top = functools.reduce(jnp.maximum, lses)
    es = [jnp.exp2(l - top) for l in lses]
    inv = 1.0 / functools.reduce(lambda a, b: a + b, es)
    mixed = None
    for o, e in zip(outs, es):
        w = e * inv
        w_hi = w.astype(_BF16)
        w_lo = (w - w_hi.astype(_F32)).astype(_BF16)
        w_full = (jnp.dot(w_hi, expand_ref[...], preferred_element_type=_F32)
                  + jnp.dot(w_lo, expand_ref[...], preferred_element_type=_F32))
        mixed = w_full * o if mixed is None else mixed + w_full * o
    return mixed


def _merge_kernel(x_ref, ga_ref, gb_ref, o1_ref, o2_ref, o3_ref, l1_ref, l2_ref, l3_ref, expand_ref, ob_ref,
                  wpa_ref, wpb_ref, wout_ref, bout_ref, g1_ref, b1_ref, wrh_ref, wrl_ref, br_ref,
                  h_ref, idx_ref, gate_ref, rank_ref, cnt_ref, o_nat, lse_nat, *, alpha):
    o_a = _mix_patterns((o1_ref, o2_ref, o3_ref), (l1_ref, l2_ref, l3_ref), expand_ref, o_nat, lse_nat)
    pa = jnp.dot(o_a.astype(_BF16), wpa_ref[...], preferred_element_type=_F32)
    pb = jnp.dot(ob_ref[...], wpb_ref[...], preferred_element_type=_F32)
    merged = jax.nn.sigmoid(ga_ref[...].astype(_F32)) * pa + jax.nn.sigmoid(gb_ref[...].astype(_F32)) * pb
    y = jnp.dot(merged.astype(_BF16), wout_ref[...], preferred_element_type=_F32) + bout_ref[...]
    h = _layer_norm(alpha * x_ref[...] + y, g1_ref[...], b1_ref[...])
    h_ref[...] = h

    h_hi = h.astype(_BF16)
    h_lo = (h - h_hi.astype(_F32)).astype(_BF16)
    nt = (((1,), (1,)), ((), ()))
    logits = (lax.dot_general(wrh_ref[...], h_hi, nt, preferred_element_type=_F32)
              + lax.dot_general(wrl_ref[...], h_hi, nt, preferred_element_type=_F32)
              + lax.dot_general(wrh_ref[...], h_lo, nt, preferred_element_type=_F32)
              + br_ref[...])

    tm = logits.shape[1]
    expert = lax.broadcasted_iota(jnp.int32, logits.shape, 0)
    work = logits
    vals, sels, idxs = [], [], []
    for _ in range(TOP_K):
        mx = jnp.max(work, axis=0, keepdims=True)
        idx = jnp.min(jnp.where(work == mx, expert, N_EXPERTS), axis=0, keepdims=True)
        sel = expert == idx
        work = jnp.where(sel, -jnp.inf, work)
        vals.append(mx)
        idxs.append(idx)
        sels.append(sel)
    ex = [jnp.exp(v - vals[0]) for v in vals]
    inv = 1.0 / (ex[0] + ex[1] + ex[2] + ex[3])
    gates = [e * inv for e in ex]

    chosen = sels[0] | sels[1] | sels[2] | sels[3]
    onehot = jnp.where(chosen, 1.0, 0.0).astype(_BF16)
    earlier = lax.broadcasted_iota(jnp.int32, (tm, tm), 0)
    later = lax.broadcasted_iota(jnp.int32, (tm, tm), 1)
    sub_shift = int(math.log2(MOE_TM))
    before = (earlier < later) & (lax.shift_right_logical(earlier, sub_shift)
                                  == lax.shift_right_logical(later, sub_shift))
    upper = jnp.where(before, 1.0, 0.0).astype(_BF16)
    prefix = jnp.dot(onehot, upper, preferred_element_type=_F32)
    for k in range(TOP_K):
        rank = jnp.sum(jnp.where(sels[k], prefix, 0.0), axis=0, keepdims=True)
        rank_ref[k:k + 1, :] = rank.astype(jnp.int32)
        idx_ref[k:k + 1, :] = idxs[k]
        gate_ref[k:k + 1, :] = gates[k]
    for sub in range(tm // MOE_TM):
        part = onehot[:, sub * MOE_TM:(sub + 1) * MOE_TM].astype(_F32)
        cnt_ref[sub] = jnp.broadcast_to(jnp.sum(part, axis=1, keepdims=True), cnt_ref.shape[1:])


def _merge_router(x2d, proj2d, gate_offset, dil_outs, dil_lses, o_b, wpa, wpb, wout, bout, g1, b1,
                  w_router, b_router, alpha):
    t, dm = x2d.shape
    tm = MERGE_TM
    bsz, _, seq_over_d0, _ = dil_outs[0].shape
    seq = seq_over_d0 * DILATED_PATTERNS[0][1]
    per_b = seq // tm
    dils = [d for _, d in DILATED_PATTERNS]
    assert gate_offset % dm == 0 and seq % tm == 0 and all(tm % d == 0 for d in dils)
    gcol = gate_offset // dm
    wr_t = w_router.T.astype(_F32)
    wr_hi = wr_t.astype(_BF16)
    wr_lo = (wr_t - wr_hi.astype(_F32)).astype(_BF16)
    expand = np.zeros((LANES, WIDTH_A), np.float32)
    for head in range(N_HEADS_A):
        expand[head, head * HEAD_DIM:(head + 1) * HEAD_DIM] = 1.0
    expand = jnp.asarray(expand, _BF16)

    def const(shape):
        return pl.BlockSpec(shape, lambda i: tuple(0 for _ in shape))

    def residue_major(d, width):
        return pl.BlockSpec((1, d, tm // d, width), lambda i: (i // per_b, 0, i % per_b, 0))

    return pl.pallas_call(
        functools.partial(_merge_kernel, alpha=alpha),
        grid=(t // tm,),
        in_specs=[
            pl.BlockSpec((tm, dm), lambda i: (i, 0)),
            pl.BlockSpec((tm, dm), lambda i: (i, gcol)),
            pl.BlockSpec((tm, dm), lambda i: (i, gcol + 1)),
            *[residue_major(d, WIDTH_A) for d in dils],
            *[residue_major(d, LANES) for d in dils],
            const((LANES, WIDTH_A)),
            pl.BlockSpec((tm, WIDTH_B), lambda i: (i, 0)),
            const((WIDTH_A, dm)), const((WIDTH_B, dm)), const((dm, dm)), const((1, dm)),
            const((1, dm)), const((1, dm)),
            const((N_EXPERTS, dm)), const((N_EXPERTS, dm)), const((N_EXPERTS, 1)),
        ],
        out_specs=[
            pl.BlockSpec((tm, dm), lambda i: (i, 0)),
            pl.BlockSpec((TOP_K, tm), lambda i: (0, i)),
            pl.BlockSpec((TOP_K, tm), lambda i: (0, i)),
            pl.BlockSpec((TOP_K, tm), lambda i: (0, i)),
            pl.BlockSpec((tm // MOE_TM, N_EXPERTS, LANES), lambda i: (i, 0, 0)),
        ],
        out_shape=[
            jax.ShapeDtypeStruct((t, dm), _F32),
            jax.ShapeDtypeStruct((TOP_K, t), jnp.int32),
            jax.ShapeDtypeStruct((TOP_K, t), _F32),
            jax.ShapeDtypeStruct((TOP_K, t), jnp.int32),
            jax.ShapeDtypeStruct((t // MOE_TM, N_EXPERTS, LANES), _F32),
        ],
        scratch_shapes=[pltpu.VMEM((len(dils), WIDTH_A // LANES, tm, LANES), _F32),
                        pltpu.VMEM((len(dils), tm, LANES), _F32)],
        compiler_params=_params("parallel"),
        name="merge_router",
    )(x2d, proj2d, proj2d, *dil_outs, *dil_lses, expand, o_b, wpa, wpb, wout, bout.reshape(1, dm),
      g1.reshape(1, dm), b1.reshape(1, dm), wr_hi, wr_lo, b_router.reshape(N_EXPERTS, 1).astype(_F32))


def _group_rows(n):
    return lax.shift_right_logical(n + (ROW_ALIGN - 1), int(math.log2(ROW_ALIGN))) * ROW_ALIGN


def _dispatch_kernel(cnt_ref, base_ref, off_ref, tail_ref, end_ref, lpos_ref, gate_ref, h_ref, xs_hbm,
                     local, zeros, pending, sem, *, tm, dm):
    i = pl.program_id(0)
    rows = local.shape[1]
    buf = i % 2

    @pl.when(i == 0)
    def _():
        pending[0] = 0
    slot = lax.broadcasted_iota(jnp.int32, (rows, tm), 0)
    perm = jnp.zeros((rows, tm), _F32)
    gsel = jnp.zeros((rows, tm), _F32)
    for k in range(TOP_K):
        hit = slot == lpos_ref[k:k + 1, :]
        perm = perm + jnp.where(hit, 1.0, 0.0)
        gsel = gsel + jnp.where(hit, gate_ref[k:k + 1, :], 0.0)
    local[buf, :, 0:dm] = jnp.dot(perm.astype(_BF16), h_ref[...].astype(_BF16), preferred_element_type=_F32)
    local[buf, :, dm:dm + LANES] = jnp.broadcast_to(jnp.sum(gsel, axis=1, keepdims=True), (rows, LANES))

    def group_copy(src_row, dst_row, n_rows):
        n_rows = pl.multiple_of(n_rows, ROW_ALIGN)
        return pltpu.make_async_copy(local.at[buf, pl.ds(pl.multiple_of(src_row, ROW_ALIGN), n_rows)],
                                     xs_hbm.at[pl.ds(pl.multiple_of(dst_row, ROW_ALIGN), n_rows)], sem)

    def drain():
        @pl.when(pending[0] > 0)
        def _():
            group_copy(0, 0, pending[0]).wait()

    drain()

    total = 0
    for e in range(N_EXPERTS):
        j = i * N_EXPERTS + e
        n_rows = _group_rows(cnt_ref[j])

        @pl.when(n_rows > 0)
        def _():
            group_copy(off_ref[j], base_ref[j], n_rows).start()

        total = total + n_rows
    pending[0] = total

    @pl.when(i == pl.num_programs(0) - 1)
    def _():
        drain()
        zeros[...] = jnp.zeros_like(zeros)

        def fill(e, carry):
            gap = end_ref[e] - tail_ref[e]
            pos = tail_ref[e]
            size = MOE_ROWS
            while size >= ROW_ALIGN:
                take = (gap & size) != 0
                cp = pltpu.make_async_copy(zeros.at[pl.ds(0, size)],
                                           xs_hbm.at[pl.ds(pl.multiple_of(pos, ROW_ALIGN), size)], sem)

                @pl.when(take)
                def _():
                    cp.start()
                    cp.wait()

                pos = pos + jnp.where(take, size, 0)
                size //= 2
            return carry

        lax.fori_loop(0, N_EXPERTS, fill, 0)

        def fill_block(blk, carry):
            cp = pltpu.make_async_copy(
                zeros, xs_hbm.at[pl.ds(pl.multiple_of(blk * MOE_ROWS, MOE_ROWS), MOE_ROWS)], sem)
            cp.start()
            cp.wait()
            return carry

        first_unused = lax.shift_right_logical(end_ref[N_EXPERTS - 1], int(math.log2(MOE_ROWS)))
        lax.fori_loop(first_unused, xs_hbm.shape[0] // MOE_ROWS, fill_block, 0)


def _dispatch(h, gates, plan):
    t, dm = h.shape
    tm = MOE_TM
    width = dm + LANES
    grid_spec = pltpu.PrefetchScalarGridSpec(
        num_scalar_prefetch=5,
        grid=(t // tm,),
        in_specs=[
            pl.BlockSpec((TOP_K, tm), lambda i, *_: (0, i)),
            pl.BlockSpec((TOP_K, tm), lambda i, *_: (0, i)),
            pl.BlockSpec((tm, dm), lambda i, *_: (i, 0)),
        ],
        out_specs=pl.BlockSpec(memory_space=pl.ANY),
        scratch_shapes=[pltpu.VMEM((2, LOCAL_ROWS, width), _F32), pltpu.VMEM((MOE_ROWS, width), _F32),
                        pltpu.SMEM((1,), jnp.int32), pltpu.SemaphoreType.DMA(())],
    )
    return pl.pallas_call(
        functools.partial(_dispatch_kernel, tm=tm, dm=dm),
        grid_spec=grid_spec,
        out_shape=jax.ShapeDtypeStruct((plan["n_rows"], width), _F32),
        compiler_params=_params("arbitrary"),
        name="moe_dispatch",
    )(plan["cnt"], plan["base"], plan["off"], plan["tail"], plan["end"], plan["lpos"], gates, h)


def _expert_kernel(blk_e_ref, blk_src_ref, n_used_ref, xs_ref, wup_ref, bup_ref, wdn_ref, bdn_ref, ys_ref,
                   wup_bf, wdn_bf, *, d_expert):
    del blk_src_ref
    i = pl.program_id(0)
    dm = ys_ref.shape[1]
    active = i < n_used_ref[0]

    @pl.when(active & ((i == 0) | (blk_e_ref[i] != blk_e_ref[jnp.maximum(i - 1, 0)])))
    def _():
        wup_bf[...] = wup_ref[0].astype(_BF16)
        wdn_bf[...] = wdn_ref[0].astype(_BF16)

    @pl.when(active)
    def _():
        x = xs_ref[:, 0:dm].astype(_BF16)
        row_gate = xs_ref[:, dm:dm + 1]
        acc = jnp.zeros(ys_ref.shape, _F32)
        for c in range(d_expert // FFN_CHUNK):
            lo, hi = c * FFN_CHUNK, (c + 1) * FFN_CHUNK
            g = jnp.dot(x, wup_bf[:, lo:hi], preferred_element_type=_F32) + bup_ref[0, :, lo:hi]
            u = (jnp.dot(x, wup_bf[:, d_expert + lo:d_expert + hi], preferred_element_type=_F32)
                 + bup_ref[0, :, d_expert + lo:d_expert + hi])
            gate = jnp.minimum(g, SWIGLU_LIMIT)
            up = jnp.clip(u, -SWIGLU_LIMIT, SWIGLU_LIMIT)
            act = gate * jax.nn.sigmoid(SWIGLU_ALPHA * gate) * (up + 1.0)
            acc = acc + jnp.dot(act.astype(_BF16), wdn_bf[lo:hi, :], preferred_element_type=_F32)
        ys_ref[...] = ((acc + bdn_ref[0]) * row_gate).astype(_BF16).astype(_F32)

    @pl.when(jnp.logical_not(active))
    def _():
        ys_ref[...] = jnp.zeros_like(ys_ref)


def _experts(xs, blk_e, blk_src, n_used, w_up, b_up, w_down, b_down):
    n_rows, width = xs.shape
    dm = width - LANES
    n_blocks = n_rows // MOE_ROWS
    n_exp, _, two_de = w_up.shape
    d_expert = two_de // 2
    grid_spec = pltpu.PrefetchScalarGridSpec(
        num_scalar_prefetch=3,
        grid=(n_blocks,),
        in_specs=[
            pl.BlockSpec((MOE_ROWS, width), lambda i, be, bs, nu: (bs[i], 0)),
            pl.BlockSpec((1, dm, two_de), lambda i, be, bs, nu: (be[i], 0, 0)),
            pl.BlockSpec((1, 1, two_de), lambda i, be, bs, nu: (be[i], 0, 0)),
            pl.BlockSpec((1, d_expert, dm), lambda i, be, bs, nu: (be[i], 0, 0)),
            pl.BlockSpec((1, 1, dm), lambda i, be, bs, nu: (be[i], 0, 0)),
        ],
        out_specs=pl.BlockSpec((MOE_ROWS, dm), lambda i, be, bs, nu: (i, 0)),
        scratch_shapes=[pltpu.VMEM((dm, two_de), _BF16), pltpu.VMEM((d_expert, dm), _BF16)],
    )
    return pl.pallas_call(
        functools.partial(_expert_kernel, d_expert=d_expert),
        grid_spec=grid_spec,
        out_shape=jax.ShapeDtypeStruct((n_rows, dm), _F32),
        compiler_params=_params("arbitrary"),
        name="moe_experts",
    )(blk_e, blk_src, n_used, xs, w_up, b_up.reshape(n_exp, 1, two_de), w_down, b_down.reshape(n_exp, 1, dm))


def _combine_kernel(cnt_ref, base_ref, off_ref, rows_ref, lpos_ref, h_ref, g2_ref, b2_ref, ys_hbm, o_ref,
                    local, sems, *, tm, alpha):
    i = pl.program_id(0)
    n_tiles = pl.num_programs(0)
    rows = local.shape[1]

    def group_copy(buf, src_row, dst_row, n_rows):
        n_rows = pl.multiple_of(n_rows, ROW_ALIGN)
        return pltpu.make_async_copy(ys_hbm.at[pl.ds(pl.multiple_of(src_row, ROW_ALIGN), n_rows)],
                                     local.at[buf, pl.ds(pl.multiple_of(dst_row, ROW_ALIGN), n_rows)],
                                     sems.at[buf])

    def fetch(tile):
        for e in range(N_EXPERTS):
            j = tile * N_EXPERTS + e
            n_rows = _group_rows(cnt_ref[j])

            @pl.when(n_rows > 0)
            def _():
                group_copy(tile % 2, base_ref[j], off_ref[j], n_rows).start()

    @pl.when(i == 0)
    def _():
        local[...] = jnp.zeros_like(local)
        fetch(i)

    @pl.when(i + 1 < n_tiles)
    def _():
        fetch(i + 1)

    slot = lax.broadcasted_iota(jnp.int32, (tm, rows), 1)
    pick = jnp.zeros((tm, rows), _F32)
    for k in range(TOP_K):
        pick = pick + jnp.where(slot == lpos_ref[:, k:k + 1], 1.0, 0.0)

    group_copy(i % 2, 0, 0, rows_ref[i]).wait()
    y = jnp.dot(pick.astype(_BF16), local[i % 2].astype(_BF16), preferred_element_type=_F32)
    o_ref[...] = _layer_norm(alpha * h_ref[...] + y, g2_ref[...], b2_ref[...])


def _combine(h, ys, plan, g2, b2, alpha):
    t, dm = h.shape
    tm = MOE_TM
    grid_spec = pltpu.PrefetchScalarGridSpec(
        num_scalar_prefetch=4,
        grid=(t // tm,),
        in_specs=[
            pl.BlockSpec((tm, TOP_K), lambda i, *_: (i, 0)),
            pl.BlockSpec((tm, dm), lambda i, *_: (i, 0)),
            pl.BlockSpec((1, dm), lambda i, *_: (0, 0)),
            pl.BlockSpec((1, dm), lambda i, *_: (0, 0)),
            pl.BlockSpec(memory_space=pl.ANY),
        ],
        out_specs=pl.BlockSpec((tm, dm), lambda i, *_: (i, 0)),
        scratch_shapes=[pltpu.VMEM((2, LOCAL_ROWS, dm), _F32), pltpu.SemaphoreType.DMA((2,))],
    )
    return pl.pallas_call(
        functools.partial(_combine_kernel, tm=tm, alpha=alpha),
        grid_spec=grid_spec,
        out_shape=jax.ShapeDtypeStruct((t, dm), _F32),
        compiler_params=_params("arbitrary"),
        name="moe_combine",
    )(plan["cnt"], plan["base"], plan["off"], plan["tile_rows"], plan["lpos"].T, h, g2.reshape(1, dm),
      b2.reshape(1, dm), ys)


def _round_up(x, m):
    return (x + m - 1) // m * m


def _moe_plan(idx, rank, tile_cnt, tm):
    n_tiles, n_exp = tile_cnt.shape
    t = idx.shape[1]
    cnt = tile_cnt.astype(jnp.int32)
    grp = _round_up(cnt, ROW_ALIGN)
    tot = jnp.sum(grp, axis=0)
    padded = _round_up(tot, MOE_ROWS)
    pend = jnp.cumsum(padded)
    pstart = pend - padded
    base = pstart[None, :] + jnp.cumsum(grp, axis=0) - grp
    off = jnp.cumsum(grp, axis=1) - grp
    onehot = idx[..., None] == jnp.arange(n_exp, dtype=jnp.int32)
    off_tok = jnp.broadcast_to(off[:, None, :], (n_tiles, tm, n_exp)).reshape(t, n_exp)
    lpos = jnp.sum(jnp.where(onehot, off_tok[None], 0), axis=-1) + rank

    n_rows = _round_up(t * idx.shape[0] + n_tiles * n_exp * (ROW_ALIGN - 1) + n_exp * (MOE_ROWS - 1), MOE_ROWS)
    n_blocks = n_rows // MOE_ROWS
    blk_start = jnp.arange(n_blocks, dtype=jnp.int32) * MOE_ROWS
    blk_e = jnp.sum((pend[None, :] <= blk_start[:, None]).astype(jnp.int32), axis=1)
    blk_e = jnp.minimum(blk_e, n_exp - 1)
    n_used = (pend[-1] // MOE_ROWS).astype(jnp.int32)
    blk_src = jnp.minimum(jnp.arange(n_blocks, dtype=jnp.int32), n_used - 1)
    return dict(
        cnt=cnt.reshape(-1), base=base.reshape(-1).astype(jnp.int32), off=off.reshape(-1).astype(jnp.int32),
        tile_rows=jnp.sum(grp, axis=1).astype(jnp.int32),
        tail=(pstart + tot).astype(jnp.int32), end=pend.astype(jnp.int32), lpos=lpos.astype(jnp.int32),
        blk_e=blk_e[blk_src], blk_src=blk_src, n_used=n_used.reshape(1), n_rows=n_rows)


def kernel(x, w_in, b_in, lambda_q1, lambda_k1, lambda_q2, lambda_k2, subln_g, w_proj_a, w_proj_b, w_out, b_out, ln1_g, ln1_b, w_router, b_router, w_up, b_up, w_down, b_down, ln2_g, ln2_b):
    bsz, seq, dm = x.shape
    depth = w_in.shape[0]
    alpha = (2.0 * depth) ** 0.25
    t = bsz * seq
    for layer in range(depth):
        lambda_init = 0.8 - 0.6 * math.exp(-0.3 * layer)
        x2d = x.reshape(t, dm)
        n_a, n_b = 3 * WIDTH_A, 3 * WIDTH_B
        w_l, b_l = w_in[layer], b_in[layer]
        query_scale = jnp.full((WIDTH_A,), QUERY_SCALE, _F32)
        ones = functools.partial(jnp.ones, dtype=_F32)
        qkv_a = _in_proj_a(x, w_l[:, :n_a].astype(_BF16), b_l[:n_a],
                           jnp.concatenate([query_scale, ones((2 * WIDTH_A,))]))
        w_rest = jnp.concatenate([w_l[:, n_a + n_b:], w_l[:, n_a:n_a + n_b]], axis=1).astype(_BF16)
        b_rest = jnp.concatenate([b_l[n_a + n_b:], b_l[n_a:n_a + n_b]])
        scale_rest = jnp.concatenate([ones((2 * dm,)), query_scale, ones((2 * WIDTH_B,))])
        proj2d = _in_proj(x2d, w_rest, b_rest, scale_rest)

        dil = [_dilated_pass(a, d) for a, (_, d) in zip(qkv_a, DILATED_PATTERNS)]

        lam_vecs = jnp.stack([lambda_q1[layer], lambda_k1[layer], lambda_q2[layer], lambda_k2[layer]]).astype(_F32)
        o_b = _diff_attention(proj2d.reshape(bsz, seq, -1), 2 * dm, lam_vecs, subln_g[layer],
                              lambda_init).reshape(t, WIDTH_B)

        h, idx, gates, rank, cnt = _merge_router(
            x2d, proj2d, 0, [o for o, _ in dil], [l for _, l in dil], o_b,
            w_proj_a[layer].astype(_BF16), w_proj_b[layer].astype(_BF16),
            w_out[layer].astype(_BF16), b_out[layer], ln1_g[layer], ln1_b[layer],
            w_router[layer], b_router[layer], alpha)

        plan = _moe_plan(idx, rank, cnt[:, :, 0], MOE_TM)
        xs = _dispatch(h, gates, plan)
        ys = _experts(xs, plan["blk_e"], plan["blk_src"], plan["n_used"], w_up[layer], b_up[layer],
                      w_down[layer], b_down[layer])
        out = _combine(h, ys, plan, ln2_g[layer], ln2_b[layer], alpha)
        x = out.reshape(bsz, seq, dm)
    return x
```

```python
import functools
import math

import numpy as np
import jax
import jax.numpy as jnp
from jax import lax
from jax.experimental import pallas as pl
from jax.experimental.pallas import tpu as pltpu

HEAD_DIM = 64
N_HEADS_A = 8
DILATED_PATTERNS = ((128, 1), (512, 4), (2048, 16))
N_HEADS_B = 4
WIDTH_A = N_HEADS_A * HEAD_DIM
WIDTH_B = N_HEADS_B * 2 * HEAD_DIM
N_ALIBI_HEADS = N_HEADS_A + N_HEADS_B
ALIBI_IDX_A = (0, 1, 3, 4, 6, 7, 9, 10)
ALIBI_IDX_B = (2, 5, 8, 11)
Q_BLOCK = 128
MASK_VALUE = -1e30
N_EXPERTS = 32
TOP_K = 4
SWIGLU_ALPHA = 1.702
SWIGLU_LIMIT = 7.0
LN_EPS = 1e-5
SUBLN_EPS = 1e-5
LOG2E = math.log2(math.e)
QUERY_SCALE = HEAD_DIM ** -0.5 * LOG2E

LANES = 128
V7X_VMEM_LIMIT_BYTES = 56 * 1024 * 1024

PROJ_TM = 1024
PROJ_A_TM = 512
DIL_ROWS = 2 * Q_BLOCK
DIL_HALO = 64
DIFF_TQ = 512
DIFF_TK = 512
MERGE_TM = 512
MOE_ROWS = 512
FFN_CHUNK = 512
MOE_TM = 256
ROW_ALIGN = 8
LOCAL_ROWS = -(-(TOP_K * MOE_TM + N_EXPERTS * (ROW_ALIGN - 1)) // LANES) * LANES

_F32 = jnp.float32
_BF16 = jnp.bfloat16


def _params(*sem):
    return pltpu.CompilerParams(dimension_semantics=sem, vmem_limit_bytes=V7X_VMEM_LIMIT_BYTES)


def _alibi_slopes():
    return (2.0 ** (-8.0 * np.arange(1, N_ALIBI_HEADS + 1) / N_ALIBI_HEADS)).astype(np.float32)


def _in_proj_kernel(x_ref, w_ref, b_ref, cs_ref, o_ref):
    x = x_ref[...].astype(_BF16)
    acc = jnp.dot(x, w_ref[...], preferred_element_type=_F32)
    o_ref[...] = ((acc + b_ref[...]) * cs_ref[...]).astype(o_ref.dtype)


def _in_proj(x2d, w_bf16, b, colscale):
    t, dm = x2d.shape
    n = w_bf16.shape[1]
    tn = n // 2
    assert n % 2 == 0 and tn % LANES == 0
    return pl.pallas_call(
        _in_proj_kernel,
        grid=(t // PROJ_TM, n // tn),
        in_specs=[
            pl.BlockSpec((PROJ_TM, dm), lambda i, j: (i, 0)),
            pl.BlockSpec((dm, tn), lambda i, j: (0, j)),
            pl.BlockSpec((1, tn), lambda i, j: (0, j)),
            pl.BlockSpec((1, tn), lambda i, j: (0, j)),
        ],
        out_specs=pl.BlockSpec((PROJ_TM, tn), lambda i, j: (i, j)),
        out_shape=jax.ShapeDtypeStruct((t, n), _BF16),
        compiler_params=_params("parallel", "arbitrary"),
        name="in_proj",
    )(x2d, w_bf16, b.reshape(1, n), colscale.reshape(1, n))


def _in_proj_a_kernel(x_ref, w_ref, b_ref, cs_ref, *refs):
    out_refs, acc_ref = refs[:-1], refs[-1]
    x = x_ref[0].astype(_BF16)
    acc = (jnp.dot(x, w_ref[...], preferred_element_type=_F32) + b_ref[...]) * cs_ref[...]
    n_chunks, tm, _ = acc_ref.shape
    for c in range(n_chunks):
        acc_ref[c] = acc[:, c * LANES:(c + 1) * LANES]
    for o_ref, (_, d) in zip(out_refs, DILATED_PATTERNS):
        if d == 1:
            o_ref[0, 0] = acc.astype(o_ref.dtype)
            continue
        for r in range(d):
            for c in range(n_chunks):
                rows = acc_ref[c, pl.ds(r, tm // d, stride=d), :]
                o_ref[0, r, :, c * LANES:(c + 1) * LANES] = rows.astype(o_ref.dtype)


def _in_proj_a(x, w_bf16, b, colscale):
    bsz, seq, dm = x.shape
    n = w_bf16.shape[1]
    tm = PROJ_A_TM
    per_b = seq // tm
    dils = [d for _, d in DILATED_PATTERNS]
    assert seq % tm == 0 and all(tm % d == 0 and (tm // d) % 16 == 0 for d in dils)
    return pl.pallas_call(
        _in_proj_a_kernel,
        grid=(bsz * per_b,),
        in_specs=[
            pl.BlockSpec((1, tm, dm), lambda i: (i // per_b, i % per_b, 0)),
            pl.BlockSpec((dm, n), lambda i: (0, 0)),
            pl.BlockSpec((1, n), lambda i: (0, 0)),
            pl.BlockSpec((1, n), lambda i: (0, 0)),
        ],
        out_specs=[pl.BlockSpec((1, d, tm // d, n), lambda i: (i // per_b, 0, i % per_b, 0)) for d in dils],
        out_shape=[jax.ShapeDtypeStruct((bsz, d, seq // d, n), _BF16) for d in dils],
        scratch_shapes=[pltpu.VMEM((n // LANES, tm, LANES), _F32)],
        compiler_params=_params("parallel"),
        name="in_proj_a",
    )(x, w_bf16, b.reshape(1, n), colscale.reshape(1, n))


def _dilated_bias_table(dilation):
    slopes = _alibi_slopes()[list(ALIBI_IDX_A)]
    band = Q_BLOCK + 2 * DIL_HALO
    qi = np.arange(Q_BLOCK)[:, None]
    kj = np.arange(band)[None, :]
    rel = qi - kj + DIL_HALO
    in_band = np.abs(rel) <= DIL_HALO
    base = -slopes[:, None, None] * (dilation * np.abs(rel)).astype(np.float32)[None]
    base = (base.astype(np.float64) * LOG2E).astype(np.float32)
    edge = (np.ones_like(kj, bool), kj >= DIL_HALO, kj < band - DIL_HALO)
    out = np.stack([np.where(in_band & e, base, np.float32(MASK_VALUE)) for e in edge])
    return out.astype(np.float32)


def _dilated_kernel(q_ref, km_ref, kp_ref, kn_ref, vm_ref, vp_ref, vn_ref, bias_ref, o_ref, lse_ref,
                    kbuf, vbuf, *, nblk):
    i = pl.program_id(2)
    h0, h1 = DIL_HALO, DIL_HALO + DIL_ROWS
    kbuf[0:h0, :] = kp_ref[0, 0]
    kbuf[h0:h1, :] = km_ref[0, 0]
    kbuf[h1:h1 + DIL_HALO, :] = kn_ref[0, 0]
    vbuf[0:h0, :] = vp_ref[0, 0]
    vbuf[h0:h1, :] = vm_ref[0, 0]
    vbuf[h1:h1 + DIL_HALO, :] = vn_ref[0, 0]

    lane = lax.broadcasted_iota(jnp.int32, (Q_BLOCK, LANES), 1)
    low_half = lane < HEAD_DIM
    band = Q_BLOCK + 2 * DIL_HALO
    variants = (jnp.where(i == 0, 1, 0), jnp.where(i == nblk - 1, 2, 0))

    for j in range(DIL_ROWS // Q_BLOCK):
        rows = slice(j * Q_BLOCK, (j + 1) * Q_BLOCK)
        krows = slice(j * Q_BLOCK, j * Q_BLOCK + band)
        lse_cols = []
        for hp in range(N_HEADS_A // 2):
            cols = slice(hp * LANES, (hp + 1) * LANES)
            q_pair = q_ref[0, 0, rows, cols]
            k_pair = kbuf[krows, cols]
            v_pair = vbuf[krows, cols]
            halves = []
            for hh in range(2):
                head = 2 * hp + hh
                keep = low_half if hh == 0 else jnp.logical_not(low_half)
                qm = jnp.where(keep, q_pair, jnp.zeros_like(q_pair))
                s = lax.dot_general(qm, k_pair, (((1,), (1,)), ((), ())), preferred_element_type=_F32)
                s = s + bias_ref[variants[j], head]
                m = jnp.max(s, axis=-1, keepdims=True)
                p = jnp.exp2(s - m)
                z = jnp.sum(p, axis=-1, keepdims=True)
                halves.append(jnp.dot(p.astype(_BF16), v_pair, preferred_element_type=_F32) * (1.0 / z))
                lse_cols.append(m + jnp.log2(z))
            o_ref[0, 0, rows, cols] = jnp.where(low_half, halves[0], halves[1])
        tile = jnp.zeros((Q_BLOCK, LANES), _F32)
        for head in range(N_HEADS_A):
            tile = jnp.where(lane == head, lse_cols[head], tile)
        lse_ref[0, 0, rows, :] = tile


def _dilated_pass(qkv, dilation):
    bsz, d, L, _ = qkv.shape
    assert d == dilation and L % DIL_ROWS == 0 and DIL_ROWS % DIL_HALO == 0
    nblk = L // DIL_ROWS
    per = DIL_ROWS // DIL_HALO
    n_halo = L // DIL_HALO
    bias = jnp.asarray(_dilated_bias_table(dilation))

    def main(s):
        return pl.BlockSpec((1, 1, DIL_ROWS, WIDTH_A), lambda b, r, i: (b, r, i, s))

    def left(s):
        return pl.BlockSpec((1, 1, DIL_HALO, WIDTH_A), lambda b, r, i: (b, r, jnp.maximum(per * i - 1, 0), s))

    def right(s):
        return pl.BlockSpec((1, 1, DIL_HALO, WIDTH_A),
                            lambda b, r, i: (b, r, jnp.minimum(per * i + per, n_halo - 1), s))

    rows_buf = DIL_ROWS + 2 * DIL_HALO
    return pl.pallas_call(
        functools.partial(_dilated_kernel, nblk=nblk),
        grid=(bsz, dilation, nblk),
        in_specs=[main(0), main(1), left(1), right(1), main(2), left(2), right(2),
                  pl.BlockSpec(bias.shape, lambda b, r, i: (0, 0, 0, 0))],
        out_specs=[pl.BlockSpec((1, 1, DIL_ROWS, WIDTH_A), lambda b, r, i: (b, r, i, 0)),
                   pl.BlockSpec((1, 1, DIL_ROWS, LANES), lambda b, r, i: (b, r, i, 0))],
        out_shape=[jax.ShapeDtypeStruct((bsz, dilation, L, WIDTH_A), _F32),
                   jax.ShapeDtypeStruct((bsz, dilation, L, LANES), _F32)],
        scratch_shapes=[pltpu.VMEM((rows_buf, WIDTH_A), _BF16), pltpu.VMEM((rows_buf, WIDTH_A), _BF16)],
        compiler_params=_params("parallel", "parallel", "arbitrary"),
        name=f"dilated_d{dilation}",
    )(qkv, qkv, qkv, qkv, qkv, qkv, qkv, bias)


def _split_bf16(x):
    hi = x.astype(jnp.bfloat16)
    lo = (x - hi.astype(np.float64)).astype(jnp.bfloat16)
    return hi, lo


def _diff_tables():
    s2 = _alibi_slopes()[list(ALIBI_IDX_B)].astype(np.float64) * LOG2E
    qaug = np.zeros((N_HEADS_B, DIFF_TQ, LANES), jnp.bfloat16)
    kaug = np.zeros((N_HEADS_B, 2, DIFF_TK, LANES), jnp.bfloat16)
    i = np.arange(DIFF_TQ, dtype=np.float64)
    j = np.arange(DIFF_TK, dtype=np.float64)
    for h in range(N_HEADS_B):
        r_hi, r_lo = _split_bf16(-s2[h] * i)
        qaug[h, :, 0] = 1.0
        qaug[h, :, 1] = 1.0
        qaug[h, :, 2] = r_hi
        qaug[h, :, 3] = r_lo
        for side, sign in enumerate((1.0, -1.0)):
            c_hi, c_lo = _split_bf16(sign * s2[h] * j)
            kaug[h, side, :, 0] = c_hi
            kaug[h, side, :, 1] = c_lo
            kaug[h, side, :, 2] = sign
            kaug[h, side, :, 3] = sign
    halves = DIFF_TK // DIFF_TQ
    off = (np.arange(halves) * DIFF_TQ)[:, None, None]
    dist = np.abs(off + i[None, :, None] - j[None, None, :])
    diag = (-s2[:, None, None, None] * dist[None]).astype(np.float32)
    return jnp.asarray(qaug), jnp.asarray(kaug), jnp.asarray(diag), jnp.asarray(s2.astype(np.float32))


def _diff_kernel(slope_ref, lam_ref, q_ref, k_ref, v_ref, qaug_ref, kaug_ref, diag_ref, g_ref, o_ref,
                 *, seq, lambda_init):
    head = pl.program_id(1)
    qb = pl.program_id(2)
    q0 = qb * DIFF_TQ
    halves = DIFF_TK // DIFF_TQ
    kd = qb // halves
    n_kb = seq // DIFF_TK
    slope2 = slope_ref[head]
    nt = (((1,), (1,)), ((), ()))

    q = q_ref[0]
    lane = lax.broadcasted_iota(jnp.int32, q.shape, 1)
    zero = jnp.zeros_like(q)
    qaug = qaug_ref[0]
    q_maps = (jnp.concatenate([jnp.where(lane < HEAD_DIM, q, zero), qaug], axis=1),
              jnp.concatenate([jnp.where(lane >= HEAD_DIM, q, zero), qaug], axis=1))
    zero_aug = jnp.zeros((DIFF_TK, LANES), _BF16)
    vlane = lax.broadcasted_iota(jnp.int32, (DIFF_TK, LANES), 1)
    ones_col = jnp.where(vlane == 0, 1.0, 0.0).astype(_BF16)

    def load(kb):
        k0 = pl.multiple_of(kb * DIFF_TK, DIFF_TK)
        return k_ref[0, pl.ds(k0, DIFF_TK), :], jnp.concatenate([v_ref[0, pl.ds(k0, DIFF_TK), :], ones_col], axis=1)

    def split(pv):
        return pv[:, LANES:LANES + 1], pv[:, 0:LANES]

    def update(s, const, vblk, carry):
        m_old, l_old, acc = carry
        m_new = jnp.maximum(m_old, jnp.max(s, axis=-1, keepdims=True) + const)
        alpha = jnp.exp2(m_old - m_new)
        p = jnp.exp2(s - (m_new - const))
        l_blk, o_blk = split(jnp.dot(p.astype(_BF16), vblk, preferred_element_type=_F32))
        return m_new, alpha * l_old + l_blk, alpha * acc + o_blk

    kblk, vblk = load(kd)
    k_all = jnp.concatenate([kblk, zero_aug], axis=1)
    bias = diag_ref[0, qb % halves]
    carries = []
    for mp in range(2):
        s = lax.dot_general(q_maps[mp], k_all, nt, preferred_element_type=_F32) + bias
        m = jnp.max(s, axis=-1, keepdims=True)
        p = jnp.exp2(s - m)
        carries.append((m,) + split(jnp.dot(p.astype(_BF16), vblk, preferred_element_type=_F32)))

    for t in range(n_kb - 1):
        kb = jnp.where(t >= kd, t + 1, t)
        side = jnp.where(t >= kd, 1, 0)
        kblk, vblk = load(kb)
        k_all = jnp.concatenate([kblk, kaug_ref[0, side]], axis=1)
        const = -slope2 * jnp.abs(q0 - kb * DIFF_TK).astype(_F32)
        for mp in range(2):
            s = lax.dot_general(q_maps[mp], k_all, nt, preferred_element_type=_F32)
            carries[mp] = update(s, const, vblk, carries[mp])

    lv = lam_ref[...]
    lam = (jnp.exp(jnp.sum(lv[0:1] * lv[1:2], axis=-1, keepdims=True))
           - jnp.exp(jnp.sum(lv[2:3] * lv[3:4], axis=-1, keepdims=True)) + lambda_init)
    (_, l1, a1), (_, l2, a2) = carries
    of = a1 * (1.0 / l1) - lam * (a2 * (1.0 / l2))
    of = of * lax.rsqrt(jnp.mean(of * of, axis=-1, keepdims=True) + SUBLN_EPS)
    of = of * g_ref[...] * (1.0 - lambda_init)
    o_ref[0] = of.astype(o_ref.dtype)


def _diff_attention(proj, q_offset, lam_vecs, subln_g, lambda_init):
    bsz, seq, _ = proj.shape
    assert seq % DIFF_TQ == 0 and seq % DIFF_TK == 0 and q_offset % LANES == 0
    qcol = q_offset // LANES
    kcol = qcol + WIDTH_B // LANES
    vcol = kcol + WIDTH_B // LANES
    assert DIFF_TK % DIFF_TQ == 0
    halves = DIFF_TK // DIFF_TQ
    qaug, kaug, diag, slopes2 = _diff_tables()
    return pl.pallas_call(
        functools.partial(_diff_kernel, seq=seq, lambda_init=lambda_init),
        grid=(bsz, N_HEADS_B, seq // DIFF_TQ),
        in_specs=[
            pl.BlockSpec(memory_space=pltpu.SMEM),
            pl.BlockSpec((4, HEAD_DIM), lambda b, h, i: (0, 0)),
            pl.BlockSpec((1, DIFF_TQ, LANES), lambda b, h, i: (b, i, qcol + h)),
            pl.BlockSpec((1, seq, LANES), lambda b, h, i: (b, 0, kcol + h)),
            pl.BlockSpec((1, seq, LANES), lambda b, h, i: (b, 0, vcol + h)),
            pl.BlockSpec((1, DIFF_TQ, LANES), lambda b, h, i: (h, 0, 0)),
            pl.BlockSpec((1, 2, DIFF_TK, LANES), lambda b, h, i: (h, 0, 0, 0)),
            pl.BlockSpec((1, halves, DIFF_TQ, DIFF_TK), lambda b, h, i: (h, 0, 0, 0)),
            pl.BlockSpec((1, LANES), lambda b, h, i: (0, 0)),
        ],
        out_specs=pl.BlockSpec((1, DIFF_TQ, LANES), lambda b, h, i: (b, i, h)),
        out_shape=jax.ShapeDtypeStruct((bsz, seq, WIDTH_B), _BF16),
        compiler_params=_params("parallel", "parallel", "arbitrary"),
        name="diff_attn",
    )(slopes2, lam_vecs, proj, proj, proj, qaug, kaug, diag, subln_g.reshape(1, LANES).astype(_F32))


def _layer_norm(z, g, b):
    mu = jnp.mean(z, axis=-1, keepdims=True)
    zc = z - mu
    var = jnp.mean(zc * zc, axis=-1, keepdims=True)
    return zc * lax.rsqrt(var + LN_EPS) * g + b


def _mix_patterns(o_refs, lse_refs, expand_ref, o_nat, lse_nat):
    n_chunks, tm = o_nat.shape[1], o_nat.shape[2]
    outs, lses = [], []
    for p, (_, d) in enumerate(DILATED_PATTERNS):
        if d == 1:
            outs.append(o_refs[p][0, 0])
            lses.append(lse_refs[p][0, 0])
            continue
        for r in range(d):
            for c in range(n_chunks):
                o_nat[p, c, pl.ds(r, tm // d, stride=d), :] = o_refs[p][0, r, :, c * LANES:(c + 1) * LANES]
            lse_nat[p, pl.ds(r, tm // d, stride=d), :] = lse_refs[p][0, r]
        outs.append(jnp.concatenate([o_nat[p, c] for c in range(n_chunks)], axis=1))
        lses.append(lse_nat[p])
    top = functools.reduce(jnp.maximum, lses)
    es = [jnp.exp2(l - top) for l in lses]
    inv = 1.0 / functools.reduce(lambda a, b: a + b, es)
    mixed = None
    for o, e in zip(outs, es):
        w = e * inv
        w_hi = w.astype(_BF16)
        w_lo = (w - w_hi.astype(_F32)).astype(_BF16)
        w_full = (jnp.dot(w_hi, expand_ref[...], preferred_element_type=_F32)
                  + jnp.dot(w_lo, expand_ref[...], preferred_element_type=_F32))
        mixed = w_full * o if mixed is None else mixed + w_full * o
    return mixed


def _merge_kernel(x_ref, ga_ref, gb_ref, o1_ref, o2_ref, o3_ref, l1_ref, l2_ref, l3_ref, expand_ref, ob_ref,
                  wpa_ref, wpb_ref, wout_ref, bout_ref, g1_ref, b1_ref, wrh_ref, wrl_ref, br_ref,
                  h_ref, idx_ref, gate_ref, rank_ref, cnt_ref, o_nat, lse_nat, *, alpha):
    o_a = _mix_patterns((o1_ref, o2_ref, o3_ref), (l1_ref, l2_ref, l3_ref), expand_ref, o_nat, lse_nat)
    pa = jnp.dot(o_a.astype(_BF16), wpa_ref[...], preferred_element_type=_F32)
    pb = jnp.dot(ob_ref[...], wpb_ref[...], preferred_element_type=_F32)
    merged = jax.nn.sigmoid(ga_ref[...].astype(_F32)) * pa + jax.nn.sigmoid(gb_ref[...].astype(_F32)) * pb
    y = jnp.dot(merged.astype(_BF16), wout_ref[...], preferred_element_type=_F32) + bout_ref[...]
    h = _layer_norm(alpha * x_ref[...] + y, g1_ref[...], b1_ref[...])
    h_ref[...] = h

    h_hi = h.astype(_BF16)
    h_lo = (h - h_hi.astype(_F32)).astype(_BF16)
    nt = (((1,), (1,)), ((), ()))
    logits = (lax.dot_general(wrh_ref[...], h_hi, nt, preferred_element_type=_F32)
              + lax.dot_general(wrl_ref[...], h_hi, nt, preferred_element_type=_F32)
              + lax.dot_general(wrh_ref[...], h_lo, nt, preferred_element_type=_F32)
              + br_ref[...])

    tm = logits.shape[1]
    expert = lax.broadcasted_iota(jnp.int32, logits.shape, 0)
    work = logits
    vals, sels, idxs = [], [], []
    for _ in range(TOP_K):
        mx = jnp.max(work, axis=0, keepdims=True)
        idx = jnp.min(jnp.where(work == mx, expert, N_EXPERTS), axis=0, keepdims=True)
        sel = expert == idx
        work = jnp.where(sel, -jnp.inf, work)
        vals.append(mx)
        idxs.append(idx)
        sels.append(sel)
    ex = [jnp.exp(v - vals[0]) for v in vals]
    inv = 1.0 / (ex[0] + ex[1] + ex[2] + ex[3])
    gates = [e * inv for e in ex]

    chosen = sels[0] | sels[1] | sels[2] | sels[3]
    onehot = jnp.where(chosen, 1.0, 0.0).astype(_BF16)
    earlier = lax.broadcasted_iota(jnp.int32, (tm, tm), 0)
    later = lax.broadcasted_iota(jnp.int32, (tm, tm), 1)
    sub_shift = int(math.log2(MOE_TM))
    before = (earlier < later) & (lax.shift_right_logical(earlier, sub_shift)
                                  == lax.shift_right_logical(later, sub_shift))
    upper = jnp.where(before, 1.0, 0.0).astype(_BF16)
    prefix = jnp.dot(onehot, upper, preferred_element_type=_F32)
    for k in range(TOP_K):
        rank = jnp.sum(jnp.where(sels[k], prefix, 0.0), axis=0, keepdims=True)
        rank_ref[k:k + 1, :] = rank.astype(jnp.int32)
        idx_ref[k:k + 1, :] = idxs[k]
        gate_ref[k:k + 1, :] = gates[k]
    for sub in range(tm // MOE_TM):
        part = onehot[:, sub * MOE_TM:(sub + 1) * MOE_TM].astype(_F32)
        cnt_ref[sub] = jnp.broadcast_to(jnp.sum(part, axis=1, keepdims=True), cnt_ref.shape[1:])


def _merge_router(x2d, proj2d, gate_offset, dil_outs, dil_lses, o_b, wpa, wpb, wout, bout, g1, b1,
                  w_router, b_router, alpha):
    t, dm = x2d.shape
    tm = MERGE_TM
    bsz, _, seq_over_d0, _ = dil_outs[0].shape
    seq = seq_over_d0 * DILATED_PATTERNS[0][1]
    per_b = seq // tm
    dils = [d for _, d in DILATED_PATTERNS]
    assert gate_offset % dm == 0 and seq % tm == 0 and all(tm % d == 0 for d in dils)
    gcol = gate_offset // dm
    wr_t = w_router.T.astype(_F32)
    wr_hi = wr_t.astype(_BF16)
    wr_lo = (wr_t - wr_hi.astype(_F32)).astype(_BF16)
    expand = np.zeros((LANES, WIDTH_A), np.float32)
    for head in range(N_HEADS_A):
        expand[head, head * HEAD_DIM:(head + 1) * HEAD_DIM] = 1.0
    expand = jnp.asarray(expand, _BF16)

    def const(shape):
        return pl.BlockSpec(shape, lambda i: tuple(0 for _ in shape))

    def residue_major(d, width):
        return pl.BlockSpec((1, d, tm // d, width), lambda i: (i // per_b, 0, i % per_b, 0))

    return pl.pallas_call(
        functools.partial(_merge_kernel, alpha=alpha),
        grid=(t // tm,),
        in_specs=[
            pl.BlockSpec((tm, dm), lambda i: (i, 0)),
            pl.BlockSpec((tm, dm), lambda i: (i, gcol)),
            pl.BlockSpec((tm, dm), lambda i: (i, gcol + 1)),
            *[residue_major(d, WIDTH_A) for d in dils],
            *[residue_major(d, LANES) for d in dils],
            const((LANES, WIDTH_A)),
            pl.BlockSpec((tm, WIDTH_B), lambda i: (i, 0)),
            const((WIDTH_A, dm)), const((WIDTH_B, dm)), const((dm, dm)), const((1, dm)),
            const((1, dm)), const((1, dm)),
            const((N_EXPERTS, dm)), const((N_EXPERTS, dm)), const((N_EXPERTS, 1)),
        ],
        out_specs=[
            pl.BlockSpec((tm, dm), lambda i: (i, 0)),
            pl.BlockSpec((TOP_K, tm), lambda i: (0, i)),
            pl.BlockSpec((TOP_K, tm), lambda i: (0, i)),
            pl.BlockSpec((TOP_K, tm), lambda i: (0, i)),
            pl.BlockSpec((tm // MOE_TM, N_EXPERTS, LANES), lambda i: (i, 0, 0)),
        ],
        out_shape=[
            jax.ShapeDtypeStruct((t, dm), _F32),
            jax.ShapeDtypeStruct((TOP_K, t), jnp.int32),
            jax.ShapeDtypeStruct((TOP_K, t), _F32),
            jax.ShapeDtypeStruct((TOP_K, t), jnp.int32),
            jax.ShapeDtypeStruct((t // MOE_TM, N_EXPERTS, LANES), _F32),
        ],
        scratch_shapes=[pltpu.VMEM((len(dils), WIDTH_A // LANES, tm, LANES), _F32),
                        pltpu.VMEM((len(dils), tm, LANES), _F32)],
        compiler_params=_params("parallel"),
        name="merge_router",
    )(x2d, proj2d, proj2d, *dil_outs, *dil_lses, expand, o_b, wpa, wpb, wout, bout.reshape(1, dm),
      g1.reshape(1, dm), b1.reshape(1, dm), wr_hi, wr_lo, b_router.reshape(N_EXPERTS, 1).astype(_F32))


def _group_rows(n):
    return lax.shift_right_logical(n + (ROW_ALIGN - 1), int(math.log2(ROW_ALIGN))) * ROW_ALIGN


def _dispatch_kernel(lpos_ref, gate_ref, h_ref, xs_ref, *, tm, dm):
    rows = xs_ref.shape[1]
    slot = lax.broadcasted_iota(jnp.int32, (rows, tm), 0)
    perm = jnp.zeros((rows, tm), _F32)
    gsel = jnp.zeros((rows, tm), _F32)
    for k in range(TOP_K):
        hit = slot == lpos_ref[k:k + 1, :]
        perm = perm + jnp.where(hit, 1.0, 0.0)
        gsel = gsel + jnp.where(hit, gate_ref[k:k + 1, :], 0.0)
    xs_ref[0, :, 0:dm] = jnp.dot(perm.astype(_BF16), h_ref[...].astype(_BF16), preferred_element_type=_F32)
    xs_ref[0, :, dm:dm + LANES] = jnp.broadcast_to(jnp.sum(gsel, axis=1, keepdims=True), (rows, LANES))


def _dispatch(h, gates, plan):
    t, dm = h.shape
    tm = MOE_TM
    width = dm + LANES
    return pl.pallas_call(
        functools.partial(_dispatch_kernel, tm=tm, dm=dm),
        grid=(t // tm,),
        in_specs=[
            pl.BlockSpec((TOP_K, tm), lambda i: (0, i)),
            pl.BlockSpec((TOP_K, tm), lambda i: (0, i)),
            pl.BlockSpec((tm, dm), lambda i: (i, 0)),
        ],
        out_specs=pl.BlockSpec((1, LOCAL_ROWS, width), lambda i: (i, 0, 0)),
        out_shape=jax.ShapeDtypeStruct((t // tm, LOCAL_ROWS, width), _F32),
        compiler_params=_params("parallel"),
        name="moe_dispatch",
    )(plan["lpos"], gates, h)


def _expert_kernel(blk_e_ref, n_used_ref, tlo_ref, thi_ref, cnt_ref, base_ref, off_ref,
                   xs_hbm, wup_ref, bup_ref, wdn_ref, bdn_ref, ys_ref,
                   xbuf, sems, pending, wup_bf, wdn_bf, *, d_expert):
    i = pl.program_id(0)
    dm = ys_ref.shape[1]
    active = i < n_used_ref[0]

    def seg_copy(buf, src_row, dst_row, n_rows):
        n_rows = pl.multiple_of(n_rows, ROW_ALIGN)
        return pltpu.make_async_copy(xs_hbm.at[pl.ds(pl.multiple_of(src_row, ROW_ALIGN), n_rows)],
                                     xbuf.at[buf, pl.ds(pl.multiple_of(dst_row, ROW_ALIGN), n_rows)],
                                     sems.at[buf])

    def fetch(blk):
        e = blk_e_ref[blk]
        first = blk * MOE_ROWS

        def per_tile(t, total):
            j = t * N_EXPERTS + e
            g0 = base_ref[j]
            lo = jnp.maximum(g0, first)
            hi = jnp.minimum(g0 + _group_rows(cnt_ref[j]), first + MOE_ROWS)
            n_rows = jnp.maximum(hi - lo, 0)

            @pl.when(n_rows > 0)
            def _():
                seg_copy(blk % 2, t * LOCAL_ROWS + off_ref[j] + (lo - g0), lo - first, n_rows).start()

            return total + n_rows

        pending[blk % 2] = lax.fori_loop(tlo_ref[blk], thi_ref[blk] + 1, per_tile, 0)

    @pl.when(i == 0)
    def _():
        xbuf[...] = jnp.zeros_like(xbuf)
        fetch(i)

    @pl.when(i + 1 < n_used_ref[0])
    def _():
        fetch(i + 1)

    @pl.when(active & ((i == 0) | (blk_e_ref[i] != blk_e_ref[jnp.maximum(i - 1, 0)])))
    def _():
        wup_bf[...] = wup_ref[0].astype(_BF16)
        wdn_bf[...] = wdn_ref[0].astype(_BF16)

    @pl.when(active)
    def _():
        @pl.when(pending[i % 2] > 0)
        def _():
            seg_copy(i % 2, 0, 0, pending[i % 2]).wait()

        x = xbuf[i % 2, :, 0:dm].astype(_BF16)
        row_gate = xbuf[i % 2, :, dm:dm + 1]
        acc = jnp.zeros(ys_ref.shape, _F32)
        for c in range(d_expert // FFN_CHUNK):
            lo, hi = c * FFN_CHUNK, (c + 1) * FFN_CHUNK
            g = jnp.dot(x, wup_bf[:, lo:hi], preferred_element_type=_F32) + bup_ref[0, :, lo:hi]
            u = (jnp.dot(x, wup_bf[:, d_expert + lo:d_expert + hi], preferred_element_type=_F32)
                 + bup_ref[0, :, d_expert + lo:d_expert + hi])
            gate = jnp.minimum(g, SWIGLU_LIMIT)
            up = jnp.clip(u, -SWIGLU_LIMIT, SWIGLU_LIMIT)
            act = gate * jax.nn.sigmoid(SWIGLU_ALPHA * gate) * (up + 1.0)
            acc = acc + jnp.dot(act.astype(_BF16), wdn_bf[lo:hi, :], preferred_element_type=_F32)
        ys_ref[...] = ((acc + bdn_ref[0]) * row_gate).astype(_BF16).astype(_F32)

    @pl.when(jnp.logical_not(active))
    def _():
        ys_ref[...] = jnp.zeros_like(ys_ref)


def _experts(xs, plan, w_up, b_up, w_down, b_down):
    n_tiles, local_rows, width = xs.shape
    dm = width - LANES
    n_blocks = plan["n_rows"] // MOE_ROWS
    n_exp, _, two_de = w_up.shape
    d_expert = two_de // 2
    grid_spec = pltpu.PrefetchScalarGridSpec(
        num_scalar_prefetch=7,
        grid=(n_blocks,),
        in_specs=[
            pl.BlockSpec(memory_space=pl.ANY),
            pl.BlockSpec((1, dm, two_de), lambda i, be, *_: (be[i], 0, 0)),
            pl.BlockSpec((1, 1, two_de), lambda i, be, *_: (be[i], 0, 0)),
            pl.BlockSpec((1, d_expert, dm), lambda i, be, *_: (be[i], 0, 0)),
            pl.BlockSpec((1, 1, dm), lambda i, be, *_: (be[i], 0, 0)),
        ],
        out_specs=pl.BlockSpec((MOE_ROWS, dm), lambda i, *_: (i, 0)),
        scratch_shapes=[pltpu.VMEM((2, MOE_ROWS, width), _F32), pltpu.SemaphoreType.DMA((2,)),
                        pltpu.SMEM((2,), jnp.int32),
                        pltpu.VMEM((dm, two_de), _BF16), pltpu.VMEM((d_expert, dm), _BF16)],
    )
    return pl.pallas_call(
        functools.partial(_expert_kernel, d_expert=d_expert),
        grid_spec=grid_spec,
        out_shape=jax.ShapeDtypeStruct((plan["n_rows"], dm), _F32),
        compiler_params=_params("arbitrary"),
        name="moe_experts",
    )(plan["blk_e"], plan["n_used"], plan["blk_tlo"], plan["blk_thi"], plan["cnt"], plan["base"], plan["off"],
      xs.reshape(n_tiles * local_rows, width), w_up, b_up.reshape(n_exp, 1, two_de), w_down,
      b_down.reshape(n_exp, 1, dm))


def _combine_kernel(cnt_ref, base_ref, off_ref, rows_ref, lpos_ref, h_ref, g2_ref, b2_ref, ys_hbm, o_ref,
                    local, sems, *, tm, alpha):
    i = pl.program_id(0)
    n_tiles = pl.num_programs(0)
    rows = local.shape[1]

    def group_copy(buf, src_row, dst_row, n_rows):
        n_rows = pl.multiple_of(n_rows, ROW_ALIGN)
        return pltpu.make_async_copy(ys_hbm.at[pl.ds(pl.multiple_of(src_row, ROW_ALIGN), n_rows)],
                                     local.at[buf, pl.ds(pl.multiple_of(dst_row, ROW_ALIGN), n_rows)],
                                     sems.at[buf])

    def fetch(tile):
        for e in range(N_EXPERTS):
            j = tile * N_EXPERTS + e
            n_rows = _group_rows(cnt_ref[j])

            @pl.when(n_rows > 0)
            def _():
                group_copy(tile % 2, base_ref[j], off_ref[j], n_rows).start()

    @pl.when(i == 0)
    def _():
        local[...] = jnp.zeros_like(local)
        fetch(i)

    @pl.when(i + 1 < n_tiles)
    def _():
        fetch(i + 1)

    slot = lax.broadcasted_iota(jnp.int32, (tm, rows), 1)
    pick = jnp.zeros((tm, rows), _F32)
    for k in range(TOP_K):
        pick = pick + jnp.where(slot == lpos_ref[:, k:k + 1], 1.0, 0.0)

    group_copy(i % 2, 0, 0, rows_ref[i]).wait()
    y = jnp.dot(pick.astype(_BF16), local[i % 2].astype(_BF16), preferred_element_type=_F32)
    o_ref[...] = _layer_norm(alpha * h_ref[...] + y, g2_ref[...], b2_ref[...])


def _combine(h, ys, plan, g2, b2, alpha):
    t, dm = h.shape
    tm = MOE_TM
    grid_spec = pltpu.PrefetchScalarGridSpec(
        num_scalar_prefetch=4,
        grid=(t // tm,),
        in_specs=[
            pl.BlockSpec((tm, TOP_K), lambda i, *_: (i, 0)),
            pl.BlockSpec((tm, dm), lambda i, *_: (i, 0)),
            pl.BlockSpec((1, dm), lambda i, *_: (0, 0)),
            pl.BlockSpec((1, dm), lambda i, *_: (0, 0)),
            pl.BlockSpec(memory_space=pl.ANY),
        ],
        out_specs=pl.BlockSpec((tm, dm), lambda i, *_: (i, 0)),
        scratch_shapes=[pltpu.VMEM((2, LOCAL_ROWS, dm), _F32), pltpu.SemaphoreType.DMA((2,))],
    )
    return pl.pallas_call(
        functools.partial(_combine_kernel, tm=tm, alpha=alpha),
        grid_spec=grid_spec,
        out_shape=jax.ShapeDtypeStruct((t, dm), _F32),
        compiler_params=_params("arbitrary"),
        name="moe_combine",
    )(plan["cnt"], plan["base"], plan["off"], plan["tile_rows"], plan["lpos"].T, h, g2.reshape(1, dm),
      b2.reshape(1, dm), ys)


def _round_up(x, m):
    return (x + m - 1) // m * m


def _moe_plan(idx, rank, tile_cnt, tm):
    n_tiles, n_exp = tile_cnt.shape
    t = idx.shape[1]
    cnt = tile_cnt.astype(jnp.int32)
    grp = _round_up(cnt, ROW_ALIGN)
    tot = jnp.sum(grp, axis=0)
    padded = _round_up(tot, MOE_ROWS)
    pend = jnp.cumsum(padded)
    pstart = pend - padded
    base = pstart[None, :] + jnp.cumsum(grp, axis=0) - grp
    off = jnp.cumsum(grp, axis=1) - grp
    onehot = idx[..., None] == jnp.arange(n_exp, dtype=jnp.int32)
    off_tok = jnp.broadcast_to(off[:, None, :], (n_tiles, tm, n_exp)).reshape(t, n_exp)
    lpos = jnp.sum(jnp.where(onehot, off_tok[None], 0), axis=-1) + rank

    n_rows = _round_up(t * idx.shape[0] + n_tiles * n_exp * (ROW_ALIGN - 1) + n_exp * (MOE_ROWS - 1), MOE_ROWS)
    n_blocks = n_rows // MOE_ROWS
    blk_start = jnp.arange(n_blocks, dtype=jnp.int32) * MOE_ROWS
    blk_e = jnp.sum((pend[None, :] <= blk_start[:, None]).astype(jnp.int32), axis=1)
    blk_e = jnp.minimum(blk_e, n_exp - 1)
    n_used = (pend[-1] // MOE_ROWS).astype(jnp.int32)
    blk_e = blk_e[jnp.minimum(jnp.arange(n_blocks, dtype=jnp.int32), n_used - 1)]
    base_b = jnp.take(base, blk_e, axis=1)
    end_b = base_b + jnp.take(grp, blk_e, axis=1)
    blk_tlo = jnp.sum((end_b <= blk_start[None, :]).astype(jnp.int32), axis=0)
    blk_thi = jnp.sum((base_b < blk_start[None, :] + MOE_ROWS).astype(jnp.int32), axis=0) - 1
    return dict(
        cnt=cnt.reshape(-1), base=base.reshape(-1).astype(jnp.int32), off=off.reshape(-1).astype(jnp.int32),
        tile_rows=jnp.sum(grp, axis=1).astype(jnp.int32), lpos=lpos.astype(jnp.int32),
        blk_e=blk_e.astype(jnp.int32), blk_tlo=blk_tlo, blk_thi=blk_thi, n_used=n_used.reshape(1), n_rows=n_rows)


def kernel(x, w_in, b_in, lambda_q1, lambda_k1, lambda_q2, lambda_k2, subln_g, w_proj_a, w_proj_b, w_out, b_out, ln1_g, ln1_b, w_router, b_router, w_up, b_up, w_down, b_down, ln2_g, ln2_b):
    bsz, seq, dm = x.shape
    depth = w_in.shape[0]
    alpha = (2.0 * depth) ** 0.25
    t = bsz * seq
    for layer in range(depth):
        lambda_init = 0.8 - 0.6 * math.exp(-0.3 * layer)
        x2d = x.reshape(t, dm)
        n_a, n_b = 3 * WIDTH_A, 3 * WIDTH_B
        w_l, b_l = w_in[layer], b_in[layer]
        query_scale = jnp.full((WIDTH_A,), QUERY_SCALE, _F32)
        ones = functools.partial(jnp.ones, dtype=_F32)
        qkv_a = _in_proj_a(x, w_l[:, :n_a].astype(_BF16), b_l[:n_a],
                           jnp.concatenate([query_scale, ones((2 * WIDTH_A,))]))
        w_rest = jnp.concatenate([w_l[:, n_a + n_b:], w_l[:, n_a:n_a + n_b]], axis=1).astype(_BF16)
        b_rest = jnp.concatenate([b_l[n_a + n_b:], b_l[n_a:n_a + n_b]])
        scale_rest = jnp.concatenate([ones((2 * dm,)), query_scale, ones((2 * WIDTH_B,))])
        proj2d = _in_proj(x2d, w_rest, b_rest, scale_rest)

        dil = [_dilated_pass(a, d) for a, (_, d) in zip(qkv_a, DILATED_PATTERNS)]

        lam_vecs = jnp.stack([lambda_q1[layer], lambda_k1[layer], lambda_q2[layer], lambda_k2[layer]]).astype(_F32)
        o_b = _diff_attention(proj2d.reshape(bsz, seq, -1), 2 * dm, lam_vecs, subln_g[layer],
                              lambda_init).reshape(t, WIDTH_B)

        h, idx, gates, rank, cnt = _merge_router(
            x2d, proj2d, 0, [o for o, _ in dil], [l for _, l in dil], o_b,
            w_proj_a[layer].astype(_BF16), w_proj_b[layer].astype(_BF16),
            w_out[layer].astype(_BF16), b_out[layer], ln1_g[layer], ln1_b[layer],
            w_router[layer], b_router[layer], alpha)

        plan = _moe_plan(idx, rank, cnt[:, :, 0], MOE_TM)
        xs = _dispatch(h, gates, plan)
        ys = _experts(xs, plan, w_up[layer], b_up[layer], w_down[layer], b_down[layer])
        out = _combine(h, ys, plan, ln2_g[layer], ln2_b[layer], alpha)
        x = out.reshape(bsz, seq, dm)
    return x
```

```python
import functools
import math

import numpy as np
import jax
import jax.numpy as jnp
from jax import lax
from jax.experimental import pallas as pl
from jax.experimental.pallas import tpu as pltpu

HEAD_DIM = 64
N_HEADS_A = 8
DILATED_PATTERNS = ((128, 1), (512, 4), (2048, 16))
N_HEADS_B = 4
WIDTH_A = N_HEADS_A * HEAD_DIM
WIDTH_B = N_HEADS_B * 2 * HEAD_DIM
N_ALIBI_HEADS = N_HEADS_A + N_HEADS_B
ALIBI_IDX_A = (0, 1, 3, 4, 6, 7, 9, 10)
ALIBI_IDX_B = (2, 5, 8, 11)
Q_BLOCK = 128
MASK_VALUE = -1e30
N_EXPERTS = 32
TOP_K = 4
SWIGLU_ALPHA = 1.702
SWIGLU_LIMIT = 7.0
LN_EPS = 1e-5
SUBLN_EPS = 1e-5
LOG2E = math.log2(math.e)
QUERY_SCALE = HEAD_DIM ** -0.5 * LOG2E

LANES = 128
V7X_VMEM_LIMIT_BYTES = 56 * 1024 * 1024

PROJ_TM = 1024
PROJ_A_TM = 512
DIL_ROWS = 2 * Q_BLOCK
DIL_HALO = 64
DIFF_TQ = 512
DIFF_TK = 512
MERGE_TM = 512
MOE_ROWS = 512
FFN_CHUNK = 512
MOE_TM = 256
ROW_ALIGN = 8
LOCAL_ROWS = -(-(TOP_K * MOE_TM + N_EXPERTS * (ROW_ALIGN - 1)) // LANES) * LANES

_F32 = jnp.float32
_BF16 = jnp.bfloat16


def _params(*sem):
    return pltpu.CompilerParams(dimension_semantics=sem, vmem_limit_bytes=V7X_VMEM_LIMIT_BYTES)


def _alibi_slopes():
    return (2.0 ** (-8.0 * np.arange(1, N_ALIBI_HEADS + 1) / N_ALIBI_HEADS)).astype(np.float32)


def _in_proj_kernel(x_ref, w_ref, b_ref, cs_ref, o_ref):
    x = x_ref[...].astype(_BF16)
    acc = jnp.dot(x, w_ref[...], preferred_element_type=_F32)
    o_ref[...] = ((acc + b_ref[...]) * cs_ref[...]).astype(o_ref.dtype)


def _in_proj(x2d, w_bf16, b, colscale):
    t, dm = x2d.shape
    n = w_bf16.shape[1]
    tn = n // 2
    assert n % 2 == 0 and tn % LANES == 0
    return pl.pallas_call(
        _in_proj_kernel,
        grid=(t // PROJ_TM, n // tn),
        in_specs=[
            pl.BlockSpec((PROJ_TM, dm), lambda i, j: (i, 0)),
            pl.BlockSpec((dm, tn), lambda i, j: (0, j)),
            pl.BlockSpec((1, tn), lambda i, j: (0, j)),
            pl.BlockSpec((1, tn), lambda i, j: (0, j)),
        ],
        out_specs=pl.BlockSpec((PROJ_TM, tn), lambda i, j: (i, j)),
        out_shape=jax.ShapeDtypeStruct((t, n), _BF16),
        compiler_params=_params("parallel", "arbitrary"),
        name="in_proj",
    )(x2d, w_bf16, b.reshape(1, n), colscale.reshape(1, n))


def _in_proj_a_kernel(x_ref, w_ref, b_ref, cs_ref, *refs):
    out_refs, acc_ref, stage_ref = refs[:-2], refs[-2], refs[-1]
    x = x_ref[0].astype(_BF16)
    acc = (jnp.dot(x, w_ref[...], preferred_element_type=_F32) + b_ref[...]) * cs_ref[...]
    n_chunks, tm, _ = acc_ref.shape
    for c in range(n_chunks):
        acc_ref[c] = acc[:, c * LANES:(c + 1) * LANES]
    for o_ref, (_, d) in zip(out_refs, DILATED_PATTERNS):
        if d == 1:
            o_ref[0, 0] = acc.astype(o_ref.dtype)
            continue
        if d == 4:
            for r in range(d):
                for c in range(n_chunks):
                    rows = acc_ref[c, pl.ds(r, tm // d, stride=d), :]
                    stage_ref[c, r] = rows
                    o_ref[0, r, :, c * LANES:(c + 1) * LANES] = rows.astype(o_ref.dtype)
            continue
        assert d == 16
        for r1 in range(4):
            for r2 in range(4):
                for c in range(n_chunks):
                    rows = stage_ref[c, r1, pl.ds(r2, tm // d, stride=4), :]
                    o_ref[0, r1 + 4 * r2, :, c * LANES:(c + 1) * LANES] = rows.astype(o_ref.dtype)


def _in_proj_a(x, w_bf16, b, colscale):
    bsz, seq, dm = x.shape
    n = w_bf16.shape[1]
    tm = PROJ_A_TM
    per_b = seq // tm
    dils = [d for _, d in DILATED_PATTERNS]
    assert seq % tm == 0 and all(tm % d == 0 and (tm // d) % 16 == 0 for d in dils) and dils == [1, 4, 16]
    return pl.pallas_call(
        _in_proj_a_kernel,
        grid=(bsz * per_b,),
        in_specs=[
            pl.BlockSpec((1, tm, dm), lambda i: (i // per_b, i % per_b, 0)),
            pl.BlockSpec((dm, n), lambda i: (0, 0)),
            pl.BlockSpec((1, n), lambda i: (0, 0)),
            pl.BlockSpec((1, n), lambda i: (0, 0)),
        ],
        out_specs=[pl.BlockSpec((1, d, tm // d, n), lambda i: (i // per_b, 0, i % per_b, 0)) for d in dils],
        out_shape=[jax.ShapeDtypeStruct((bsz, d, seq // d, n), _BF16) for d in dils],
        scratch_shapes=[pltpu.VMEM((n // LANES, tm, LANES), _F32),
                        pltpu.VMEM((n // LANES, 4, tm // 4, LANES), _F32)],
        compiler_params=_params("parallel"),
        name="in_proj_a",
    )(x, w_bf16, b.reshape(1, n), colscale.reshape(1, n))


def _dilated_bias_table(dilation):
    slopes = _alibi_slopes()[list(ALIBI_IDX_A)]
    band = Q_BLOCK + 2 * DIL_HALO
    qi = np.arange(Q_BLOCK)[:, None]
    kj = np.arange(band)[None, :]
    rel = qi - kj + DIL_HALO
    in_band = np.abs(rel) <= DIL_HALO
    base = -slopes[:, None, None] * (dilation * np.abs(rel)).astype(np.float32)[None]
    base = (base.astype(np.float64) * LOG2E).astype(np.float32)
    edge = (np.ones_like(kj, bool), kj >= DIL_HALO, kj < band - DIL_HALO)
    out = np.stack([np.where(in_band & e, base, np.float32(MASK_VALUE)) for e in edge])
    return out.astype(np.float32)


def _dilated_kernel(q_ref, km_ref, kp_ref, kn_ref, vm_ref, vp_ref, vn_ref, bias_ref, o_ref, lse_ref,
                    kbuf, vbuf, *, nblk):
    i = pl.program_id(2)
    h0, h1 = DIL_HALO, DIL_HALO + DIL_ROWS
    kbuf[0:h0, :] = kp_ref[0, 0]
    kbuf[h0:h1, :] = km_ref[0, 0]
    kbuf[h1:h1 + DIL_HALO, :] = kn_ref[0, 0]
    vbuf[0:h0, :] = vp_ref[0, 0]
    vbuf[h0:h1, :] = vm_ref[0, 0]
    vbuf[h1:h1 + DIL_HALO, :] = vn_ref[0, 0]

    lane = lax.broadcasted_iota(jnp.int32, (Q_BLOCK, LANES), 1)
    low_half = lane < HEAD_DIM
    band = Q_BLOCK + 2 * DIL_HALO
    variants = (jnp.where(i == 0, 1, 0), jnp.where(i == nblk - 1, 2, 0))

    for j in range(DIL_ROWS // Q_BLOCK):
        rows = slice(j * Q_BLOCK, (j + 1) * Q_BLOCK)
        krows = slice(j * Q_BLOCK, j * Q_BLOCK + band)
        lse_cols = []
        for hp in range(N_HEADS_A // 2):
            cols = slice(hp * LANES, (hp + 1) * LANES)
            q_pair = q_ref[0, 0, rows, cols]
            k_pair = kbuf[krows, cols]
            v_pair = vbuf[krows, cols]
            halves = []
            for hh in range(2):
                head = 2 * hp + hh
                keep = low_half if hh == 0 else jnp.logical_not(low_half)
                qm = jnp.where(keep, q_pair, jnp.zeros_like(q_pair))
                s = lax.dot_general(qm, k_pair, (((1,), (1,)), ((), ())), preferred_element_type=_F32)
                s = s + bias_ref[variants[j], head]
                m = jnp.max(s, axis=-1, keepdims=True)
                p = jnp.exp2(s - m)
                z = jnp.sum(p, axis=-1, keepdims=True)
                halves.append(jnp.dot(p.astype(_BF16), v_pair, preferred_element_type=_F32) * (1.0 / z))
                lse_cols.append(m + jnp.log2(z))
            o_ref[0, 0, rows, cols] = jnp.where(low_half, halves[0], halves[1])
        tile = jnp.zeros((Q_BLOCK, LANES), _F32)
        for head in range(N_HEADS_A):
            tile = jnp.where(lane == head, lse_cols[head], tile)
        lse_ref[0, 0, rows, :] = tile


def _dilated_pass(qkv, dilation):
    bsz, d, L, _ = qkv.shape
    assert d == dilation and L % DIL_ROWS == 0 and DIL_ROWS % DIL_HALO == 0
    nblk = L // DIL_ROWS
    per = DIL_ROWS // DIL_HALO
    n_halo = L // DIL_HALO
    bias = jnp.asarray(_dilated_bias_table(dilation))

    def main(s):
        return pl.BlockSpec((1, 1, DIL_ROWS, WIDTH_A), lambda b, r, i: (b, r, i, s))

    def left(s):
        return pl.BlockSpec((1, 1, DIL_HALO, WIDTH_A), lambda b, r, i: (b, r, jnp.maximum(per * i - 1, 0), s))

    def right(s):
        return pl.BlockSpec((1, 1, DIL_HALO, WIDTH_A),
                            lambda b, r, i: (b, r, jnp.minimum(per * i + per, n_halo - 1), s))

    rows_buf = DIL_ROWS + 2 * DIL_HALO
    return pl.pallas_call(
        functools.partial(_dilated_kernel, nblk=nblk),
        grid=(bsz, dilation, nblk),
        in_specs=[main(0), main(1), left(1), right(1), main(2), left(2), right(2),
                  pl.BlockSpec(bias.shape, lambda b, r, i: (0, 0, 0, 0))],
        out_specs=[pl.BlockSpec((1, 1, DIL_ROWS, WIDTH_A), lambda b, r, i: (b, r, i, 0)),
                   pl.BlockSpec((1, 1, DIL_ROWS, LANES), lambda b, r, i: (b, r, i, 0))],
        out_shape=[jax.ShapeDtypeStruct((bsz, dilation, L, WIDTH_A), _F32),
                   jax.ShapeDtypeStruct((bsz, dilation, L, LANES), _F32)],
        scratch_shapes=[pltpu.VMEM((rows_buf, WIDTH_A), _BF16), pltpu.VMEM((rows_buf, WIDTH_A), _BF16)],
        compiler_params=_params("parallel", "parallel", "arbitrary"),
        name=f"dilated_d{dilation}",
    )(qkv, qkv, qkv, qkv, qkv, qkv, qkv, bias)


def _split_bf16(x):
    hi = x.astype(jnp.bfloat16)
    lo = (x - hi.astype(np.float64)).astype(jnp.bfloat16)
    return hi, lo


def _diff_tables():
    s2 = _alibi_slopes()[list(ALIBI_IDX_B)].astype(np.float64) * LOG2E
    qaug = np.zeros((N_HEADS_B, DIFF_TQ, LANES), jnp.bfloat16)
    kaug = np.zeros((N_HEADS_B, 2, DIFF_TK, LANES), jnp.bfloat16)
    i = np.arange(DIFF_TQ, dtype=np.float64)
    j = np.arange(DIFF_TK, dtype=np.float64)
    for h in range(N_HEADS_B):
        r_hi, r_lo = _split_bf16(-s2[h] * i)
        qaug[h, :, 0] = 1.0
        qaug[h, :, 1] = 1.0
        qaug[h, :, 2] = r_hi
        qaug[h, :, 3] = r_lo
        for side, sign in enumerate((1.0, -1.0)):
            c_hi, c_lo = _split_bf16(sign * s2[h] * j)
            kaug[h, side, :, 0] = c_hi
            kaug[h, side, :, 1] = c_lo
            kaug[h, side, :, 2] = sign
            kaug[h, side, :, 3] = sign
    halves = DIFF_TK // DIFF_TQ
    off = (np.arange(halves) * DIFF_TQ)[:, None, None]
    dist = np.abs(off + i[None, :, None] - j[None, None, :])
    diag = (-s2[:, None, None, None] * dist[None]).astype(np.float32)
    return jnp.asarray(qaug), jnp.asarray(kaug), jnp.asarray(diag), jnp.asarray(s2.astype(np.float32))


def _diff_kernel(slope_ref, lam_ref, q_ref, k_ref, v_ref, qaug_ref, kaug_ref, diag_ref, g_ref, o_ref,
                 *, seq, lambda_init):
    head = pl.program_id(1)
    qb = pl.program_id(2)
    q0 = qb * DIFF_TQ
    halves = DIFF_TK // DIFF_TQ
    kd = qb // halves
    n_kb = seq // DIFF_TK
    slope2 = slope_ref[head]
    nt = (((1,), (1,)), ((), ()))

    q = q_ref[0]
    lane = lax.broadcasted_iota(jnp.int32, q.shape, 1)
    zero = jnp.zeros_like(q)
    qaug = qaug_ref[0]
    q_maps = (jnp.concatenate([jnp.where(lane < HEAD_DIM, q, zero), qaug], axis=1),
              jnp.concatenate([jnp.where(lane >= HEAD_DIM, q, zero), qaug], axis=1))
    zero_aug = jnp.zeros((DIFF_TK, LANES), _BF16)
    vlane = lax.broadcasted_iota(jnp.int32, (DIFF_TK, LANES), 1)
    ones_col = jnp.where(vlane == 0, 1.0, 0.0).astype(_BF16)

    def load(kb):
        k0 = pl.multiple_of(kb * DIFF_TK, DIFF_TK)
        return k_ref[0, pl.ds(k0, DIFF_TK), :], jnp.concatenate([v_ref[0, pl.ds(k0, DIFF_TK), :], ones_col], axis=1)

    def split(pv):
        return pv[:, LANES:LANES + 1], pv[:, 0:LANES]

    def update(s, const, vblk, carry):
        m_old, l_old, acc = carry
        m_new = jnp.maximum(m_old, jnp.max(s, axis=-1, keepdims=True) + const)
        alpha = jnp.exp2(m_old - m_new)
        p = jnp.exp2(s - (m_new - const))
        l_blk, o_blk = split(jnp.dot(p.astype(_BF16), vblk, preferred_element_type=_F32))
        return m_new, alpha * l_old + l_blk, alpha * acc + o_blk

    kblk, vblk = load(kd)
    k_all = jnp.concatenate([kblk, zero_aug], axis=1)
    bias = diag_ref[0, qb % halves]
    carries = []
    for mp in range(2):
        s = lax.dot_general(q_maps[mp], k_all, nt, preferred_element_type=_F32) + bias
        m = jnp.max(s, axis=-1, keepdims=True)
        p = jnp.exp2(s - m)
        carries.append((m,) + split(jnp.dot(p.astype(_BF16), vblk, preferred_element_type=_F32)))

    for t in range(n_kb - 1):
        kb = jnp.where(t >= kd, t + 1, t)
        side = jnp.where(t >= kd, 1, 0)
        kblk, vblk = load(kb)
        k_all = jnp.concatenate([kblk, kaug_ref[0, side]], axis=1)
        const = -slope2 * jnp.abs(q0 - kb * DIFF_TK).astype(_F32)
        for mp in range(2):
            s = lax.dot_general(q_maps[mp], k_all, nt, preferred_element_type=_F32)
            carries[mp] = update(s, const, vblk, carries[mp])

    lv = lam_ref[...]
    lam = (jnp.exp(jnp.sum(lv[0:1] * lv[1:2], axis=-1, keepdims=True))
           - jnp.exp(jnp.sum(lv[2:3] * lv[3:4], axis=-1, keepdims=True)) + lambda_init)
    (_, l1, a1), (_, l2, a2) = carries
    of = a1 * (1.0 / l1) - lam * (a2 * (1.0 / l2))
    of = of * lax.rsqrt(jnp.mean(of * of, axis=-1, keepdims=True) + SUBLN_EPS)
    of = of * g_ref[...] * (1.0 - lambda_init)
    o_ref[0] = of.astype(o_ref.dtype)


def _diff_attention(proj, q_offset, lam_vecs, subln_g, lambda_init):
    bsz, seq, _ = proj.shape
    assert seq % DIFF_TQ == 0 and seq % DIFF_TK == 0 and q_offset % LANES == 0
    qcol = q_offset // LANES
    kcol = qcol + WIDTH_B // LANES
    vcol = kcol + WIDTH_B // LANES
    assert DIFF_TK % DIFF_TQ == 0
    halves = DIFF_TK // DIFF_TQ
    qaug, kaug, diag, slopes2 = _diff_tables()
    return pl.pallas_call(
        functools.partial(_diff_kernel, seq=seq, lambda_init=lambda_init),
        grid=(bsz, N_HEADS_B, seq // DIFF_TQ),
        in_specs=[
            pl.BlockSpec(memory_space=pltpu.SMEM),
            pl.BlockSpec((4, HEAD_DIM), lambda b, h, i: (0, 0)),
            pl.BlockSpec((1, DIFF_TQ, LANES), lambda b, h, i: (b, i, qcol + h)),
            pl.BlockSpec((1, seq, LANES), lambda b, h, i: (b, 0, kcol + h)),
            pl.BlockSpec((1, seq, LANES), lambda b, h, i: (b, 0, vcol + h)),
            pl.BlockSpec((1, DIFF_TQ, LANES), lambda b, h, i: (h, 0, 0)),
            pl.BlockSpec((1, 2, DIFF_TK, LANES), lambda b, h, i: (h, 0, 0, 0)),
            pl.BlockSpec((1, halves, DIFF_TQ, DIFF_TK), lambda b, h, i: (h, 0, 0, 0)),
            pl.BlockSpec((1, LANES), lambda b, h, i: (0, 0)),
        ],
        out_specs=pl.BlockSpec((1, DIFF_TQ, LANES), lambda b, h, i: (b, i, h)),
        out_shape=jax.ShapeDtypeStruct((bsz, seq, WIDTH_B), _BF16),
        compiler_params=_params("parallel", "parallel", "arbitrary"),
        name="diff_attn",
    )(slopes2, lam_vecs, proj, proj, proj, qaug, kaug, diag, subln_g.reshape(1, LANES).astype(_F32))


def _layer_norm(z, g, b):
    mu = jnp.mean(z, axis=-1, keepdims=True)
    zc = z - mu
    var = jnp.mean(zc * zc, axis=-1, keepdims=True)
    return zc * lax.rsqrt(var + LN_EPS) * g + b


def _mix_patterns(o_refs, lse_refs, expand_ref, o_nat, lse_nat):
    n_chunks, tm = o_nat.shape[1], o_nat.shape[2]
    outs, lses = [], []
    for p, (_, d) in enumerate(DILATED_PATTERNS):
        if d == 1:
            outs.append(o_refs[p][0, 0])
            lses.append(lse_refs[p][0, 0])
            continue
        for r in range(d):
            for c in range(n_chunks):
                o_nat[p, c, pl.ds(r, tm // d, stride=d), :] = o_refs[p][0, r, :, c * LANES:(c + 1) * LANES]
            lse_nat[p, pl.ds(r, tm // d, stride=d), :] = lse_refs[p][0, r]
        outs.append(jnp.concatenate([o_nat[p, c] for c in range(n_chunks)], axis=1))
        lses.append(lse_nat[p])
    top = functools.reduce(jnp.maximum, lses)
    es = [jnp.exp2(l - top) for l in lses]
    inv = 1.0 / functools.reduce(lambda a, b: a + b, es)
    mixed = None
    for o, e in zip(outs, es):
        w = e * inv
        w_hi = w.astype(_BF16)
        w_lo = (w - w_hi.astype(_F32)).astype(_BF16)
        w_full = (jnp.dot(w_hi, expand_ref[...], preferred_element_type=_F32)
                  + jnp.dot(w_lo, expand_ref[...], preferred_element_type=_F32))
        mixed = w_full * o if mixed is None else mixed + w_full * o
    return mixed


def _merge_kernel(x_ref, ga_ref, gb_ref, o1_ref, o2_ref, o3_ref, l1_ref, l2_ref, l3_ref, expand_ref, ob_ref,
                  wpa_ref, wpb_ref, wout_ref, bout_ref, g1_ref, b1_ref, wrh_ref, wrl_ref, br_ref,
                  h_ref, idx_ref, gate_ref, rank_ref, cnt_ref, o_nat, lse_nat, *, alpha):
    o_a = _mix_patterns((o1_ref, o2_ref, o3_ref), (l1_ref, l2_ref, l3_ref), expand_ref, o_nat, lse_nat)
    pa = jnp.dot(o_a.astype(_BF16), wpa_ref[...], preferred_element_type=_F32)
    pb = jnp.dot(ob_ref[...], wpb_ref[...], preferred_element_type=_F32)
    merged = jax.nn.sigmoid(ga_ref[...].astype(_F32)) * pa + jax.nn.sigmoid(gb_ref[...].astype(_F32)) * pb
    y = jnp.dot(merged.astype(_BF16), wout_ref[...], preferred_element_type=_F32) + bout_ref[...]
    h = _layer_norm(alpha * x_ref[...] + y, g1_ref[...], b1_ref[...])
    h_ref[...] = h

    h_hi = h.astype(_BF16)
    h_lo = (h - h_hi.astype(_F32)).astype(_BF16)
    nt = (((1,), (1,)), ((), ()))
    logits = (lax.dot_general(wrh_ref[...], h_hi, nt, preferred_element_type=_F32)
              + lax.dot_general(wrl_ref[...], h_hi, nt, preferred_element_type=_F32)
              + lax.dot_general(wrh_ref[...], h_lo, nt, preferred_element_type=_F32)
              + br_ref[...])

    tm = logits.shape[1]
    expert = lax.broadcasted_iota(jnp.int32, logits.shape, 0)
    work = logits
    vals, sels, idxs = [], [], []
    for _ in range(TOP_K):
        mx = jnp.max(work, axis=0, keepdims=True)
        idx = jnp.min(jnp.where(work == mx, expert, N_EXPERTS), axis=0, keepdims=True)
        sel = expert == idx
        work = jnp.where(sel, -jnp.inf, work)
        vals.append(mx)
        idxs.append(idx)
        sels.append(sel)
    ex = [jnp.exp(v - vals[0]) for v in vals]
    inv = 1.0 / (ex[0] + ex[1] + ex[2] + ex[3])
    gates = [e * inv for e in ex]

    chosen = sels[0] | sels[1] | sels[2] | sels[3]
    onehot = jnp.where(chosen, 1.0, 0.0).astype(_BF16)
    earlier = lax.broadcasted_iota(jnp.int32, (tm, tm), 0)
    later = lax.broadcasted_iota(jnp.int32, (tm, tm), 1)
    sub_shift = int(math.log2(MOE_TM))
    before = (earlier < later) & (lax.shift_right_logical(earlier, sub_shift)
                                  == lax.shift_right_logical(later, sub_shift))
    upper = jnp.where(before, 1.0, 0.0).astype(_BF16)
    prefix = jnp.dot(onehot, upper, preferred_element_type=_F32)
    for k in range(TOP_K):
        rank = jnp.sum(jnp.where(sels[k], prefix, 0.0), axis=0, keepdims=True)
        rank_ref[k:k + 1, :] = rank.astype(jnp.int32)
        idx_ref[k:k + 1, :] = idxs[k]
        gate_ref[k:k + 1, :] = gates[k]
    for sub in range(tm // MOE_TM):
        part = onehot[:, sub * MOE_TM:(sub + 1) * MOE_TM].astype(_F32)
        cnt_ref[sub] = jnp.broadcast_to(jnp.sum(part, axis=1, keepdims=True), cnt_ref.shape[1:])


def _merge_router(x2d, proj2d, gate_offset, dil_outs, dil_lses, o_b, wpa, wpb, wout, bout, g1, b1,
                  w_router, b_router, alpha):
    t, dm = x2d.shape
    tm = MERGE_TM
    bsz, _, seq_over_d0, _ = dil_outs[0].shape
    seq = seq_over_d0 * DILATED_PATTERNS[0][1]
    per_b = seq // tm
    dils = [d for _, d in DILATED_PATTERNS]
    assert gate_offset % dm == 0 and seq % tm == 0 and all(tm % d == 0 for d in dils)
    gcol = gate_offset // dm
    wr_t = w_router.T.astype(_F32)
    wr_hi = wr_t.astype(_BF16)
    wr_lo = (wr_t - wr_hi.astype(_F32)).astype(_BF16)
    expand = np.zeros((LANES, WIDTH_A), np.float32)
    for head in range(N_HEADS_A):
        expand[head, head * HEAD_DIM:(head + 1) * HEAD_DIM] = 1.0
    expand = jnp.asarray(expand, _BF16)

    def const(shape):
        return pl.BlockSpec(shape, lambda i: tuple(0 for _ in shape))

    def residue_major(d, width):
        return pl.BlockSpec((1, d, tm // d, width), lambda i: (i // per_b, 0, i % per_b, 0))

    return pl.pallas_call(
        functools.partial(_merge_kernel, alpha=alpha),
        grid=(t // tm,),
        in_specs=[
            pl.BlockSpec((tm, dm), lambda i: (i, 0)),
            pl.BlockSpec((tm, dm), lambda i: (i, gcol)),
            pl.BlockSpec((tm, dm), lambda i: (i, gcol + 1)),
            *[residue_major(d, WIDTH_A) for d in dils],
            *[residue_major(d, LANES) for d in dils],
            const((LANES, WIDTH_A)),
            pl.BlockSpec((tm, WIDTH_B), lambda i: (i, 0)),
            const((WIDTH_A, dm)), const((WIDTH_B, dm)), const((dm, dm)), const((1, dm)),
            const((1, dm)), const((1, dm)),
            const((N_EXPERTS, dm)), const((N_EXPERTS, dm)), const((N_EXPERTS, 1)),
        ],
        out_specs=[
            pl.BlockSpec((tm, dm), lambda i: (i, 0)),
            pl.BlockSpec((TOP_K, tm), lambda i: (0, i)),
            pl.BlockSpec((TOP_K, tm), lambda i: (0, i)),
            pl.BlockSpec((TOP_K, tm), lambda i: (0, i)),
            pl.BlockSpec((tm // MOE_TM, N_EXPERTS, LANES), lambda i: (i, 0, 0)),
        ],
        out_shape=[
            jax.ShapeDtypeStruct((t, dm), _F32),
            jax.ShapeDtypeStruct((TOP_K, t), jnp.int32),
            jax.ShapeDtypeStruct((TOP_K, t), _F32),
            jax.ShapeDtypeStruct((TOP_K, t), jnp.int32),
            jax.ShapeDtypeStruct((t // MOE_TM, N_EXPERTS, LANES), _F32),
        ],
        scratch_shapes=[pltpu.VMEM((len(dils), WIDTH_A // LANES, tm, LANES), _F32),
                        pltpu.VMEM((len(dils), tm, LANES), _F32)],
        compiler_params=_params("parallel"),
        name="merge_router",
    )(x2d, proj2d, proj2d, *dil_outs, *dil_lses, expand, o_b, wpa, wpb, wout, bout.reshape(1, dm),
      g1.reshape(1, dm), b1.reshape(1, dm), wr_hi, wr_lo, b_router.reshape(N_EXPERTS, 1).astype(_F32))


def _group_rows(n):
    return lax.shift_right_logical(n + (ROW_ALIGN - 1), int(math.log2(ROW_ALIGN))) * ROW_ALIGN


def _dispatch_kernel(lpos_ref, gate_ref, h_ref, xs_ref, *, tm, dm):
    rows = xs_ref.shape[1]
    slot = lax.broadcasted_iota(jnp.int32, (rows, tm), 0)
    perm = jnp.zeros((rows, tm), _F32)
    gsel = jnp.zeros((rows, tm), _F32)
    for k in range(TOP_K):
        hit = slot == lpos_ref[k:k + 1, :]
        perm = perm + jnp.where(hit, 1.0, 0.0)
        gsel = gsel + jnp.where(hit, gate_ref[k:k + 1, :], 0.0)
    xs_ref[0, :, 0:dm] = jnp.dot(perm.astype(_BF16), h_ref[...].astype(_BF16), preferred_element_type=_F32)
    xs_ref[0, :, dm:dm + LANES] = jnp.broadcast_to(jnp.sum(gsel, axis=1, keepdims=True), (rows, LANES))


def _dispatch(h, gates, plan):
    t, dm = h.shape
    tm = MOE_TM
    width = dm + LANES
    return pl.pallas_call(
        functools.partial(_dispatch_kernel, tm=tm, dm=dm),
        grid=(t // tm,),
        in_specs=[
            pl.BlockSpec((TOP_K, tm), lambda i: (0, i)),
            pl.BlockSpec((TOP_K, tm), lambda i: (0, i)),
            pl.BlockSpec((tm, dm), lambda i: (i, 0)),
        ],
        out_specs=pl.BlockSpec((1, LOCAL_ROWS, width), lambda i: (i, 0, 0)),
        out_shape=jax.ShapeDtypeStruct((t // tm, LOCAL_ROWS, width), _F32),
        compiler_params=_params("parallel"),
        name="moe_dispatch",
    )(plan["lpos"], gates, h)


def _expert_kernel(blk_e_ref, n_used_ref, tlo_ref, thi_ref, cnt_ref, base_ref, off_ref, ord_ref, next_ref,
                   xs_hbm, wup_hbm, bup_ref, wdn_hbm, bdn_ref, ys_ref,
                   xbuf, sems, pending, wup_f32, wdn_f32, wsems, wup_bf, wdn_bf, *, d_expert):
    i = pl.program_id(0)
    dm = ys_ref.shape[1]
    active = i < n_used_ref[0]

    def seg_copy(buf, src_row, dst_row, n_rows):
        n_rows = pl.multiple_of(n_rows, ROW_ALIGN)
        return pltpu.make_async_copy(xs_hbm.at[pl.ds(pl.multiple_of(src_row, ROW_ALIGN), n_rows)],
                                     xbuf.at[buf, pl.ds(pl.multiple_of(dst_row, ROW_ALIGN), n_rows)],
                                     sems.at[buf])

    def fetch(blk):
        e = blk_e_ref[blk]
        first = blk * MOE_ROWS

        def per_tile(t, total):
            j = t * N_EXPERTS + e
            g0 = base_ref[j]
            lo = jnp.maximum(g0, first)
            hi = jnp.minimum(g0 + _group_rows(cnt_ref[j]), first + MOE_ROWS)
            n_rows = jnp.maximum(hi - lo, 0)

            @pl.when(n_rows > 0)
            def _():
                seg_copy(blk % 2, t * LOCAL_ROWS + off_ref[j] + (lo - g0), lo - first, n_rows).start()

            return total + n_rows

        pending[blk % 2] = lax.fori_loop(tlo_ref[blk], thi_ref[blk] + 1, per_tile, 0)

    @pl.when(i == 0)
    def _():
        xbuf[...] = jnp.zeros_like(xbuf)
        fetch(i)

    @pl.when(i + 1 < n_used_ref[0])
    def _():
        fetch(i + 1)

    def weight_copies(expert, slot):
        return (pltpu.make_async_copy(wup_hbm.at[expert], wup_f32.at[slot], wsems.at[slot]),
                pltpu.make_async_copy(wdn_hbm.at[expert], wdn_f32.at[slot], wsems.at[slot]))

    @pl.when(i == 0)
    def _():
        for cp in weight_copies(blk_e_ref[0], 0):
            cp.start()

    @pl.when(active & ((i == 0) | (blk_e_ref[i] != blk_e_ref[jnp.maximum(i - 1, 0)])))
    def _():
        slot = ord_ref[i] % 2
        for cp in weight_copies(blk_e_ref[i], slot):
            cp.wait()
        wup_bf[...] = wup_f32[slot].astype(_BF16)
        wdn_bf[...] = wdn_f32[slot].astype(_BF16)

        @pl.when(next_ref[i] >= 0)
        def _():
            for cp in weight_copies(next_ref[i], 1 - slot):
                cp.start()

    @pl.when(active)
    def _():
        @pl.when(pending[i % 2] > 0)
        def _():
            seg_copy(i % 2, 0, 0, pending[i % 2]).wait()

        x = xbuf[i % 2, :, 0:dm].astype(_BF16)
        row_gate = xbuf[i % 2, :, dm:dm + 1]
        acc = jnp.zeros(ys_ref.shape, _F32)
        for c in range(d_expert // FFN_CHUNK):
            lo, hi = c * FFN_CHUNK, (c + 1) * FFN_CHUNK
            g = jnp.dot(x, wup_bf[:, lo:hi], preferred_element_type=_F32) + bup_ref[0, :, lo:hi]
            u = (jnp.dot(x, wup_bf[:, d_expert + lo:d_expert + hi], preferred_element_type=_F32)
                 + bup_ref[0, :, d_expert + lo:d_expert + hi])
            gate = jnp.minimum(g, SWIGLU_LIMIT)
            up = jnp.clip(u, -SWIGLU_LIMIT, SWIGLU_LIMIT)
            act = gate * jax.nn.sigmoid(SWIGLU_ALPHA * gate) * (up + 1.0)
            acc = acc + jnp.dot(act.astype(_BF16), wdn_bf[lo:hi, :], preferred_element_type=_F32)
        ys_ref[...] = ((acc + bdn_ref[0]) * row_gate).astype(_BF16).astype(_F32)

    @pl.when(jnp.logical_not(active))
    def _():
        ys_ref[...] = jnp.zeros_like(ys_ref)


def _experts(xs, plan, w_up, b_up, w_down, b_down):
    n_tiles, local_rows, width = xs.shape
    dm = width - LANES
    n_blocks = plan["n_rows"] // MOE_ROWS
    n_exp, _, two_de = w_up.shape
    d_expert = two_de // 2
    grid_spec = pltpu.PrefetchScalarGridSpec(
        num_scalar_prefetch=9,
        grid=(n_blocks,),
        in_specs=[
            pl.BlockSpec(memory_space=pl.ANY),
            pl.BlockSpec(memory_space=pl.ANY),
            pl.BlockSpec((1, 1, two_de), lambda i, be, *_: (be[i], 0, 0)),
            pl.BlockSpec(memory_space=pl.ANY),
            pl.BlockSpec((1, 1, dm), lambda i, be, *_: (be[i], 0, 0)),
        ],
        out_specs=pl.BlockSpec((MOE_ROWS, dm), lambda i, *_: (i, 0)),
        scratch_shapes=[pltpu.VMEM((2, MOE_ROWS, width), _F32), pltpu.SemaphoreType.DMA((2,)),
                        pltpu.SMEM((2,), jnp.int32),
                        pltpu.VMEM((2, dm, two_de), _F32), pltpu.VMEM((2, d_expert, dm), _F32),
                        pltpu.SemaphoreType.DMA((2,)),
                        pltpu.VMEM((dm, two_de), _BF16), pltpu.VMEM((d_expert, dm), _BF16)],
    )
    return pl.pallas_call(
        functools.partial(_expert_kernel, d_expert=d_expert),
        grid_spec=grid_spec,
        out_shape=jax.ShapeDtypeStruct((plan["n_rows"], dm), _F32),
        compiler_params=_params("arbitrary"),
        name="moe_experts",
    )(plan["blk_e"], plan["n_used"], plan["blk_tlo"], plan["blk_thi"], plan["cnt"], plan["base"], plan["off"],
      plan["blk_ord"], plan["blk_next"], xs.reshape(n_tiles * local_rows, width), w_up, b_up.reshape(n_exp, 1, two_de), w_down,
      b_down.reshape(n_exp, 1, dm))


def _combine_kernel(cnt_ref, base_ref, off_ref, rows_ref, lpos_ref, h_ref, g2_ref, b2_ref, ys_hbm, o_ref,
                    local, sems, *, tm, alpha):
    i = pl.program_id(0)
    n_tiles = pl.num_programs(0)
    rows = local.shape[1]

    def group_copy(buf, src_row, dst_row, n_rows):
        n_rows = pl.multiple_of(n_rows, ROW_ALIGN)
        return pltpu.make_async_copy(ys_hbm.at[pl.ds(pl.multiple_of(src_row, ROW_ALIGN), n_rows)],
                                     local.at[buf, pl.ds(pl.multiple_of(dst_row, ROW_ALIGN), n_rows)],
                                     sems.at[buf])

    def fetch(tile):
        for e in range(N_EXPERTS):
            j = tile * N_EXPERTS + e
            n_rows = _group_rows(cnt_ref[j])

            @pl.when(n_rows > 0)
            def _():
                group_copy(tile % 2, base_ref[j], off_ref[j], n_rows).start()

    @pl.when(i == 0)
    def _():
        local[...] = jnp.zeros_like(local)
        fetch(i)

    @pl.when(i + 1 < n_tiles)
    def _():
        fetch(i + 1)

    slot = lax.broadcasted_iota(jnp.int32, (tm, rows), 1)
    pick = jnp.zeros((tm, rows), _F32)
    for k in range(TOP_K):
        pick = pick + jnp.where(slot == lpos_ref[:, k:k + 1], 1.0, 0.0)

    group_copy(i % 2, 0, 0, rows_ref[i]).wait()
    y = jnp.dot(pick.astype(_BF16), local[i % 2].astype(_BF16), preferred_element_type=_F32)
    o_ref[...] = _layer_norm(alpha * h_ref[...] + y, g2_ref[...], b2_ref[...])


def _combine(h, ys, plan, g2, b2, alpha):
    t, dm = h.shape
    tm = MOE_TM
    grid_spec = pltpu.PrefetchScalarGridSpec(
        num_scalar_prefetch=4,
        grid=(t // tm,),
        in_specs=[
            pl.BlockSpec((tm, TOP_K), lambda i, *_: (i, 0)),
            pl.BlockSpec((tm, dm), lambda i, *_: (i, 0)),
            pl.BlockSpec((1, dm), lambda i, *_: (0, 0)),
            pl.BlockSpec((1, dm), lambda i, *_: (0, 0)),
            pl.BlockSpec(memory_space=pl.ANY),
        ],
        out_specs=pl.BlockSpec((tm, dm), lambda i, *_: (i, 0)),
        scratch_shapes=[pltpu.VMEM((2, LOCAL_ROWS, dm), _F32), pltpu.SemaphoreType.DMA((2,))],
    )
    return pl.pallas_call(
        functools.partial(_combine_kernel, tm=tm, alpha=alpha),
        grid_spec=grid_spec,
        out_shape=jax.ShapeDtypeStruct((t, dm), _F32),
        compiler_params=_params("arbitrary"),
        name="moe_combine",
    )(plan["cnt"], plan["base"], plan["off"], plan["tile_rows"], plan["lpos"].T, h, g2.reshape(1, dm),
      b2.reshape(1, dm), ys)


def _round_up(x, m):
    return (x + m - 1) // m * m


def _moe_plan(idx, rank, tile_cnt, tm):
    n_tiles, n_exp = tile_cnt.shape
    t = idx.shape[1]
    cnt = tile_cnt.astype(jnp.int32)
    grp = _round_up(cnt, ROW_ALIGN)
    tot = jnp.sum(grp, axis=0)
    padded = _round_up(tot, MOE_ROWS)
    pend = jnp.cumsum(padded)
    pstart = pend - padded
    base = pstart[None, :] + jnp.cumsum(grp, axis=0) - grp
    off = jnp.cumsum(grp, axis=1) - grp
    onehot = idx[..., None] == jnp.arange(n_exp, dtype=jnp.int32)
    off_tok = jnp.broadcast_to(off[:, None, :], (n_tiles, tm, n_exp)).reshape(t, n_exp)
    lpos = jnp.sum(jnp.where(onehot, off_tok[None], 0), axis=-1) + rank

    n_rows = _round_up(t * idx.shape[0] + n_tiles * n_exp * (ROW_ALIGN - 1) + n_exp * (MOE_ROWS - 1), MOE_ROWS)
    n_blocks = n_rows // MOE_ROWS
    blk_start = jnp.arange(n_blocks, dtype=jnp.int32) * MOE_ROWS
    blk_e = jnp.sum((pend[None, :] <= blk_start[:, None]).astype(jnp.int32), axis=1)
    blk_e = jnp.minimum(blk_e, n_exp - 1)
    n_used = (pend[-1] // MOE_ROWS).astype(jnp.int32)
    blk_e = blk_e[jnp.minimum(jnp.arange(n_blocks, dtype=jnp.int32), n_used - 1)]
    base_b = jnp.take(base, blk_e, axis=1)
    end_b = base_b + jnp.take(grp, blk_e, axis=1)
    blk_tlo = jnp.sum((end_b <= blk_start[None, :]).astype(jnp.int32), axis=0)
    blk_thi = jnp.sum((base_b < blk_start[None, :] + MOE_ROWS).astype(jnp.int32), axis=0) - 1
    nonempty = padded > 0
    expert_ord = jnp.cumsum(nonempty.astype(jnp.int32)) - nonempty.astype(jnp.int32)
    ids = jnp.arange(n_exp, dtype=jnp.int32)
    later = (ids[None, :] > ids[:, None]) & nonempty[None, :]
    expert_next = jnp.min(jnp.where(later, ids[None, :], n_exp), axis=1)
    expert_next = jnp.where(expert_next == n_exp, -1, expert_next)
    return dict(
        cnt=cnt.reshape(-1), base=base.reshape(-1).astype(jnp.int32), off=off.reshape(-1).astype(jnp.int32),
        tile_rows=jnp.sum(grp, axis=1).astype(jnp.int32), lpos=lpos.astype(jnp.int32),
        blk_e=blk_e.astype(jnp.int32), blk_tlo=blk_tlo, blk_thi=blk_thi,
        blk_ord=expert_ord[blk_e].astype(jnp.int32), blk_next=expert_next[blk_e].astype(jnp.int32),
        n_used=n_used.reshape(1), n_rows=n_rows)


def kernel(x, w_in, b_in, lambda_q1, lambda_k1, lambda_q2, lambda_k2, subln_g, w_proj_a, w_proj_b, w_out, b_out, ln1_g, ln1_b, w_router, b_router, w_up, b_up, w_down, b_down, ln2_g, ln2_b):
    bsz, seq, dm = x.shape
    depth = w_in.shape[0]
    alpha = (2.0 * depth) ** 0.25
    t = bsz * seq
    for layer in range(depth):
        lambda_init = 0.8 - 0.6 * math.exp(-0.3 * layer)
        x2d = x.reshape(t, dm)
        n_a, n_b = 3 * WIDTH_A, 3 * WIDTH_B
        w_l, b_l = w_in[layer], b_in[layer]
        query_scale = jnp.full((WIDTH_A,), QUERY_SCALE, _F32)
        ones = functools.partial(jnp.ones, dtype=_F32)
        qkv_a = _in_proj_a(x, w_l[:, :n_a].astype(_BF16), b_l[:n_a],
                           jnp.concatenate([query_scale, ones((2 * WIDTH_A,))]))
        w_rest = jnp.concatenate([w_l[:, n_a + n_b:], w_l[:, n_a:n_a + n_b]], axis=1).astype(_BF16)
        b_rest = jnp.concatenate([b_l[n_a + n_b:], b_l[n_a:n_a + n_b]])
        scale_rest = jnp.concatenate([ones((2 * dm,)), query_scale, ones((2 * WIDTH_B,))])
        proj2d = _in_proj(x2d, w_rest, b_rest, scale_rest)

        dil = [_dilated_pass(a, d) for a, (_, d) in zip(qkv_a, DILATED_PATTERNS)]

        lam_vecs = jnp.stack([lambda_q1[layer], lambda_k1[layer], lambda_q2[layer], lambda_k2[layer]]).astype(_F32)
        o_b = _diff_attention(proj2d.reshape(bsz, seq, -1), 2 * dm, lam_vecs, subln_g[layer],
                              lambda_init).reshape(t, WIDTH_B)

        h, idx, gates, rank, cnt = _merge_router(
            x2d, proj2d, 0, [o for o, _ in dil], [l for _, l in dil], o_b,
            w_proj_a[layer].astype(_BF16), w_proj_b[layer].astype(_BF16),
            w_out[layer].astype(_BF16), b_out[layer], ln1_g[layer], ln1_b[layer],
            w_router[layer], b_router[layer], alpha)

        plan = _moe_plan(idx, rank, cnt[:, :, 0], MOE_TM)
        xs = _dispatch(h, gates, plan)
        ys = _experts(xs, plan, w_up[layer], b_up[layer], w_down[layer], b_down[layer])
        out = _combine(h, ys, plan, ln2_g[layer], ln2_b[layer], alpha)
        x = out.reshape(bsz, seq, dm)
    return x
```

```python
import functools
import math

import numpy as np
import jax
import jax.numpy as jnp
from jax import lax
from jax.experimental import pallas as pl
from jax.experimental.pallas import tpu as pltpu

HEAD_DIM = 64
N_HEADS_A = 8
DILATED_PATTERNS = ((128, 1), (512, 4), (2048, 16))
N_HEADS_B = 4
WIDTH_A = N_HEADS_A * HEAD_DIM
WIDTH_B = N_HEADS_B * 2 * HEAD_DIM
N_ALIBI_HEADS = N_HEADS_A + N_HEADS_B
ALIBI_IDX_A = (0, 1, 3, 4, 6, 7, 9, 10)
ALIBI_IDX_B = (2, 5, 8, 11)
Q_BLOCK = 128
MASK_VALUE = -1e30
N_EXPERTS = 32
TOP_K = 4
SWIGLU_ALPHA = 1.702
SWIGLU_LIMIT = 7.0
LN_EPS = 1e-5
SUBLN_EPS = 1e-5
LOG2E = math.log2(math.e)
QUERY_SCALE = HEAD_DIM ** -0.5 * LOG2E

LANES = 128
V7X_VMEM_LIMIT_BYTES = 56 * 1024 * 1024

PROJ_TM = 1024
PROJ_A_TM = 512
DIL_ROWS = 2 * Q_BLOCK
DIL_HALO = 64
DIFF_TQ = 512
DIFF_TK = 512
MOE_ROWS = 512
FFN_CHUNK = 512
MERGE_TM = 512
MOE_TM = 256
ROW_ALIGN = 8
LOCAL_ROWS = -(-(TOP_K * MOE_TM + N_EXPERTS * (ROW_ALIGN - 1)) // LANES) * LANES

_F32 = jnp.float32
_BF16 = jnp.bfloat16


def _params(*sem):
    return pltpu.CompilerParams(dimension_semantics=sem, vmem_limit_bytes=V7X_VMEM_LIMIT_BYTES)


def _alibi_slopes():
    return (2.0 ** (-8.0 * np.arange(1, N_ALIBI_HEADS + 1) / N_ALIBI_HEADS)).astype(np.float32)


def _in_proj_kernel(x_ref, w_ref, b_ref, cs_ref, o_ref):
    x = x_ref[...].astype(_BF16)
    acc = jnp.dot(x, w_ref[...], preferred_element_type=_F32)
    o_ref[...] = ((acc + b_ref[...]) * cs_ref[...]).astype(o_ref.dtype)


def _in_proj(x2d, w_bf16, b, colscale):
    t, dm = x2d.shape
    n = w_bf16.shape[1]
    tn = n // 2
    assert n % 2 == 0 and tn % LANES == 0
    return pl.pallas_call(
        _in_proj_kernel,
        grid=(t // PROJ_TM, n // tn),
        in_specs=[
            pl.BlockSpec((PROJ_TM, dm), lambda i, j: (i, 0)),
            pl.BlockSpec((dm, tn), lambda i, j: (0, j)),
            pl.BlockSpec((1, tn), lambda i, j: (0, j)),
            pl.BlockSpec((1, tn), lambda i, j: (0, j)),
        ],
        out_specs=pl.BlockSpec((PROJ_TM, tn), lambda i, j: (i, j)),
        out_shape=jax.ShapeDtypeStruct((t, n), _BF16),
        compiler_params=_params("parallel", "arbitrary"),
        name="in_proj",
    )(x2d, w_bf16, b.reshape(1, n), colscale.reshape(1, n))


def _in_proj_a_kernel(x_ref, w_ref, b_ref, cs_ref, *refs):
    out_refs, acc_ref, stage_ref = refs[:-2], refs[-2], refs[-1]
    x = x_ref[0].astype(_BF16)
    acc = (jnp.dot(x, w_ref[...], preferred_element_type=_F32) + b_ref[...]) * cs_ref[...]
    n_chunks, tm, _ = acc_ref.shape
    for c in range(n_chunks):
        acc_ref[c] = acc[:, c * LANES:(c + 1) * LANES]
    for o_ref, (_, d) in zip(out_refs, DILATED_PATTERNS):
        if d == 1:
            o_ref[0, 0] = acc.astype(o_ref.dtype)
            continue
        if d == 4:
            for r in range(d):
                for c in range(n_chunks):
                    rows = acc_ref[c, pl.ds(r, tm // d, stride=d), :]
                    stage_ref[c, r] = rows
                    o_ref[0, r, :, c * LANES:(c + 1) * LANES] = rows.astype(o_ref.dtype)
            continue
        assert d == 16
        for r1 in range(4):
            for r2 in range(4):
                for c in range(n_chunks):
                    rows = stage_ref[c, r1, pl.ds(r2, tm // d, stride=4), :]
                    o_ref[0, r1 + 4 * r2, :, c * LANES:(c + 1) * LANES] = rows.astype(o_ref.dtype)


def _in_proj_a(x, w_bf16, b, colscale):
    bsz, seq, dm = x.shape
    n = w_bf16.shape[1]
    tm = PROJ_A_TM
    per_b = seq // tm
    dils = [d for _, d in DILATED_PATTERNS]
    assert seq % tm == 0 and all(tm % d == 0 and (tm // d) % 16 == 0 for d in dils) and dils == [1, 4, 16]
    return pl.pallas_call(
        _in_proj_a_kernel,
        grid=(bsz * per_b,),
        in_specs=[
            pl.BlockSpec((1, tm, dm), lambda i: (i // per_b, i % per_b, 0)),
            pl.BlockSpec((dm, n), lambda i: (0, 0)),
            pl.BlockSpec((1, n), lambda i: (0, 0)),
            pl.BlockSpec((1, n), lambda i: (0, 0)),
        ],
        out_specs=[pl.BlockSpec((1, d, tm // d, n), lambda i: (i // per_b, 0, i % per_b, 0)) for d in dils],
        out_shape=[jax.ShapeDtypeStruct((bsz, d, seq // d, n), _BF16) for d in dils],
        scratch_shapes=[pltpu.VMEM((n // LANES, tm, LANES), _F32),
                        pltpu.VMEM((n // LANES, 4, tm // 4, LANES), _F32)],
        compiler_params=_params("parallel"),
        name="in_proj_a",
    )(x, w_bf16, b.reshape(1, n), colscale.reshape(1, n))


def _dilated_bias_table(dilation):
    slopes = _alibi_slopes()[list(ALIBI_IDX_A)]
    band = Q_BLOCK + 2 * DIL_HALO
    qi = np.arange(Q_BLOCK)[:, None]
    kj = np.arange(band)[None, :]
    rel = qi - kj + DIL_HALO
    in_band = np.abs(rel) <= DIL_HALO
    base = -slopes[:, None, None] * (dilation * np.abs(rel)).astype(np.float32)[None]
    base = (base.astype(np.float64) * LOG2E).astype(np.float32)
    edge = (np.ones_like(kj, bool), kj >= DIL_HALO, kj < band - DIL_HALO)
    out = np.stack([np.where(in_band & e, base, np.float32(MASK_VALUE)) for e in edge])
    return out.astype(np.float32)


def _dilated_kernel(q_ref, km_ref, kp_ref, kn_ref, vm_ref, vp_ref, vn_ref, bias_ref, o_ref, lse_ref,
                    kbuf, vbuf, *, nblk):
    i = pl.program_id(2)
    h0, h1 = DIL_HALO, DIL_HALO + DIL_ROWS
    kbuf[0:h0, :] = kp_ref[0, 0]
    kbuf[h0:h1, :] = km_ref[0, 0]
    kbuf[h1:h1 + DIL_HALO, :] = kn_ref[0, 0]
    vbuf[0:h0, :] = vp_ref[0, 0]
    vbuf[h0:h1, :] = vm_ref[0, 0]
    vbuf[h1:h1 + DIL_HALO, :] = vn_ref[0, 0]

    lane = lax.broadcasted_iota(jnp.int32, (Q_BLOCK, LANES), 1)
    low_half = lane < HEAD_DIM
    band = Q_BLOCK + 2 * DIL_HALO
    variants = (jnp.where(i == 0, 1, 0), jnp.where(i == nblk - 1, 2, 0))

    for j in range(DIL_ROWS // Q_BLOCK):
        rows = slice(j * Q_BLOCK, (j + 1) * Q_BLOCK)
        krows = slice(j * Q_BLOCK, j * Q_BLOCK + band)
        lse_cols = []
        for hp in range(N_HEADS_A // 2):
            cols = slice(hp * LANES, (hp + 1) * LANES)
            q_pair = q_ref[0, 0, rows, cols]
            k_pair = kbuf[krows, cols]
            v_pair = vbuf[krows, cols]
            halves = []
            for hh in range(2):
                head = 2 * hp + hh
                keep = low_half if hh == 0 else jnp.logical_not(low_half)
                qm = jnp.where(keep, q_pair, jnp.zeros_like(q_pair))
                s = lax.dot_general(qm, k_pair, (((1,), (1,)), ((), ())), preferred_element_type=_F32)
                s = s + bias_ref[variants[j], head]
                m = jnp.max(s, axis=-1, keepdims=True)
                p = jnp.exp2(s - m)
                z = jnp.sum(p, axis=-1, keepdims=True)
                halves.append(jnp.dot(p.astype(_BF16), v_pair, preferred_element_type=_F32) * (1.0 / z))
                lse_cols.append(m + jnp.log2(z))
            o_ref[0, 0, rows, cols] = jnp.where(low_half, halves[0], halves[1])
        tile = jnp.zeros((Q_BLOCK, LANES), _F32)
        for head in range(N_HEADS_A):
            tile = jnp.where(lane == head, lse_cols[head], tile)
        lse_ref[0, 0, rows, :] = tile


def _dilated_pass(qkv, dilation):
    bsz, d, L, _ = qkv.shape
    assert d == dilation and L % DIL_ROWS == 0 and DIL_ROWS % DIL_HALO == 0
    nblk = L // DIL_ROWS
    per = DIL_ROWS // DIL_HALO
    n_halo = L // DIL_HALO
    bias = jnp.asarray(_dilated_bias_table(dilation))

    def main(s):
        return pl.BlockSpec((1, 1, DIL_ROWS, WIDTH_A), lambda b, r, i: (b, r, i, s))

    def left(s):
        return pl.BlockSpec((1, 1, DIL_HALO, WIDTH_A), lambda b, r, i: (b, r, jnp.maximum(per * i - 1, 0), s))

    def right(s):
        return pl.BlockSpec((1, 1, DIL_HALO, WIDTH_A),
                            lambda b, r, i: (b, r, jnp.minimum(per * i + per, n_halo - 1), s))

    rows_buf = DIL_ROWS + 2 * DIL_HALO
    return pl.pallas_call(
        functools.partial(_dilated_kernel, nblk=nblk),
        grid=(bsz, dilation, nblk),
        in_specs=[main(0), main(1), left(1), right(1), main(2), left(2), right(2),
                  pl.BlockSpec(bias.shape, lambda b, r, i: (0, 0, 0, 0))],
        out_specs=[pl.BlockSpec((1, 1, DIL_ROWS, WIDTH_A), lambda b, r, i: (b, r, i, 0)),
                   pl.BlockSpec((1, 1, DIL_ROWS, LANES), lambda b, r, i: (b, r, i, 0))],
        out_shape=[jax.ShapeDtypeStruct((bsz, dilation, L, WIDTH_A), _F32),
                   jax.ShapeDtypeStruct((bsz, dilation, L, LANES), _F32)],
        scratch_shapes=[pltpu.VMEM((rows_buf, WIDTH_A), _BF16), pltpu.VMEM((rows_buf, WIDTH_A), _BF16)],
        compiler_params=_params("parallel", "parallel", "arbitrary"),
        name=f"dilated_d{dilation}",
    )(qkv, qkv, qkv, qkv, qkv, qkv, qkv, bias)


def _split_bf16(x):
    hi = x.astype(jnp.bfloat16)
    lo = (x - hi.astype(np.float64)).astype(jnp.bfloat16)
    return hi, lo


def _diff_tables():
    s2 = _alibi_slopes()[list(ALIBI_IDX_B)].astype(np.float64) * LOG2E
    qaug = np.zeros((N_HEADS_B, DIFF_TQ, LANES), jnp.bfloat16)
    kaug = np.zeros((N_HEADS_B, 2, DIFF_TK, LANES), jnp.bfloat16)
    i = np.arange(DIFF_TQ, dtype=np.float64)
    j = np.arange(DIFF_TK, dtype=np.float64)
    for h in range(N_HEADS_B):
        r_hi, r_lo = _split_bf16(-s2[h] * i)
        qaug[h, :, 0] = 1.0
        qaug[h, :, 1] = 1.0
        qaug[h, :, 2] = r_hi
        qaug[h, :, 3] = r_lo
        for side, sign in enumerate((1.0, -1.0)):
            c_hi, c_lo = _split_bf16(sign * s2[h] * j)
            kaug[h, side, :, 0] = c_hi
            kaug[h, side, :, 1] = c_lo
            kaug[h, side, :, 2] = sign
            kaug[h, side, :, 3] = sign
    halves = DIFF_TK // DIFF_TQ
    off = (np.arange(halves) * DIFF_TQ)[:, None, None]
    dist = np.abs(off + i[None, :, None] - j[None, None, :])
    diag = (-s2[:, None, None, None] * dist[None]).astype(np.float32)
    return jnp.asarray(qaug), jnp.asarray(kaug), jnp.asarray(diag), jnp.asarray(s2.astype(np.float32))


def _diff_kernel(slope_ref, lam_ref, q_ref, k_ref, v_ref, qaug_ref, kaug_ref, diag_ref, g_ref, o_ref,
                 *, seq, lambda_init):
    head = pl.program_id(1)
    qb = pl.program_id(2)
    q0 = qb * DIFF_TQ
    halves = DIFF_TK // DIFF_TQ
    kd = qb // halves
    n_kb = seq // DIFF_TK
    slope2 = slope_ref[head]
    nt = (((1,), (1,)), ((), ()))

    q = q_ref[0]
    lane = lax.broadcasted_iota(jnp.int32, q.shape, 1)
    zero = jnp.zeros_like(q)
    qaug = qaug_ref[0]
    q_maps = (jnp.concatenate([jnp.where(lane < HEAD_DIM, q, zero), qaug], axis=1),
              jnp.concatenate([jnp.where(lane >= HEAD_DIM, q, zero), qaug], axis=1))
    zero_aug = jnp.zeros((DIFF_TK, LANES), _BF16)
    vlane = lax.broadcasted_iota(jnp.int32, (DIFF_TK, LANES), 1)
    ones_col = jnp.where(vlane == 0, 1.0, 0.0).astype(_BF16)

    def load(kb):
        k0 = pl.multiple_of(kb * DIFF_TK, DIFF_TK)
        return k_ref[0, pl.ds(k0, DIFF_TK), :], jnp.concatenate([v_ref[0, pl.ds(k0, DIFF_TK), :], ones_col], axis=1)

    def split(pv):
        return pv[:, LANES:LANES + 1], pv[:, 0:LANES]

    def update(s, const, vblk, carry):
        m_old, l_old, acc = carry
        m_new = jnp.maximum(m_old, jnp.max(s, axis=-1, keepdims=True) + const)
        alpha = jnp.exp2(m_old - m_new)
        p = jnp.exp2(s - (m_new - const))
        l_blk, o_blk = split(jnp.dot(p.astype(_BF16), vblk, preferred_element_type=_F32))
        return m_new, alpha * l_old + l_blk, alpha * acc + o_blk

    kblk, vblk = load(kd)
    k_all = jnp.concatenate([kblk, zero_aug], axis=1)
    bias = diag_ref[0, qb % halves]
    carries = []
    for mp in range(2):
        s = lax.dot_general(q_maps[mp], k_all, nt, preferred_element_type=_F32) + bias
        m = jnp.max(s, axis=-1, keepdims=True)
        p = jnp.exp2(s - m)
        carries.append((m,) + split(jnp.dot(p.astype(_BF16), vblk, preferred_element_type=_F32)))

    for t in range(n_kb - 1):
        kb = jnp.where(t >= kd, t + 1, t)
        side = jnp.where(t >= kd, 1, 0)
        kblk, vblk = load(kb)
        k_all = jnp.concatenate([kblk, kaug_ref[0, side]], axis=1)
        const = -slope2 * jnp.abs(q0 - kb * DIFF_TK).astype(_F32)
        for mp in range(2):
            s = lax.dot_general(q_maps[mp], k_all, nt, preferred_element_type=_F32)
            carries[mp] = update(s, const, vblk, carries[mp])

    lv = lam_ref[...]
    lam = (jnp.exp(jnp.sum(lv[0:1] * lv[1:2], axis=-1, keepdims=True))
           - jnp.exp(jnp.sum(lv[2:3] * lv[3:4], axis=-1, keepdims=True)) + lambda_init)
    (_, l1, a1), (_, l2, a2) = carries
    of = a1 * (1.0 / l1) - lam * (a2 * (1.0 / l2))
    of = of * lax.rsqrt(jnp.mean(of * of, axis=-1, keepdims=True) + SUBLN_EPS)
    of = of * g_ref[...] * (1.0 - lambda_init)
    o_ref[0] = of.astype(o_ref.dtype)


def _diff_attention(proj, q_offset, lam_vecs, subln_g, lambda_init):
    bsz, seq, _ = proj.shape
    assert seq % DIFF_TQ == 0 and seq % DIFF_TK == 0 and q_offset % LANES == 0
    qcol = q_offset // LANES
    kcol = qcol + WIDTH_B // LANES
    vcol = kcol + WIDTH_B // LANES
    assert DIFF_TK % DIFF_TQ == 0
    halves = DIFF_TK // DIFF_TQ
    qaug, kaug, diag, slopes2 = _diff_tables()
    return pl.pallas_call(
        functools.partial(_diff_kernel, seq=seq, lambda_init=lambda_init),
        grid=(bsz, N_HEADS_B, seq // DIFF_TQ),
        in_specs=[
            pl.BlockSpec(memory_space=pltpu.SMEM),
            pl.BlockSpec((4, HEAD_DIM), lambda b, h, i: (0, 0)),
            pl.BlockSpec((1, DIFF_TQ, LANES), lambda b, h, i: (b, i, qcol + h)),
            pl.BlockSpec((1, seq, LANES), lambda b, h, i: (b, 0, kcol + h)),
            pl.BlockSpec((1, seq, LANES), lambda b, h, i: (b, 0, vcol + h)),
            pl.BlockSpec((1, DIFF_TQ, LANES), lambda b, h, i: (h, 0, 0)),
            pl.BlockSpec((1, 2, DIFF_TK, LANES), lambda b, h, i: (h, 0, 0, 0)),
            pl.BlockSpec((1, halves, DIFF_TQ, DIFF_TK), lambda b, h, i: (h, 0, 0, 0)),
            pl.BlockSpec((1, LANES), lambda b, h, i: (0, 0)),
        ],
        out_specs=pl.BlockSpec((1, DIFF_TQ, LANES), lambda b, h, i: (b, i, h)),
        out_shape=jax.ShapeDtypeStruct((bsz, seq, WIDTH_B), _BF16),
        compiler_params=_params("parallel", "parallel", "arbitrary"),
        name="diff_attn",
    )(slopes2, lam_vecs, proj, proj, proj, qaug, kaug, diag, subln_g.reshape(1, LANES).astype(_F32))


def _layer_norm(z, g, b):
    mu = jnp.mean(z, axis=-1, keepdims=True)
    zc = z - mu
    var = jnp.mean(zc * zc, axis=-1, keepdims=True)
    return zc * lax.rsqrt(var + LN_EPS) * g + b


def _mix_patterns(o_refs, lse_refs, expand_ref, o_nat, lse_nat):
    n_chunks, tm = o_nat.shape[1], o_nat.shape[2]
    outs, lses = [], []
    for p, (_, d) in enumerate(DILATED_PATTERNS):
        if d == 1:
            outs.append(o_refs[p][0, 0])
            lses.append(lse_refs[p][0, 0])
            continue
        for r in range(d):
            for c in range(n_chunks):
                o_nat[p, c, pl.ds(r, tm // d, stride=d), :] = o_refs[p][0, r, :, c * LANES:(c + 1) * LANES]
            lse_nat[p, pl.ds(r, tm // d, stride=d), :] = lse_refs[p][0, r]
        outs.append(jnp.concatenate([o_nat[p, c] for c in range(n_chunks)], axis=1))
        lses.append(lse_nat[p])
    top = functools.reduce(jnp.maximum, lses)
    es = [jnp.exp2(l - top) for l in lses]
    inv = 1.0 / functools.reduce(lambda a, b: a + b, es)
    mixed = None
    for o, e in zip(outs, es):
        w = e * inv
        w_hi = w.astype(_BF16)
        w_lo = (w - w_hi.astype(_F32)).astype(_BF16)
        w_full = (jnp.dot(w_hi, expand_ref[...], preferred_element_type=_F32)
                  + jnp.dot(w_lo, expand_ref[...], preferred_element_type=_F32))
        mixed = w_full * o if mixed is None else mixed + w_full * o
    return mixed


def _merge_kernel(x_ref, ga_ref, gb_ref, o1_ref, o2_ref, o3_ref, l1_ref, l2_ref, l3_ref, expand_ref, ob_ref,
                  wpa_ref, wpb_ref, wout_ref, bout_ref, g1_ref, b1_ref, wrh_ref, wrl_ref, br_ref,
                  h_ref, lpos_ref, cnt_ref, xs_ref, o_nat, lse_nat, *, alpha):
    o_a = _mix_patterns((o1_ref, o2_ref, o3_ref), (l1_ref, l2_ref, l3_ref), expand_ref, o_nat, lse_nat)
    pa = jnp.dot(o_a.astype(_BF16), wpa_ref[...], preferred_element_type=_F32)
    pb = jnp.dot(ob_ref[...], wpb_ref[...], preferred_element_type=_F32)
    merged = jax.nn.sigmoid(ga_ref[...].astype(_F32)) * pa + jax.nn.sigmoid(gb_ref[...].astype(_F32)) * pb
    y = jnp.dot(merged.astype(_BF16), wout_ref[...], preferred_element_type=_F32) + bout_ref[...]
    h = _layer_norm(alpha * x_ref[...] + y, g1_ref[...], b1_ref[...])
    h_ref[...] = h

    h_hi = h.astype(_BF16)
    h_lo = (h - h_hi.astype(_F32)).astype(_BF16)
    nt = (((1,), (1,)), ((), ()))
    logits = (lax.dot_general(wrh_ref[...], h_hi, nt, preferred_element_type=_F32)
              + lax.dot_general(wrl_ref[...], h_hi, nt, preferred_element_type=_F32)
              + lax.dot_general(wrh_ref[...], h_lo, nt, preferred_element_type=_F32)
              + br_ref[...])

    tm = logits.shape[1]
    expert = lax.broadcasted_iota(jnp.int32, logits.shape, 0)
    work = logits
    vals, sels = [], []
    for _ in range(TOP_K):
        mx = jnp.max(work, axis=0, keepdims=True)
        idx = jnp.min(jnp.where(work == mx, expert, N_EXPERTS), axis=0, keepdims=True)
        sel = expert == idx
        work = jnp.where(sel, -jnp.inf, work)
        vals.append(mx)
        sels.append(sel)
    ex = [jnp.exp(v - vals[0]) for v in vals]
    inv = 1.0 / (ex[0] + ex[1] + ex[2] + ex[3])
    gates = [e * inv for e in ex]

    chosen = sels[0] | sels[1] | sels[2] | sels[3]
    onehot = jnp.where(chosen, 1.0, 0.0).astype(_BF16)
    earlier = lax.broadcasted_iota(jnp.int32, (tm, tm), 0)
    later = lax.broadcasted_iota(jnp.int32, (tm, tm), 1)
    sub_shift = int(math.log2(MOE_TM))
    before = (earlier < later) & (lax.shift_right_logical(earlier, sub_shift)
                                  == lax.shift_right_logical(later, sub_shift))
    upper = jnp.where(before, 1.0, 0.0).astype(_BF16)
    in_tile = jnp.dot(onehot, upper, preferred_element_type=_F32)
    lower = (lax.broadcasted_iota(jnp.int32, (N_EXPERTS, N_EXPERTS), 1)
             < lax.broadcasted_iota(jnp.int32, (N_EXPERTS, N_EXPERTS), 0))
    lower = jnp.where(lower, 1.0, 0.0).astype(_BF16)
    rows = xs_ref.shape[1]
    dm = h.shape[1]
    slot = lax.broadcasted_iota(jnp.int32, (rows, MOE_TM), 0)
    for sub in range(tm // MOE_TM):
        cols = slice(sub * MOE_TM, (sub + 1) * MOE_TM)
        count = jnp.sum(onehot[:, cols].astype(_F32), axis=1, keepdims=True)
        cnt_ref[sub] = jnp.broadcast_to(count, cnt_ref.shape[1:])
        group = jnp.floor((count + (ROW_ALIGN - 1)) * (1.0 / ROW_ALIGN)) * ROW_ALIGN
        first_row = jnp.dot(lower, jnp.broadcast_to(group, (N_EXPERTS, LANES)).astype(_BF16),
                            preferred_element_type=_F32)[:, 0:1]
        place = in_tile[:, cols] + first_row
        perm = jnp.zeros((rows, MOE_TM), _F32)
        gsel = jnp.zeros((rows, MOE_TM), _F32)
        for k in range(TOP_K):
            pos = jnp.sum(jnp.where(sels[k][:, cols], place, 0.0), axis=0, keepdims=True).astype(jnp.int32)
            lpos_ref[k:k + 1, cols] = pos
            hit = slot == pos
            perm = perm + jnp.where(hit, 1.0, 0.0)
            gsel = gsel + jnp.where(hit, gates[k][:, cols], 0.0)
        xs_ref[sub, :, 0:dm] = jnp.dot(perm.astype(_BF16), h_hi[cols, :], preferred_element_type=_F32)
        xs_ref[sub, :, dm:dm + LANES] = jnp.broadcast_to(jnp.sum(gsel, axis=1, keepdims=True), (rows, LANES))


def _merge_router(x2d, proj2d, gate_offset, dil_outs, dil_lses, o_b, wpa, wpb, wout, bout, g1, b1,
                  w_router, b_router, alpha):
    t, dm = x2d.shape
    tm = MERGE_TM
    bsz, _, seq_over_d0, _ = dil_outs[0].shape
    seq = seq_over_d0 * DILATED_PATTERNS[0][1]
    per_b = seq // tm
    dils = [d for _, d in DILATED_PATTERNS]
    assert gate_offset % dm == 0 and seq % tm == 0 and all(tm % d == 0 for d in dils)
    gcol = gate_offset // dm
    wr_t = w_router.T.astype(_F32)
    wr_hi = wr_t.astype(_BF16)
    wr_lo = (wr_t - wr_hi.astype(_F32)).astype(_BF16)
    expand = np.zeros((LANES, WIDTH_A), np.float32)
    for head in range(N_HEADS_A):
        expand[head, head * HEAD_DIM:(head + 1) * HEAD_DIM] = 1.0
    expand = jnp.asarray(expand, _BF16)

    def const(shape):
        return pl.BlockSpec(shape, lambda i: tuple(0 for _ in shape))

    def residue_major(d, width):
        return pl.BlockSpec((1, d, tm // d, width), lambda i: (i // per_b, 0, i % per_b, 0))

    return pl.pallas_call(
        functools.partial(_merge_kernel, alpha=alpha),
        grid=(t // tm,),
        in_specs=[
            pl.BlockSpec((tm, dm), lambda i: (i, 0)),
            pl.BlockSpec((tm, dm), lambda i: (i, gcol)),
            pl.BlockSpec((tm, dm), lambda i: (i, gcol + 1)),
            *[residue_major(d, WIDTH_A) for d in dils],
            *[residue_major(d, LANES) for d in dils],
            const((LANES, WIDTH_A)),
            pl.BlockSpec((tm, WIDTH_B), lambda i: (i, 0)),
            const((WIDTH_A, dm)), const((WIDTH_B, dm)), const((dm, dm)), const((1, dm)),
            const((1, dm)), const((1, dm)),
            const((N_EXPERTS, dm)), const((N_EXPERTS, dm)), const((N_EXPERTS, 1)),
        ],
        out_specs=[
            pl.BlockSpec((tm, dm), lambda i: (i, 0)),
            pl.BlockSpec((TOP_K, tm), lambda i: (0, i)),
            pl.BlockSpec((tm // MOE_TM, N_EXPERTS, LANES), lambda i: (i, 0, 0)),
            pl.BlockSpec((tm // MOE_TM, LOCAL_ROWS, dm + LANES), lambda i: (i, 0, 0)),
        ],
        out_shape=[
            jax.ShapeDtypeStruct((t, dm), _F32),
            jax.ShapeDtypeStruct((TOP_K, t), jnp.int32),
            jax.ShapeDtypeStruct((t // MOE_TM, N_EXPERTS, LANES), _F32),
            jax.ShapeDtypeStruct((t // MOE_TM, LOCAL_ROWS, dm + LANES), _F32),
        ],
        scratch_shapes=[pltpu.VMEM((len(dils), WIDTH_A // LANES, tm, LANES), _F32),
                        pltpu.VMEM((len(dils), tm, LANES), _F32)],
        compiler_params=_params("parallel"),
        name="merge_router",
    )(x2d, proj2d, proj2d, *dil_outs, *dil_lses, expand, o_b, wpa, wpb, wout, bout.reshape(1, dm),
      g1.reshape(1, dm), b1.reshape(1, dm), wr_hi, wr_lo, b_router.reshape(N_EXPERTS, 1).astype(_F32))


def _group_rows(n):
    return lax.shift_right_logical(n + (ROW_ALIGN - 1), int(math.log2(ROW_ALIGN))) * ROW_ALIGN


def _expert_kernel(blk_e_ref, n_used_ref, tlo_ref, thi_ref, cnt_ref, base_ref, off_ref, ord_ref, next_ref,
                   xs_hbm, wup_hbm, bup_ref, wdn_hbm, bdn_ref, ys_ref,
                   xbuf, sems, pending, wup_f32, wdn_f32, wsems, wup_bf, wdn_bf, *, d_expert):
    i = pl.program_id(0)
    dm = ys_ref.shape[1]
    active = i < n_used_ref[0]

    def seg_copy(buf, src_row, dst_row, n_rows):
        n_rows = pl.multiple_of(n_rows, ROW_ALIGN)
        return pltpu.make_async_copy(xs_hbm.at[pl.ds(pl.multiple_of(src_row, ROW_ALIGN), n_rows)],
                                     xbuf.at[buf, pl.ds(pl.multiple_of(dst_row, ROW_ALIGN), n_rows)],
                                     sems.at[buf])

    def fetch(blk):
        e = blk_e_ref[blk]
        first = blk * MOE_ROWS

        def per_tile(t, total):
            j = t * N_EXPERTS + e
            g0 = base_ref[j]
            lo = jnp.maximum(g0, first)
            hi = jnp.minimum(g0 + _group_rows(cnt_ref[j]), first + MOE_ROWS)
            n_rows = jnp.maximum(hi - lo, 0)

            @pl.when(n_rows > 0)
            def _():
                seg_copy(blk % 2, t * LOCAL_ROWS + off_ref[j] + (lo - g0), lo - first, n_rows).start()

            return total + n_rows

        pending[blk % 2] = lax.fori_loop(tlo_ref[blk], thi_ref[blk] + 1, per_tile, 0)

    @pl.when(i == 0)
    def _():
        xbuf[...] = jnp.zeros_like(xbuf)
        fetch(i)

    @pl.when(i + 1 < n_used_ref[0])
    def _():
        fetch(i + 1)

    def weight_copies(expert, slot):
        return (pltpu.make_async_copy(wup_hbm.at[expert], wup_f32.at[slot], wsems.at[slot]),
                pltpu.make_async_copy(wdn_hbm.at[expert], wdn_f32.at[slot], wsems.at[slot]))

    @pl.when(i == 0)
    def _():
        for cp in weight_copies(blk_e_ref[0], 0):
            cp.start()

    @pl.when(active & ((i == 0) | (blk_e_ref[i] != blk_e_ref[jnp.maximum(i - 1, 0)])))
    def _():
        slot = ord_ref[i] % 2
        for cp in weight_copies(blk_e_ref[i], slot):
            cp.wait()
        wup_bf[...] = wup_f32[slot].astype(_BF16)
        wdn_bf[...] = wdn_f32[slot].astype(_BF16)

        @pl.when(next_ref[i] >= 0)
        def _():
            for cp in weight_copies(next_ref[i], 1 - slot):
                cp.start()

    @pl.when(active)
    def _():
        @pl.when(pending[i % 2] > 0)
        def _():
            seg_copy(i % 2, 0, 0, pending[i % 2]).wait()

        x = xbuf[i % 2, :, 0:dm].astype(_BF16)
        row_gate = xbuf[i % 2, :, dm:dm + 1]
        acc = jnp.zeros(ys_ref.shape, _F32)
        for c in range(d_expert // FFN_CHUNK):
            lo, hi = c * FFN_CHUNK, (c + 1) * FFN_CHUNK
            g = jnp.dot(x, wup_bf[:, lo:hi], preferred_element_type=_F32) + bup_ref[0, :, lo:hi]
            u = (jnp.dot(x, wup_bf[:, d_expert + lo:d_expert + hi], preferred_element_type=_F32)
                 + bup_ref[0, :, d_expert + lo:d_expert + hi])
            gate = jnp.minimum(g, SWIGLU_LIMIT)
            up = jnp.clip(u, -SWIGLU_LIMIT, SWIGLU_LIMIT)
            act = gate * jax.nn.sigmoid(SWIGLU_ALPHA * gate) * (up + 1.0)
            acc = acc + jnp.dot(act.astype(_BF16), wdn_bf[lo:hi, :], preferred_element_type=_F32)
        ys_ref[...] = ((acc + bdn_ref[0]) * row_gate).astype(_BF16).astype(_F32)

    @pl.when(jnp.logical_not(active))
    def _():
        ys_ref[...] = jnp.zeros_like(ys_ref)


def _experts(xs, plan, w_up, b_up, w_down, b_down):
    n_tiles, local_rows, width = xs.shape
    dm = width - LANES
    n_blocks = plan["n_rows"] // MOE_ROWS
    n_exp, _, two_de = w_up.shape
    d_expert = two_de // 2
    grid_spec = pltpu.PrefetchScalarGridSpec(
        num_scalar_prefetch=9,
        grid=(n_blocks,),
        in_specs=[
            pl.BlockSpec(memory_space=pl.ANY),
            pl.BlockSpec(memory_space=pl.ANY),
            pl.BlockSpec((1, 1, two_de), lambda i, be, *_: (be[i], 0, 0)),
            pl.BlockSpec(memory_space=pl.ANY),
            pl.BlockSpec((1, 1, dm), lambda i, be, *_: (be[i], 0, 0)),
        ],
        out_specs=pl.BlockSpec((MOE_ROWS, dm), lambda i, *_: (i, 0)),
        scratch_shapes=[pltpu.VMEM((2, MOE_ROWS, width), _F32), pltpu.SemaphoreType.DMA((2,)),
                        pltpu.SMEM((2,), jnp.int32),
                        pltpu.VMEM((2, dm, two_de), _F32), pltpu.VMEM((2, d_expert, dm), _F32),
                        pltpu.SemaphoreType.DMA((2,)),
                        pltpu.VMEM((dm, two_de), _BF16), pltpu.VMEM((d_expert, dm), _BF16)],
    )
    return pl.pallas_call(
        functools.partial(_expert_kernel, d_expert=d_expert),
        grid_spec=grid_spec,
        out_shape=jax.ShapeDtypeStruct((plan["n_rows"], dm), _F32),
        compiler_params=_params("arbitrary"),
        name="moe_experts",
    )(plan["blk_e"], plan["n_used"], plan["blk_tlo"], plan["blk_thi"], plan["cnt"], plan["base"], plan["off"],
      plan["blk_ord"], plan["blk_next"], xs.reshape(n_tiles * local_rows, width), w_up, b_up.reshape(n_exp, 1, two_de), w_down,
      b_down.reshape(n_exp, 1, dm))


def _combine_kernel(cnt_ref, base_ref, off_ref, rows_ref, lpos_ref, h_ref, g2_ref, b2_ref, ys_hbm, o_ref,
                    local, sems, *, tm, alpha):
    i = pl.program_id(0)
    n_tiles = pl.num_programs(0)
    rows = local.shape[1]

    def group_copy(buf, src_row, dst_row, n_rows):
        n_rows = pl.multiple_of(n_rows, ROW_ALIGN)
        return pltpu.make_async_copy(ys_hbm.at[pl.ds(pl.multiple_of(src_row, ROW_ALIGN), n_rows)],
                                     local.at[buf, pl.ds(pl.multiple_of(dst_row, ROW_ALIGN), n_rows)],
                                     sems.at[buf])

    def fetch(tile):
        for e in range(N_EXPERTS):
            j = tile * N_EXPERTS + e
            n_rows = _group_rows(cnt_ref[j])

            @pl.when(n_rows > 0)
            def _():
                group_copy(tile % 2, base_ref[j], off_ref[j], n_rows).start()

    @pl.when(i == 0)
    def _():
        local[...] = jnp.zeros_like(local)
        fetch(i)

    @pl.when(i + 1 < n_tiles)
    def _():
        fetch(i + 1)

    slot = lax.broadcasted_iota(jnp.int32, (tm, rows), 1)
    pick = jnp.zeros((tm, rows), _F32)
    for k in range(TOP_K):
        pick = pick + jnp.where(slot == lpos_ref[:, k:k + 1], 1.0, 0.0)

    group_copy(i % 2, 0, 0, rows_ref[i]).wait()
    y = jnp.dot(pick.astype(_BF16), local[i % 2].astype(_BF16), preferred_element_type=_F32)
    o_ref[...] = _layer_norm(alpha * h_ref[...] + y, g2_ref[...], b2_ref[...])


def _combine(h, ys, lpos, plan, g2, b2, alpha):
    t, dm = h.shape
    tm = MOE_TM
    grid_spec = pltpu.PrefetchScalarGridSpec(
        num_scalar_prefetch=4,
        grid=(t // tm,),
        in_specs=[
            pl.BlockSpec((tm, TOP_K), lambda i, *_: (i, 0)),
            pl.BlockSpec((tm, dm), lambda i, *_: (i, 0)),
            pl.BlockSpec((1, dm), lambda i, *_: (0, 0)),
            pl.BlockSpec((1, dm), lambda i, *_: (0, 0)),
            pl.BlockSpec(memory_space=pl.ANY),
        ],
        out_specs=pl.BlockSpec((tm, dm), lambda i, *_: (i, 0)),
        scratch_shapes=[pltpu.VMEM((2, LOCAL_ROWS, dm), _F32), pltpu.SemaphoreType.DMA((2,))],
    )
    return pl.pallas_call(
        functools.partial(_combine_kernel, tm=tm, alpha=alpha),
        grid_spec=grid_spec,
        out_shape=jax.ShapeDtypeStruct((t, dm), _F32),
        compiler_params=_params("arbitrary"),
        name="moe_combine",
    )(plan["cnt"], plan["base"], plan["off"], plan["tile_rows"], lpos.T, h, g2.reshape(1, dm),
      b2.reshape(1, dm), ys)


def _round_up(x, m):
    return (x + m - 1) // m * m


def _moe_plan(tile_cnt, n_assign):
    n_tiles, n_exp = tile_cnt.shape
    cnt = tile_cnt.astype(jnp.int32)
    grp = _round_up(cnt, ROW_ALIGN)
    tot = jnp.sum(grp, axis=0)
    padded = _round_up(tot, MOE_ROWS)
    pend = jnp.cumsum(padded)
    pstart = pend - padded
    base = pstart[None, :] + jnp.cumsum(grp, axis=0) - grp
    off = jnp.cumsum(grp, axis=1) - grp

    n_rows = _round_up(n_assign + n_tiles * n_exp * (ROW_ALIGN - 1) + n_exp * (MOE_ROWS - 1), MOE_ROWS)
    n_blocks = n_rows // MOE_ROWS
    blk_start = jnp.arange(n_blocks, dtype=jnp.int32) * MOE_ROWS
    blk_e = jnp.sum((pend[None, :] <= blk_start[:, None]).astype(jnp.int32), axis=1)
    blk_e = jnp.minimum(blk_e, n_exp - 1)
    n_used = (pend[-1] // MOE_ROWS).astype(jnp.int32)
    blk_e = blk_e[jnp.minimum(jnp.arange(n_blocks, dtype=jnp.int32), n_used - 1)]
    base_b = jnp.take(base, blk_e, axis=1)
    end_b = base_b + jnp.take(grp, blk_e, axis=1)
    blk_tlo = jnp.sum((end_b <= blk_start[None, :]).astype(jnp.int32), axis=0)
    blk_thi = jnp.sum((base_b < blk_start[None, :] + MOE_ROWS).astype(jnp.int32), axis=0) - 1
    nonempty = padded > 0
    expert_ord = jnp.cumsum(nonempty.astype(jnp.int32)) - nonempty.astype(jnp.int32)
    ids = jnp.arange(n_exp, dtype=jnp.int32)
    later = (ids[None, :] > ids[:, None]) & nonempty[None, :]
    expert_next = jnp.min(jnp.where(later, ids[None, :], n_exp), axis=1)
    expert_next = jnp.where(expert_next == n_exp, -1, expert_next)
    return dict(
        cnt=cnt.reshape(-1), base=base.reshape(-1).astype(jnp.int32), off=off.reshape(-1).astype(jnp.int32),
        tile_rows=jnp.sum(grp, axis=1).astype(jnp.int32),
        blk_e=blk_e.astype(jnp.int32), blk_tlo=blk_tlo, blk_thi=blk_thi,
        blk_ord=expert_ord[blk_e].astype(jnp.int32), blk_next=expert_next[blk_e].astype(jnp.int32),
        n_used=n_used.reshape(1), n_rows=n_rows)


def kernel(x, w_in, b_in, lambda_q1, lambda_k1, lambda_q2, lambda_k2, subln_g, w_proj_a, w_proj_b, w_out, b_out, ln1_g, ln1_b, w_router, b_router, w_up, b_up, w_down, b_down, ln2_g, ln2_b):
    bsz, seq, dm = x.shape
    depth = w_in.shape[0]
    alpha = (2.0 * depth) ** 0.25
    t = bsz * seq
    for layer in range(depth):
        lambda_init = 0.8 - 0.6 * math.exp(-0.3 * layer)
        x2d = x.reshape(t, dm)
        n_a, n_b = 3 * WIDTH_A, 3 * WIDTH_B
        w_l, b_l = w_in[layer], b_in[layer]
        query_scale = jnp.full((WIDTH_A,), QUERY_SCALE, _F32)
        ones = functools.partial(jnp.ones, dtype=_F32)
        qkv_a = _in_proj_a(x, w_l[:, :n_a].astype(_BF16), b_l[:n_a],
                           jnp.concatenate([query_scale, ones((2 * WIDTH_A,))]))
        w_rest = jnp.concatenate([w_l[:, n_a + n_b:], w_l[:, n_a:n_a + n_b]], axis=1).astype(_BF16)
        b_rest = jnp.concatenate([b_l[n_a + n_b:], b_l[n_a:n_a + n_b]])
        scale_rest = jnp.concatenate([ones((2 * dm,)), query_scale, ones((2 * WIDTH_B,))])
        proj2d = _in_proj(x2d, w_rest, b_rest, scale_rest)

        dil = [_dilated_pass(a, d) for a, (_, d) in zip(qkv_a, DILATED_PATTERNS)]

        lam_vecs = jnp.stack([lambda_q1[layer], lambda_k1[layer], lambda_q2[layer], lambda_k2[layer]]).astype(_F32)
        o_b = _diff_attention(proj2d.reshape(bsz, seq, -1), 2 * dm, lam_vecs, subln_g[layer],
                              lambda_init).reshape(t, WIDTH_B)

        h, lpos, cnt, xs = _merge_router(
            x2d, proj2d, 0, [o for o, _ in dil], [l for _, l in dil], o_b,
            w_proj_a[layer].astype(_BF16), w_proj_b[layer].astype(_BF16),
            w_out[layer].astype(_BF16), b_out[layer], ln1_g[layer], ln1_b[layer],
            w_router[layer], b_router[layer], alpha)

        plan = _moe_plan(cnt[:, :, 0], t * TOP_K)
        ys = _experts(xs, plan, w_up[layer], b_up[layer], w_down[layer], b_down[layer])
        out = _combine(h, ys, lpos, plan, ln2_g[layer], ln2_b[layer], alpha)
        x = out.reshape(bsz, seq, dm)
    return x
```

```python
import functools
import math

import numpy as np
import jax
import jax.numpy as jnp
from jax import lax
from jax.experimental import pallas as pl
from jax.experimental.pallas import tpu as pltpu

HEAD_DIM = 64
N_HEADS_A = 8
DILATED_PATTERNS = ((128, 1), (512, 4), (2048, 16))
N_HEADS_B = 4
WIDTH_A = N_HEADS_A * HEAD_DIM
WIDTH_B = N_HEADS_B * 2 * HEAD_DIM
N_ALIBI_HEADS = N_HEADS_A + N_HEADS_B
ALIBI_IDX_A = (0, 1, 3, 4, 6, 7, 9, 10)
ALIBI_IDX_B = (2, 5, 8, 11)
Q_BLOCK = 128
MASK_VALUE = -1e30
N_EXPERTS = 32
TOP_K = 4
SWIGLU_ALPHA = 1.702
SWIGLU_LIMIT = 7.0
LN_EPS = 1e-5
SUBLN_EPS = 1e-5
LOG2E = math.log2(math.e)
QUERY_SCALE = HEAD_DIM ** -0.5 * LOG2E

LANES = 128
V7X_VMEM_LIMIT_BYTES = 56 * 1024 * 1024

PROJ_TM = 1024
PROJ_A_TM = 512
DIL_ROWS_MAX = 4 * Q_BLOCK
DIL_HALO = 64
DIFF_TQ = 512
DIFF_TK = 512
MOE_ROWS = 512
FFN_CHUNK = 512
MERGE_TM = 512
MOE_TM = 256
ROW_ALIGN = 8
LOCAL_ROWS = -(-(TOP_K * MOE_TM + N_EXPERTS * (ROW_ALIGN - 1)) // LANES) * LANES

_F32 = jnp.float32
_BF16 = jnp.bfloat16


def _params(*sem):
    return pltpu.CompilerParams(dimension_semantics=sem, vmem_limit_bytes=V7X_VMEM_LIMIT_BYTES)


def _alibi_slopes():
    return (2.0 ** (-8.0 * np.arange(1, N_ALIBI_HEADS + 1) / N_ALIBI_HEADS)).astype(np.float32)


def _in_proj_kernel(x_ref, w_ref, b_ref, cs_ref, o_ref):
    x = x_ref[...].astype(_BF16)
    acc = jnp.dot(x, w_ref[...], preferred_element_type=_F32)
    o_ref[...] = ((acc + b_ref[...]) * cs_ref[...]).astype(o_ref.dtype)


def _in_proj(x2d, w_bf16, b, colscale):
    t, dm = x2d.shape
    n = w_bf16.shape[1]
    tn = n // 2
    assert n % 2 == 0 and tn % LANES == 0
    return pl.pallas_call(
        _in_proj_kernel,
        grid=(t // PROJ_TM, n // tn),
        in_specs=[
            pl.BlockSpec((PROJ_TM, dm), lambda i, j: (i, 0)),
            pl.BlockSpec((dm, tn), lambda i, j: (0, j)),
            pl.BlockSpec((1, tn), lambda i, j: (0, j)),
            pl.BlockSpec((1, tn), lambda i, j: (0, j)),
        ],
        out_specs=pl.BlockSpec((PROJ_TM, tn), lambda i, j: (i, j)),
        out_shape=jax.ShapeDtypeStruct((t, n), _BF16),
        compiler_params=_params("parallel", "arbitrary"),
        name="in_proj",
    )(x2d, w_bf16, b.reshape(1, n), colscale.reshape(1, n))


def _in_proj_a_kernel(x_ref, w_ref, b_ref, cs_ref, *refs):
    out_refs, acc_ref, stage_ref = refs[:-2], refs[-2], refs[-1]
    x = x_ref[0].astype(_BF16)
    acc = (jnp.dot(x, w_ref[...], preferred_element_type=_F32) + b_ref[...]) * cs_ref[...]
    n_chunks, tm, _ = acc_ref.shape
    for c in range(n_chunks):
        acc_ref[c] = acc[:, c * LANES:(c + 1) * LANES]
    for o_ref, (_, d) in zip(out_refs, DILATED_PATTERNS):
        if d == 1:
            o_ref[0, 0] = acc.astype(o_ref.dtype)
            continue
        if d == 4:
            for r in range(d):
                for c in range(n_chunks):
                    rows = acc_ref[c, pl.ds(r, tm // d, stride=d), :]
                    stage_ref[c, r] = rows
                    o_ref[0, r, :, c * LANES:(c + 1) * LANES] = rows.astype(o_ref.dtype)
            continue
        assert d == 16
        for r1 in range(4):
            for r2 in range(4):
                for c in range(n_chunks):
                    rows = stage_ref[c, r1, pl.ds(r2, tm // d, stride=4), :]
                    o_ref[0, r1 + 4 * r2, :, c * LANES:(c + 1) * LANES] = rows.astype(o_ref.dtype)


def _in_proj_a(x, w_bf16, b, colscale):
    bsz, seq, dm = x.shape
    n = w_bf16.shape[1]
    tm = PROJ_A_TM
    per_b = seq // tm
    dils = [d for _, d in DILATED_PATTERNS]
    assert seq % tm == 0 and all(tm % d == 0 and (tm // d) % 16 == 0 for d in dils) and dils == [1, 4, 16]
    return pl.pallas_call(
        _in_proj_a_kernel,
        grid=(bsz * per_b,),
        in_specs=[
            pl.BlockSpec((1, tm, dm), lambda i: (i // per_b, i % per_b, 0)),
            pl.BlockSpec((dm, n), lambda i: (0, 0)),
            pl.BlockSpec((1, n), lambda i: (0, 0)),
            pl.BlockSpec((1, n), lambda i: (0, 0)),
        ],
        out_specs=[pl.BlockSpec((1, d, tm // d, n), lambda i: (i // per_b, 0, i % per_b, 0)) for d in dils],
        out_shape=[jax.ShapeDtypeStruct((bsz, d, seq // d, n), _BF16) for d in dils],
        scratch_shapes=[pltpu.VMEM((n // LANES, tm, LANES), _F32),
                        pltpu.VMEM((n // LANES, 4, tm // 4, LANES), _F32)],
        compiler_params=_params("parallel"),
        name="in_proj_a",
    )(x, w_bf16, b.reshape(1, n), colscale.reshape(1, n))


def _dilated_bias_table(dilation):
    slopes = _alibi_slopes()[list(ALIBI_IDX_A)]
    band = Q_BLOCK + 2 * DIL_HALO
    qi = np.arange(Q_BLOCK)[:, None]
    kj = np.arange(band)[None, :]
    rel = qi - kj + DIL_HALO
    in_band = np.abs(rel) <= DIL_HALO
    base = -slopes[:, None, None] * (dilation * np.abs(rel)).astype(np.float32)[None]
    base = (base.astype(np.float64) * LOG2E).astype(np.float32)
    edge = (np.ones_like(kj, bool), kj >= DIL_HALO, kj < band - DIL_HALO)
    out = np.stack([np.where(in_band & e, base, np.float32(MASK_VALUE)) for e in edge])
    return out.astype(np.float32)


def _dilated_kernel(q_ref, km_ref, kp_ref, kn_ref, vm_ref, vp_ref, vn_ref, bias_ref, o_ref, lse_ref,
                    kbuf, vbuf, *, nblk, n_rows):
    i = pl.program_id(2)
    h0, h1 = DIL_HALO, DIL_HALO + n_rows
    kbuf[0:h0, :] = kp_ref[0, 0]
    kbuf[h0:h1, :] = km_ref[0, 0]
    kbuf[h1:h1 + DIL_HALO, :] = kn_ref[0, 0]
    vbuf[0:h0, :] = vp_ref[0, 0]
    vbuf[h0:h1, :] = vm_ref[0, 0]
    vbuf[h1:h1 + DIL_HALO, :] = vn_ref[0, 0]

    lane = lax.broadcasted_iota(jnp.int32, (Q_BLOCK, LANES), 1)
    low_half = lane < HEAD_DIM
    band = Q_BLOCK + 2 * DIL_HALO
    n_sub = n_rows // Q_BLOCK
    variants = [0] * n_sub
    variants[0] = jnp.where(i == 0, 1, 0)
    variants[-1] = jnp.where(i == nblk - 1, 2, 0)

    for j in range(n_sub):
        rows = slice(j * Q_BLOCK, (j + 1) * Q_BLOCK)
        krows = slice(j * Q_BLOCK, j * Q_BLOCK + band)
        lse_cols = []
        for hp in range(N_HEADS_A // 2):
            cols = slice(hp * LANES, (hp + 1) * LANES)
            q_pair = q_ref[0, 0, rows, cols]
            k_pair = kbuf[krows, cols]
            v_pair = vbuf[krows, cols]
            halves = []
            for hh in range(2):
                head = 2 * hp + hh
                keep = low_half if hh == 0 else jnp.logical_not(low_half)
                qm = jnp.where(keep, q_pair, jnp.zeros_like(q_pair))
                s = lax.dot_general(qm, k_pair, (((1,), (1,)), ((), ())), preferred_element_type=_F32)
                s = s + bias_ref[variants[j], head]
                m = jnp.max(s, axis=-1, keepdims=True)
                p = jnp.exp2(s - m)
                z = jnp.sum(p, axis=-1, keepdims=True)
                halves.append(jnp.dot(p.astype(_BF16), v_pair, preferred_element_type=_F32) * (1.0 / z))
                lse_cols.append(m + jnp.log2(z))
            o_ref[0, 0, rows, cols] = jnp.where(low_half, halves[0], halves[1])
        tile = jnp.zeros((Q_BLOCK, LANES), _F32)
        for head in range(N_HEADS_A):
            tile = jnp.where(lane == head, lse_cols[head], tile)
        lse_ref[0, 0, rows, :] = tile


def _dilated_pass(qkv, dilation):
    bsz, d, L, _ = qkv.shape
    n_rows = min(DIL_ROWS_MAX, L)
    assert d == dilation and L % n_rows == 0 and n_rows % DIL_HALO == 0 and n_rows >= 2 * Q_BLOCK
    nblk = L // n_rows
    per = n_rows // DIL_HALO
    n_halo = L // DIL_HALO
    bias = jnp.asarray(_dilated_bias_table(dilation))

    def main(s):
        return pl.BlockSpec((1, 1, n_rows, WIDTH_A), lambda b, r, i: (b, r, i, s))

    def left(s):
        return pl.BlockSpec((1, 1, DIL_HALO, WIDTH_A), lambda b, r, i: (b, r, jnp.maximum(per * i - 1, 0), s))

    def right(s):
        return pl.BlockSpec((1, 1, DIL_HALO, WIDTH_A),
                            lambda b, r, i: (b, r, jnp.minimum(per * i + per, n_halo - 1), s))

    rows_buf = n_rows + 2 * DIL_HALO
    return pl.pallas_call(
        functools.partial(_dilated_kernel, nblk=nblk, n_rows=n_rows),
        grid=(bsz, dilation, nblk),
        in_specs=[main(0), main(1), left(1), right(1), main(2), left(2), right(2),
                  pl.BlockSpec(bias.shape, lambda b, r, i: (0, 0, 0, 0))],
        out_specs=[pl.BlockSpec((1, 1, n_rows, WIDTH_A), lambda b, r, i: (b, r, i, 0)),
                   pl.BlockSpec((1, 1, n_rows, LANES), lambda b, r, i: (b, r, i, 0))],
        out_shape=[jax.ShapeDtypeStruct((bsz, dilation, L, WIDTH_A), _F32),
                   jax.ShapeDtypeStruct((bsz, dilation, L, LANES), _F32)],
        scratch_shapes=[pltpu.VMEM((rows_buf, WIDTH_A), _BF16), pltpu.VMEM((rows_buf, WIDTH_A), _BF16)],
        compiler_params=_params("parallel", "parallel", "arbitrary"),
        name=f"dilated_d{dilation}",
    )(qkv, qkv, qkv, qkv, qkv, qkv, qkv, bias)


def _split_bf16(x):
    hi = x.astype(jnp.bfloat16)
    lo = (x - hi.astype(np.float64)).astype(jnp.bfloat16)
    return hi, lo


def _diff_tables():
    s2 = _alibi_slopes()[list(ALIBI_IDX_B)].astype(np.float64) * LOG2E
    qaug = np.zeros((N_HEADS_B, DIFF_TQ, LANES), jnp.bfloat16)
    kaug = np.zeros((N_HEADS_B, 2, DIFF_TK, LANES), jnp.bfloat16)
    i = np.arange(DIFF_TQ, dtype=np.float64)
    j = np.arange(DIFF_TK, dtype=np.float64)
    for h in range(N_HEADS_B):
        r_hi, r_lo = _split_bf16(-s2[h] * i)
        qaug[h, :, 0] = 1.0
        qaug[h, :, 1] = 1.0
        qaug[h, :, 2] = r_hi
        qaug[h, :, 3] = r_lo
        for side, sign in enumerate((1.0, -1.0)):
            c_hi, c_lo = _split_bf16(sign * s2[h] * j)
            kaug[h, side, :, 0] = c_hi
            kaug[h, side, :, 1] = c_lo
            kaug[h, side, :, 2] = sign
            kaug[h, side, :, 3] = sign
    halves = DIFF_TK // DIFF_TQ
    off = (np.arange(halves) * DIFF_TQ)[:, None, None]
    dist = np.abs(off + i[None, :, None] - j[None, None, :])
    diag = (-s2[:, None, None, None] * dist[None]).astype(np.float32)
    return jnp.asarray(qaug), jnp.asarray(kaug), jnp.asarray(diag), jnp.asarray(s2.astype(np.float32))


def _diff_kernel(slope_ref, lam_ref, q_ref, k_ref, v_ref, qaug_ref, kaug_ref, diag_ref, g_ref, o_ref,
                 *, seq, lambda_init):
    head = pl.program_id(1)
    qb = pl.program_id(2)
    q0 = qb * DIFF_TQ
    halves = DIFF_TK // DIFF_TQ
    kd = qb // halves
    n_kb = seq // DIFF_TK
    slope2 = slope_ref[head]
    nt = (((1,), (1,)), ((), ()))

    q = q_ref[0]
    lane = lax.broadcasted_iota(jnp.int32, q.shape, 1)
    zero = jnp.zeros_like(q)
    qaug = qaug_ref[0]
    q_maps = (jnp.concatenate([jnp.where(lane < HEAD_DIM, q, zero), qaug], axis=1),
              jnp.concatenate([jnp.where(lane >= HEAD_DIM, q, zero), qaug], axis=1))
    zero_aug = jnp.zeros((DIFF_TK, LANES), _BF16)
    vlane = lax.broadcasted_iota(jnp.int32, (DIFF_TK, LANES), 1)
    ones_col = jnp.where(vlane == 0, 1.0, 0.0).astype(_BF16)

    def load(kb):
        k0 = pl.multiple_of(kb * DIFF_TK, DIFF_TK)
        return k_ref[0, pl.ds(k0, DIFF_TK), :], jnp.concatenate([v_ref[0, pl.ds(k0, DIFF_TK), :], ones_col], axis=1)

    def split(pv):
        return pv[:, LANES:LANES + 1], pv[:, 0:LANES]

    def update(s, const, vblk, carry):
        m_old, l_old, acc = carry
        m_new = jnp.maximum(m_old, jnp.max(s, axis=-1, keepdims=True) + const)
        alpha = jnp.exp2(m_old - m_new)
        p = jnp.exp2(s - (m_new - const))
        l_blk, o_blk = split(jnp.dot(p.astype(_BF16), vblk, preferred_element_type=_F32))
        return m_new, alpha * l_old + l_blk, alpha * acc + o_blk

    kblk, vblk = load(kd)
    k_all = jnp.concatenate([kblk, zero_aug], axis=1)
    bias = diag_ref[0, qb % halves]
    carries = []
    for mp in range(2):
        s = lax.dot_general(q_maps[mp], k_all, nt, preferred_element_type=_F32) + bias
        m = jnp.max(s, axis=-1, keepdims=True)
        p = jnp.exp2(s - m)
        carries.append((m,) + split(jnp.dot(p.astype(_BF16), vblk, preferred_element_type=_F32)))

    for t in range(n_kb - 1):
        kb = jnp.where(t >= kd, t + 1, t)
        side = jnp.where(t >= kd, 1, 0)
        kblk, vblk = load(kb)
        k_all = jnp.concatenate([kblk, kaug_ref[0, side]], axis=1)
        const = -slope2 * jnp.abs(q0 - kb * DIFF_TK).astype(_F32)
        for mp in range(2):
            s = lax.dot_general(q_maps[mp], k_all, nt, preferred_element_type=_F32)
            carries[mp] = update(s, const, vblk, carries[mp])

    lv = lam_ref[...]
    lam = (jnp.exp(jnp.sum(lv[0:1] * lv[1:2], axis=-1, keepdims=True))
           - jnp.exp(jnp.sum(lv[2:3] * lv[3:4], axis=-1, keepdims=True)) + lambda_init)
    (_, l1, a1), (_, l2, a2) = carries
    of = a1 * (1.0 / l1) - lam * (a2 * (1.0 / l2))
    of = of * lax.rsqrt(jnp.mean(of * of, axis=-1, keepdims=True) + SUBLN_EPS)
    of = of * g_ref[...] * (1.0 - lambda_init)
    o_ref[0] = of.astype(o_ref.dtype)


def _diff_attention(proj, q_offset, lam_vecs, subln_g, lambda_init):
    bsz, seq, _ = proj.shape
    assert seq % DIFF_TQ == 0 and seq % DIFF_TK == 0 and q_offset % LANES == 0
    qcol = q_offset // LANES
    kcol = qcol + WIDTH_B // LANES
    vcol = kcol + WIDTH_B // LANES
    assert DIFF_TK % DIFF_TQ == 0
    halves = DIFF_TK // DIFF_TQ
    qaug, kaug, diag, slopes2 = _diff_tables()
    return pl.pallas_call(
        functools.partial(_diff_kernel, seq=seq, lambda_init=lambda_init),
        grid=(bsz, N_HEADS_B, seq // DIFF_TQ),
        in_specs=[
            pl.BlockSpec(memory_space=pltpu.SMEM),
            pl.BlockSpec((4, HEAD_DIM), lambda b, h, i: (0, 0)),
            pl.BlockSpec((1, DIFF_TQ, LANES), lambda b, h, i: (b, i, qcol + h)),
            pl.BlockSpec((1, seq, LANES), lambda b, h, i: (b, 0, kcol + h)),
            pl.BlockSpec((1, seq, LANES), lambda b, h, i: (b, 0, vcol + h)),
            pl.BlockSpec((1, DIFF_TQ, LANES), lambda b, h, i: (h, 0, 0)),
            pl.BlockSpec((1, 2, DIFF_TK, LANES), lambda b, h, i: (h, 0, 0, 0)),
            pl.BlockSpec((1, halves, DIFF_TQ, DIFF_TK), lambda b, h, i: (h, 0, 0, 0)),
            pl.BlockSpec((1, LANES), lambda b, h, i: (0, 0)),
        ],
        out_specs=pl.BlockSpec((1, DIFF_TQ, LANES), lambda b, h, i: (b, i, h)),
        out_shape=jax.ShapeDtypeStruct((bsz, seq, WIDTH_B), _BF16),
        compiler_params=_params("parallel", "parallel", "arbitrary"),
        name="diff_attn",
    )(slopes2, lam_vecs, proj, proj, proj, qaug, kaug, diag, subln_g.reshape(1, LANES).astype(_F32))


def _layer_norm(z, g, b):
    mu = jnp.mean(z, axis=-1, keepdims=True)
    zc = z - mu
    var = jnp.mean(zc * zc, axis=-1, keepdims=True)
    return zc * lax.rsqrt(var + LN_EPS) * g + b


def _mix_patterns(o_refs, lse_refs, expand_ref, o_nat, lse_nat):
    n_chunks, tm = o_nat.shape[1], o_nat.shape[2]
    outs, lses = [], []
    for p, (_, d) in enumerate(DILATED_PATTERNS):
        if d == 1:
            outs.append(o_refs[p][0, 0])
            lses.append(lse_refs[p][0, 0])
            continue
        for r in range(d):
            for c in range(n_chunks):
                o_nat[p, c, pl.ds(r, tm // d, stride=d), :] = o_refs[p][0, r, :, c * LANES:(c + 1) * LANES]
            lse_nat[p, pl.ds(r, tm // d, stride=d), :] = lse_refs[p][0, r]
        outs.append(jnp.concatenate([o_nat[p, c] for c in range(n_chunks)], axis=1))
        lses.append(lse_nat[p])
    top = functools.reduce(jnp.maximum, lses)
    es = [jnp.exp2(l - top) for l in lses]
    inv = 1.0 / functools.reduce(lambda a, b: a + b, es)
    mixed = None
    for o, e in zip(outs, es):
        w = e * inv
        w_hi = w.astype(_BF16)
        w_lo = (w - w_hi.astype(_F32)).astype(_BF16)
        w_full = (jnp.dot(w_hi, expand_ref[...], preferred_element_type=_F32)
                  + jnp.dot(w_lo, expand_ref[...], preferred_element_type=_F32))
        mixed = w_full * o if mixed is None else mixed + w_full * o
    return mixed


def _merge_kernel(x_ref, ga_ref, gb_ref, o1_ref, o2_ref, o3_ref, l1_ref, l2_ref, l3_ref, expand_ref, ob_ref,
                  wpa_ref, wpb_ref, wout_ref, bout_ref, g1_ref, b1_ref, wrh_ref, wrl_ref, br_ref,
                  h_ref, lpos_ref, cnt_ref, xs_ref, o_nat, lse_nat, *, alpha):
    o_a = _mix_patterns((o1_ref, o2_ref, o3_ref), (l1_ref, l2_ref, l3_ref), expand_ref, o_nat, lse_nat)
    pa = jnp.dot(o_a.astype(_BF16), wpa_ref[...], preferred_element_type=_F32)
    pb = jnp.dot(ob_ref[...], wpb_ref[...], preferred_element_type=_F32)
    merged = jax.nn.sigmoid(ga_ref[...].astype(_F32)) * pa + jax.nn.sigmoid(gb_ref[...].astype(_F32)) * pb
    y = jnp.dot(merged.astype(_BF16), wout_ref[...], preferred_element_type=_F32) + bout_ref[...]
    h = _layer_norm(alpha * x_ref[...] + y, g1_ref[...], b1_ref[...])
    h_ref[...] = h

    h_hi = h.astype(_BF16)
    h_lo = (h - h_hi.astype(_F32)).astype(_BF16)
    nt = (((1,), (1,)), ((), ()))
    logits = (lax.dot_general(wrh_ref[...], h_hi, nt, preferred_element_type=_F32)
              + lax.dot_general(wrl_ref[...], h_hi, nt, preferred_element_type=_F32)
              + lax.dot_general(wrh_ref[...], h_lo, nt, preferred_element_type=_F32)
              + br_ref[...])

    tm = logits.shape[1]
    expert = lax.broadcasted_iota(jnp.int32, logits.shape, 0)
    work = logits
    vals, sels = [], []
    for _ in range(TOP_K):
        mx = jnp.max(work, axis=0, keepdims=True)
        idx = jnp.min(jnp.where(work == mx, expert, N_EXPERTS), axis=0, keepdims=True)
        sel = expert == idx
        work = jnp.where(sel, -jnp.inf, work)
        vals.append(mx)
        sels.append(sel)
    ex = [jnp.exp(v - vals[0]) for v in vals]
    inv = 1.0 / (ex[0] + ex[1] + ex[2] + ex[3])
    gates = [e * inv for e in ex]

    chosen = sels[0] | sels[1] | sels[2] | sels[3]
    onehot = jnp.where(chosen, 1.0, 0.0).astype(_BF16)
    earlier = lax.broadcasted_iota(jnp.int32, (tm, tm), 0)
    later = lax.broadcasted_iota(jnp.int32, (tm, tm), 1)
    sub_shift = int(math.log2(MOE_TM))
    before = (earlier < later) & (lax.shift_right_logical(earlier, sub_shift)
                                  == lax.shift_right_logical(later, sub_shift))
    upper = jnp.where(before, 1.0, 0.0).astype(_BF16)
    in_tile = jnp.dot(onehot, upper, preferred_element_type=_F32)
    lower = (lax.broadcasted_iota(jnp.int32, (N_EXPERTS, N_EXPERTS), 1)
             < lax.broadcasted_iota(jnp.int32, (N_EXPERTS, N_EXPERTS), 0))
    lower = jnp.where(lower, 1.0, 0.0).astype(_BF16)
    rows = xs_ref.shape[1]
    dm = h.shape[1]
    slot = lax.broadcasted_iota(jnp.int32, (rows, MOE_TM), 0)
    for sub in range(tm // MOE_TM):
        cols = slice(sub * MOE_TM, (sub + 1) * MOE_TM)
        count = jnp.sum(onehot[:, cols].astype(_F32), axis=1, keepdims=True)
        cnt_ref[sub] = jnp.broadcast_to(count, cnt_ref.shape[1:])
        group = jnp.floor((count + (ROW_ALIGN - 1)) * (1.0 / ROW_ALIGN)) * ROW_ALIGN
        first_row = jnp.dot(lower, jnp.broadcast_to(group, (N_EXPERTS, LANES)).astype(_BF16),
                            preferred_element_type=_F32)[:, 0:1]
        place = in_tile[:, cols] + first_row
        perm = jnp.zeros((rows, MOE_TM), _F32)
        gsel = jnp.zeros((rows, MOE_TM), _F32)
        for k in range(TOP_K):
            pos = jnp.sum(jnp.where(sels[k][:, cols], place, 0.0), axis=0, keepdims=True).astype(jnp.int32)
            lpos_ref[k:k + 1, cols] = pos
            hit = slot == pos
            perm = perm + jnp.where(hit, 1.0, 0.0)
            gsel = gsel + jnp.where(hit, gates[k][:, cols], 0.0)
        xs_ref[sub, :, 0:dm] = jnp.dot(perm.astype(_BF16), h_hi[cols, :], preferred_element_type=_F32)
        xs_ref[sub, :, dm:dm + LANES] = jnp.broadcast_to(jnp.sum(gsel, axis=1, keepdims=True), (rows, LANES))


def _merge_router(x2d, proj2d, gate_offset, dil_outs, dil_lses, o_b, wpa, wpb, wout, bout, g1, b1,
                  w_router, b_router, alpha):
    t, dm = x2d.shape
    tm = MERGE_TM
    bsz, _, seq_over_d0, _ = dil_outs[0].shape
    seq = seq_over_d0 * DILATED_PATTERNS[0][1]
    per_b = seq // tm
    dils = [d for _, d in DILATED_PATTERNS]
    assert gate_offset % dm == 0 and seq % tm == 0 and all(tm % d == 0 for d in dils)
    gcol = gate_offset // dm
    wr_t = w_router.T.astype(_F32)
    wr_hi = wr_t.astype(_BF16)
    wr_lo = (wr_t - wr_hi.astype(_F32)).astype(_BF16)
    expand = np.zeros((LANES, WIDTH_A), np.float32)
    for head in range(N_HEADS_A):
        expand[head, head * HEAD_DIM:(head + 1) * HEAD_DIM] = 1.0
    expand = jnp.asarray(expand, _BF16)

    def const(shape):
        return pl.BlockSpec(shape, lambda i: tuple(0 for _ in shape))

    def residue_major(d, width):
        return pl.BlockSpec((1, d, tm // d, width), lambda i: (i // per_b, 0, i % per_b, 0))

    return pl.pallas_call(
        functools.partial(_merge_kernel, alpha=alpha),
        grid=(t // tm,),
        in_specs=[
            pl.BlockSpec((tm, dm), lambda i: (i, 0)),
            pl.BlockSpec((tm, dm), lambda i: (i, gcol)),
            pl.BlockSpec((tm, dm), lambda i: (i, gcol + 1)),
            *[residue_major(d, WIDTH_A) for d in dils],
            *[residue_major(d, LANES) for d in dils],
            const((LANES, WIDTH_A)),
            pl.BlockSpec((tm, WIDTH_B), lambda i: (i, 0)),
            const((WIDTH_A, dm)), const((WIDTH_B, dm)), const((dm, dm)), const((1, dm)),
            const((1, dm)), const((1, dm)),
            const((N_EXPERTS, dm)), const((N_EXPERTS, dm)), const((N_EXPERTS, 1)),
        ],
        out_specs=[
            pl.BlockSpec((tm, dm), lambda i: (i, 0)),
            pl.BlockSpec((TOP_K, tm), lambda i: (0, i)),
            pl.BlockSpec((tm // MOE_TM, N_EXPERTS, LANES), lambda i: (i, 0, 0)),
            pl.BlockSpec((tm // MOE_TM, LOCAL_ROWS, dm + LANES), lambda i: (i, 0, 0)),
        ],
        out_shape=[
            jax.ShapeDtypeStruct((t, dm), _F32),
            jax.ShapeDtypeStruct((TOP_K, t), jnp.int32),
            jax.ShapeDtypeStruct((t // MOE_TM, N_EXPERTS, LANES), _F32),
            jax.ShapeDtypeStruct((t // MOE_TM, LOCAL_ROWS, dm + LANES), _F32),
        ],
        scratch_shapes=[pltpu.VMEM((len(dils), WIDTH_A // LANES, tm, LANES), _F32),
                        pltpu.VMEM((len(dils), tm, LANES), _F32)],
        compiler_params=_params("parallel"),
        name="merge_router",
    )(x2d, proj2d, proj2d, *dil_outs, *dil_lses, expand, o_b, wpa, wpb, wout, bout.reshape(1, dm),
      g1.reshape(1, dm), b1.reshape(1, dm), wr_hi, wr_lo, b_router.reshape(N_EXPERTS, 1).astype(_F32))


def _group_rows(n):
    return lax.shift_right_logical(n + (ROW_ALIGN - 1), int(math.log2(ROW_ALIGN))) * ROW_ALIGN


def _expert_kernel(blk_e_ref, n_used_ref, tlo_ref, thi_ref, cnt_ref, base_ref, off_ref, ord_ref, next_ref,
                   xs_hbm, wup_hbm, bup_ref, wdn_hbm, bdn_ref, ys_ref,
                   xbuf, sems, pending, wup_f32, wdn_f32, wsems, wup_bf, wdn_bf, *, d_expert):
    i = pl.program_id(0)
    dm = ys_ref.shape[1]
    active = i < n_used_ref[0]

    def seg_copy(buf, src_row, dst_row, n_rows):
        n_rows = pl.multiple_of(n_rows, ROW_ALIGN)
        return pltpu.make_async_copy(xs_hbm.at[pl.ds(pl.multiple_of(src_row, ROW_ALIGN), n_rows)],
                                     xbuf.at[buf, pl.ds(pl.multiple_of(dst_row, ROW_ALIGN), n_rows)],
                                     sems.at[buf])

    def fetch(blk):
        e = blk_e_ref[blk]
        first = blk * MOE_ROWS

        def per_tile(t, total):
            j = t * N_EXPERTS + e
            g0 = base_ref[j]
            lo = jnp.maximum(g0, first)
            hi = jnp.minimum(g0 + _group_rows(cnt_ref[j]), first + MOE_ROWS)
            n_rows = jnp.maximum(hi - lo, 0)

            @pl.when(n_rows > 0)
            def _():
                seg_copy(blk % 2, t * LOCAL_ROWS + off_ref[j] + (lo - g0), lo - first, n_rows).start()

            return total + n_rows

        pending[blk % 2] = lax.fori_loop(tlo_ref[blk], thi_ref[blk] + 1, per_tile, 0)

    @pl.when(i == 0)
    def _():
        xbuf[...] = jnp.zeros_like(xbuf)
        fetch(i)

    @pl.when(i + 1 < n_used_ref[0])
    def _():
        fetch(i + 1)

    def weight_copies(expert, slot):
        return (pltpu.make_async_copy(wup_hbm.at[expert], wup_f32.at[slot], wsems.at[slot]),
                pltpu.make_async_copy(wdn_hbm.at[expert], wdn_f32.at[slot], wsems.at[slot]))

    @pl.when(i == 0)
    def _():
        for cp in weight_copies(blk_e_ref[0], 0):
            cp.start()

    @pl.when(active & ((i == 0) | (blk_e_ref[i] != blk_e_ref[jnp.maximum(i - 1, 0)])))
    def _():
        slot = ord_ref[i] % 2
        for cp in weight_copies(blk_e_ref[i], slot):
            cp.wait()
        wup_bf[...] = wup_f32[slot].astype(_BF16)
        wdn_bf[...] = wdn_f32[slot].astype(_BF16)

        @pl.when(next_ref[i] >= 0)
        def _():
            for cp in weight_copies(next_ref[i], 1 - slot):
                cp.start()

    @pl.when(active)
    def _():
        @pl.when(pending[i % 2] > 0)
        def _():
            seg_copy(i % 2, 0, 0, pending[i % 2]).wait()

        x = xbuf[i % 2, :, 0:dm].astype(_BF16)
        row_gate = xbuf[i % 2, :, dm:dm + 1]
        acc = jnp.zeros(ys_ref.shape, _F32)
        for c in range(d_expert // FFN_CHUNK):
            lo, hi = c * FFN_CHUNK, (c + 1) * FFN_CHUNK
            g = jnp.dot(x, wup_bf[:, lo:hi], preferred_element_type=_F32) + bup_ref[0, :, lo:hi]
            u = (jnp.dot(x, wup_bf[:, d_expert + lo:d_expert + hi], preferred_element_type=_F32)
                 + bup_ref[0, :, d_expert + lo:d_expert + hi])
            gate = jnp.minimum(g, SWIGLU_LIMIT)
            up = jnp.clip(u, -SWIGLU_LIMIT, SWIGLU_LIMIT)
            act = gate * jax.nn.sigmoid(SWIGLU_ALPHA * gate) * (up + 1.0)
            acc = acc + jnp.dot(act.astype(_BF16), wdn_bf[lo:hi, :], preferred_element_type=_F32)
        ys_ref[...] = ((acc + bdn_ref[0]) * row_gate).astype(_BF16).astype(_F32)

    @pl.when(jnp.logical_not(active))
    def _():
        ys_ref[...] = jnp.zeros_like(ys_ref)


def _experts(xs, plan, w_up, b_up, w_down, b_down):
    n_tiles, local_rows, width = xs.shape
    dm = width - LANES
    n_blocks = plan["n_rows"] // MOE_ROWS
    n_exp, _, two_de = w_up.shape
    d_expert = two_de // 2
    grid_spec = pltpu.PrefetchScalarGridSpec(
        num_scalar_prefetch=9,
        grid=(n_blocks,),
        in_specs=[
            pl.BlockSpec(memory_space=pl.ANY),
            pl.BlockSpec(memory_space=pl.ANY),
            pl.BlockSpec((1, 1, two_de), lambda i, be, *_: (be[i], 0, 0)),
            pl.BlockSpec(memory_space=pl.ANY),
            pl.BlockSpec((1, 1, dm), lambda i, be, *_: (be[i], 0, 0)),
        ],
        out_specs=pl.BlockSpec((MOE_ROWS, dm), lambda i, *_: (i, 0)),
        scratch_shapes=[pltpu.VMEM((2, MOE_ROWS, width), _F32), pltpu.SemaphoreType.DMA((2,)),
                        pltpu.SMEM((2,), jnp.int32),
                        pltpu.VMEM((2, dm, two_de), _F32), pltpu.VMEM((2, d_expert, dm), _F32),
                        pltpu.SemaphoreType.DMA((2,)),
                        pltpu.VMEM((dm, two_de), _BF16), pltpu.VMEM((d_expert, dm), _BF16)],
    )
    return pl.pallas_call(
        functools.partial(_expert_kernel, d_expert=d_expert),
        grid_spec=grid_spec,
        out_shape=jax.ShapeDtypeStruct((plan["n_rows"], dm), _F32),
        compiler_params=_params("arbitrary"),
        name="moe_experts",
    )(plan["blk_e"], plan["n_used"], plan["blk_tlo"], plan["blk_thi"], plan["cnt"], plan["base"], plan["off"],
      plan["blk_ord"], plan["blk_next"], xs.reshape(n_tiles * local_rows, width), w_up, b_up.reshape(n_exp, 1, two_de), w_down,
      b_down.reshape(n_exp, 1, dm))


def _combine_kernel(cnt_ref, base_ref, off_ref, rows_ref, lpos_ref, h_ref, g2_ref, b2_ref, ys_hbm, o_ref,
                    local, sems, *, tm, alpha):
    i = pl.program_id(0)
    n_tiles = pl.num_programs(0)
    rows = local.shape[1]

    def group_copy(buf, src_row, dst_row, n_rows):
        n_rows = pl.multiple_of(n_rows, ROW_ALIGN)
        return pltpu.make_async_copy(ys_hbm.at[pl.ds(pl.multiple_of(src_row, ROW_ALIGN), n_rows)],
                                     local.at[buf, pl.ds(pl.multiple_of(dst_row, ROW_ALIGN), n_rows)],
                                     sems.at[buf])

    def fetch(tile):
        for e in range(N_EXPERTS):
            j = tile * N_EXPERTS + e
            n_rows = _group_rows(cnt_ref[j])

            @pl.when(n_rows > 0)
            def _():
                group_copy(tile % 2, base_ref[j], off_ref[j], n_rows).start()

    @pl.when(i == 0)
    def _():
        local[...] = jnp.zeros_like(local)
        fetch(i)

    @pl.when(i + 1 < n_tiles)
    def _():
        fetch(i + 1)

    slot = lax.broadcasted_iota(jnp.int32, (tm, rows), 1)
    pick = jnp.zeros((tm, rows), _F32)
    for k in range(TOP_K):
        pick = pick + jnp.where(slot == lpos_ref[:, k:k + 1], 1.0, 0.0)

    group_copy(i % 2, 0, 0, rows_ref[i]).wait()
    y = jnp.dot(pick.astype(_BF16), local[i % 2].astype(_BF16), preferred_element_type=_F32)
    o_ref[...] = _layer_norm(alpha * h_ref[...] + y, g2_ref[...], b2_ref[...])


def _combine(h, ys, lpos, plan, g2, b2, alpha):
    t, dm = h.shape
    tm = MOE_TM
    grid_spec = pltpu.PrefetchScalarGridSpec(
        num_scalar_prefetch=4,
        grid=(t // tm,),
        in_specs=[
            pl.BlockSpec((tm, TOP_K), lambda i, *_: (i, 0)),
            pl.BlockSpec((tm, dm), lambda i, *_: (i, 0)),
            pl.BlockSpec((1, dm), lambda i, *_: (0, 0)),
            pl.BlockSpec((1, dm), lambda i, *_: (0, 0)),
            pl.BlockSpec(memory_space=pl.ANY),
        ],
        out_specs=pl.BlockSpec((tm, dm), lambda i, *_: (i, 0)),
        scratch_shapes=[pltpu.VMEM((2, LOCAL_ROWS, dm), _F32), pltpu.SemaphoreType.DMA((2,))],
    )
    return pl.pallas_call(
        functools.partial(_combine_kernel, tm=tm, alpha=alpha),
        grid_spec=grid_spec,
        out_shape=jax.ShapeDtypeStruct((t, dm), _F32),
        compiler_params=_params("arbitrary"),
        name="moe_combine",
    )(plan["cnt"], plan["base"], plan["off"], plan["tile_rows"], lpos.T, h, g2.reshape(1, dm),
      b2.reshape(1, dm), ys)


def _round_up(x, m):
    return (x + m - 1) // m * m


def _moe_plan(tile_cnt, n_assign):
    n_tiles, n_exp = tile_cnt.shape
    cnt = tile_cnt.astype(jnp.int32)
    grp = _round_up(cnt, ROW_ALIGN)
    tot = jnp.sum(grp, axis=0)
    padded = _round_up(tot, MOE_ROWS)
    pend = jnp.cumsum(padded)
    pstart = pend - padded
    base = pstart[None, :] + jnp.cumsum(grp, axis=0) - grp
    off = jnp.cumsum(grp, axis=1) - grp

    n_rows = _round_up(n_assign + n_tiles * n_exp * (ROW_ALIGN - 1) + n_exp * (MOE_ROWS - 1), MOE_ROWS)
    n_blocks = n_rows // MOE_ROWS
    blk_start = jnp.arange(n_blocks, dtype=jnp.int32) * MOE_ROWS
    blk_e = jnp.sum((pend[None, :] <= blk_start[:, None]).astype(jnp.int32), axis=1)
    blk_e = jnp.minimum(blk_e, n_exp - 1)
    n_used = (pend[-1] // MOE_ROWS).astype(jnp.int32)
    blk_e = blk_e[jnp.minimum(jnp.arange(n_blocks, dtype=jnp.int32), n_used - 1)]
    base_b = jnp.take(base, blk_e, axis=1)
    end_b = base_b + jnp.take(grp, blk_e, axis=1)
    blk_tlo = jnp.sum((end_b <= blk_start[None, :]).astype(jnp.int32), axis=0)
    blk_thi = jnp.sum((base_b < blk_start[None, :] + MOE_ROWS).astype(jnp.int32), axis=0) - 1
    nonempty = padded > 0
    expert_ord = jnp.cumsum(nonempty.astype(jnp.int32)) - nonempty.astype(jnp.int32)
    ids = jnp.arange(n_exp, dtype=jnp.int32)
    later = (ids[None, :] > ids[:, None]) & nonempty[None, :]
    expert_next = jnp.min(jnp.where(later, ids[None, :], n_exp), axis=1)
    expert_next = jnp.where(expert_next == n_exp, -1, expert_next)
    return dict(
        cnt=cnt.reshape(-1), base=base.reshape(-1).astype(jnp.int32), off=off.reshape(-1).astype(jnp.int32),
        tile_rows=jnp.sum(grp, axis=1).astype(jnp.int32),
        blk_e=blk_e.astype(jnp.int32), blk_tlo=blk_tlo, blk_thi=blk_thi,
        blk_ord=expert_ord[blk_e].astype(jnp.int32), blk_next=expert_next[blk_e].astype(jnp.int32),
        n_used=n_used.reshape(1), n_rows=n_rows)


def kernel(x, w_in, b_in, lambda_q1, lambda_k1, lambda_q2, lambda_k2, subln_g, w_proj_a, w_proj_b, w_out, b_out, ln1_g, ln1_b, w_router, b_router, w_up, b_up, w_down, b_down, ln2_g, ln2_b):
    bsz, seq, dm = x.shape
    depth = w_in.shape[0]
    alpha = (2.0 * depth) ** 0.25
    t = bsz * seq
    for layer in range(depth):
        lambda_init = 0.8 - 0.6 * math.exp(-0.3 * layer)
        x2d = x.reshape(t, dm)
        n_a, n_b = 3 * WIDTH_A, 3 * WIDTH_B
        w_l, b_l = w_in[layer], b_in[layer]
        query_scale = jnp.full((WIDTH_A,), QUERY_SCALE, _F32)
        ones = functools.partial(jnp.ones, dtype=_F32)
        qkv_a = _in_proj_a(x, w_l[:, :n_a].astype(_BF16), b_l[:n_a],
                           jnp.concatenate([query_scale, ones((2 * WIDTH_A,))]))
        w_rest = jnp.concatenate([w_l[:, n_a + n_b:], w_l[:, n_a:n_a + n_b]], axis=1).astype(_BF16)
        b_rest = jnp.concatenate([b_l[n_a + n_b:], b_l[n_a:n_a + n_b]])
        scale_rest = jnp.concatenate([ones((2 * dm,)), query_scale, ones((2 * WIDTH_B,))])
        proj2d = _in_proj(x2d, w_rest, b_rest, scale_rest)

        dil = [_dilated_pass(a, d) for a, (_, d) in zip(qkv_a, DILATED_PATTERNS)]

        lam_vecs = jnp.stack([lambda_q1[layer], lambda_k1[layer], lambda_q2[layer], lambda_k2[layer]]).astype(_F32)
        o_b = _diff_attention(proj2d.reshape(bsz, seq, -1), 2 * dm, lam_vecs, subln_g[layer],
                              lambda_init).reshape(t, WIDTH_B)

        h, lpos, cnt, xs = _merge_router(
            x2d, proj2d, 0, [o for o, _ in dil], [l for _, l in dil], o_b,
            w_proj_a[layer].astype(_BF16), w_proj_b[layer].astype(_BF16),
            w_out[layer].astype(_BF16), b_out[layer], ln1_g[layer], ln1_b[layer],
            w_router[layer], b_router[layer], alpha)

        plan = _moe_plan(cnt[:, :, 0], t * TOP_K)
        ys = _experts(xs, plan, w_up[layer], b_up[layer], w_down[layer], b_down[layer])
        out = _combine(h, ys, lpos, plan, ln2_g[layer], ln2_b[layer], alpha)
        x = out.reshape(bsz, seq, dm)
    return x
```

```python
import functools
import itertools
import math

import numpy as np
import jax
import jax.numpy as jnp
from jax import lax
from jax.experimental import pallas as pl
from jax.experimental.pallas import tpu as pltpu

HEAD_DIM = 64
N_HEADS_A = 8
DILATED_PATTERNS = ((128, 1), (512, 4), (2048, 16))
N_HEADS_B = 4
WIDTH_A = N_HEADS_A * HEAD_DIM
WIDTH_B = N_HEADS_B * 2 * HEAD_DIM
N_ALIBI_HEADS = N_HEADS_A + N_HEADS_B
ALIBI_IDX_A = (0, 1, 3, 4, 6, 7, 9, 10)
ALIBI_IDX_B = (2, 5, 8, 11)
Q_BLOCK = 128
MASK_VALUE = -1e30
N_EXPERTS = 32
TOP_K = 4
SWIGLU_ALPHA = 1.702
SWIGLU_LIMIT = 7.0
LN_EPS = 1e-5
SUBLN_EPS = 1e-5
LOG2E = math.log2(math.e)
QUERY_SCALE = HEAD_DIM ** -0.5 * LOG2E

LANES = 128
V7X_VMEM_LIMIT_BYTES = 56 * 1024 * 1024

PROJ_TM = 1024
PROJ_A_TM = 512
DIL_ROWS_MAX = 8 * Q_BLOCK
DIL_HALO = 64
DIFF_TQ = 512
DIFF_TK = 512
MOE_ROWS = 512
FFN_CHUNK = 512
MERGE_TM = 512
MOE_TM = 256
ROW_ALIGN = 8
LOCAL_ROWS = -(-(TOP_K * MOE_TM + N_EXPERTS * (ROW_ALIGN - 1)) // LANES) * LANES

_F32 = jnp.float32
_BF16 = jnp.bfloat16


def _params(*sem):
    return pltpu.CompilerParams(dimension_semantics=sem, vmem_limit_bytes=V7X_VMEM_LIMIT_BYTES)


def _alibi_slopes():
    return (2.0 ** (-8.0 * np.arange(1, N_ALIBI_HEADS + 1) / N_ALIBI_HEADS)).astype(np.float32)


def _in_proj_kernel(x_ref, w_ref, b_ref, cs_ref, o_ref):
    x = x_ref[...].astype(_BF16)
    acc = jnp.dot(x, w_ref[...], preferred_element_type=_F32)
    o_ref[...] = ((acc + b_ref[...]) * cs_ref[...]).astype(o_ref.dtype)


def _in_proj(x2d, w_bf16, b, colscale):
    t, dm = x2d.shape
    n = w_bf16.shape[1]
    tn = n // 2
    assert n % 2 == 0 and tn % LANES == 0
    return pl.pallas_call(
        _in_proj_kernel,
        grid=(t // PROJ_TM, n // tn),
        in_specs=[
            pl.BlockSpec((PROJ_TM, dm), lambda i, j: (i, 0)),
            pl.BlockSpec((dm, tn), lambda i, j: (0, j)),
            pl.BlockSpec((1, tn), lambda i, j: (0, j)),
            pl.BlockSpec((1, tn), lambda i, j: (0, j)),
        ],
        out_specs=pl.BlockSpec((PROJ_TM, tn), lambda i, j: (i, j)),
        out_shape=jax.ShapeDtypeStruct((t, n), _BF16),
        compiler_params=_params("parallel", "arbitrary"),
        name="in_proj",
    )(x2d, w_bf16, b.reshape(1, n), colscale.reshape(1, n))


def _in_proj_a_kernel(x_ref, w_ref, b_ref, cs_ref, *refs):
    out_refs, acc_ref, stage_ref = refs[:-2], refs[-2], refs[-1]
    x = x_ref[0].astype(_BF16)
    acc = (jnp.dot(x, w_ref[...], preferred_element_type=_F32) + b_ref[...]) * cs_ref[...]
    n_chunks, tm, _ = acc_ref.shape
    for c in range(n_chunks):
        acc_ref[c] = acc[:, c * LANES:(c + 1) * LANES]
    for o_ref, (_, d) in zip(out_refs, DILATED_PATTERNS):
        if d == 1:
            o_ref[0, 0] = acc.astype(o_ref.dtype)
            continue
        if d == 4:
            for r in range(d):
                for c in range(n_chunks):
                    rows = acc_ref[c, pl.ds(r, tm // d, stride=d), :]
                    stage_ref[c, r] = rows
                    o_ref[0, r, :, c * LANES:(c + 1) * LANES] = rows.astype(o_ref.dtype)
            continue
        assert d == 16
        for r1 in range(4):
            for r2 in range(4):
                for c in range(n_chunks):
                    rows = stage_ref[c, r1, pl.ds(r2, tm // d, stride=4), :]
                    o_ref[0, r1 + 4 * r2, :, c * LANES:(c + 1) * LANES] = rows.astype(o_ref.dtype)


def _in_proj_a(x, w_bf16, b, colscale):
    bsz, seq, dm = x.shape
    n = w_bf16.shape[1]
    tm = PROJ_A_TM
    per_b = seq // tm
    dils = [d for _, d in DILATED_PATTERNS]
    assert seq % tm == 0 and all(tm % d == 0 and (tm // d) % 16 == 0 for d in dils) and dils == [1, 4, 16]
    return pl.pallas_call(
        _in_proj_a_kernel,
        grid=(bsz * per_b,),
        in_specs=[
            pl.BlockSpec((1, tm, dm), lambda i: (i // per_b, i % per_b, 0)),
            pl.BlockSpec((dm, n), lambda i: (0, 0)),
            pl.BlockSpec((1, n), lambda i: (0, 0)),
            pl.BlockSpec((1, n), lambda i: (0, 0)),
        ],
        out_specs=[pl.BlockSpec((1, d, tm // d, n), lambda i: (i // per_b, 0, i % per_b, 0)) for d in dils],
        out_shape=[jax.ShapeDtypeStruct((bsz, d, seq // d, n), _BF16) for d in dils],
        scratch_shapes=[pltpu.VMEM((n // LANES, tm, LANES), _F32),
                        pltpu.VMEM((n // LANES, 4, tm // 4, LANES), _F32)],
        compiler_params=_params("parallel"),
        name="in_proj_a",
    )(x, w_bf16, b.reshape(1, n), colscale.reshape(1, n))


def _dilated_bias_table(dilation):
    slopes = _alibi_slopes()[list(ALIBI_IDX_A)]
    band = Q_BLOCK + 2 * DIL_HALO
    qi = np.arange(Q_BLOCK)[:, None]
    kj = np.arange(band)[None, :]
    rel = qi - kj + DIL_HALO
    in_band = np.abs(rel) <= DIL_HALO
    base = -slopes[:, None, None] * (dilation * np.abs(rel)).astype(np.float32)[None]
    base = (base.astype(np.float64) * LOG2E).astype(np.float32)
    edge = (np.ones_like(kj, bool), kj >= DIL_HALO, kj < band - DIL_HALO)
    out = np.stack([np.where(in_band & e, base, np.float32(MASK_VALUE)) for e in edge])
    return out.astype(np.float32)


def _dilated_kernel(q_ref, km_ref, kp_ref, kn_ref, vm_ref, vp_ref, vn_ref, bias_ref, o_ref, lse_ref,
                    kbuf, vbuf, *, nblk, n_rows, n_res):
    i = pl.program_id(2)
    h0, h1 = DIL_HALO, DIL_HALO + n_rows
    for res in range(n_res):
        kbuf[res, 0:h0, :] = kp_ref[0, res]
        kbuf[res, h0:h1, :] = km_ref[0, res]
        kbuf[res, h1:h1 + DIL_HALO, :] = kn_ref[0, res]
        vbuf[res, 0:h0, :] = vp_ref[0, res]
        vbuf[res, h0:h1, :] = vm_ref[0, res]
        vbuf[res, h1:h1 + DIL_HALO, :] = vn_ref[0, res]

    lane = lax.broadcasted_iota(jnp.int32, (Q_BLOCK, LANES), 1)
    low_half = lane < HEAD_DIM
    band = Q_BLOCK + 2 * DIL_HALO
    n_sub = n_rows // Q_BLOCK
    variants = [0] * n_sub
    variants[0] = jnp.where(i == 0, 1, 0)
    variants[-1] = jnp.where(i == nblk - 1, 2, 0)

    for res, j in itertools.product(range(n_res), range(n_sub)):
        rows = slice(j * Q_BLOCK, (j + 1) * Q_BLOCK)
        krows = slice(j * Q_BLOCK, j * Q_BLOCK + band)
        lse_cols = []
        for hp in range(N_HEADS_A // 2):
            cols = slice(hp * LANES, (hp + 1) * LANES)
            q_pair = q_ref[0, res, rows, cols]
            k_pair = kbuf[res, krows, cols]
            v_pair = vbuf[res, krows, cols]
            halves = []
            for hh in range(2):
                head = 2 * hp + hh
                keep = low_half if hh == 0 else jnp.logical_not(low_half)
                qm = jnp.where(keep, q_pair, jnp.zeros_like(q_pair))
                s = lax.dot_general(qm, k_pair, (((1,), (1,)), ((), ())), preferred_element_type=_F32)
                s = s + bias_ref[variants[j], head]
                m = jnp.max(s, axis=-1, keepdims=True)
                p = jnp.exp2(s - m)
                z = jnp.sum(p, axis=-1, keepdims=True)
                halves.append(jnp.dot(p.astype(_BF16), v_pair, preferred_element_type=_F32) * (1.0 / z))
                lse_cols.append(m + jnp.log2(z))
            o_ref[0, res, rows, cols] = jnp.where(low_half, halves[0], halves[1])
        tile = jnp.zeros((Q_BLOCK, LANES), _F32)
        for head in range(N_HEADS_A):
            tile = jnp.where(lane == head, lse_cols[head], tile)
        lse_ref[0, res, rows, :] = tile


def _dilated_pass(qkv, dilation):
    bsz, d, L, _ = qkv.shape
    n_rows = min(DIL_ROWS_MAX, L)
    n_res = min(DIL_ROWS_MAX // n_rows, d)
    assert d == dilation and L % n_rows == 0 and n_rows % DIL_HALO == 0 and n_rows >= 2 * Q_BLOCK
    assert d % n_res == 0
    nblk = L // n_rows
    per = n_rows // DIL_HALO
    n_halo = L // DIL_HALO
    bias = jnp.asarray(_dilated_bias_table(dilation))

    def main(s):
        return pl.BlockSpec((1, n_res, n_rows, WIDTH_A), lambda b, r, i: (b, r, i, s))

    def left(s):
        return pl.BlockSpec((1, n_res, DIL_HALO, WIDTH_A), lambda b, r, i: (b, r, jnp.maximum(per * i - 1, 0), s))

    def right(s):
        return pl.BlockSpec((1, n_res, DIL_HALO, WIDTH_A),
                            lambda b, r, i: (b, r, jnp.minimum(per * i + per, n_halo - 1), s))

    rows_buf = n_rows + 2 * DIL_HALO
    return pl.pallas_call(
        functools.partial(_dilated_kernel, nblk=nblk, n_rows=n_rows, n_res=n_res),
        grid=(bsz, dilation // n_res, nblk),
        in_specs=[main(0), main(1), left(1), right(1), main(2), left(2), right(2),
                  pl.BlockSpec(bias.shape, lambda b, r, i: (0, 0, 0, 0))],
        out_specs=[pl.BlockSpec((1, n_res, n_rows, WIDTH_A), lambda b, r, i: (b, r, i, 0)),
                   pl.BlockSpec((1, n_res, n_rows, LANES), lambda b, r, i: (b, r, i, 0))],
        out_shape=[jax.ShapeDtypeStruct((bsz, dilation, L, WIDTH_A), _F32),
                   jax.ShapeDtypeStruct((bsz, dilation, L, LANES), _F32)],
        scratch_shapes=[pltpu.VMEM((n_res, rows_buf, WIDTH_A), _BF16),
                        pltpu.VMEM((n_res, rows_buf, WIDTH_A), _BF16)],
        compiler_params=_params("parallel", "parallel", "arbitrary"),
        name=f"dilated_d{dilation}",
    )(qkv, qkv, qkv, qkv, qkv, qkv, qkv, bias)


def _split_bf16(x):
    hi = x.astype(jnp.bfloat16)
    lo = (x - hi.astype(np.float64)).astype(jnp.bfloat16)
    return hi, lo


def _diff_tables():
    s2 = _alibi_slopes()[list(ALIBI_IDX_B)].astype(np.float64) * LOG2E
    qaug = np.zeros((N_HEADS_B, DIFF_TQ, LANES), jnp.bfloat16)
    kaug = np.zeros((N_HEADS_B, 2, DIFF_TK, LANES), jnp.bfloat16)
    i = np.arange(DIFF_TQ, dtype=np.float64)
    j = np.arange(DIFF_TK, dtype=np.float64)
    for h in range(N_HEADS_B):
        r_hi, r_lo = _split_bf16(-s2[h] * i)
        qaug[h, :, 0] = 1.0
        qaug[h, :, 1] = 1.0
        qaug[h, :, 2] = r_hi
        qaug[h, :, 3] = r_lo
        for side, sign in enumerate((1.0, -1.0)):
            c_hi, c_lo = _split_bf16(sign * s2[h] * j)
            kaug[h, side, :, 0] = c_hi
            kaug[h, side, :, 1] = c_lo
            kaug[h, side, :, 2] = sign
            kaug[h, side, :, 3] = sign
    halves = DIFF_TK // DIFF_TQ
    off = (np.arange(halves) * DIFF_TQ)[:, None, None]
    dist = np.abs(off + i[None, :, None] - j[None, None, :])
    diag = (-s2[:, None, None, None] * dist[None]).astype(np.float32)
    return jnp.asarray(qaug), jnp.asarray(kaug), jnp.asarray(diag), jnp.asarray(s2.astype(np.float32))


def _diff_kernel(slope_ref, lam_ref, q_ref, k_ref, v_ref, qaug_ref, kaug_ref, diag_ref, g_ref, o_ref,
                 *, seq, lambda_init):
    head = pl.program_id(1)
    qb = pl.program_id(2)
    q0 = qb * DIFF_TQ
    halves = DIFF_TK // DIFF_TQ
    kd = qb // halves
    n_kb = seq // DIFF_TK
    slope2 = slope_ref[head]
    nt = (((1,), (1,)), ((), ()))

    q = q_ref[0]
    lane = lax.broadcasted_iota(jnp.int32, q.shape, 1)
    zero = jnp.zeros_like(q)
    qaug = qaug_ref[0]
    q_maps = (jnp.concatenate([jnp.where(lane < HEAD_DIM, q, zero), qaug], axis=1),
              jnp.concatenate([jnp.where(lane >= HEAD_DIM, q, zero), qaug], axis=1))
    zero_aug = jnp.zeros((DIFF_TK, LANES), _BF16)
    vlane = lax.broadcasted_iota(jnp.int32, (DIFF_TK, LANES), 1)
    ones_col = jnp.where(vlane == 0, 1.0, 0.0).astype(_BF16)

    def load(kb):
        k0 = pl.multiple_of(kb * DIFF_TK, DIFF_TK)
        return k_ref[0, pl.ds(k0, DIFF_TK), :], jnp.concatenate([v_ref[0, pl.ds(k0, DIFF_TK), :], ones_col], axis=1)

    def split(pv):
        return pv[:, LANES:LANES + 1], pv[:, 0:LANES]

    def update(s, const, vblk, carry):
        m_old, l_old, acc = carry
        m_new = jnp.maximum(m_old, jnp.max(s, axis=-1, keepdims=True) + const)
        alpha = jnp.exp2(m_old - m_new)
        p = jnp.exp2(s - (m_new - const))
        l_blk, o_blk = split(jnp.dot(p.astype(_BF16), vblk, preferred_element_type=_F32))
        return m_new, alpha * l_old + l_blk, alpha * acc + o_blk

    kblk, vblk = load(kd)
    k_all = jnp.concatenate([kblk, zero_aug], axis=1)
    bias = diag_ref[0, qb % halves]
    carries = []
    for mp in range(2):
        s = lax.dot_general(q_maps[mp], k_all, nt, preferred_element_type=_F32) + bias
        m = jnp.max(s, axis=-1, keepdims=True)
        p = jnp.exp2(s - m)
        carries.append((m,) + split(jnp.dot(p.astype(_BF16), vblk, preferred_element_type=_F32)))

    for t in range(n_kb - 1):
        kb = jnp.where(t >= kd, t + 1, t)
        side = jnp.where(t >= kd, 1, 0)
        kblk, vblk = load(kb)
        k_all = jnp.concatenate([kblk, kaug_ref[0, side]], axis=1)
        const = -slope2 * jnp.abs(q0 - kb * DIFF_TK).astype(_F32)
        for mp in range(2):
            s = lax.dot_general(q_maps[mp], k_all, nt, preferred_element_type=_F32)
            carries[mp] = update(s, const, vblk, carries[mp])

    lv = lam_ref[...]
    lam = (jnp.exp(jnp.sum(lv[0:1] * lv[1:2], axis=-1, keepdims=True))
           - jnp.exp(jnp.sum(lv[2:3] * lv[3:4], axis=-1, keepdims=True)) + lambda_init)
    (_, l1, a1), (_, l2, a2) = carries
    of = a1 * (1.0 / l1) - lam * (a2 * (1.0 / l2))
    of = of * lax.rsqrt(jnp.mean(of * of, axis=-1, keepdims=True) + SUBLN_EPS)
    of = of * g_ref[...] * (1.0 - lambda_init)
    o_ref[0] = of.astype(o_ref.dtype)


def _diff_attention(proj, q_offset, lam_vecs, subln_g, lambda_init):
    bsz, seq, _ = proj.shape
    assert seq % DIFF_TQ == 0 and seq % DIFF_TK == 0 and q_offset % LANES == 0
    qcol = q_offset // LANES
    kcol = qcol + WIDTH_B // LANES
    vcol = kcol + WIDTH_B // LANES
    assert DIFF_TK % DIFF_TQ == 0
    halves = DIFF_TK // DIFF_TQ
    qaug, kaug, diag, slopes2 = _diff_tables()
    return pl.pallas_call(
        functools.partial(_diff_kernel, seq=seq, lambda_init=lambda_init),
        grid=(bsz, N_HEADS_B, seq // DIFF_TQ),
        in_specs=[
            pl.BlockSpec(memory_space=pltpu.SMEM),
            pl.BlockSpec((4, HEAD_DIM), lambda b, h, i: (0, 0)),
            pl.BlockSpec((1, DIFF_TQ, LANES), lambda b, h, i: (b, i, qcol + h)),
            pl.BlockSpec((1, seq, LANES), lambda b, h, i: (b, 0, kcol + h)),
            pl.BlockSpec((1, seq, LANES), lambda b, h, i: (b, 0, vcol + h)),
            pl.BlockSpec((1, DIFF_TQ, LANES), lambda b, h, i: (h, 0, 0)),
            pl.BlockSpec((1, 2, DIFF_TK, LANES), lambda b, h, i: (h, 0, 0, 0)),
            pl.BlockSpec((1, halves, DIFF_TQ, DIFF_TK), lambda b, h, i: (h, 0, 0, 0)),
            pl.BlockSpec((1, LANES), lambda b, h, i: (0, 0)),
        ],
        out_specs=pl.BlockSpec((1, DIFF_TQ, LANES), lambda b, h, i: (b, i, h)),
        out_shape=jax.ShapeDtypeStruct((bsz, seq, WIDTH_B), _BF16),
        compiler_params=_params("parallel", "parallel", "arbitrary"),
        name="diff_attn",
    )(slopes2, lam_vecs, proj, proj, proj, qaug, kaug, diag, subln_g.reshape(1, LANES).astype(_F32))


def _layer_norm(z, g, b):
    mu = jnp.mean(z, axis=-1, keepdims=True)
    zc = z - mu
    var = jnp.mean(zc * zc, axis=-1, keepdims=True)
    return zc * lax.rsqrt(var + LN_EPS) * g + b


def _mix_patterns(o_refs, lse_refs, expand_ref, o_nat, lse_nat):
    n_chunks, tm = o_nat.shape[1], o_nat.shape[2]
    outs, lses = [], []
    for p, (_, d) in enumerate(DILATED_PATTERNS):
        if d == 1:
            outs.append(o_refs[p][0, 0])
            lses.append(lse_refs[p][0, 0])
            continue
        for r in range(d):
            for c in range(n_chunks):
                o_nat[p, c, pl.ds(r, tm // d, stride=d), :] = o_refs[p][0, r, :, c * LANES:(c + 1) * LANES]
            lse_nat[p, pl.ds(r, tm // d, stride=d), :] = lse_refs[p][0, r]
        outs.append(jnp.concatenate([o_nat[p, c] for c in range(n_chunks)], axis=1))
        lses.append(lse_nat[p])
    top = functools.reduce(jnp.maximum, lses)
    es = [jnp.exp2(l - top) for l in lses]
    inv = 1.0 / functools.reduce(lambda a, b: a + b, es)
    mixed = None
    for o, e in zip(outs, es):
        w = e * inv
        w_hi = w.astype(_BF16)
        w_lo = (w - w_hi.astype(_F32)).astype(_BF16)
        w_full = (jnp.dot(w_hi, expand_ref[...], preferred_element_type=_F32)
                  + jnp.dot(w_lo, expand_ref[...], preferred_element_type=_F32))
        mixed = w_full * o if mixed is None else mixed + w_full * o
    return mixed


def _merge_kernel(x_ref, ga_ref, gb_ref, o1_ref, o2_ref, o3_ref, l1_ref, l2_ref, l3_ref, expand_ref, ob_ref,
                  wpa_ref, wpb_ref, wout_ref, bout_ref, g1_ref, b1_ref, wrh_ref, wrl_ref, br_ref,
                  h_ref, lpos_ref, cnt_ref, xs_ref, o_nat, lse_nat, *, alpha):
    o_a = _mix_patterns((o1_ref, o2_ref, o3_ref), (l1_ref, l2_ref, l3_ref), expand_ref, o_nat, lse_nat)
    pa = jnp.dot(o_a.astype(_BF16), wpa_ref[...], preferred_element_type=_F32)
    pb = jnp.dot(ob_ref[...], wpb_ref[...], preferred_element_type=_F32)
    merged = jax.nn.sigmoid(ga_ref[...].astype(_F32)) * pa + jax.nn.sigmoid(gb_ref[...].astype(_F32)) * pb
    y = jnp.dot(merged.astype(_BF16), wout_ref[...], preferred_element_type=_F32) + bout_ref[...]
    h = _layer_norm(alpha * x_ref[...] + y, g1_ref[...], b1_ref[...])
    h_ref[...] = h

    h_hi = h.astype(_BF16)
    h_lo = (h - h_hi.astype(_F32)).astype(_BF16)
    nt = (((1,), (1,)), ((), ()))
    logits = (lax.dot_general(wrh_ref[...], h_hi, nt, preferred_element_type=_F32)
              + lax.dot_general(wrl_ref[...], h_hi, nt, preferred_element_type=_F32)
              + lax.dot_general(wrh_ref[...], h_lo, nt, preferred_element_type=_F32)
              + br_ref[...])

    tm = logits.shape[1]
    expert = lax.broadcasted_iota(jnp.int32, logits.shape, 0)
    work = logits
    vals, sels = [], []
    for _ in range(TOP_K):
        mx = jnp.max(work, axis=0, keepdims=True)
        idx = jnp.min(jnp.where(work == mx, expert, N_EXPERTS), axis=0, keepdims=True)
        sel = expert == idx
        work = jnp.where(sel, -jnp.inf, work)
        vals.append(mx)
        sels.append(sel)
    ex = [jnp.exp(v - vals[0]) for v in vals]
    inv = 1.0 / (ex[0] + ex[1] + ex[2] + ex[3])
    gates = [e * inv for e in ex]

    chosen = sels[0] | sels[1] | sels[2] | sels[3]
    onehot = jnp.where(chosen, 1.0, 0.0).astype(_BF16)
    earlier = lax.broadcasted_iota(jnp.int32, (tm, tm), 0)
    later = lax.broadcasted_iota(jnp.int32, (tm, tm), 1)
    sub_shift = int(math.log2(MOE_TM))
    before = (earlier < later) & (lax.shift_right_logical(earlier, sub_shift)
                                  == lax.shift_right_logical(later, sub_shift))
    upper = jnp.where(before, 1.0, 0.0).astype(_BF16)
    in_tile = jnp.dot(onehot, upper, preferred_element_type=_F32)
    lower = (lax.broadcasted_iota(jnp.int32, (N_EXPERTS, N_EXPERTS), 1)
             < lax.broadcasted_iota(jnp.int32, (N_EXPERTS, N_EXPERTS), 0))
    lower = jnp.where(lower, 1.0, 0.0).astype(_BF16)
    rows = xs_ref.shape[1]
    dm = h.shape[1]
    slot = lax.broadcasted_iota(jnp.int32, (rows, MOE_TM), 0)
    for sub in range(tm // MOE_TM):
        cols = slice(sub * MOE_TM, (sub + 1) * MOE_TM)
        count = jnp.sum(onehot[:, cols].astype(_F32), axis=1, keepdims=True)
        cnt_ref[sub] = jnp.broadcast_to(count, cnt_ref.shape[1:])
        group = jnp.floor((count + (ROW_ALIGN - 1)) * (1.0 / ROW_ALIGN)) * ROW_ALIGN
        first_row = jnp.dot(lower, jnp.broadcast_to(group, (N_EXPERTS, LANES)).astype(_BF16),
                            preferred_element_type=_F32)[:, 0:1]
        place = in_tile[:, cols] + first_row
        perm = jnp.zeros((rows, MOE_TM), _F32)
        gsel = jnp.zeros((rows, MOE_TM), _F32)
        for k in range(TOP_K):
            pos = jnp.sum(jnp.where(sels[k][:, cols], place, 0.0), axis=0, keepdims=True).astype(jnp.int32)
            lpos_ref[k:k + 1, cols] = pos
            hit = slot == pos
            perm = perm + jnp.where(hit, 1.0, 0.0)
            gsel = gsel + jnp.where(hit, gates[k][:, cols], 0.0)
        xs_ref[sub, :, 0:dm] = jnp.dot(perm.astype(_BF16), h_hi[cols, :], preferred_element_type=_F32)
        xs_ref[sub, :, dm:dm + LANES] = jnp.broadcast_to(jnp.sum(gsel, axis=1, keepdims=True), (rows, LANES))


def _merge_router(x2d, proj2d, gate_offset, dil_outs, dil_lses, o_b, wpa, wpb, wout, bout, g1, b1,
                  w_router, b_router, alpha):
    t, dm = x2d.shape
    tm = MERGE_TM
    bsz, _, seq_over_d0, _ = dil_outs[0].shape
    seq = seq_over_d0 * DILATED_PATTERNS[0][1]
    per_b = seq // tm
    dils = [d for _, d in DILATED_PATTERNS]
    assert gate_offset % dm == 0 and seq % tm == 0 and all(tm % d == 0 for d in dils)
    gcol = gate_offset // dm
    wr_t = w_router.T.astype(_F32)
    wr_hi = wr_t.astype(_BF16)
    wr_lo = (wr_t - wr_hi.astype(_F32)).astype(_BF16)
    expand = np.zeros((LANES, WIDTH_A), np.float32)
    for head in range(N_HEADS_A):
        expand[head, head * HEAD_DIM:(head + 1) * HEAD_DIM] = 1.0
    expand = jnp.asarray(expand, _BF16)

    def const(shape):
        return pl.BlockSpec(shape, lambda i: tuple(0 for _ in shape))

    def residue_major(d, width):
        return pl.BlockSpec((1, d, tm // d, width), lambda i: (i // per_b, 0, i % per_b, 0))

    return pl.pallas_call(
        functools.partial(_merge_kernel, alpha=alpha),
        grid=(t // tm,),
        in_specs=[
            pl.BlockSpec((tm, dm), lambda i: (i, 0)),
            pl.BlockSpec((tm, dm), lambda i: (i, gcol)),
            pl.BlockSpec((tm, dm), lambda i: (i, gcol + 1)),
            *[residue_major(d, WIDTH_A) for d in dils],
            *[residue_major(d, LANES) for d in dils],
            const((LANES, WIDTH_A)),
            pl.BlockSpec((tm, WIDTH_B), lambda i: (i, 0)),
            const((WIDTH_A, dm)), const((WIDTH_B, dm)), const((dm, dm)), const((1, dm)),
            const((1, dm)), const((1, dm)),
            const((N_EXPERTS, dm)), const((N_EXPERTS, dm)), const((N_EXPERTS, 1)),
        ],
        out_specs=[
            pl.BlockSpec((tm, dm), lambda i: (i, 0)),
            pl.BlockSpec((TOP_K, tm), lambda i: (0, i)),
            pl.BlockSpec((tm // MOE_TM, N_EXPERTS, LANES), lambda i: (i, 0, 0)),
            pl.BlockSpec((tm // MOE_TM, LOCAL_ROWS, dm + LANES), lambda i: (i, 0, 0)),
        ],
        out_shape=[
            jax.ShapeDtypeStruct((t, dm), _F32),
            jax.ShapeDtypeStruct((TOP_K, t), jnp.int32),
            jax.ShapeDtypeStruct((t // MOE_TM, N_EXPERTS, LANES), _F32),
            jax.ShapeDtypeStruct((t // MOE_TM, LOCAL_ROWS, dm + LANES), _F32),
        ],
        scratch_shapes=[pltpu.VMEM((len(dils), WIDTH_A // LANES, tm, LANES), _F32),
                        pltpu.VMEM((len(dils), tm, LANES), _F32)],
        compiler_params=_params("parallel"),
        name="merge_router",
    )(x2d, proj2d, proj2d, *dil_outs, *dil_lses, expand, o_b, wpa, wpb, wout, bout.reshape(1, dm),
      g1.reshape(1, dm), b1.reshape(1, dm), wr_hi, wr_lo, b_router.reshape(N_EXPERTS, 1).astype(_F32))


def _group_rows(n):
    return lax.shift_right_logical(n + (ROW_ALIGN - 1), int(math.log2(ROW_ALIGN))) * ROW_ALIGN


def _expert_kernel(blk_e_ref, n_used_ref, tlo_ref, thi_ref, cnt_ref, base_ref, off_ref, ord_ref, next_ref,
                   xs_hbm, wup_hbm, bup_ref, wdn_hbm, bdn_ref, ys_ref,
                   xbuf, sems, pending, wup_f32, wdn_f32, wsems, wup_bf, wdn_bf, *, d_expert):
    i = pl.program_id(0)
    dm = ys_ref.shape[1]
    active = i < n_used_ref[0]

    def seg_copy(buf, src_row, dst_row, n_rows):
        n_rows = pl.multiple_of(n_rows, ROW_ALIGN)
        return pltpu.make_async_copy(xs_hbm.at[pl.ds(pl.multiple_of(src_row, ROW_ALIGN), n_rows)],
                                     xbuf.at[buf, pl.ds(pl.multiple_of(dst_row, ROW_ALIGN), n_rows)],
                                     sems.at[buf])

    def fetch(blk):
        e = blk_e_ref[blk]
        first = blk * MOE_ROWS

        def per_tile(t, total):
            j = t * N_EXPERTS + e
            g0 = base_ref[j]
            lo = jnp.maximum(g0, first)
            hi = jnp.minimum(g0 + _group_rows(cnt_ref[j]), first + MOE_ROWS)
            n_rows = jnp.maximum(hi - lo, 0)

            @pl.when(n_rows > 0)
            def _():
                seg_copy(blk % 2, t * LOCAL_ROWS + off_ref[j] + (lo - g0), lo - first, n_rows).start()

            return total + n_rows

        pending[blk % 2] = lax.fori_loop(tlo_ref[blk], thi_ref[blk] + 1, per_tile, 0)

    @pl.when(i == 0)
    def _():
        xbuf[...] = jnp.zeros_like(xbuf)
        fetch(i)

    @pl.when(i + 1 < n_used_ref[0])
    def _():
        fetch(i + 1)

    def weight_copies(expert, slot):
        return (pltpu.make_async_copy(wup_hbm.at[expert], wup_f32.at[slot], wsems.at[slot]),
                pltpu.make_async_copy(wdn_hbm.at[expert], wdn_f32.at[slot], wsems.at[slot]))

    @pl.when(i == 0)
    def _():
        for cp in weight_copies(blk_e_ref[0], 0):
            cp.start()

    @pl.when(active & ((i == 0) | (blk_e_ref[i] != blk_e_ref[jnp.maximum(i - 1, 0)])))
    def _():
        slot = ord_ref[i] % 2
        for cp in weight_copies(blk_e_ref[i], slot):
            cp.wait()
        wup_bf[...] = wup_f32[slot].astype(_BF16)
        wdn_bf[...] = wdn_f32[slot].astype(_BF16)

        @pl.when(next_ref[i] >= 0)
        def _():
            for cp in weight_copies(next_ref[i], 1 - slot):
                cp.start()

    @pl.when(active)
    def _():
        @pl.when(pending[i % 2] > 0)
        def _():
            seg_copy(i % 2, 0, 0, pending[i % 2]).wait()

        x = xbuf[i % 2, :, 0:dm].astype(_BF16)
        row_gate = xbuf[i % 2, :, dm:dm + 1]
        acc = jnp.zeros(ys_ref.shape, _F32)
        for c in range(d_expert // FFN_CHUNK):
            lo, hi = c * FFN_CHUNK, (c + 1) * FFN_CHUNK
            g = jnp.dot(x, wup_bf[:, lo:hi], preferred_element_type=_F32) + bup_ref[0, :, lo:hi]
            u = (jnp.dot(x, wup_bf[:, d_expert + lo:d_expert + hi], preferred_element_type=_F32)
                 + bup_ref[0, :, d_expert + lo:d_expert + hi])
            gate = jnp.minimum(g, SWIGLU_LIMIT)
            up = jnp.clip(u, -SWIGLU_LIMIT, SWIGLU_LIMIT)
            act = gate * jax.nn.sigmoid(SWIGLU_ALPHA * gate) * (up + 1.0)
            acc = acc + jnp.dot(act.astype(_BF16), wdn_bf[lo:hi, :], preferred_element_type=_F32)
        ys_ref[...] = ((acc + bdn_ref[0]) * row_gate).astype(_BF16).astype(_F32)

    @pl.when(jnp.logical_not(active))
    def _():
        ys_ref[...] = jnp.zeros_like(ys_ref)


def _experts(xs, plan, w_up, b_up, w_down, b_down):
    n_tiles, local_rows, width = xs.shape
    dm = width - LANES
    n_blocks = plan["n_rows"] // MOE_ROWS
    n_exp, _, two_de = w_up.shape
    d_expert = two_de // 2
    grid_spec = pltpu.PrefetchScalarGridSpec(
        num_scalar_prefetch=9,
        grid=(n_blocks,),
        in_specs=[
            pl.BlockSpec(memory_space=pl.ANY),
            pl.BlockSpec(memory_space=pl.ANY),
            pl.BlockSpec((1, 1, two_de), lambda i, be, *_: (be[i], 0, 0)),
            pl.BlockSpec(memory_space=pl.ANY),
            pl.BlockSpec((1, 1, dm), lambda i, be, *_: (be[i], 0, 0)),
        ],
        out_specs=pl.BlockSpec((MOE_ROWS, dm), lambda i, *_: (i, 0)),
        scratch_shapes=[pltpu.VMEM((2, MOE_ROWS, width), _F32), pltpu.SemaphoreType.DMA((2,)),
                        pltpu.SMEM((2,), jnp.int32),
                        pltpu.VMEM((2, dm, two_de), _F32), pltpu.VMEM((2, d_expert, dm), _F32),
                        pltpu.SemaphoreType.DMA((2,)),
                        pltpu.VMEM((dm, two_de), _BF16), pltpu.VMEM((d_expert, dm), _BF16)],
    )
    return pl.pallas_call(
        functools.partial(_expert_kernel, d_expert=d_expert),
        grid_spec=grid_spec,
        out_shape=jax.ShapeDtypeStruct((plan["n_rows"], dm), _F32),
        compiler_params=_params("arbitrary"),
        name="moe_experts",
    )(plan["blk_e"], plan["n_used"], plan["blk_tlo"], plan["blk_thi"], plan["cnt"], plan["base"], plan["off"],
      plan["blk_ord"], plan["blk_next"], xs.reshape(n_tiles * local_rows, width), w_up, b_up.reshape(n_exp, 1, two_de), w_down,
      b_down.reshape(n_exp, 1, dm))


def _combine_kernel(cnt_ref, base_ref, off_ref, rows_ref, lpos_ref, h_ref, g2_ref, b2_ref, ys_hbm, o_ref,
                    local, sems, *, tm, alpha):
    i = pl.program_id(0)
    n_tiles = pl.num_programs(0)
    rows = local.shape[1]

    def group_copy(buf, src_row, dst_row, n_rows):
        n_rows = pl.multiple_of(n_rows, ROW_ALIGN)
        return pltpu.make_async_copy(ys_hbm.at[pl.ds(pl.multiple_of(src_row, ROW_ALIGN), n_rows)],
                                     local.at[buf, pl.ds(pl.multiple_of(dst_row, ROW_ALIGN), n_rows)],
                                     sems.at[buf])

    def fetch(tile):
        for e in range(N_EXPERTS):
            j = tile * N_EXPERTS + e
            n_rows = _group_rows(cnt_ref[j])

            @pl.when(n_rows > 0)
            def _():
                group_copy(tile % 2, base_ref[j], off_ref[j], n_rows).start()

    @pl.when(i == 0)
    def _():
        local[...] = jnp.zeros_like(local)
        fetch(i)

    @pl.when(i + 1 < n_tiles)
    def _():
        fetch(i + 1)

    slot = lax.broadcasted_iota(jnp.int32, (tm, rows), 1)
    pick = jnp.zeros((tm, rows), _F32)
    for k in range(TOP_K):
        pick = pick + jnp.where(slot == lpos_ref[:, k:k + 1], 1.0, 0.0)

    group_copy(i % 2, 0, 0, rows_ref[i]).wait()
    y = jnp.dot(pick.astype(_BF16), local[i % 2].astype(_BF16), preferred_element_type=_F32)
    o_ref[...] = _layer_norm(alpha * h_ref[...] + y, g2_ref[...], b2_ref[...])


def _combine(h, ys, lpos, plan, g2, b2, alpha):
    t, dm = h.shape
    tm = MOE_TM
    grid_spec = pltpu.PrefetchScalarGridSpec(
        num_scalar_prefetch=4,
        grid=(t // tm,),
        in_specs=[
            pl.BlockSpec((tm, TOP_K), lambda i, *_: (i, 0)),
            pl.BlockSpec((tm, dm), lambda i, *_: (i, 0)),
            pl.BlockSpec((1, dm), lambda i, *_: (0, 0)),
            pl.BlockSpec((1, dm), lambda i, *_: (0, 0)),
            pl.BlockSpec(memory_space=pl.ANY),
        ],
        out_specs=pl.BlockSpec((tm, dm), lambda i, *_: (i, 0)),
        scratch_shapes=[pltpu.VMEM((2, LOCAL_ROWS, dm), _F32), pltpu.SemaphoreType.DMA((2,))],
    )
    return pl.pallas_call(
        functools.partial(_combine_kernel, tm=tm, alpha=alpha),
        grid_spec=grid_spec,
        out_shape=jax.ShapeDtypeStruct((t, dm), _F32),
        compiler_params=_params("arbitrary"),
        name="moe_combine",
    )(plan["cnt"], plan["base"], plan["off"], plan["tile_rows"], lpos.T, h, g2.reshape(1, dm),
      b2.reshape(1, dm), ys)


def _round_up(x, m):
    return (x + m - 1) // m * m


def _moe_plan(tile_cnt, n_assign):
    n_tiles, n_exp = tile_cnt.shape
    cnt = tile_cnt.astype(jnp.int32)
    grp = _round_up(cnt, ROW_ALIGN)
    tot = jnp.sum(grp, axis=0)
    padded = _round_up(tot, MOE_ROWS)
    pend = jnp.cumsum(padded)
    pstart = pend - padded
    base = pstart[None, :] + jnp.cumsum(grp, axis=0) - grp
    off = jnp.cumsum(grp, axis=1) - grp

    n_rows = _round_up(n_assign + n_tiles * n_exp * (ROW_ALIGN - 1) + n_exp * (MOE_ROWS - 1), MOE_ROWS)
    n_blocks = n_rows // MOE_ROWS
    blk_start = jnp.arange(n_blocks, dtype=jnp.int32) * MOE_ROWS
    blk_e = jnp.sum((pend[None, :] <= blk_start[:, None]).astype(jnp.int32), axis=1)
    blk_e = jnp.minimum(blk_e, n_exp - 1)
    n_used = (pend[-1] // MOE_ROWS).astype(jnp.int32)
    blk_e = blk_e[jnp.minimum(jnp.arange(n_blocks, dtype=jnp.int32), n_used - 1)]
    base_b = jnp.take(base, blk_e, axis=1)
    end_b = base_b + jnp.take(grp, blk_e, axis=1)
    blk_tlo = jnp.sum((end_b <= blk_start[None, :]).astype(jnp.int32), axis=0)
    blk_thi = jnp.sum((base_b < blk_start[None, :] + MOE_ROWS).astype(jnp.int32), axis=0) - 1
    nonempty = padded > 0
    expert_ord = jnp.cumsum(nonempty.astype(jnp.int32)) - nonempty.astype(jnp.int32)
    ids = jnp.arange(n_exp, dtype=jnp.int32)
    later = (ids[None, :] > ids[:, None]) & nonempty[None, :]
    expert_next = jnp.min(jnp.where(later, ids[None, :], n_exp), axis=1)
    expert_next = jnp.where(expert_next == n_exp, -1, expert_next)
    return dict(
        cnt=cnt.reshape(-1), base=base.reshape(-1).astype(jnp.int32), off=off.reshape(-1).astype(jnp.int32),
        tile_rows=jnp.sum(grp, axis=1).astype(jnp.int32),
        blk_e=blk_e.astype(jnp.int32), blk_tlo=blk_tlo, blk_thi=blk_thi,
        blk_ord=expert_ord[blk_e].astype(jnp.int32), blk_next=expert_next[blk_e].astype(jnp.int32),
        n_used=n_used.reshape(1), n_rows=n_rows)


def kernel(x, w_in, b_in, lambda_q1, lambda_k1, lambda_q2, lambda_k2, subln_g, w_proj_a, w_proj_b, w_out, b_out, ln1_g, ln1_b, w_router, b_router, w_up, b_up, w_down, b_down, ln2_g, ln2_b):
    bsz, seq, dm = x.shape
    depth = w_in.shape[0]
    alpha = (2.0 * depth) ** 0.25
    t = bsz * seq
    for layer in range(depth):
        lambda_init = 0.8 - 0.6 * math.exp(-0.3 * layer)
        x2d = x.reshape(t, dm)
        n_a, n_b = 3 * WIDTH_A, 3 * WIDTH_B
        w_l, b_l = w_in[layer], b_in[layer]
        query_scale = jnp.full((WIDTH_A,), QUERY_SCALE, _F32)
        ones = functools.partial(jnp.ones, dtype=_F32)
        qkv_a = _in_proj_a(x, w_l[:, :n_a].astype(_BF16), b_l[:n_a],
                           jnp.concatenate([query_scale, ones((2 * WIDTH_A,))]))
        w_rest = jnp.concatenate([w_l[:, n_a + n_b:], w_l[:, n_a:n_a + n_b]], axis=1).astype(_BF16)
        b_rest = jnp.concatenate([b_l[n_a + n_b:], b_l[n_a:n_a + n_b]])
        scale_rest = jnp.concatenate([ones((2 * dm,)), query_scale, ones((2 * WIDTH_B,))])
        proj2d = _in_proj(x2d, w_rest, b_rest, scale_rest)

        dil = [_dilated_pass(a, d) for a, (_, d) in zip(qkv_a, DILATED_PATTERNS)]

        lam_vecs = jnp.stack([lambda_q1[layer], lambda_k1[layer], lambda_q2[layer], lambda_k2[layer]]).astype(_F32)
        o_b = _diff_attention(proj2d.reshape(bsz, seq, -1), 2 * dm, lam_vecs, subln_g[layer],
                              lambda_init).reshape(t, WIDTH_B)

        h, lpos, cnt, xs = _merge_router(
            x2d, proj2d, 0, [o for o, _ in dil], [l for _, l in dil], o_b,
            w_proj_a[layer].astype(_BF16), w_proj_b[layer].astype(_BF16),
            w_out[layer].astype(_BF16), b_out[layer], ln1_g[layer], ln1_b[layer],
            w_router[layer], b_router[layer], alpha)

        plan = _moe_plan(cnt[:, :, 0], t * TOP_K)
        ys = _experts(xs, plan, w_up[layer], b_up[layer], w_down[layer], b_down[layer])
        out = _combine(h, ys, lpos, plan, ln2_g[layer], ln2_b[layer], alpha)
        x = out.reshape(bsz, seq, dm)
    return x
```

```python
import functools
import itertools
import math

import numpy as np
import jax
import jax.numpy as jnp
from jax import lax
from jax.experimental import pallas as pl
from jax.experimental.pallas import tpu as pltpu

HEAD_DIM = 64
N_HEADS_A = 8
DILATED_PATTERNS = ((128, 1), (512, 4), (2048, 16))
N_HEADS_B = 4
WIDTH_A = N_HEADS_A * HEAD_DIM
WIDTH_B = N_HEADS_B * 2 * HEAD_DIM
N_ALIBI_HEADS = N_HEADS_A + N_HEADS_B
ALIBI_IDX_A = (0, 1, 3, 4, 6, 7, 9, 10)
ALIBI_IDX_B = (2, 5, 8, 11)
Q_BLOCK = 128
MASK_VALUE = -1e30
N_EXPERTS = 32
TOP_K = 4
SWIGLU_ALPHA = 1.702
SWIGLU_LIMIT = 7.0
LN_EPS = 1e-5
SUBLN_EPS = 1e-5
LOG2E = math.log2(math.e)
QUERY_SCALE = HEAD_DIM ** -0.5 * LOG2E

LANES = 128
V7X_VMEM_LIMIT_BYTES = 56 * 1024 * 1024

PROJ_TM = 1024
PROJ_A_TM = 512
DIL_ROWS_MAX = 8 * Q_BLOCK
DIL_HALO = 64
DIFF_TQ = 512
DIFF_TK = 512
MOE_ROWS = 512
FFN_CHUNK = 512
MERGE_TM = 512
MOE_TM = 256
ROW_ALIGN = 8
LOCAL_ROWS = -(-(TOP_K * MOE_TM + N_EXPERTS * (ROW_ALIGN - 1)) // LANES) * LANES

_F32 = jnp.float32
_BF16 = jnp.bfloat16


def _params(*sem):
    return pltpu.CompilerParams(dimension_semantics=sem, vmem_limit_bytes=V7X_VMEM_LIMIT_BYTES)


def _alibi_slopes():
    return (2.0 ** (-8.0 * np.arange(1, N_ALIBI_HEADS + 1) / N_ALIBI_HEADS)).astype(np.float32)


def _in_proj_kernel(x_ref, w_ref, b_ref, cs_ref, o_ref):
    x = x_ref[...].astype(_BF16)
    acc = jnp.dot(x, w_ref[...], preferred_element_type=_F32)
    o_ref[...] = ((acc + b_ref[...]) * cs_ref[...]).astype(o_ref.dtype)


def _in_proj(x2d, w_bf16, b, colscale):
    t, dm = x2d.shape
    n = w_bf16.shape[1]
    tn = n // 2
    assert n % 2 == 0 and tn % LANES == 0
    return pl.pallas_call(
        _in_proj_kernel,
        grid=(t // PROJ_TM, n // tn),
        in_specs=[
            pl.BlockSpec((PROJ_TM, dm), lambda i, j: (i, 0)),
            pl.BlockSpec((dm, tn), lambda i, j: (0, j)),
            pl.BlockSpec((1, tn), lambda i, j: (0, j)),
            pl.BlockSpec((1, tn), lambda i, j: (0, j)),
        ],
        out_specs=pl.BlockSpec((PROJ_TM, tn), lambda i, j: (i, j)),
        out_shape=jax.ShapeDtypeStruct((t, n), _BF16),
        compiler_params=_params("parallel", "arbitrary"),
        name="in_proj",
    )(x2d, w_bf16, b.reshape(1, n), colscale.reshape(1, n))


def _in_proj_a_kernel(x_ref, w_ref, b_ref, cs_ref, *refs):
    out_refs, acc_ref, stage_ref = refs[:-2], refs[-2], refs[-1]
    x = x_ref[0].astype(_BF16)
    acc = (jnp.dot(x, w_ref[...], preferred_element_type=_F32) + b_ref[...]) * cs_ref[...]
    n_chunks, tm, _ = acc_ref.shape
    for c in range(n_chunks):
        acc_ref[c] = acc[:, c * LANES:(c + 1) * LANES]
    for o_ref, (_, d) in zip(out_refs, DILATED_PATTERNS):
        if d == 1:
            o_ref[0, 0] = acc.astype(o_ref.dtype)
            continue
        if d == 4:
            for r in range(d):
                for c in range(n_chunks):
                    rows = acc_ref[c, pl.ds(r, tm // d, stride=d), :]
                    stage_ref[c, r] = rows
                    o_ref[0, r, :, c * LANES:(c + 1) * LANES] = rows.astype(o_ref.dtype)
            continue
        assert d == 16
        for r1 in range(4):
            for r2 in range(4):
                for c in range(n_chunks):
                    rows = stage_ref[c, r1, pl.ds(r2, tm // d, stride=4), :]
                    o_ref[0, r1 + 4 * r2, :, c * LANES:(c + 1) * LANES] = rows.astype(o_ref.dtype)


def _in_proj_a(x, w_bf16, b, colscale):
    bsz, seq, dm = x.shape
    n = w_bf16.shape[1]
    tm = PROJ_A_TM
    per_b = seq // tm
    dils = [d for _, d in DILATED_PATTERNS]
    assert seq % tm == 0 and all(tm % d == 0 and (tm // d) % 16 == 0 for d in dils) and dils == [1, 4, 16]
    return pl.pallas_call(
        _in_proj_a_kernel,
        grid=(bsz * per_b,),
        in_specs=[
            pl.BlockSpec((1, tm, dm), lambda i: (i // per_b, i % per_b, 0)),
            pl.BlockSpec((dm, n), lambda i: (0, 0)),
            pl.BlockSpec((1, n), lambda i: (0, 0)),
            pl.BlockSpec((1, n), lambda i: (0, 0)),
        ],
        out_specs=[pl.BlockSpec((1, d, tm // d, n), lambda i: (i // per_b, 0, i % per_b, 0)) for d in dils],
        out_shape=[jax.ShapeDtypeStruct((bsz, d, seq // d, n), _BF16) for d in dils],
        scratch_shapes=[pltpu.VMEM((n // LANES, tm, LANES), _F32),
                        pltpu.VMEM((n // LANES, 4, tm // 4, LANES), _F32)],
        compiler_params=_params("parallel"),
        name="in_proj_a",
    )(x, w_bf16, b.reshape(1, n), colscale.reshape(1, n))


def _dilated_bias_table(dilation):
    slopes = _alibi_slopes()[list(ALIBI_IDX_A)]
    band = Q_BLOCK + 2 * DIL_HALO
    qi = np.arange(Q_BLOCK)[:, None]
    kj = np.arange(band)[None, :]
    rel = qi - kj + DIL_HALO
    in_band = np.abs(rel) <= DIL_HALO
    base = -slopes[:, None, None] * (dilation * np.abs(rel)).astype(np.float32)[None]
    base = (base.astype(np.float64) * LOG2E).astype(np.float32)
    edge = (np.ones_like(kj, bool), kj >= DIL_HALO, kj < band - DIL_HALO)
    out = np.stack([np.where(in_band & e, base, np.float32(MASK_VALUE)) for e in edge])
    return out.astype(np.float32)


def _dilated_kernel(q_ref, km_ref, kp_ref, kn_ref, vm_ref, vp_ref, vn_ref, bias_ref, o_ref, lse_ref,
                    kbuf, vbuf, *, nblk, n_rows, n_res):
    i = pl.program_id(2)
    h0, h1 = DIL_HALO, DIL_HALO + n_rows
    for res in range(n_res):
        kbuf[res, 0:h0, :] = kp_ref[0, res]
        kbuf[res, h0:h1, :] = km_ref[0, res]
        kbuf[res, h1:h1 + DIL_HALO, :] = kn_ref[0, res]
        vbuf[res, 0:h0, :] = vp_ref[0, res]
        vbuf[res, h0:h1, :] = vm_ref[0, res]
        vbuf[res, h1:h1 + DIL_HALO, :] = vn_ref[0, res]

    lane = lax.broadcasted_iota(jnp.int32, (Q_BLOCK, LANES), 1)
    low_half = lane < HEAD_DIM
    band = Q_BLOCK + 2 * DIL_HALO
    n_sub = n_rows // Q_BLOCK
    variants = [0] * n_sub
    variants[0] = jnp.where(i == 0, 1, 0)
    variants[-1] = jnp.where(i == nblk - 1, 2, 0)

    for res, j in itertools.product(range(n_res), range(n_sub)):
        rows = slice(j * Q_BLOCK, (j + 1) * Q_BLOCK)
        krows = slice(j * Q_BLOCK, j * Q_BLOCK + band)
        lse_cols = []
        for hp in range(N_HEADS_A // 2):
            cols = slice(hp * LANES, (hp + 1) * LANES)
            q_pair = q_ref[0, res, rows, cols]
            k_pair = kbuf[res, krows, cols]
            v_pair = vbuf[res, krows, cols]
            halves = []
            for hh in range(2):
                head = 2 * hp + hh
                keep = low_half if hh == 0 else jnp.logical_not(low_half)
                qm = jnp.where(keep, q_pair, jnp.zeros_like(q_pair))
                s = lax.dot_general(qm, k_pair, (((1,), (1,)), ((), ())), preferred_element_type=_F32)
                s = s + bias_ref[variants[j], head]
                m = jnp.max(s, axis=-1, keepdims=True)
                p = jnp.exp2(s - m)
                z = jnp.sum(p, axis=-1, keepdims=True)
                halves.append(jnp.dot(p.astype(_BF16), v_pair, preferred_element_type=_F32) * (1.0 / z))
                lse_cols.append(m + jnp.log2(z))
            o_ref[0, res, rows, cols] = jnp.where(low_half, halves[0], halves[1])
        tile = jnp.zeros((Q_BLOCK, LANES), _F32)
        for head in range(N_HEADS_A):
            tile = jnp.where(lane == head, lse_cols[head], tile)
        lse_ref[0, res, rows, :] = tile


def _dilated_pass(qkv, dilation):
    bsz, d, L, _ = qkv.shape
    n_rows = min(DIL_ROWS_MAX, L)
    n_res = min(DIL_ROWS_MAX // n_rows, d)
    assert d == dilation and L % n_rows == 0 and n_rows % DIL_HALO == 0 and n_rows >= 2 * Q_BLOCK
    assert d % n_res == 0
    nblk = L // n_rows
    per = n_rows // DIL_HALO
    n_halo = L // DIL_HALO
    bias = jnp.asarray(_dilated_bias_table(dilation))

    def main(s):
        return pl.BlockSpec((1, n_res, n_rows, WIDTH_A), lambda b, r, i: (b, r, i, s))

    def left(s):
        return pl.BlockSpec((1, n_res, DIL_HALO, WIDTH_A), lambda b, r, i: (b, r, jnp.maximum(per * i - 1, 0), s))

    def right(s):
        return pl.BlockSpec((1, n_res, DIL_HALO, WIDTH_A),
                            lambda b, r, i: (b, r, jnp.minimum(per * i + per, n_halo - 1), s))

    rows_buf = n_rows + 2 * DIL_HALO
    return pl.pallas_call(
        functools.partial(_dilated_kernel, nblk=nblk, n_rows=n_rows, n_res=n_res),
        grid=(bsz, dilation // n_res, nblk),
        in_specs=[main(0), main(1), left(1), right(1), main(2), left(2), right(2),
                  pl.BlockSpec(bias.shape, lambda b, r, i: (0, 0, 0, 0))],
        out_specs=[pl.BlockSpec((1, n_res, n_rows, WIDTH_A), lambda b, r, i: (b, r, i, 0)),
                   pl.BlockSpec((1, n_res, n_rows, LANES), lambda b, r, i: (b, r, i, 0))],
        out_shape=[jax.ShapeDtypeStruct((bsz, dilation, L, WIDTH_A), _F32),
                   jax.ShapeDtypeStruct((bsz, dilation, L, LANES), _F32)],
        scratch_shapes=[pltpu.VMEM((n_res, rows_buf, WIDTH_A), _BF16),
                        pltpu.VMEM((n_res, rows_buf, WIDTH_A), _BF16)],
        compiler_params=_params("parallel", "parallel", "arbitrary"),
        name=f"dilated_d{dilation}",
    )(qkv, qkv, qkv, qkv, qkv, qkv, qkv, bias)


def _split_bf16(x):
    hi = x.astype(jnp.bfloat16)
    lo = (x - hi.astype(np.float64)).astype(jnp.bfloat16)
    return hi, lo


def _diff_tables():
    s2 = _alibi_slopes()[list(ALIBI_IDX_B)].astype(np.float64) * LOG2E
    qaug = np.zeros((N_HEADS_B, DIFF_TQ, LANES), jnp.bfloat16)
    kaug = np.zeros((N_HEADS_B, 2, DIFF_TK, LANES), jnp.bfloat16)
    i = np.arange(DIFF_TQ, dtype=np.float64)
    j = np.arange(DIFF_TK, dtype=np.float64)
    for h in range(N_HEADS_B):
        r_hi, r_lo = _split_bf16(-s2[h] * i)
        qaug[h, :, 0] = 1.0
        qaug[h, :, 1] = 1.0
        qaug[h, :, 2] = r_hi
        qaug[h, :, 3] = r_lo
        for side, sign in enumerate((1.0, -1.0)):
            c_hi, c_lo = _split_bf16(sign * s2[h] * j)
            kaug[h, side, :, 0] = c_hi
            kaug[h, side, :, 1] = c_lo
            kaug[h, side, :, 2] = sign
            kaug[h, side, :, 3] = sign
    halves = DIFF_TK // DIFF_TQ
    off = (np.arange(halves) * DIFF_TQ)[:, None, None]
    dist = np.abs(off + i[None, :, None] - j[None, None, :])
    diag = (-s2[:, None, None, None] * dist[None]).astype(np.float32)
    return jnp.asarray(qaug), jnp.asarray(kaug), jnp.asarray(diag), jnp.asarray(s2.astype(np.float32))


def _diff_kernel(slope_ref, lam_ref, q_ref, k_ref, v_ref, qaug_ref, kaug_ref, diag_ref, g_ref, o_ref,
                 *, seq, lambda_init):
    head = pl.program_id(1)
    qb = pl.program_id(2)
    q0 = qb * DIFF_TQ
    halves = DIFF_TK // DIFF_TQ
    kd = qb // halves
    n_kb = seq // DIFF_TK
    slope2 = slope_ref[head]
    nt = (((1,), (1,)), ((), ()))

    q = q_ref[0]
    lane = lax.broadcasted_iota(jnp.int32, q.shape, 1)
    zero = jnp.zeros_like(q)
    qaug = qaug_ref[0]
    q_maps = (jnp.concatenate([jnp.where(lane < HEAD_DIM, q, zero), qaug], axis=1),
              jnp.concatenate([jnp.where(lane >= HEAD_DIM, q, zero), qaug], axis=1))
    zero_aug = jnp.zeros((DIFF_TK, LANES), _BF16)
    vlane = lax.broadcasted_iota(jnp.int32, (DIFF_TK, LANES), 1)
    ones_col = jnp.where(vlane == 0, 1.0, 0.0).astype(_BF16)

    def load(kb):
        k0 = pl.multiple_of(kb * DIFF_TK, DIFF_TK)
        return k_ref[0, pl.ds(k0, DIFF_TK), :], jnp.concatenate([v_ref[0, pl.ds(k0, DIFF_TK), :], ones_col], axis=1)

    def split(pv):
        return pv[:, LANES:LANES + 1], pv[:, 0:LANES]

    def update(s, const, vblk, carry):
        m_old, l_old, acc = carry
        m_new = jnp.maximum(m_old, jnp.max(s, axis=-1, keepdims=True) + const)
        alpha = jnp.exp2(m_old - m_new)
        p = jnp.exp2(s - (m_new - const))
        l_blk, o_blk = split(jnp.dot(p.astype(_BF16), vblk, preferred_element_type=_F32))
        return m_new, alpha * l_old + l_blk, alpha * acc + o_blk

    kblk, vblk = load(kd)
    k_all = jnp.concatenate([kblk, zero_aug], axis=1)
    bias = diag_ref[0, qb % halves]
    carries = []
    for mp in range(2):
        s = lax.dot_general(q_maps[mp], k_all, nt, preferred_element_type=_F32) + bias
        m = jnp.max(s, axis=-1, keepdims=True)
        p = jnp.exp2(s - m)
        carries.append((m,) + split(jnp.dot(p.astype(_BF16), vblk, preferred_element_type=_F32)))

    for t in range(n_kb - 1):
        kb = jnp.where(t >= kd, t + 1, t)
        side = jnp.where(t >= kd, 1, 0)
        kblk, vblk = load(kb)
        k_all = jnp.concatenate([kblk, kaug_ref[0, side]], axis=1)
        const = -slope2 * jnp.abs(q0 - kb * DIFF_TK).astype(_F32)
        for mp in range(2):
            s = lax.dot_general(q_maps[mp], k_all, nt, preferred_element_type=_F32)
            carries[mp] = update(s, const, vblk, carries[mp])

    lv = lam_ref[...]
    lam = (jnp.exp(jnp.sum(lv[0:1] * lv[1:2], axis=-1, keepdims=True))
           - jnp.exp(jnp.sum(lv[2:3] * lv[3:4], axis=-1, keepdims=True)) + lambda_init)
    (_, l1, a1), (_, l2, a2) = carries
    of = a1 * (1.0 / l1) - lam * (a2 * (1.0 / l2))
    of = of * lax.rsqrt(jnp.mean(of * of, axis=-1, keepdims=True) + SUBLN_EPS)
    of = of * g_ref[...] * (1.0 - lambda_init)
    o_ref[0] = of.astype(o_ref.dtype)


def _diff_attention(proj, q_offset, lam_vecs, subln_g, lambda_init):
    bsz, seq, _ = proj.shape
    assert seq % DIFF_TQ == 0 and seq % DIFF_TK == 0 and q_offset % LANES == 0
    qcol = q_offset // LANES
    kcol = qcol + WIDTH_B // LANES
    vcol = kcol + WIDTH_B // LANES
    assert DIFF_TK % DIFF_TQ == 0
    halves = DIFF_TK // DIFF_TQ
    qaug, kaug, diag, slopes2 = _diff_tables()
    return pl.pallas_call(
        functools.partial(_diff_kernel, seq=seq, lambda_init=lambda_init),
        grid=(bsz, N_HEADS_B, seq // DIFF_TQ),
        in_specs=[
            pl.BlockSpec(memory_space=pltpu.SMEM),
            pl.BlockSpec((4, HEAD_DIM), lambda b, h, i: (0, 0)),
            pl.BlockSpec((1, DIFF_TQ, LANES), lambda b, h, i: (b, i, qcol + h)),
            pl.BlockSpec((1, seq, LANES), lambda b, h, i: (b, 0, kcol + h)),
            pl.BlockSpec((1, seq, LANES), lambda b, h, i: (b, 0, vcol + h)),
            pl.BlockSpec((1, DIFF_TQ, LANES), lambda b, h, i: (h, 0, 0)),
            pl.BlockSpec((1, 2, DIFF_TK, LANES), lambda b, h, i: (h, 0, 0, 0)),
            pl.BlockSpec((1, halves, DIFF_TQ, DIFF_TK), lambda b, h, i: (h, 0, 0, 0)),
            pl.BlockSpec((1, LANES), lambda b, h, i: (0, 0)),
        ],
        out_specs=pl.BlockSpec((1, DIFF_TQ, LANES), lambda b, h, i: (b, i, h)),
        out_shape=jax.ShapeDtypeStruct((bsz, seq, WIDTH_B), _BF16),
        compiler_params=_params("parallel", "parallel", "arbitrary"),
        name="diff_attn",
    )(slopes2, lam_vecs, proj, proj, proj, qaug, kaug, diag, subln_g.reshape(1, LANES).astype(_F32))


def _layer_norm(z, g, b):
    mu = jnp.mean(z, axis=-1, keepdims=True)
    zc = z - mu
    var = jnp.mean(zc * zc, axis=-1, keepdims=True)
    return zc * lax.rsqrt(var + LN_EPS) * g + b


def _mix_patterns(o_refs, lse_refs, expand_ref, o_nat, lse_nat):
    n_chunks, tm = o_nat.shape[1], o_nat.shape[2]
    outs, lses = [], []
    for p, (_, d) in enumerate(DILATED_PATTERNS):
        if d == 1:
            outs.append(o_refs[p][0, 0])
            lses.append(lse_refs[p][0, 0])
            continue
        for r in range(d):
            for c in range(n_chunks):
                o_nat[p, c, pl.ds(r, tm // d, stride=d), :] = o_refs[p][0, r, :, c * LANES:(c + 1) * LANES]
            lse_nat[p, pl.ds(r, tm // d, stride=d), :] = lse_refs[p][0, r]
        outs.append(jnp.concatenate([o_nat[p, c] for c in range(n_chunks)], axis=1))
        lses.append(lse_nat[p])
    top = functools.reduce(jnp.maximum, lses)
    es = [jnp.exp2(l - top) for l in lses]
    inv = 1.0 / functools.reduce(lambda a, b: a + b, es)
    mixed = None
    for o, e in zip(outs, es):
        w = e * inv
        w_hi = w.astype(_BF16)
        w_lo = (w - w_hi.astype(_F32)).astype(_BF16)
        w_full = (jnp.dot(w_hi, expand_ref[...], preferred_element_type=_F32)
                  + jnp.dot(w_lo, expand_ref[...], preferred_element_type=_F32))
        mixed = w_full * o if mixed is None else mixed + w_full * o
    return mixed


def _merge_kernel(x_ref, ga_ref, gb_ref, o1_ref, o2_ref, o3_ref, l1_ref, l2_ref, l3_ref, expand_ref, ob_ref,
                  wpa_ref, wpb_ref, wout_ref, bout_ref, g1_ref, b1_ref, wrh_ref, wrl_ref, br_ref,
                  h_ref, lpos_ref, cnt_ref, xs_ref, o_nat, lse_nat, *, alpha):
    o_a = _mix_patterns((o1_ref, o2_ref, o3_ref), (l1_ref, l2_ref, l3_ref), expand_ref, o_nat, lse_nat)
    pa = jnp.dot(o_a.astype(_BF16), wpa_ref[...], preferred_element_type=_F32)
    pb = jnp.dot(ob_ref[...], wpb_ref[...], preferred_element_type=_F32)
    merged = jax.nn.sigmoid(ga_ref[...].astype(_F32)) * pa + jax.nn.sigmoid(gb_ref[...].astype(_F32)) * pb
    y = jnp.dot(merged.astype(_BF16), wout_ref[...], preferred_element_type=_F32) + bout_ref[...]
    h = _layer_norm(alpha * x_ref[...] + y, g1_ref[...], b1_ref[...])
    h_ref[...] = h

    h_hi = h.astype(_BF16)
    h_lo = (h - h_hi.astype(_F32)).astype(_BF16)
    nt = (((1,), (1,)), ((), ()))
    logits = (lax.dot_general(wrh_ref[...], h_hi, nt, preferred_element_type=_F32)
              + lax.dot_general(wrl_ref[...], h_hi, nt, preferred_element_type=_F32)
              + lax.dot_general(wrh_ref[...], h_lo, nt, preferred_element_type=_F32)
              + br_ref[...])

    tm = logits.shape[1]
    expert = lax.broadcasted_iota(jnp.int32, logits.shape, 0)
    work = logits
    vals, sels = [], []
    for _ in range(TOP_K):
        mx = jnp.max(work, axis=0, keepdims=True)
        idx = jnp.min(jnp.where(work == mx, expert, N_EXPERTS), axis=0, keepdims=True)
        sel = expert == idx
        work = jnp.where(sel, -jnp.inf, work)
        vals.append(mx)
        sels.append(sel)
    ex = [jnp.exp(v - vals[0]) for v in vals]
    inv = 1.0 / (ex[0] + ex[1] + ex[2] + ex[3])
    gates = [e * inv for e in ex]

    chosen = sels[0] | sels[1] | sels[2] | sels[3]
    onehot = jnp.where(chosen, 1.0, 0.0).astype(_BF16)
    earlier = lax.broadcasted_iota(jnp.int32, (tm, tm), 0)
    later = lax.broadcasted_iota(jnp.int32, (tm, tm), 1)
    sub_shift = int(math.log2(MOE_TM))
    before = (earlier < later) & (lax.shift_right_logical(earlier, sub_shift)
                                  == lax.shift_right_logical(later, sub_shift))
    upper = jnp.where(before, 1.0, 0.0).astype(_BF16)
    in_tile = jnp.dot(onehot, upper, preferred_element_type=_F32)
    lower = (lax.broadcasted_iota(jnp.int32, (N_EXPERTS, N_EXPERTS), 1)
             < lax.broadcasted_iota(jnp.int32, (N_EXPERTS, N_EXPERTS), 0))
    lower = jnp.where(lower, 1.0, 0.0).astype(_BF16)
    rows = xs_ref.shape[1]
    dm = h.shape[1]
    slot = lax.broadcasted_iota(jnp.int32, (rows, MOE_TM), 0)
    for sub in range(tm // MOE_TM):
        cols = slice(sub * MOE_TM, (sub + 1) * MOE_TM)
        count = jnp.sum(onehot[:, cols].astype(_F32), axis=1, keepdims=True)
        cnt_ref[sub] = jnp.broadcast_to(count, cnt_ref.shape[1:])
        group = jnp.floor((count + (ROW_ALIGN - 1)) * (1.0 / ROW_ALIGN)) * ROW_ALIGN
        first_row = jnp.dot(lower, jnp.broadcast_to(group, (N_EXPERTS, LANES)).astype(_BF16),
                            preferred_element_type=_F32)[:, 0:1]
        place = in_tile[:, cols] + first_row
        perm = jnp.zeros((rows, MOE_TM), _F32)
        gsel = jnp.zeros((rows, MOE_TM), _F32)
        for k in range(TOP_K):
            pos = jnp.sum(jnp.where(sels[k][:, cols], place, 0.0), axis=0, keepdims=True).astype(jnp.int32)
            lpos_ref[k:k + 1, cols] = pos
            hit = slot == pos
            perm = jnp.where(hit, 1.0, perm)
            gsel = jnp.where(hit, gates[k][:, cols], gsel)
        xs_ref[sub, :, 0:dm] = jnp.dot(perm.astype(_BF16), h_hi[cols, :], preferred_element_type=_F32)
        xs_ref[sub, :, dm:dm + LANES] = jnp.broadcast_to(jnp.sum(gsel, axis=1, keepdims=True), (rows, LANES))


def _merge_router(x2d, proj2d, gate_offset, dil_outs, dil_lses, o_b, wpa, wpb, wout, bout, g1, b1,
                  w_router, b_router, alpha):
    t, dm = x2d.shape
    tm = MERGE_TM
    bsz, _, seq_over_d0, _ = dil_outs[0].shape
    seq = seq_over_d0 * DILATED_PATTERNS[0][1]
    per_b = seq // tm
    dils = [d for _, d in DILATED_PATTERNS]
    assert gate_offset % dm == 0 and seq % tm == 0 and all(tm % d == 0 for d in dils)
    gcol = gate_offset // dm
    wr_t = w_router.T.astype(_F32)
    wr_hi = wr_t.astype(_BF16)
    wr_lo = (wr_t - wr_hi.astype(_F32)).astype(_BF16)
    expand = np.zeros((LANES, WIDTH_A), np.float32)
    for head in range(N_HEADS_A):
        expand[head, head * HEAD_DIM:(head + 1) * HEAD_DIM] = 1.0
    expand = jnp.asarray(expand, _BF16)

    def const(shape):
        return pl.BlockSpec(shape, lambda i: tuple(0 for _ in shape))

    def residue_major(d, width):
        return pl.BlockSpec((1, d, tm // d, width), lambda i: (i // per_b, 0, i % per_b, 0))

    return pl.pallas_call(
        functools.partial(_merge_kernel, alpha=alpha),
        grid=(t // tm,),
        in_specs=[
            pl.BlockSpec((tm, dm), lambda i: (i, 0)),
            pl.BlockSpec((tm, dm), lambda i: (i, gcol)),
            pl.BlockSpec((tm, dm), lambda i: (i, gcol + 1)),
            *[residue_major(d, WIDTH_A) for d in dils],
            *[residue_major(d, LANES) for d in dils],
            const((LANES, WIDTH_A)),
            pl.BlockSpec((tm, WIDTH_B), lambda i: (i, 0)),
            const((WIDTH_A, dm)), const((WIDTH_B, dm)), const((dm, dm)), const((1, dm)),
            const((1, dm)), const((1, dm)),
            const((N_EXPERTS, dm)), const((N_EXPERTS, dm)), const((N_EXPERTS, 1)),
        ],
        out_specs=[
            pl.BlockSpec((tm, dm), lambda i: (i, 0)),
            pl.BlockSpec((TOP_K, tm), lambda i: (0, i)),
            pl.BlockSpec((tm // MOE_TM, N_EXPERTS, LANES), lambda i: (i, 0, 0)),
            pl.BlockSpec((tm // MOE_TM, LOCAL_ROWS, dm + LANES), lambda i: (i, 0, 0)),
        ],
        out_shape=[
            jax.ShapeDtypeStruct((t, dm), _F32),
            jax.ShapeDtypeStruct((TOP_K, t), jnp.int32),
            jax.ShapeDtypeStruct((t // MOE_TM, N_EXPERTS, LANES), _F32),
            jax.ShapeDtypeStruct((t // MOE_TM, LOCAL_ROWS, dm + LANES), _F32),
        ],
        scratch_shapes=[pltpu.VMEM((len(dils), WIDTH_A // LANES, tm, LANES), _F32),
                        pltpu.VMEM((len(dils), tm, LANES), _F32)],
        compiler_params=_params("parallel"),
        name="merge_router",
    )(x2d, proj2d, proj2d, *dil_outs, *dil_lses, expand, o_b, wpa, wpb, wout, bout.reshape(1, dm),
      g1.reshape(1, dm), b1.reshape(1, dm), wr_hi, wr_lo, b_router.reshape(N_EXPERTS, 1).astype(_F32))


def _group_rows(n):
    return lax.shift_right_logical(n + (ROW_ALIGN - 1), int(math.log2(ROW_ALIGN))) * ROW_ALIGN


def _expert_kernel(blk_e_ref, n_used_ref, tlo_ref, thi_ref, cnt_ref, base_ref, off_ref, ord_ref, next_ref,
                   xs_hbm, wup_hbm, bup_ref, wdn_hbm, bdn_ref, ys_ref,
                   xbuf, sems, pending, wup_f32, wdn_f32, wsems, wup_bf, wdn_bf, *, d_expert):
    i = pl.program_id(0)
    dm = ys_ref.shape[1]
    active = i < n_used_ref[0]

    def seg_copy(buf, src_row, dst_row, n_rows):
        n_rows = pl.multiple_of(n_rows, ROW_ALIGN)
        return pltpu.make_async_copy(xs_hbm.at[pl.ds(pl.multiple_of(src_row, ROW_ALIGN), n_rows)],
                                     xbuf.at[buf, pl.ds(pl.multiple_of(dst_row, ROW_ALIGN), n_rows)],
                                     sems.at[buf])

    def fetch(blk):
        e = blk_e_ref[blk]
        first = blk * MOE_ROWS

        def per_tile(t, total):
            j = t * N_EXPERTS + e
            g0 = base_ref[j]
            lo = jnp.maximum(g0, first)
            hi = jnp.minimum(g0 + _group_rows(cnt_ref[j]), first + MOE_ROWS)
            n_rows = jnp.maximum(hi - lo, 0)

            @pl.when(n_rows > 0)
            def _():
                seg_copy(blk % 2, t * LOCAL_ROWS + off_ref[j] + (lo - g0), lo - first, n_rows).start()

            return total + n_rows

        pending[blk % 2] = lax.fori_loop(tlo_ref[blk], thi_ref[blk] + 1, per_tile, 0)

    @pl.when(i == 0)
    def _():
        xbuf[...] = jnp.zeros_like(xbuf)
        fetch(i)

    @pl.when(i + 1 < n_used_ref[0])
    def _():
        fetch(i + 1)

    def weight_copies(expert, slot):
        return (pltpu.make_async_copy(wup_hbm.at[expert], wup_f32.at[slot], wsems.at[slot]),
                pltpu.make_async_copy(wdn_hbm.at[expert], wdn_f32.at[slot], wsems.at[slot]))

    @pl.when(i == 0)
    def _():
        for cp in weight_copies(blk_e_ref[0], 0):
            cp.start()

    @pl.when(active & ((i == 0) | (blk_e_ref[i] != blk_e_ref[jnp.maximum(i - 1, 0)])))
    def _():
        slot = ord_ref[i] % 2
        for cp in weight_copies(blk_e_ref[i], slot):
            cp.wait()
        wup_bf[...] = wup_f32[slot].astype(_BF16)
        wdn_bf[...] = wdn_f32[slot].astype(_BF16)

        @pl.when(next_ref[i] >= 0)
        def _():
            for cp in weight_copies(next_ref[i], 1 - slot):
                cp.start()

    @pl.when(active)
    def _():
        @pl.when(pending[i % 2] > 0)
        def _():
            seg_copy(i % 2, 0, 0, pending[i % 2]).wait()

        x = xbuf[i % 2, :, 0:dm].astype(_BF16)
        row_gate = xbuf[i % 2, :, dm:dm + 1]
        acc = jnp.zeros(ys_ref.shape, _F32)
        for c in range(d_expert // FFN_CHUNK):
            lo, hi = c * FFN_CHUNK, (c + 1) * FFN_CHUNK
            g = jnp.dot(x, wup_bf[:, lo:hi], preferred_element_type=_F32) + bup_ref[0, :, lo:hi]
            u = (jnp.dot(x, wup_bf[:, d_expert + lo:d_expert + hi], preferred_element_type=_F32)
                 + bup_ref[0, :, d_expert + lo:d_expert + hi])
            gate = jnp.minimum(g, SWIGLU_LIMIT)
            up = jnp.clip(u, -SWIGLU_LIMIT, SWIGLU_LIMIT)
            act = gate * jax.nn.sigmoid(SWIGLU_ALPHA * gate) * (up + 1.0)
            acc = acc + jnp.dot(act.astype(_BF16), wdn_bf[lo:hi, :], preferred_element_type=_F32)
        ys_ref[...] = ((acc + bdn_ref[0]) * row_gate).astype(_BF16).astype(_F32)

    @pl.when(jnp.logical_not(active))
    def _():
        ys_ref[...] = jnp.zeros_like(ys_ref)


def _experts(xs, plan, w_up, b_up, w_down, b_down):
    n_tiles, local_rows, width = xs.shape
    dm = width - LANES
    n_blocks = plan["n_rows"] // MOE_ROWS
    n_exp, _, two_de = w_up.shape
    d_expert = two_de // 2
    grid_spec = pltpu.PrefetchScalarGridSpec(
        num_scalar_prefetch=9,
        grid=(n_blocks,),
        in_specs=[
            pl.BlockSpec(memory_space=pl.ANY),
            pl.BlockSpec(memory_space=pl.ANY),
            pl.BlockSpec((1, 1, two_de), lambda i, be, *_: (be[i], 0, 0)),
            pl.BlockSpec(memory_space=pl.ANY),
            pl.BlockSpec((1, 1, dm), lambda i, be, *_: (be[i], 0, 0)),
        ],
        out_specs=pl.BlockSpec((MOE_ROWS, dm), lambda i, *_: (i, 0)),
        scratch_shapes=[pltpu.VMEM((2, MOE_ROWS, width), _F32), pltpu.SemaphoreType.DMA((2,)),
                        pltpu.SMEM((2,), jnp.int32),
                        pltpu.VMEM((2, dm, two_de), _F32), pltpu.VMEM((2, d_expert, dm), _F32),
                        pltpu.SemaphoreType.DMA((2,)),
                        pltpu.VMEM((dm, two_de), _BF16), pltpu.VMEM((d_expert, dm), _BF16)],
    )
    return pl.pallas_call(
        functools.partial(_expert_kernel, d_expert=d_expert),
        grid_spec=grid_spec,
        out_shape=jax.ShapeDtypeStruct((plan["n_rows"], dm), _F32),
        compiler_params=_params("arbitrary"),
        name="moe_experts",
    )(plan["blk_e"], plan["n_used"], plan["blk_tlo"], plan["blk_thi"], plan["cnt"], plan["base"], plan["off"],
      plan["blk_ord"], plan["blk_next"], xs.reshape(n_tiles * local_rows, width), w_up, b_up.reshape(n_exp, 1, two_de), w_down,
      b_down.reshape(n_exp, 1, dm))


def _combine_kernel(cnt_ref, base_ref, off_ref, rows_ref, lpos_ref, h_ref, g2_ref, b2_ref, ys_hbm, o_ref,
                    local, sems, *, tm, alpha):
    i = pl.program_id(0)
    n_tiles = pl.num_programs(0)
    rows = local.shape[1]

    def group_copy(buf, src_row, dst_row, n_rows):
        n_rows = pl.multiple_of(n_rows, ROW_ALIGN)
        return pltpu.make_async_copy(ys_hbm.at[pl.ds(pl.multiple_of(src_row, ROW_ALIGN), n_rows)],
                                     local.at[buf, pl.ds(pl.multiple_of(dst_row, ROW_ALIGN), n_rows)],
                                     sems.at[buf])

    def fetch(tile):
        for e in range(N_EXPERTS):
            j = tile * N_EXPERTS + e
            n_rows = _group_rows(cnt_ref[j])

            @pl.when(n_rows > 0)
            def _():
                group_copy(tile % 2, base_ref[j], off_ref[j], n_rows).start()

    @pl.when(i == 0)
    def _():
        local[...] = jnp.zeros_like(local)
        fetch(i)

    @pl.when(i + 1 < n_tiles)
    def _():
        fetch(i + 1)

    slot = lax.broadcasted_iota(jnp.int32, (tm, rows), 1)
    pick = jnp.zeros((tm, rows), _F32)
    for k in range(TOP_K):
        pick = jnp.where(slot == lpos_ref[:, k:k + 1], 1.0, pick)

    group_copy(i % 2, 0, 0, rows_ref[i]).wait()
    y = jnp.dot(pick.astype(_BF16), local[i % 2].astype(_BF16), preferred_element_type=_F32)
    o_ref[...] = _layer_norm(alpha * h_ref[...] + y, g2_ref[...], b2_ref[...])


def _combine(h, ys, lpos, plan, g2, b2, alpha):
    t, dm = h.shape
    tm = MOE_TM
    grid_spec = pltpu.PrefetchScalarGridSpec(
        num_scalar_prefetch=4,
        grid=(t // tm,),
        in_specs=[
            pl.BlockSpec((tm, TOP_K), lambda i, *_: (i, 0)),
            pl.BlockSpec((tm, dm), lambda i, *_: (i, 0)),
            pl.BlockSpec((1, dm), lambda i, *_: (0, 0)),
            pl.BlockSpec((1, dm), lambda i, *_: (0, 0)),
            pl.BlockSpec(memory_space=pl.ANY),
        ],
        out_specs=pl.BlockSpec((tm, dm), lambda i, *_: (i, 0)),
        scratch_shapes=[pltpu.VMEM((2, LOCAL_ROWS, dm), _F32), pltpu.SemaphoreType.DMA((2,))],
    )
    return pl.pallas_call(
        functools.partial(_combine_kernel, tm=tm, alpha=alpha),
        grid_spec=grid_spec,
        out_shape=jax.ShapeDtypeStruct((t, dm), _F32),
        compiler_params=_params("arbitrary"),
        name="moe_combine",
    )(plan["cnt"], plan["base"], plan["off"], plan["tile_rows"], lpos.T, h, g2.reshape(1, dm),
      b2.reshape(1, dm), ys)


def _round_up(x, m):
    return (x + m - 1) // m * m


def _moe_plan(tile_cnt, n_assign):
    n_tiles, n_exp = tile_cnt.shape
    cnt = tile_cnt.astype(jnp.int32)
    grp = _round_up(cnt, ROW_ALIGN)
    tot = jnp.sum(grp, axis=0)
    padded = _round_up(tot, MOE_ROWS)
    pend = jnp.cumsum(padded)
    pstart = pend - padded
    base = pstart[None, :] + jnp.cumsum(grp, axis=0) - grp
    off = jnp.cumsum(grp, axis=1) - grp

    n_rows = _round_up(n_assign + n_tiles * n_exp * (ROW_ALIGN - 1) + n_exp * (MOE_ROWS - 1), MOE_ROWS)
    n_blocks = n_rows // MOE_ROWS
    blk_start = jnp.arange(n_blocks, dtype=jnp.int32) * MOE_ROWS
    blk_e = jnp.sum((pend[None, :] <= blk_start[:, None]).astype(jnp.int32), axis=1)
    blk_e = jnp.minimum(blk_e, n_exp - 1)
    n_used = (pend[-1] // MOE_ROWS).astype(jnp.int32)
    blk_e = blk_e[jnp.minimum(jnp.arange(n_blocks, dtype=jnp.int32), n_used - 1)]
    base_b = jnp.take(base, blk_e, axis=1)
    end_b = base_b + jnp.take(grp, blk_e, axis=1)
    blk_tlo = jnp.sum((end_b <= blk_start[None, :]).astype(jnp.int32), axis=0)
    blk_thi = jnp.sum((base_b < blk_start[None, :] + MOE_ROWS).astype(jnp.int32), axis=0) - 1
    nonempty = padded > 0
    expert_ord = jnp.cumsum(nonempty.astype(jnp.int32)) - nonempty.astype(jnp.int32)
    ids = jnp.arange(n_exp, dtype=jnp.int32)
    later = (ids[None, :] > ids[:, None]) & nonempty[None, :]
    expert_next = jnp.min(jnp.where(later, ids[None, :], n_exp), axis=1)
    expert_next = jnp.where(expert_next == n_exp, -1, expert_next)
    return dict(
        cnt=cnt.reshape(-1), base=base.reshape(-1).astype(jnp.int32), off=off.reshape(-1).astype(jnp.int32),
        tile_rows=jnp.sum(grp, axis=1).astype(jnp.int32),
        blk_e=blk_e.astype(jnp.int32), blk_tlo=blk_tlo, blk_thi=blk_thi,
        blk_ord=expert_ord[blk_e].astype(jnp.int32), blk_next=expert_next[blk_e].astype(jnp.int32),
        n_used=n_used.reshape(1), n_rows=n_rows)


def kernel(x, w_in, b_in, lambda_q1, lambda_k1, lambda_q2, lambda_k2, subln_g, w_proj_a, w_proj_b, w_out, b_out, ln1_g, ln1_b, w_router, b_router, w_up, b_up, w_down, b_down, ln2_g, ln2_b):
    bsz, seq, dm = x.shape
    depth = w_in.shape[0]
    alpha = (2.0 * depth) ** 0.25
    t = bsz * seq
    for layer in range(depth):
        lambda_init = 0.8 - 0.6 * math.exp(-0.3 * layer)
        x2d = x.reshape(t, dm)
        n_a, n_b = 3 * WIDTH_A, 3 * WIDTH_B
        w_l, b_l = w_in[layer], b_in[layer]
        query_scale = jnp.full((WIDTH_A,), QUERY_SCALE, _F32)
        ones = functools.partial(jnp.ones, dtype=_F32)
        qkv_a = _in_proj_a(x, w_l[:, :n_a].astype(_BF16), b_l[:n_a],
                           jnp.concatenate([query_scale, ones((2 * WIDTH_A,))]))
        w_rest = jnp.concatenate([w_l[:, n_a + n_b:], w_l[:, n_a:n_a + n_b]], axis=1).astype(_BF16)
        b_rest = jnp.concatenate([b_l[n_a + n_b:], b_l[n_a:n_a + n_b]])
        scale_rest = jnp.concatenate([ones((2 * dm,)), query_scale, ones((2 * WIDTH_B,))])
        proj2d = _in_proj(x2d, w_rest, b_rest, scale_rest)

        dil = [_dilated_pass(a, d) for a, (_, d) in zip(qkv_a, DILATED_PATTERNS)]

        lam_vecs = jnp.stack([lambda_q1[layer], lambda_k1[layer], lambda_q2[layer], lambda_k2[layer]]).astype(_F32)
        o_b = _diff_attention(proj2d.reshape(bsz, seq, -1), 2 * dm, lam_vecs, subln_g[layer],
                              lambda_init).reshape(t, WIDTH_B)

        h, lpos, cnt, xs = _merge_router(
            x2d, proj2d, 0, [o for o, _ in dil], [l for _, l in dil], o_b,
            w_proj_a[layer].astype(_BF16), w_proj_b[layer].astype(_BF16),
            w_out[layer].astype(_BF16), b_out[layer], ln1_g[layer], ln1_b[layer],
            w_router[layer], b_router[layer], alpha)

        plan = _moe_plan(cnt[:, :, 0], t * TOP_K)
        ys = _experts(xs, plan, w_up[layer], b_up[layer], w_down[layer], b_down[layer])
        out = _combine(h, ys, lpos, plan, ln2_g[layer], ln2_b[layer], alpha)
        x = out.reshape(bsz, seq, dm)
    return x
```

```python
import functools
import itertools
import math

import numpy as np
import jax
import jax.numpy as jnp
from jax import lax
from jax.experimental import pallas as pl
from jax.experimental.pallas import tpu as pltpu

HEAD_DIM = 64
N_HEADS_A = 8
DILATED_PATTERNS = ((128, 1), (512, 4), (2048, 16))
N_HEADS_B = 4
WIDTH_A = N_HEADS_A * HEAD_DIM
WIDTH_B = N_HEADS_B * 2 * HEAD_DIM
N_ALIBI_HEADS = N_HEADS_A + N_HEADS_B
ALIBI_IDX_A = (0, 1, 3, 4, 6, 7, 9, 10)
ALIBI_IDX_B = (2, 5, 8, 11)
Q_BLOCK = 128
MASK_VALUE = -1e30
N_EXPERTS = 32
TOP_K = 4
SWIGLU_ALPHA = 1.702
SWIGLU_LIMIT = 7.0
LN_EPS = 1e-5
SUBLN_EPS = 1e-5
LOG2E = math.log2(math.e)
QUERY_SCALE = HEAD_DIM ** -0.5 * LOG2E

LANES = 128
V7X_VMEM_LIMIT_BYTES = 56 * 1024 * 1024

PROJ_TM = 1024
PROJ_A_TM = 512
DIL_ROWS_MAX = 8 * Q_BLOCK
DIL_HALO = 64
DIFF_TQ = 512
DIFF_TK = 512
MOE_ROWS = 512
FFN_CHUNK = 512
MERGE_TM = 512
MOE_TM = 256
ROW_ALIGN = 8
LOCAL_ROWS = -(-(TOP_K * MOE_TM + N_EXPERTS * (ROW_ALIGN - 1)) // LANES) * LANES

_F32 = jnp.float32
_BF16 = jnp.bfloat16


def _params(*sem):
    return pltpu.CompilerParams(dimension_semantics=sem, vmem_limit_bytes=V7X_VMEM_LIMIT_BYTES)


def _alibi_slopes():
    return (2.0 ** (-8.0 * np.arange(1, N_ALIBI_HEADS + 1) / N_ALIBI_HEADS)).astype(np.float32)


def _in_proj_kernel(x_ref, w_ref, b_ref, cs_ref, o_ref):
    x = x_ref[...].astype(_BF16)
    acc = jnp.dot(x, w_ref[...], preferred_element_type=_F32)
    o_ref[...] = ((acc + b_ref[...]) * cs_ref[...]).astype(o_ref.dtype)


def _in_proj(x2d, w_bf16, b, colscale):
    t, dm = x2d.shape
    n = w_bf16.shape[1]
    tn = n // 2
    assert n % 2 == 0 and tn % LANES == 0
    return pl.pallas_call(
        _in_proj_kernel,
        grid=(t // PROJ_TM, n // tn),
        in_specs=[
            pl.BlockSpec((PROJ_TM, dm), lambda i, j: (i, 0)),
            pl.BlockSpec((dm, tn), lambda i, j: (0, j)),
            pl.BlockSpec((1, tn), lambda i, j: (0, j)),
            pl.BlockSpec((1, tn), lambda i, j: (0, j)),
        ],
        out_specs=pl.BlockSpec((PROJ_TM, tn), lambda i, j: (i, j)),
        out_shape=jax.ShapeDtypeStruct((t, n), _BF16),
        compiler_params=_params("parallel", "arbitrary"),
        name="in_proj",
    )(x2d, w_bf16, b.reshape(1, n), colscale.reshape(1, n))


def _in_proj_a_kernel(x_ref, w_ref, b_ref, cs_ref, *refs):
    out_refs, acc_ref, stage_ref = refs[:-2], refs[-2], refs[-1]
    x = x_ref[0].astype(_BF16)
    acc = (jnp.dot(x, w_ref[...], preferred_element_type=_F32) + b_ref[...]) * cs_ref[...]
    n_chunks, tm, _ = acc_ref.shape
    for c in range(n_chunks):
        acc_ref[c] = acc[:, c * LANES:(c + 1) * LANES]
    for o_ref, (_, d) in zip(out_refs, DILATED_PATTERNS):
        if d == 1:
            o_ref[0, 0] = acc.astype(o_ref.dtype)
            continue
        if d == 4:
            for r in range(d):
                for c in range(n_chunks):
                    rows = acc_ref[c, pl.ds(r, tm // d, stride=d), :]
                    stage_ref[c, r] = rows
                    o_ref[0, r, :, c * LANES:(c + 1) * LANES] = rows.astype(o_ref.dtype)
            continue
        assert d == 16
        for r1 in range(4):
            for r2 in range(4):
                for c in range(n_chunks):
                    rows = stage_ref[c, r1, pl.ds(r2, tm // d, stride=4), :]
                    o_ref[0, r1 + 4 * r2, :, c * LANES:(c + 1) * LANES] = rows.astype(o_ref.dtype)


def _in_proj_a(x, w_bf16, b, colscale):
    bsz, seq, dm = x.shape
    n = w_bf16.shape[1]
    tm = PROJ_A_TM
    per_b = seq // tm
    dils = [d for _, d in DILATED_PATTERNS]
    assert seq % tm == 0 and all(tm % d == 0 and (tm // d) % 16 == 0 for d in dils) and dils == [1, 4, 16]
    return pl.pallas_call(
        _in_proj_a_kernel,
        grid=(bsz * per_b,),
        in_specs=[
            pl.BlockSpec((1, tm, dm), lambda i: (i // per_b, i % per_b, 0)),
            pl.BlockSpec((dm, n), lambda i: (0, 0)),
            pl.BlockSpec((1, n), lambda i: (0, 0)),
            pl.BlockSpec((1, n), lambda i: (0, 0)),
        ],
        out_specs=[pl.BlockSpec((1, d, tm // d, n), lambda i: (i // per_b, 0, i % per_b, 0)) for d in dils],
        out_shape=[jax.ShapeDtypeStruct((bsz, d, seq // d, n), _BF16) for d in dils],
        scratch_shapes=[pltpu.VMEM((n // LANES, tm, LANES), _F32),
                        pltpu.VMEM((n // LANES, 4, tm // 4, LANES), _F32)],
        compiler_params=_params("parallel"),
        name="in_proj_a",
    )(x, w_bf16, b.reshape(1, n), colscale.reshape(1, n))


def _dilated_bias_table(dilation):
    slopes = _alibi_slopes()[list(ALIBI_IDX_A)]
    band = Q_BLOCK + 2 * DIL_HALO
    qi = np.arange(Q_BLOCK)[:, None]
    kj = np.arange(band)[None, :]
    rel = qi - kj + DIL_HALO
    in_band = np.abs(rel) <= DIL_HALO
    base = -slopes[:, None, None] * (dilation * np.abs(rel)).astype(np.float32)[None]
    base = (base.astype(np.float64) * LOG2E).astype(np.float32)
    edge = (np.ones_like(kj, bool), kj >= DIL_HALO, kj < band - DIL_HALO)
    out = np.stack([np.where(in_band & e, base, np.float32(MASK_VALUE)) for e in edge])
    return out.astype(np.float32)


def _dilated_kernel(q_ref, km_ref, kp_ref, kn_ref, vm_ref, vp_ref, vn_ref, bias_ref, o_ref, lse_ref,
                    kbuf, vbuf, *, nblk, n_rows, n_res):
    i = pl.program_id(2)
    h0, h1 = DIL_HALO, DIL_HALO + n_rows
    for res in range(n_res):
        kbuf[res, 0:h0, :] = kp_ref[0, res]
        kbuf[res, h0:h1, :] = km_ref[0, res]
        kbuf[res, h1:h1 + DIL_HALO, :] = kn_ref[0, res]
        vbuf[res, 0:h0, :] = vp_ref[0, res]
        vbuf[res, h0:h1, :] = vm_ref[0, res]
        vbuf[res, h1:h1 + DIL_HALO, :] = vn_ref[0, res]

    lane = lax.broadcasted_iota(jnp.int32, (Q_BLOCK, LANES), 1)
    low_half = lane < HEAD_DIM
    band = Q_BLOCK + 2 * DIL_HALO
    n_sub = n_rows // Q_BLOCK
    variants = [0] * n_sub
    variants[0] = jnp.where(i == 0, 1, 0)
    variants[-1] = jnp.where(i == nblk - 1, 2, 0)

    for res, j in itertools.product(range(n_res), range(n_sub)):
        rows = slice(j * Q_BLOCK, (j + 1) * Q_BLOCK)
        krows = slice(j * Q_BLOCK, j * Q_BLOCK + band)
        lse_cols = []
        for hp in range(N_HEADS_A // 2):
            cols = slice(hp * LANES, (hp + 1) * LANES)
            q_pair = q_ref[0, res, rows, cols]
            k_pair = kbuf[res, krows, cols]
            v_pair = vbuf[res, krows, cols]
            halves = []
            for hh in range(2):
                head = 2 * hp + hh
                keep = low_half if hh == 0 else jnp.logical_not(low_half)
                qm = jnp.where(keep, q_pair, jnp.zeros_like(q_pair))
                s = lax.dot_general(qm, k_pair, (((1,), (1,)), ((), ())), preferred_element_type=_F32)
                s = s + bias_ref[variants[j], head]
                m = jnp.max(s, axis=-1, keepdims=True)
                p = jnp.exp2(s - m)
                z = jnp.sum(p, axis=-1, keepdims=True)
                halves.append(jnp.dot(p.astype(_BF16), v_pair, preferred_element_type=_F32) * (1.0 / z))
                lse_cols.append(m + jnp.log2(z))
            o_ref[0, res, rows, cols] = jnp.where(low_half, halves[0], halves[1]).astype(o_ref.dtype)
        tile = jnp.zeros((Q_BLOCK, LANES), _F32)
        for head in range(N_HEADS_A):
            tile = jnp.where(lane == head, lse_cols[head], tile)
        lse_ref[0, res, rows, :] = tile


def _dilated_pass(qkv, dilation):
    bsz, d, L, _ = qkv.shape
    n_rows = min(DIL_ROWS_MAX, L)
    n_res = min(DIL_ROWS_MAX // n_rows, d)
    assert d == dilation and L % n_rows == 0 and n_rows % DIL_HALO == 0 and n_rows >= 2 * Q_BLOCK
    assert d % n_res == 0
    nblk = L // n_rows
    per = n_rows // DIL_HALO
    n_halo = L // DIL_HALO
    bias = jnp.asarray(_dilated_bias_table(dilation))

    def main(s):
        return pl.BlockSpec((1, n_res, n_rows, WIDTH_A), lambda b, r, i: (b, r, i, s))

    def left(s):
        return pl.BlockSpec((1, n_res, DIL_HALO, WIDTH_A), lambda b, r, i: (b, r, jnp.maximum(per * i - 1, 0), s))

    def right(s):
        return pl.BlockSpec((1, n_res, DIL_HALO, WIDTH_A),
                            lambda b, r, i: (b, r, jnp.minimum(per * i + per, n_halo - 1), s))

    rows_buf = n_rows + 2 * DIL_HALO
    return pl.pallas_call(
        functools.partial(_dilated_kernel, nblk=nblk, n_rows=n_rows, n_res=n_res),
        grid=(bsz, dilation // n_res, nblk),
        in_specs=[main(0), main(1), left(1), right(1), main(2), left(2), right(2),
                  pl.BlockSpec(bias.shape, lambda b, r, i: (0, 0, 0, 0))],
        out_specs=[pl.BlockSpec((1, n_res, n_rows, WIDTH_A), lambda b, r, i: (b, r, i, 0)),
                   pl.BlockSpec((1, n_res, n_rows, LANES), lambda b, r, i: (b, r, i, 0))],
        out_shape=[jax.ShapeDtypeStruct((bsz, dilation, L, WIDTH_A), _BF16),
                   jax.ShapeDtypeStruct((bsz, dilation, L, LANES), _F32)],
        scratch_shapes=[pltpu.VMEM((n_res, rows_buf, WIDTH_A), _BF16),
                        pltpu.VMEM((n_res, rows_buf, WIDTH_A), _BF16)],
        compiler_params=_params("parallel", "parallel", "arbitrary"),
        name=f"dilated_d{dilation}",
    )(qkv, qkv, qkv, qkv, qkv, qkv, qkv, bias)


def _split_bf16(x):
    hi = x.astype(jnp.bfloat16)
    lo = (x - hi.astype(np.float64)).astype(jnp.bfloat16)
    return hi, lo


def _diff_tables():
    s2 = _alibi_slopes()[list(ALIBI_IDX_B)].astype(np.float64) * LOG2E
    qaug = np.zeros((N_HEADS_B, DIFF_TQ, LANES), jnp.bfloat16)
    kaug = np.zeros((N_HEADS_B, 2, DIFF_TK, LANES), jnp.bfloat16)
    i = np.arange(DIFF_TQ, dtype=np.float64)
    j = np.arange(DIFF_TK, dtype=np.float64)
    for h in range(N_HEADS_B):
        r_hi, r_lo = _split_bf16(-s2[h] * i)
        qaug[h, :, 0] = 1.0
        qaug[h, :, 1] = 1.0
        qaug[h, :, 2] = r_hi
        qaug[h, :, 3] = r_lo
        for side, sign in enumerate((1.0, -1.0)):
            c_hi, c_lo = _split_bf16(sign * s2[h] * j)
            kaug[h, side, :, 0] = c_hi
            kaug[h, side, :, 1] = c_lo
            kaug[h, side, :, 2] = sign
            kaug[h, side, :, 3] = sign
    halves = DIFF_TK // DIFF_TQ
    off = (np.arange(halves) * DIFF_TQ)[:, None, None]
    dist = np.abs(off + i[None, :, None] - j[None, None, :])
    diag = (-s2[:, None, None, None] * dist[None]).astype(np.float32)
    return jnp.asarray(qaug), jnp.asarray(kaug), jnp.asarray(diag), jnp.asarray(s2.astype(np.float32))


def _diff_kernel(slope_ref, lam_ref, q_ref, k_ref, v_ref, qaug_ref, kaug_ref, diag_ref, g_ref, o_ref,
                 *, seq, lambda_init):
    head = pl.program_id(1)
    qb = pl.program_id(2)
    q0 = qb * DIFF_TQ
    halves = DIFF_TK // DIFF_TQ
    kd = qb // halves
    n_kb = seq // DIFF_TK
    slope2 = slope_ref[head]
    nt = (((1,), (1,)), ((), ()))

    q = q_ref[0]
    lane = lax.broadcasted_iota(jnp.int32, q.shape, 1)
    zero = jnp.zeros_like(q)
    qaug = qaug_ref[0]
    q_maps = (jnp.concatenate([jnp.where(lane < HEAD_DIM, q, zero), qaug], axis=1),
              jnp.concatenate([jnp.where(lane >= HEAD_DIM, q, zero), qaug], axis=1))
    zero_aug = jnp.zeros((DIFF_TK, LANES), _BF16)
    vlane = lax.broadcasted_iota(jnp.int32, (DIFF_TK, LANES), 1)
    ones_col = jnp.where(vlane == 0, 1.0, 0.0).astype(_BF16)

    def load(kb):
        k0 = pl.multiple_of(kb * DIFF_TK, DIFF_TK)
        return k_ref[0, pl.ds(k0, DIFF_TK), :], jnp.concatenate([v_ref[0, pl.ds(k0, DIFF_TK), :], ones_col], axis=1)

    def split(pv):
        return pv[:, LANES:LANES + 1], pv[:, 0:LANES]

    def update(s, const, vblk, carry):
        m_old, l_old, acc = carry
        m_new = jnp.maximum(m_old, jnp.max(s, axis=-1, keepdims=True) + const)
        alpha = jnp.exp2(m_old - m_new)
        p = jnp.exp2(s - (m_new - const))
        l_blk, o_blk = split(jnp.dot(p.astype(_BF16), vblk, preferred_element_type=_F32))
        return m_new, alpha * l_old + l_blk, alpha * acc + o_blk

    kblk, vblk = load(kd)
    k_all = jnp.concatenate([kblk, zero_aug], axis=1)
    bias = diag_ref[0, qb % halves]
    carries = []
    for mp in range(2):
        s = lax.dot_general(q_maps[mp], k_all, nt, preferred_element_type=_F32) + bias
        m = jnp.max(s, axis=-1, keepdims=True)
        p = jnp.exp2(s - m)
        carries.append((m,) + split(jnp.dot(p.astype(_BF16), vblk, preferred_element_type=_F32)))

    for t in range(n_kb - 1):
        kb = jnp.where(t >= kd, t + 1, t)
        side = jnp.where(t >= kd, 1, 0)
        kblk, vblk = load(kb)
        k_all = jnp.concatenate([kblk, kaug_ref[0, side]], axis=1)
        const = -slope2 * jnp.abs(q0 - kb * DIFF_TK).astype(_F32)
        for mp in range(2):
            s = lax.dot_general(q_maps[mp], k_all, nt, preferred_element_type=_F32)
            carries[mp] = update(s, const, vblk, carries[mp])

    lv = lam_ref[...]
    lam = (jnp.exp(jnp.sum(lv[0:1] * lv[1:2], axis=-1, keepdims=True))
           - jnp.exp(jnp.sum(lv[2:3] * lv[3:4], axis=-1, keepdims=True)) + lambda_init)
    (_, l1, a1), (_, l2, a2) = carries
    of = a1 * (1.0 / l1) - lam * (a2 * (1.0 / l2))
    of = of * lax.rsqrt(jnp.mean(of * of, axis=-1, keepdims=True) + SUBLN_EPS)
    of = of * g_ref[...] * (1.0 - lambda_init)
    o_ref[0] = of.astype(o_ref.dtype)


def _diff_attention(proj, q_offset, lam_vecs, subln_g, lambda_init):
    bsz, seq, _ = proj.shape
    assert seq % DIFF_TQ == 0 and seq % DIFF_TK == 0 and q_offset % LANES == 0
    qcol = q_offset // LANES
    kcol = qcol + WIDTH_B // LANES
    vcol = kcol + WIDTH_B // LANES
    assert DIFF_TK % DIFF_TQ == 0
    halves = DIFF_TK // DIFF_TQ
    qaug, kaug, diag, slopes2 = _diff_tables()
    return pl.pallas_call(
        functools.partial(_diff_kernel, seq=seq, lambda_init=lambda_init),
        grid=(bsz, N_HEADS_B, seq // DIFF_TQ),
        in_specs=[
            pl.BlockSpec(memory_space=pltpu.SMEM),
            pl.BlockSpec((4, HEAD_DIM), lambda b, h, i: (0, 0)),
            pl.BlockSpec((1, DIFF_TQ, LANES), lambda b, h, i: (b, i, qcol + h)),
            pl.BlockSpec((1, seq, LANES), lambda b, h, i: (b, 0, kcol + h)),
            pl.BlockSpec((1, seq, LANES), lambda b, h, i: (b, 0, vcol + h)),
            pl.BlockSpec((1, DIFF_TQ, LANES), lambda b, h, i: (h, 0, 0)),
            pl.BlockSpec((1, 2, DIFF_TK, LANES), lambda b, h, i: (h, 0, 0, 0)),
            pl.BlockSpec((1, halves, DIFF_TQ, DIFF_TK), lambda b, h, i: (h, 0, 0, 0)),
            pl.BlockSpec((1, LANES), lambda b, h, i: (0, 0)),
        ],
        out_specs=pl.BlockSpec((1, DIFF_TQ, LANES), lambda b, h, i: (b, i, h)),
        out_shape=jax.ShapeDtypeStruct((bsz, seq, WIDTH_B), _BF16),
        compiler_params=_params("parallel", "parallel", "arbitrary"),
        name="diff_attn",
    )(slopes2, lam_vecs, proj, proj, proj, qaug, kaug, diag, subln_g.reshape(1, LANES).astype(_F32))


def _layer_norm(z, g, b):
    mu = jnp.mean(z, axis=-1, keepdims=True)
    zc = z - mu
    var = jnp.mean(zc * zc, axis=-1, keepdims=True)
    return zc * lax.rsqrt(var + LN_EPS) * g + b


def _mix_patterns(o_refs, lse_refs, expand_ref, o_nat, lse_nat):
    n_chunks, tm = o_nat.shape[1], o_nat.shape[2]
    outs, lses = [], []
    for p, (_, d) in enumerate(DILATED_PATTERNS):
        if d == 1:
            outs.append(o_refs[p][0, 0].astype(_F32))
            lses.append(lse_refs[p][0, 0])
            continue
        for r in range(d):
            for c in range(n_chunks):
                o_nat[p, c, pl.ds(r, tm // d, stride=d), :] = (
                    o_refs[p][0, r, :, c * LANES:(c + 1) * LANES].astype(_F32))
            lse_nat[p, pl.ds(r, tm // d, stride=d), :] = lse_refs[p][0, r]
        outs.append(jnp.concatenate([o_nat[p, c] for c in range(n_chunks)], axis=1))
        lses.append(lse_nat[p])
    top = functools.reduce(jnp.maximum, lses)
    es = [jnp.exp2(l - top) for l in lses]
    inv = 1.0 / functools.reduce(lambda a, b: a + b, es)
    mixed = None
    for o, e in zip(outs, es):
        w = e * inv
        w_hi = w.astype(_BF16)
        w_lo = (w - w_hi.astype(_F32)).astype(_BF16)
        w_full = (jnp.dot(w_hi, expand_ref[...], preferred_element_type=_F32)
                  + jnp.dot(w_lo, expand_ref[...], preferred_element_type=_F32))
        mixed = w_full * o if mixed is None else mixed + w_full * o
    return mixed


def _merge_kernel(x_ref, ga_ref, gb_ref, o1_ref, o2_ref, o3_ref, l1_ref, l2_ref, l3_ref, expand_ref, ob_ref,
                  wpa_ref, wpb_ref, wout_ref, bout_ref, g1_ref, b1_ref, wrh_ref, wrl_ref, br_ref,
                  h_ref, lpos_ref, cnt_ref, xs_ref, o_nat, lse_nat, *, alpha):
    o_a = _mix_patterns((o1_ref, o2_ref, o3_ref), (l1_ref, l2_ref, l3_ref), expand_ref, o_nat, lse_nat)
    pa = jnp.dot(o_a.astype(_BF16), wpa_ref[...], preferred_element_type=_F32)
    pb = jnp.dot(ob_ref[...], wpb_ref[...], preferred_element_type=_F32)
    merged = jax.nn.sigmoid(ga_ref[...].astype(_F32)) * pa + jax.nn.sigmoid(gb_ref[...].astype(_F32)) * pb
    y = jnp.dot(merged.astype(_BF16), wout_ref[...], preferred_element_type=_F32) + bout_ref[...]
    h = _layer_norm(alpha * x_ref[...] + y, g1_ref[...], b1_ref[...])
    h_ref[...] = h

    h_hi = h.astype(_BF16)
    h_lo = (h - h_hi.astype(_F32)).astype(_BF16)
    nt = (((1,), (1,)), ((), ()))
    logits = (lax.dot_general(wrh_ref[...], h_hi, nt, preferred_element_type=_F32)
              + lax.dot_general(wrl_ref[...], h_hi, nt, preferred_element_type=_F32)
              + lax.dot_general(wrh_ref[...], h_lo, nt, preferred_element_type=_F32)
              + br_ref[...])

    tm = logits.shape[1]
    expert = lax.broadcasted_iota(jnp.int32, logits.shape, 0)
    work = logits
    vals, sels = [], []
    for _ in range(TOP_K):
        mx = jnp.max(work, axis=0, keepdims=True)
        idx = jnp.min(jnp.where(work == mx, expert, N_EXPERTS), axis=0, keepdims=True)
        sel = expert == idx
        work = jnp.where(sel, -jnp.inf, work)
        vals.append(mx)
        sels.append(sel)
    ex = [jnp.exp(v - vals[0]) for v in vals]
    inv = 1.0 / (ex[0] + ex[1] + ex[2] + ex[3])
    gates = [e * inv for e in ex]

    chosen = sels[0] | sels[1] | sels[2] | sels[3]
    onehot = jnp.where(chosen, 1.0, 0.0).astype(_BF16)
    earlier = lax.broadcasted_iota(jnp.int32, (tm, tm), 0)
    later = lax.broadcasted_iota(jnp.int32, (tm, tm), 1)
    sub_shift = int(math.log2(MOE_TM))
    before = (earlier < later) & (lax.shift_right_logical(earlier, sub_shift)
                                  == lax.shift_right_logical(later, sub_shift))
    upper = jnp.where(before, 1.0, 0.0).astype(_BF16)
    in_tile = jnp.dot(onehot, upper, preferred_element_type=_F32)
    lower = (lax.broadcasted_iota(jnp.int32, (N_EXPERTS, N_EXPERTS), 1)
             < lax.broadcasted_iota(jnp.int32, (N_EXPERTS, N_EXPERTS), 0))
    lower = jnp.where(lower, 1.0, 0.0).astype(_BF16)
    rows = xs_ref.shape[1]
    dm = h.shape[1]
    slot = lax.broadcasted_iota(jnp.int32, (rows, MOE_TM), 0)
    for sub in range(tm // MOE_TM):
        cols = slice(sub * MOE_TM, (sub + 1) * MOE_TM)
        count = jnp.sum(onehot[:, cols].astype(_F32), axis=1, keepdims=True)
        cnt_ref[sub] = jnp.broadcast_to(count, cnt_ref.shape[1:])
        group = jnp.floor((count + (ROW_ALIGN - 1)) * (1.0 / ROW_ALIGN)) * ROW_ALIGN
        first_row = jnp.dot(lower, jnp.broadcast_to(group, (N_EXPERTS, LANES)).astype(_BF16),
                            preferred_element_type=_F32)[:, 0:1]
        place = in_tile[:, cols] + first_row
        perm = jnp.zeros((rows, MOE_TM), _F32)
        gsel = jnp.zeros((rows, MOE_TM), _F32)
        for k in range(TOP_K):
            pos = jnp.sum(jnp.where(sels[k][:, cols], place, 0.0), axis=0, keepdims=True).astype(jnp.int32)
            lpos_ref[k:k + 1, cols] = pos
            hit = slot == pos
            perm = jnp.where(hit, 1.0, perm)
            gsel = jnp.where(hit, gates[k][:, cols], gsel)
        xs_ref[sub, :, 0:dm] = jnp.dot(perm.astype(_BF16), h_hi[cols, :], preferred_element_type=_F32)
        xs_ref[sub, :, dm:dm + LANES] = jnp.broadcast_to(jnp.sum(gsel, axis=1, keepdims=True), (rows, LANES))


def _merge_router(x2d, proj2d, gate_offset, dil_outs, dil_lses, o_b, wpa, wpb, wout, bout, g1, b1,
                  w_router, b_router, alpha):
    t, dm = x2d.shape
    tm = MERGE_TM
    bsz, _, seq_over_d0, _ = dil_outs[0].shape
    seq = seq_over_d0 * DILATED_PATTERNS[0][1]
    per_b = seq // tm
    dils = [d for _, d in DILATED_PATTERNS]
    assert gate_offset % dm == 0 and seq % tm == 0 and all(tm % d == 0 for d in dils)
    gcol = gate_offset // dm
    wr_t = w_router.T.astype(_F32)
    wr_hi = wr_t.astype(_BF16)
    wr_lo = (wr_t - wr_hi.astype(_F32)).astype(_BF16)
    expand = np.zeros((LANES, WIDTH_A), np.float32)
    for head in range(N_HEADS_A):
        expand[head, head * HEAD_DIM:(head + 1) * HEAD_DIM] = 1.0
    expand = jnp.asarray(expand, _BF16)

    def const(shape):
        return pl.BlockSpec(shape, lambda i: tuple(0 for _ in shape))

    def residue_major(d, width):
        return pl.BlockSpec((1, d, tm // d, width), lambda i: (i // per_b, 0, i % per_b, 0))

    return pl.pallas_call(
        functools.partial(_merge_kernel, alpha=alpha),
        grid=(t // tm,),
        in_specs=[
            pl.BlockSpec((tm, dm), lambda i: (i, 0)),
            pl.BlockSpec((tm, dm), lambda i: (i, gcol)),
            pl.BlockSpec((tm, dm), lambda i: (i, gcol + 1)),
            *[residue_major(d, WIDTH_A) for d in dils],
            *[residue_major(d, LANES) for d in dils],
            const((LANES, WIDTH_A)),
            pl.BlockSpec((tm, WIDTH_B), lambda i: (i, 0)),
            const((WIDTH_A, dm)), const((WIDTH_B, dm)), const((dm, dm)), const((1, dm)),
            const((1, dm)), const((1, dm)),
            const((N_EXPERTS, dm)), const((N_EXPERTS, dm)), const((N_EXPERTS, 1)),
        ],
        out_specs=[
            pl.BlockSpec((tm, dm), lambda i: (i, 0)),
            pl.BlockSpec((TOP_K, tm), lambda i: (0, i)),
            pl.BlockSpec((tm // MOE_TM, N_EXPERTS, LANES), lambda i: (i, 0, 0)),
            pl.BlockSpec((tm // MOE_TM, LOCAL_ROWS, dm + LANES), lambda i: (i, 0, 0)),
        ],
        out_shape=[
            jax.ShapeDtypeStruct((t, dm), _F32),
            jax.ShapeDtypeStruct((TOP_K, t), jnp.int32),
            jax.ShapeDtypeStruct((t // MOE_TM, N_EXPERTS, LANES), _F32),
            jax.ShapeDtypeStruct((t // MOE_TM, LOCAL_ROWS, dm + LANES), _F32),
        ],
        scratch_shapes=[pltpu.VMEM((len(dils), WIDTH_A // LANES, tm, LANES), _F32),
                        pltpu.VMEM((len(dils), tm, LANES), _F32)],
        compiler_params=_params("parallel"),
        name="merge_router",
    )(x2d, proj2d, proj2d, *dil_outs, *dil_lses, expand, o_b, wpa, wpb, wout, bout.reshape(1, dm),
      g1.reshape(1, dm), b1.reshape(1, dm), wr_hi, wr_lo, b_router.reshape(N_EXPERTS, 1).astype(_F32))


def _group_rows(n):
    return lax.shift_right_logical(n + (ROW_ALIGN - 1), int(math.log2(ROW_ALIGN))) * ROW_ALIGN


def _expert_kernel(blk_e_ref, n_used_ref, tlo_ref, thi_ref, cnt_ref, base_ref, off_ref, ord_ref, next_ref,
                   xs_hbm, wup_hbm, bup_ref, wdn_hbm, bdn_ref, ys_ref,
                   xbuf, sems, pending, wup_f32, wdn_f32, wsems, wup_bf, wdn_bf, *, d_expert):
    i = pl.program_id(0)
    dm = ys_ref.shape[1]
    active = i < n_used_ref[0]

    def seg_copy(buf, src_row, dst_row, n_rows):
        n_rows = pl.multiple_of(n_rows, ROW_ALIGN)
        return pltpu.make_async_copy(xs_hbm.at[pl.ds(pl.multiple_of(src_row, ROW_ALIGN), n_rows)],
                                     xbuf.at[buf, pl.ds(pl.multiple_of(dst_row, ROW_ALIGN), n_rows)],
                                     sems.at[buf])

    def fetch(blk):
        e = blk_e_ref[blk]
        first = blk * MOE_ROWS

        def per_tile(t, total):
            j = t * N_EXPERTS + e
            g0 = base_ref[j]
            lo = jnp.maximum(g0, first)
            hi = jnp.minimum(g0 + _group_rows(cnt_ref[j]), first + MOE_ROWS)
            n_rows = jnp.maximum(hi - lo, 0)

            @pl.when(n_rows > 0)
            def _():
                seg_copy(blk % 2, t * LOCAL_ROWS + off_ref[j] + (lo - g0), lo - first, n_rows).start()

            return total + n_rows

        pending[blk % 2] = lax.fori_loop(tlo_ref[blk], thi_ref[blk] + 1, per_tile, 0)

    @pl.when(i == 0)
    def _():
        xbuf[...] = jnp.zeros_like(xbuf)
        fetch(i)

    @pl.when(i + 1 < n_used_ref[0])
    def _():
        fetch(i + 1)

    def weight_copies(expert, slot):
        return (pltpu.make_async_copy(wup_hbm.at[expert], wup_f32.at[slot], wsems.at[slot]),
                pltpu.make_async_copy(wdn_hbm.at[expert], wdn_f32.at[slot], wsems.at[slot]))

    @pl.when(i == 0)
    def _():
        for cp in weight_copies(blk_e_ref[0], 0):
            cp.start()

    @pl.when(active & ((i == 0) | (blk_e_ref[i] != blk_e_ref[jnp.maximum(i - 1, 0)])))
    def _():
        slot = ord_ref[i] % 2
        for cp in weight_copies(blk_e_ref[i], slot):
            cp.wait()
        wup_bf[...] = wup_f32[slot].astype(_BF16)
        wdn_bf[...] = wdn_f32[slot].astype(_BF16)

        @pl.when(next_ref[i] >= 0)
        def _():
            for cp in weight_copies(next_ref[i], 1 - slot):
                cp.start()

    @pl.when(active)
    def _():
        @pl.when(pending[i % 2] > 0)
        def _():
            seg_copy(i % 2, 0, 0, pending[i % 2]).wait()

        x = xbuf[i % 2, :, 0:dm].astype(_BF16)
        row_gate = xbuf[i % 2, :, dm:dm + 1]
        acc = jnp.zeros(ys_ref.shape, _F32)
        for c in range(d_expert // FFN_CHUNK):
            lo, hi = c * FFN_CHUNK, (c + 1) * FFN_CHUNK
            g = jnp.dot(x, wup_bf[:, lo:hi], preferred_element_type=_F32) + bup_ref[0, :, lo:hi]
            u = (jnp.dot(x, wup_bf[:, d_expert + lo:d_expert + hi], preferred_element_type=_F32)
                 + bup_ref[0, :, d_expert + lo:d_expert + hi])
            gate = jnp.minimum(g, SWIGLU_LIMIT)
            up = jnp.clip(u, -SWIGLU_LIMIT, SWIGLU_LIMIT)
            act = gate * jax.nn.sigmoid(SWIGLU_ALPHA * gate) * (up + 1.0)
            acc = acc + jnp.dot(act.astype(_BF16), wdn_bf[lo:hi, :], preferred_element_type=_F32)
        ys_ref[...] = ((acc + bdn_ref[0]) * row_gate).astype(_BF16).astype(_F32)

    @pl.when(jnp.logical_not(active))
    def _():
        ys_ref[...] = jnp.zeros_like(ys_ref)


def _experts(xs, plan, w_up, b_up, w_down, b_down):
    n_tiles, local_rows, width = xs.shape
    dm = width - LANES
    n_blocks = plan["n_rows"] // MOE_ROWS
    n_exp, _, two_de = w_up.shape
    d_expert = two_de // 2
    grid_spec = pltpu.PrefetchScalarGridSpec(
        num_scalar_prefetch=9,
        grid=(n_blocks,),
        in_specs=[
            pl.BlockSpec(memory_space=pl.ANY),
            pl.BlockSpec(memory_space=pl.ANY),
            pl.BlockSpec((1, 1, two_de), lambda i, be, *_: (be[i], 0, 0)),
            pl.BlockSpec(memory_space=pl.ANY),
            pl.BlockSpec((1, 1, dm), lambda i, be, *_: (be[i], 0, 0)),
        ],
        out_specs=pl.BlockSpec((MOE_ROWS, dm), lambda i, *_: (i, 0)),
        scratch_shapes=[pltpu.VMEM((2, MOE_ROWS, width), _F32), pltpu.SemaphoreType.DMA((2,)),
                        pltpu.SMEM((2,), jnp.int32),
                        pltpu.VMEM((2, dm, two_de), _F32), pltpu.VMEM((2, d_expert, dm), _F32),
                        pltpu.SemaphoreType.DMA((2,)),
                        pltpu.VMEM((dm, two_de), _BF16), pltpu.VMEM((d_expert, dm), _BF16)],
    )
    return pl.pallas_call(
        functools.partial(_expert_kernel, d_expert=d_expert),
        grid_spec=grid_spec,
        out_shape=jax.ShapeDtypeStruct((plan["n_rows"], dm), _F32),
        compiler_params=_params("arbitrary"),
        name="moe_experts",
    )(plan["blk_e"], plan["n_used"], plan["blk_tlo"], plan["blk_thi"], plan["cnt"], plan["base"], plan["off"],
      plan["blk_ord"], plan["blk_next"], xs.reshape(n_tiles * local_rows, width), w_up, b_up.reshape(n_exp, 1, two_de), w_down,
      b_down.reshape(n_exp, 1, dm))


def _combine_kernel(cnt_ref, base_ref, off_ref, rows_ref, lpos_ref, h_ref, g2_ref, b2_ref, ys_hbm, o_ref,
                    local, sems, *, tm, alpha):
    i = pl.program_id(0)
    n_tiles = pl.num_programs(0)
    rows = local.shape[1]

    def group_copy(buf, src_row, dst_row, n_rows):
        n_rows = pl.multiple_of(n_rows, ROW_ALIGN)
        return pltpu.make_async_copy(ys_hbm.at[pl.ds(pl.multiple_of(src_row, ROW_ALIGN), n_rows)],
                                     local.at[buf, pl.ds(pl.multiple_of(dst_row, ROW_ALIGN), n_rows)],
                                     sems.at[buf])

    def fetch(tile):
        for e in range(N_EXPERTS):
            j = tile * N_EXPERTS + e
            n_rows = _group_rows(cnt_ref[j])

            @pl.when(n_rows > 0)
            def _():
                group_copy(tile % 2, base_ref[j], off_ref[j], n_rows).start()

    @pl.when(i == 0)
    def _():
        local[...] = jnp.zeros_like(local)
        fetch(i)

    @pl.when(i + 1 < n_tiles)
    def _():
        fetch(i + 1)

    slot = lax.broadcasted_iota(jnp.int32, (tm, rows), 1)
    pick = jnp.zeros((tm, rows), _F32)
    for k in range(TOP_K):
        pick = jnp.where(slot == lpos_ref[:, k:k + 1], 1.0, pick)

    group_copy(i % 2, 0, 0, rows_ref[i]).wait()
    y = jnp.dot(pick.astype(_BF16), local[i % 2].astype(_BF16), preferred_element_type=_F32)
    o_ref[...] = _layer_norm(alpha * h_ref[...] + y, g2_ref[...], b2_ref[...])


def _combine(h, ys, lpos, plan, g2, b2, alpha):
    t, dm = h.shape
    tm = MOE_TM
    grid_spec = pltpu.PrefetchScalarGridSpec(
        num_scalar_prefetch=4,
        grid=(t // tm,),
        in_specs=[
            pl.BlockSpec((tm, TOP_K), lambda i, *_: (i, 0)),
            pl.BlockSpec((tm, dm), lambda i, *_: (i, 0)),
            pl.BlockSpec((1, dm), lambda i, *_: (0, 0)),
            pl.BlockSpec((1, dm), lambda i, *_: (0, 0)),
            pl.BlockSpec(memory_space=pl.ANY),
        ],
        out_specs=pl.BlockSpec((tm, dm), lambda i, *_: (i, 0)),
        scratch_shapes=[pltpu.VMEM((2, LOCAL_ROWS, dm), _F32), pltpu.SemaphoreType.DMA((2,))],
    )
    return pl.pallas_call(
        functools.partial(_combine_kernel, tm=tm, alpha=alpha),
        grid_spec=grid_spec,
        out_shape=jax.ShapeDtypeStruct((t, dm), _F32),
        compiler_params=_params("arbitrary"),
        name="moe_combine",
    )(plan["cnt"], plan["base"], plan["off"], plan["tile_rows"], lpos.T, h, g2.reshape(1, dm),
      b2.reshape(1, dm), ys)


def _round_up(x, m):
    return (x + m - 1) // m * m


def _moe_plan(tile_cnt, n_assign):
    n_tiles, n_exp = tile_cnt.shape
    cnt = tile_cnt.astype(jnp.int32)
    grp = _round_up(cnt, ROW_ALIGN)
    tot = jnp.sum(grp, axis=0)
    padded = _round_up(tot, MOE_ROWS)
    pend = jnp.cumsum(padded)
    pstart = pend - padded
    base = pstart[None, :] + jnp.cumsum(grp, axis=0) - grp
    off = jnp.cumsum(grp, axis=1) - grp

    n_rows = _round_up(n_assign + n_tiles * n_exp * (ROW_ALIGN - 1) + n_exp * (MOE_ROWS - 1), MOE_ROWS)
    n_blocks = n_rows // MOE_ROWS
    blk_start = jnp.arange(n_blocks, dtype=jnp.int32) * MOE_ROWS
    blk_e = jnp.sum((pend[None, :] <= blk_start[:, None]).astype(jnp.int32), axis=1)
    blk_e = jnp.minimum(blk_e, n_exp - 1)
    n_used = (pend[-1] // MOE_ROWS).astype(jnp.int32)
    blk_e = blk_e[jnp.minimum(jnp.arange(n_blocks, dtype=jnp.int32), n_used - 1)]
    base_b = jnp.take(base, blk_e, axis=1)
    end_b = base_b + jnp.take(grp, blk_e, axis=1)
    blk_tlo = jnp.sum((end_b <= blk_start[None, :]).astype(jnp.int32), axis=0)
    blk_thi = jnp.sum((base_b < blk_start[None, :] + MOE_ROWS).astype(jnp.int32), axis=0) - 1
    nonempty = padded > 0
    expert_ord = jnp.cumsum(nonempty.astype(jnp.int32)) - nonempty.astype(jnp.int32)
    ids = jnp.arange(n_exp, dtype=jnp.int32)
    later = (ids[None, :] > ids[:, None]) & nonempty[None, :]
    expert_next = jnp.min(jnp.where(later, ids[None, :], n_exp), axis=1)
    expert_next = jnp.where(expert_next == n_exp, -1, expert_next)
    return dict(
        cnt=cnt.reshape(-1), base=base.reshape(-1).astype(jnp.int32), off=off.reshape(-1).astype(jnp.int32),
        tile_rows=jnp.sum(grp, axis=1).astype(jnp.int32),
        blk_e=blk_e.astype(jnp.int32), blk_tlo=blk_tlo, blk_thi=blk_thi,
        blk_ord=expert_ord[blk_e].astype(jnp.int32), blk_next=expert_next[blk_e].astype(jnp.int32),
        n_used=n_used.reshape(1), n_rows=n_rows)


def kernel(x, w_in, b_in, lambda_q1, lambda_k1, lambda_q2, lambda_k2, subln_g, w_proj_a, w_proj_b, w_out, b_out, ln1_g, ln1_b, w_router, b_router, w_up, b_up, w_down, b_down, ln2_g, ln2_b):
    bsz, seq, dm = x.shape
    depth = w_in.shape[0]
    alpha = (2.0 * depth) ** 0.25
    t = bsz * seq
    for layer in range(depth):
        lambda_init = 0.8 - 0.6 * math.exp(-0.3 * layer)
        x2d = x.reshape(t, dm)
        n_a, n_b = 3 * WIDTH_A, 3 * WIDTH_B
        w_l, b_l = w_in[layer], b_in[layer]
        query_scale = jnp.full((WIDTH_A,), QUERY_SCALE, _F32)
        ones = functools.partial(jnp.ones, dtype=_F32)
        qkv_a = _in_proj_a(x, w_l[:, :n_a].astype(_BF16), b_l[:n_a],
                           jnp.concatenate([query_scale, ones((2 * WIDTH_A,))]))
        w_rest = jnp.concatenate([w_l[:, n_a + n_b:], w_l[:, n_a:n_a + n_b]], axis=1).astype(_BF16)
        b_rest = jnp.concatenate([b_l[n_a + n_b:], b_l[n_a:n_a + n_b]])
        scale_rest = jnp.concatenate([ones((2 * dm,)), query_scale, ones((2 * WIDTH_B,))])
        proj2d = _in_proj(x2d, w_rest, b_rest, scale_rest)

        dil = [_dilated_pass(a, d) for a, (_, d) in zip(qkv_a, DILATED_PATTERNS)]

        lam_vecs = jnp.stack([lambda_q1[layer], lambda_k1[layer], lambda_q2[layer], lambda_k2[layer]]).astype(_F32)
        o_b = _diff_attention(proj2d.reshape(bsz, seq, -1), 2 * dm, lam_vecs, subln_g[layer],
                              lambda_init).reshape(t, WIDTH_B)

        h, lpos, cnt, xs = _merge_router(
            x2d, proj2d, 0, [o for o, _ in dil], [l for _, l in dil], o_b,
            w_proj_a[layer].astype(_BF16), w_proj_b[layer].astype(_BF16),
            w_out[layer].astype(_BF16), b_out[layer], ln1_g[layer], ln1_b[layer],
            w_router[layer], b_router[layer], alpha)

        plan = _moe_plan(cnt[:, :, 0], t * TOP_K)
        ys = _experts(xs, plan, w_up[layer], b_up[layer], w_down[layer], b_down[layer])
        out = _combine(h, ys, lpos, plan, ln2_g[layer], ln2_b[layer], alpha)
        x = out.reshape(bsz, seq, dm)
    return x
```

```python
import functools
import itertools
import math

import numpy as np
import jax
import jax.numpy as jnp
from jax import lax
from jax.experimental import pallas as pl
from jax.experimental.pallas import tpu as pltpu

HEAD_DIM = 64
N_HEADS_A = 8
DILATED_PATTERNS = ((128, 1), (512, 4), (2048, 16))
N_HEADS_B = 4
WIDTH_A = N_HEADS_A * HEAD_DIM
WIDTH_B = N_HEADS_B * 2 * HEAD_DIM
N_ALIBI_HEADS = N_HEADS_A + N_HEADS_B
ALIBI_IDX_A = (0, 1, 3, 4, 6, 7, 9, 10)
ALIBI_IDX_B = (2, 5, 8, 11)
Q_BLOCK = 128
MASK_VALUE = -1e30
N_EXPERTS = 32
TOP_K = 4
SWIGLU_ALPHA = 1.702
SWIGLU_LIMIT = 7.0
LN_EPS = 1e-5
SUBLN_EPS = 1e-5
LOG2E = math.log2(math.e)
QUERY_SCALE = HEAD_DIM ** -0.5 * LOG2E

LANES = 128
V7X_VMEM_LIMIT_BYTES = 56 * 1024 * 1024

PROJ_TM = 1024
PROJ_A_TM = 512
DIL_ROWS_MAX = 8 * Q_BLOCK
DIL_HALO = 64
DIFF_TQ = 512
DIFF_TK = 512
DIFF_HEADS = 2
MOE_ROWS = 512
FFN_CHUNK = 512
MERGE_TM = 512
MOE_TM = 256
ROW_ALIGN = 8
LOCAL_ROWS = -(-(TOP_K * MOE_TM + N_EXPERTS * (ROW_ALIGN - 1)) // LANES) * LANES

_F32 = jnp.float32
_BF16 = jnp.bfloat16


def _params(*sem):
    return pltpu.CompilerParams(dimension_semantics=sem, vmem_limit_bytes=V7X_VMEM_LIMIT_BYTES)


def _alibi_slopes():
    return (2.0 ** (-8.0 * np.arange(1, N_ALIBI_HEADS + 1) / N_ALIBI_HEADS)).astype(np.float32)


def _in_proj_kernel(x_ref, w_ref, b_ref, cs_ref, o_ref):
    x = x_ref[...].astype(_BF16)
    acc = jnp.dot(x, w_ref[...], preferred_element_type=_F32)
    o_ref[...] = ((acc + b_ref[...]) * cs_ref[...]).astype(o_ref.dtype)


def _in_proj(x2d, w_bf16, b, colscale):
    t, dm = x2d.shape
    n = w_bf16.shape[1]
    tn = n // 2
    assert n % 2 == 0 and tn % LANES == 0
    return pl.pallas_call(
        _in_proj_kernel,
        grid=(t // PROJ_TM, n // tn),
        in_specs=[
            pl.BlockSpec((PROJ_TM, dm), lambda i, j: (i, 0)),
            pl.BlockSpec((dm, tn), lambda i, j: (0, j)),
            pl.BlockSpec((1, tn), lambda i, j: (0, j)),
            pl.BlockSpec((1, tn), lambda i, j: (0, j)),
        ],
        out_specs=pl.BlockSpec((PROJ_TM, tn), lambda i, j: (i, j)),
        out_shape=jax.ShapeDtypeStruct((t, n), _BF16),
        compiler_params=_params("parallel", "arbitrary"),
        name="in_proj",
    )(x2d, w_bf16, b.reshape(1, n), colscale.reshape(1, n))


def _in_proj_a_kernel(x_ref, w_ref, b_ref, cs_ref, *refs):
    out_refs, acc_ref, stage_ref = refs[:-2], refs[-2], refs[-1]
    x = x_ref[0].astype(_BF16)
    acc = (jnp.dot(x, w_ref[...], preferred_element_type=_F32) + b_ref[...]) * cs_ref[...]
    n_chunks, tm, _ = acc_ref.shape
    for c in range(n_chunks):
        acc_ref[c] = acc[:, c * LANES:(c + 1) * LANES]
    for o_ref, (_, d) in zip(out_refs, DILATED_PATTERNS):
        if d == 1:
            o_ref[0, 0] = acc.astype(o_ref.dtype)
            continue
        if d == 4:
            for r in range(d):
                for c in range(n_chunks):
                    rows = acc_ref[c, pl.ds(r, tm // d, stride=d), :]
                    stage_ref[c, r] = rows
                    o_ref[0, r, :, c * LANES:(c + 1) * LANES] = rows.astype(o_ref.dtype)
            continue
        assert d == 16
        for r1 in range(4):
            for r2 in range(4):
                for c in range(n_chunks):
                    rows = stage_ref[c, r1, pl.ds(r2, tm // d, stride=4), :]
                    o_ref[0, r1 + 4 * r2, :, c * LANES:(c + 1) * LANES] = rows.astype(o_ref.dtype)


def _in_proj_a(x, w_bf16, b, colscale):
    bsz, seq, dm = x.shape
    n = w_bf16.shape[1]
    tm = PROJ_A_TM
    per_b = seq // tm
    dils = [d for _, d in DILATED_PATTERNS]
    assert seq % tm == 0 and all(tm % d == 0 and (tm // d) % 16 == 0 for d in dils) and dils == [1, 4, 16]
    return pl.pallas_call(
        _in_proj_a_kernel,
        grid=(bsz * per_b,),
        in_specs=[
            pl.BlockSpec((1, tm, dm), lambda i: (i // per_b, i % per_b, 0)),
            pl.BlockSpec((dm, n), lambda i: (0, 0)),
            pl.BlockSpec((1, n), lambda i: (0, 0)),
            pl.BlockSpec((1, n), lambda i: (0, 0)),
        ],
        out_specs=[pl.BlockSpec((1, d, tm // d, n), lambda i: (i // per_b, 0, i % per_b, 0)) for d in dils],
        out_shape=[jax.ShapeDtypeStruct((bsz, d, seq // d, n), _BF16) for d in dils],
        scratch_shapes=[pltpu.VMEM((n // LANES, tm, LANES), _F32),
                        pltpu.VMEM((n // LANES, 4, tm // 4, LANES), _F32)],
        compiler_params=_params("parallel"),
        name="in_proj_a",
    )(x, w_bf16, b.reshape(1, n), colscale.reshape(1, n))


def _dilated_bias_table(dilation):
    slopes = _alibi_slopes()[list(ALIBI_IDX_A)]
    band = Q_BLOCK + 2 * DIL_HALO
    qi = np.arange(Q_BLOCK)[:, None]
    kj = np.arange(band)[None, :]
    rel = qi - kj + DIL_HALO
    in_band = np.abs(rel) <= DIL_HALO
    base = -slopes[:, None, None] * (dilation * np.abs(rel)).astype(np.float32)[None]
    base = (base.astype(np.float64) * LOG2E).astype(np.float32)
    edge = (np.ones_like(kj, bool), kj >= DIL_HALO, kj < band - DIL_HALO)
    out = np.stack([np.where(in_band & e, base, np.float32(MASK_VALUE)) for e in edge])
    return out.astype(np.float32)


def _dilated_kernel(q_ref, km_ref, kp_ref, kn_ref, vm_ref, vp_ref, vn_ref, bias_ref, o_ref, lse_ref,
                    kbuf, vbuf, *, nblk, n_rows, n_res):
    i = pl.program_id(2)
    h0, h1 = DIL_HALO, DIL_HALO + n_rows
    for res in range(n_res):
        kbuf[res, 0:h0, :] = kp_ref[0, res]
        kbuf[res, h0:h1, :] = km_ref[0, res]
        kbuf[res, h1:h1 + DIL_HALO, :] = kn_ref[0, res]
        vbuf[res, 0:h0, :] = vp_ref[0, res]
        vbuf[res, h0:h1, :] = vm_ref[0, res]
        vbuf[res, h1:h1 + DIL_HALO, :] = vn_ref[0, res]

    lane = lax.broadcasted_iota(jnp.int32, (Q_BLOCK, LANES), 1)
    low_half = lane < HEAD_DIM
    band = Q_BLOCK + 2 * DIL_HALO
    n_sub = n_rows // Q_BLOCK
    variants = [0] * n_sub
    variants[0] = jnp.where(i == 0, 1, 0)
    variants[-1] = jnp.where(i == nblk - 1, 2, 0)

    for res, j in itertools.product(range(n_res), range(n_sub)):
        rows = slice(j * Q_BLOCK, (j + 1) * Q_BLOCK)
        krows = slice(j * Q_BLOCK, j * Q_BLOCK + band)
        lse_cols = []
        for hp in range(N_HEADS_A // 2):
            cols = slice(hp * LANES, (hp + 1) * LANES)
            q_pair = q_ref[0, res, rows, cols]
            k_pair = kbuf[res, krows, cols]
            v_pair = vbuf[res, krows, cols]
            halves = []
            for hh in range(2):
                head = 2 * hp + hh
                keep = low_half if hh == 0 else jnp.logical_not(low_half)
                qm = jnp.where(keep, q_pair, jnp.zeros_like(q_pair))
                s = lax.dot_general(qm, k_pair, (((1,), (1,)), ((), ())), preferred_element_type=_F32)
                s = s + bias_ref[variants[j], head]
                m = jnp.max(s, axis=-1, keepdims=True)
                p = jnp.exp2(s - m)
                z = jnp.sum(p, axis=-1, keepdims=True)
                halves.append(jnp.dot(p.astype(_BF16), v_pair, preferred_element_type=_F32) * (1.0 / z))
                lse_cols.append(m + jnp.log2(z))
            o_ref[0, res, rows, cols] = jnp.where(low_half, halves[0], halves[1])
        tile = jnp.zeros((Q_BLOCK, LANES), _F32)
        for head in range(N_HEADS_A):
            tile = jnp.where(lane == head, lse_cols[head], tile)
        lse_ref[0, res, rows, :] = tile


def _dilated_pass(qkv, dilation):
    bsz, d, L, _ = qkv.shape
    n_rows = min(DIL_ROWS_MAX, L)
    n_res = min(DIL_ROWS_MAX // n_rows, d)
    assert d == dilation and L % n_rows == 0 and n_rows % DIL_HALO == 0 and n_rows >= 2 * Q_BLOCK
    assert d % n_res == 0
    nblk = L // n_rows
    per = n_rows // DIL_HALO
    n_halo = L // DIL_HALO
    bias = jnp.asarray(_dilated_bias_table(dilation))

    def main(s):
        return pl.BlockSpec((1, n_res, n_rows, WIDTH_A), lambda b, r, i: (b, r, i, s))

    def left(s):
        return pl.BlockSpec((1, n_res, DIL_HALO, WIDTH_A), lambda b, r, i: (b, r, jnp.maximum(per * i - 1, 0), s))

    def right(s):
        return pl.BlockSpec((1, n_res, DIL_HALO, WIDTH_A),
                            lambda b, r, i: (b, r, jnp.minimum(per * i + per, n_halo - 1), s))

    rows_buf = n_rows + 2 * DIL_HALO
    return pl.pallas_call(
        functools.partial(_dilated_kernel, nblk=nblk, n_rows=n_rows, n_res=n_res),
        grid=(bsz, dilation // n_res, nblk),
        in_specs=[main(0), main(1), left(1), right(1), main(2), left(2), right(2),
                  pl.BlockSpec(bias.shape, lambda b, r, i: (0, 0, 0, 0))],
        out_specs=[pl.BlockSpec((1, n_res, n_rows, WIDTH_A), lambda b, r, i: (b, r, i, 0)),
                   pl.BlockSpec((1, n_res, n_rows, LANES), lambda b, r, i: (b, r, i, 0))],
        out_shape=[jax.ShapeDtypeStruct((bsz, dilation, L, WIDTH_A), _F32),
                   jax.ShapeDtypeStruct((bsz, dilation, L, LANES), _F32)],
        scratch_shapes=[pltpu.VMEM((n_res, rows_buf, WIDTH_A), _BF16),
                        pltpu.VMEM((n_res, rows_buf, WIDTH_A), _BF16)],
        compiler_params=_params("parallel", "parallel", "arbitrary"),
        name=f"dilated_d{dilation}",
    )(qkv, qkv, qkv, qkv, qkv, qkv, qkv, bias)


def _split_bf16(x):
    hi = x.astype(jnp.bfloat16)
    lo = (x - hi.astype(np.float64)).astype(jnp.bfloat16)
    return hi, lo


def _diff_tables():
    s2 = _alibi_slopes()[list(ALIBI_IDX_B)].astype(np.float64) * LOG2E
    qaug = np.zeros((N_HEADS_B, DIFF_TQ, LANES), jnp.bfloat16)
    kaug = np.zeros((N_HEADS_B, 2, DIFF_TK, LANES), jnp.bfloat16)
    i = np.arange(DIFF_TQ, dtype=np.float64)
    j = np.arange(DIFF_TK, dtype=np.float64)
    for h in range(N_HEADS_B):
        r_hi, r_lo = _split_bf16(-s2[h] * i)
        qaug[h, :, 0] = 1.0
        qaug[h, :, 1] = 1.0
        qaug[h, :, 2] = r_hi
        qaug[h, :, 3] = r_lo
        for side, sign in enumerate((1.0, -1.0)):
            c_hi, c_lo = _split_bf16(sign * s2[h] * j)
            kaug[h, side, :, 0] = c_hi
            kaug[h, side, :, 1] = c_lo
            kaug[h, side, :, 2] = sign
            kaug[h, side, :, 3] = sign
    halves = DIFF_TK // DIFF_TQ
    off = (np.arange(halves) * DIFF_TQ)[:, None, None]
    dist = np.abs(off + i[None, :, None] - j[None, None, :])
    diag = (-s2[:, None, None, None] * dist[None]).astype(np.float32)
    return jnp.asarray(qaug), jnp.asarray(kaug), jnp.asarray(diag), jnp.asarray(s2.astype(np.float32))


def _diff_kernel(slope_ref, lam_ref, q_ref, k_ref, v_ref, qaug_ref, kaug_ref, diag_ref, g_ref, o_ref,
                 *, seq, lambda_init):
    for hh in range(DIFF_HEADS):
        _diff_head(hh, slope_ref, lam_ref, q_ref, k_ref, v_ref, qaug_ref, kaug_ref, diag_ref, g_ref, o_ref,
                   seq=seq, lambda_init=lambda_init)


def _diff_head(hh, slope_ref, lam_ref, q_ref, k_ref, v_ref, qaug_ref, kaug_ref, diag_ref, g_ref, o_ref,
               *, seq, lambda_init):
    head = pl.program_id(1) * DIFF_HEADS + hh
    cols = slice(hh * LANES, (hh + 1) * LANES)
    qb = pl.program_id(2)
    q0 = qb * DIFF_TQ
    halves = DIFF_TK // DIFF_TQ
    kd = qb // halves
    n_kb = seq // DIFF_TK
    slope2 = slope_ref[head]
    nt = (((1,), (1,)), ((), ()))

    q = q_ref[0, :, cols]
    lane = lax.broadcasted_iota(jnp.int32, q.shape, 1)
    zero = jnp.zeros_like(q)
    qaug = qaug_ref[hh]
    q_maps = (jnp.concatenate([jnp.where(lane < HEAD_DIM, q, zero), qaug], axis=1),
              jnp.concatenate([jnp.where(lane >= HEAD_DIM, q, zero), qaug], axis=1))
    zero_aug = jnp.zeros((DIFF_TK, LANES), _BF16)
    vlane = lax.broadcasted_iota(jnp.int32, (DIFF_TK, LANES), 1)
    ones_col = jnp.where(vlane == 0, 1.0, 0.0).astype(_BF16)

    def load(kb):
        k0 = pl.multiple_of(kb * DIFF_TK, DIFF_TK)
        return (k_ref[0, pl.ds(k0, DIFF_TK), cols],
                jnp.concatenate([v_ref[0, pl.ds(k0, DIFF_TK), cols], ones_col], axis=1))

    def split(pv):
        return pv[:, LANES:LANES + 1], pv[:, 0:LANES]

    def update(s, const, vblk, carry):
        m_old, l_old, acc = carry
        m_new = jnp.maximum(m_old, jnp.max(s, axis=-1, keepdims=True) + const)
        alpha = jnp.exp2(m_old - m_new)
        p = jnp.exp2(s - (m_new - const))
        l_blk, o_blk = split(jnp.dot(p.astype(_BF16), vblk, preferred_element_type=_F32))
        return m_new, alpha * l_old + l_blk, alpha * acc + o_blk

    kblk, vblk = load(kd)
    k_all = jnp.concatenate([kblk, zero_aug], axis=1)
    bias = diag_ref[hh, qb % halves]
    carries = []
    for mp in range(2):
        s = lax.dot_general(q_maps[mp], k_all, nt, preferred_element_type=_F32) + bias
        m = jnp.max(s, axis=-1, keepdims=True)
        p = jnp.exp2(s - m)
        carries.append((m,) + split(jnp.dot(p.astype(_BF16), vblk, preferred_element_type=_F32)))

    for t in range(n_kb - 1):
        kb = jnp.where(t >= kd, t + 1, t)
        side = jnp.where(t >= kd, 1, 0)
        kblk, vblk = load(kb)
        k_all = jnp.concatenate([kblk, kaug_ref[hh, side]], axis=1)
        const = -slope2 * jnp.abs(q0 - kb * DIFF_TK).astype(_F32)
        for mp in range(2):
            s = lax.dot_general(q_maps[mp], k_all, nt, preferred_element_type=_F32)
            carries[mp] = update(s, const, vblk, carries[mp])

    lv = lam_ref[...]
    lam = (jnp.exp(jnp.sum(lv[0:1] * lv[1:2], axis=-1, keepdims=True))
           - jnp.exp(jnp.sum(lv[2:3] * lv[3:4], axis=-1, keepdims=True)) + lambda_init)
    (_, l1, a1), (_, l2, a2) = carries
    of = a1 * (1.0 / l1) - lam * (a2 * (1.0 / l2))
    of = of * lax.rsqrt(jnp.mean(of * of, axis=-1, keepdims=True) + SUBLN_EPS)
    of = of * g_ref[...] * (1.0 - lambda_init)
    o_ref[0, :, cols] = of.astype(o_ref.dtype)


def _diff_attention(proj, q_offset, lam_vecs, subln_g, lambda_init):
    bsz, seq, _ = proj.shape
    assert seq % DIFF_TQ == 0 and seq % DIFF_TK == 0 and q_offset % LANES == 0
    qcol = q_offset // LANES
    kcol = qcol + WIDTH_B // LANES
    vcol = kcol + WIDTH_B // LANES
    assert DIFF_TK % DIFF_TQ == 0 and N_HEADS_B % DIFF_HEADS == 0 and qcol % DIFF_HEADS == 0
    halves = DIFF_TK // DIFF_TQ
    hp = DIFF_HEADS
    qcol, kcol, vcol = qcol // hp, kcol // hp, vcol // hp
    qaug, kaug, diag, slopes2 = _diff_tables()
    return pl.pallas_call(
        functools.partial(_diff_kernel, seq=seq, lambda_init=lambda_init),
        grid=(bsz, N_HEADS_B // hp, seq // DIFF_TQ),
        in_specs=[
            pl.BlockSpec(memory_space=pltpu.SMEM),
            pl.BlockSpec((4, HEAD_DIM), lambda b, h, i: (0, 0)),
            pl.BlockSpec((1, DIFF_TQ, hp * LANES), lambda b, h, i: (b, i, qcol + h)),
            pl.BlockSpec((1, seq, hp * LANES), lambda b, h, i: (b, 0, kcol + h)),
            pl.BlockSpec((1, seq, hp * LANES), lambda b, h, i: (b, 0, vcol + h)),
            pl.BlockSpec((hp, DIFF_TQ, LANES), lambda b, h, i: (h, 0, 0)),
            pl.BlockSpec((hp, 2, DIFF_TK, LANES), lambda b, h, i: (h, 0, 0, 0)),
            pl.BlockSpec((hp, halves, DIFF_TQ, DIFF_TK), lambda b, h, i: (h, 0, 0, 0)),
            pl.BlockSpec((1, LANES), lambda b, h, i: (0, 0)),
        ],
        out_specs=pl.BlockSpec((1, DIFF_TQ, hp * LANES), lambda b, h, i: (b, i, h)),
        out_shape=jax.ShapeDtypeStruct((bsz, seq, WIDTH_B), _BF16),
        compiler_params=_params("parallel", "parallel", "arbitrary"),
        name="diff_attn",
    )(slopes2, lam_vecs, proj, proj, proj, qaug, kaug, diag, subln_g.reshape(1, LANES).astype(_F32))


def _layer_norm(z, g, b):
    mu = jnp.mean(z, axis=-1, keepdims=True)
    zc = z - mu
    var = jnp.mean(zc * zc, axis=-1, keepdims=True)
    return zc * lax.rsqrt(var + LN_EPS) * g + b


def _mix_patterns(o_refs, lse_refs, expand_ref, o_nat, lse_nat):
    n_chunks, tm = o_nat.shape[1], o_nat.shape[2]
    outs, lses = [], []
    for p, (_, d) in enumerate(DILATED_PATTERNS):
        if d == 1:
            outs.append(o_refs[p][0, 0])
            lses.append(lse_refs[p][0, 0])
            continue
        for r in range(d):
            for c in range(n_chunks):
                o_nat[p, c, pl.ds(r, tm // d, stride=d), :] = o_refs[p][0, r, :, c * LANES:(c + 1) * LANES]
            lse_nat[p, pl.ds(r, tm // d, stride=d), :] = lse_refs[p][0, r]
        outs.append(jnp.concatenate([o_nat[p, c] for c in range(n_chunks)], axis=1))
        lses.append(lse_nat[p])
    top = functools.reduce(jnp.maximum, lses)
    es = [jnp.exp2(l - top) for l in lses]
    inv = 1.0 / functools.reduce(lambda a, b: a + b, es)
    mixed = None
    for o, e in zip(outs, es):
        w = e * inv
        w_hi = w.astype(_BF16)
        w_lo = (w - w_hi.astype(_F32)).astype(_BF16)
        w_full = (jnp.dot(w_hi, expand_ref[...], preferred_element_type=_F32)
                  + jnp.dot(w_lo, expand_ref[...], preferred_element_type=_F32))
        mixed = w_full * o if mixed is None else mixed + w_full * o
    return mixed


def _merge_kernel(x_ref, ga_ref, gb_ref, o1_ref, o2_ref, o3_ref, l1_ref, l2_ref, l3_ref, expand_ref, ob_ref,
                  wpa_ref, wpb_ref, wout_ref, bout_ref, g1_ref, b1_ref, wrh_ref, wrl_ref, br_ref,
                  h_ref, lpos_ref, cnt_ref, xs_ref, o_nat, lse_nat, *, alpha):
    o_a = _mix_patterns((o1_ref, o2_ref, o3_ref), (l1_ref, l2_ref, l3_ref), expand_ref, o_nat, lse_nat)
    pa = jnp.dot(o_a.astype(_BF16), wpa_ref[...], preferred_element_type=_F32)
    pb = jnp.dot(ob_ref[...], wpb_ref[...], preferred_element_type=_F32)
    merged = jax.nn.sigmoid(ga_ref[...].astype(_F32)) * pa + jax.nn.sigmoid(gb_ref[...].astype(_F32)) * pb
    y = jnp.dot(merged.astype(_BF16), wout_ref[...], preferred_element_type=_F32) + bout_ref[...]
    h = _layer_norm(alpha * x_ref[...] + y, g1_ref[...], b1_ref[...])
    h_ref[...] = h

    h_hi = h.astype(_BF16)
    h_lo = (h - h_hi.astype(_F32)).astype(_BF16)
    nt = (((1,), (1,)), ((), ()))
    logits = (lax.dot_general(wrh_ref[...], h_hi, nt, preferred_element_type=_F32)
              + lax.dot_general(wrl_ref[...], h_hi, nt, preferred_element_type=_F32)
              + lax.dot_general(wrh_ref[...], h_lo, nt, preferred_element_type=_F32)
              + br_ref[...])

    tm = logits.shape[1]
    expert = lax.broadcasted_iota(jnp.int32, logits.shape, 0)
    work = logits
    vals, sels = [], []
    for _ in range(TOP_K):
        mx = jnp.max(work, axis=0, keepdims=True)
        idx = jnp.min(jnp.where(work == mx, expert, N_EXPERTS), axis=0, keepdims=True)
        sel = expert == idx
        work = jnp.where(sel, -jnp.inf, work)
        vals.append(mx)
        sels.append(sel)
    ex = [jnp.exp(v - vals[0]) for v in vals]
    inv = 1.0 / (ex[0] + ex[1] + ex[2] + ex[3])
    gates = [e * inv for e in ex]

    chosen = sels[0] | sels[1] | sels[2] | sels[3]
    onehot = jnp.where(chosen, 1.0, 0.0).astype(_BF16)
    earlier = lax.broadcasted_iota(jnp.int32, (tm, tm), 0)
    later = lax.broadcasted_iota(jnp.int32, (tm, tm), 1)
    sub_shift = int(math.log2(MOE_TM))
    before = (earlier < later) & (lax.shift_right_logical(earlier, sub_shift)
                                  == lax.shift_right_logical(later, sub_shift))
    upper = jnp.where(before, 1.0, 0.0).astype(_BF16)
    in_tile = jnp.dot(onehot, upper, preferred_element_type=_F32)
    lower = (lax.broadcasted_iota(jnp.int32, (N_EXPERTS, N_EXPERTS), 1)
             < lax.broadcasted_iota(jnp.int32, (N_EXPERTS, N_EXPERTS), 0))
    lower = jnp.where(lower, 1.0, 0.0).astype(_BF16)
    rows = xs_ref.shape[1]
    dm = h.shape[1]
    slot = lax.broadcasted_iota(jnp.int32, (rows, MOE_TM), 0)
    for sub in range(tm // MOE_TM):
        cols = slice(sub * MOE_TM, (sub + 1) * MOE_TM)
        count = jnp.sum(onehot[:, cols].astype(_F32), axis=1, keepdims=True)
        cnt_ref[sub] = jnp.broadcast_to(count, cnt_ref.shape[1:])
        group = jnp.floor((count + (ROW_ALIGN - 1)) * (1.0 / ROW_ALIGN)) * ROW_ALIGN
        first_row = jnp.dot(lower, jnp.broadcast_to(group, (N_EXPERTS, LANES)).astype(_BF16),
                            preferred_element_type=_F32)[:, 0:1]
        place = in_tile[:, cols] + first_row
        perm = jnp.zeros((rows, MOE_TM), _F32)
        gsel = jnp.zeros((rows, MOE_TM), _F32)
        for k in range(TOP_K):
            pos = jnp.sum(jnp.where(sels[k][:, cols], place, 0.0), axis=0, keepdims=True).astype(jnp.int32)
            lpos_ref[k:k + 1, cols] = pos
            hit = slot == pos
            perm = jnp.where(hit, 1.0, perm)
            gsel = jnp.where(hit, gates[k][:, cols], gsel)
        xs_ref[sub, :, 0:dm] = jnp.dot(perm.astype(_BF16), h_hi[cols, :], preferred_element_type=_F32)
        xs_ref[sub, :, dm:dm + LANES] = jnp.broadcast_to(jnp.sum(gsel, axis=1, keepdims=True), (rows, LANES))


def _merge_router(x2d, proj2d, gate_offset, dil_outs, dil_lses, o_b, wpa, wpb, wout, bout, g1, b1,
                  w_router, b_router, alpha):
    t, dm = x2d.shape
    tm = MERGE_TM
    bsz, _, seq_over_d0, _ = dil_outs[0].shape
    seq = seq_over_d0 * DILATED_PATTERNS[0][1]
    per_b = seq // tm
    dils = [d for _, d in DILATED_PATTERNS]
    assert gate_offset % dm == 0 and seq % tm == 0 and all(tm % d == 0 for d in dils)
    gcol = gate_offset // dm
    wr_t = w_router.T.astype(_F32)
    wr_hi = wr_t.astype(_BF16)
    wr_lo = (wr_t - wr_hi.astype(_F32)).astype(_BF16)
    expand = np.zeros((LANES, WIDTH_A), np.float32)
    for head in range(N_HEADS_A):
        expand[head, head * HEAD_DIM:(head + 1) * HEAD_DIM] = 1.0
    expand = jnp.asarray(expand, _BF16)

    def const(shape):
        return pl.BlockSpec(shape, lambda i: tuple(0 for _ in shape))

    def residue_major(d, width):
        return pl.BlockSpec((1, d, tm // d, width), lambda i: (i // per_b, 0, i % per_b, 0))

    return pl.pallas_call(
        functools.partial(_merge_kernel, alpha=alpha),
        grid=(t // tm,),
        in_specs=[
            pl.BlockSpec((tm, dm), lambda i: (i, 0)),
            pl.BlockSpec((tm, dm), lambda i: (i, gcol)),
            pl.BlockSpec((tm, dm), lambda i: (i, gcol + 1)),
            *[residue_major(d, WIDTH_A) for d in dils],
            *[residue_major(d, LANES) for d in dils],
            const((LANES, WIDTH_A)),
            pl.BlockSpec((tm, WIDTH_B), lambda i: (i, 0)),
            const((WIDTH_A, dm)), const((WIDTH_B, dm)), const((dm, dm)), const((1, dm)),
            const((1, dm)), const((1, dm)),
            const((N_EXPERTS, dm)), const((N_EXPERTS, dm)), const((N_EXPERTS, 1)),
        ],
        out_specs=[
            pl.BlockSpec((tm, dm), lambda i: (i, 0)),
            pl.BlockSpec((TOP_K, tm), lambda i: (0, i)),
            pl.BlockSpec((tm // MOE_TM, N_EXPERTS, LANES), lambda i: (i, 0, 0)),
            pl.BlockSpec((tm // MOE_TM, LOCAL_ROWS, dm + LANES), lambda i: (i, 0, 0)),
        ],
        out_shape=[
            jax.ShapeDtypeStruct((t, dm), _F32),
            jax.ShapeDtypeStruct((TOP_K, t), jnp.int32),
            jax.ShapeDtypeStruct((t // MOE_TM, N_EXPERTS, LANES), _F32),
            jax.ShapeDtypeStruct((t // MOE_TM, LOCAL_ROWS, dm + LANES), _F32),
        ],
        scratch_shapes=[pltpu.VMEM((len(dils), WIDTH_A // LANES, tm, LANES), _F32),
                        pltpu.VMEM((len(dils), tm, LANES), _F32)],
        compiler_params=_params("parallel"),
        name="merge_router",
    )(x2d, proj2d, proj2d, *dil_outs, *dil_lses, expand, o_b, wpa, wpb, wout, bout.reshape(1, dm),
      g1.reshape(1, dm), b1.reshape(1, dm), wr_hi, wr_lo, b_router.reshape(N_EXPERTS, 1).astype(_F32))


def _group_rows(n):
    return lax.shift_right_logical(n + (ROW_ALIGN - 1), int(math.log2(ROW_ALIGN))) * ROW_ALIGN


def _expert_kernel(blk_e_ref, n_used_ref, tlo_ref, thi_ref, cnt_ref, base_ref, off_ref, ord_ref, next_ref,
                   xs_hbm, wup_hbm, bup_ref, wdn_hbm, bdn_ref, ys_ref,
                   xbuf, sems, pending, wup_f32, wdn_f32, wsems, wup_bf, wdn_bf, *, d_expert):
    i = pl.program_id(0)
    dm = ys_ref.shape[1]
    active = i < n_used_ref[0]

    def seg_copy(buf, src_row, dst_row, n_rows):
        n_rows = pl.multiple_of(n_rows, ROW_ALIGN)
        return pltpu.make_async_copy(xs_hbm.at[pl.ds(pl.multiple_of(src_row, ROW_ALIGN), n_rows)],
                                     xbuf.at[buf, pl.ds(pl.multiple_of(dst_row, ROW_ALIGN), n_rows)],
                                     sems.at[buf])

    def fetch(blk):
        e = blk_e_ref[blk]
        first = blk * MOE_ROWS

        def per_tile(t, total):
            j = t * N_EXPERTS + e
            g0 = base_ref[j]
            lo = jnp.maximum(g0, first)
            hi = jnp.minimum(g0 + _group_rows(cnt_ref[j]), first + MOE_ROWS)
            n_rows = jnp.maximum(hi - lo, 0)

            @pl.when(n_rows > 0)
            def _():
                seg_copy(blk % 2, t * LOCAL_ROWS + off_ref[j] + (lo - g0), lo - first, n_rows).start()

            return total + n_rows

        pending[blk % 2] = lax.fori_loop(tlo_ref[blk], thi_ref[blk] + 1, per_tile, 0)

    @pl.when(i == 0)
    def _():
        xbuf[...] = jnp.zeros_like(xbuf)
        fetch(i)

    @pl.when(i + 1 < n_used_ref[0])
    def _():
        fetch(i + 1)

    def weight_copies(expert, slot):
        return (pltpu.make_async_copy(wup_hbm.at[expert], wup_f32.at[slot], wsems.at[slot]),
                pltpu.make_async_copy(wdn_hbm.at[expert], wdn_f32.at[slot], wsems.at[slot]))

    @pl.when(i == 0)
    def _():
        for cp in weight_copies(blk_e_ref[0], 0):
            cp.start()

    @pl.when(active & ((i == 0) | (blk_e_ref[i] != blk_e_ref[jnp.maximum(i - 1, 0)])))
    def _():
        slot = ord_ref[i] % 2
        for cp in weight_copies(blk_e_ref[i], slot):
            cp.wait()
        wup_bf[...] = wup_f32[slot].astype(_BF16)
        wdn_bf[...] = wdn_f32[slot].astype(_BF16)

        @pl.when(next_ref[i] >= 0)
        def _():
            for cp in weight_copies(next_ref[i], 1 - slot):
                cp.start()

    @pl.when(active)
    def _():
        @pl.when(pending[i % 2] > 0)
        def _():
            seg_copy(i % 2, 0, 0, pending[i % 2]).wait()

        x = xbuf[i % 2, :, 0:dm].astype(_BF16)
        row_gate = xbuf[i % 2, :, dm:dm + 1]
        acc = jnp.zeros(ys_ref.shape, _F32)
        for c in range(d_expert // FFN_CHUNK):
            lo, hi = c * FFN_CHUNK, (c + 1) * FFN_CHUNK
            g = jnp.dot(x, wup_bf[:, lo:hi], preferred_element_type=_F32) + bup_ref[0, :, lo:hi]
            u = (jnp.dot(x, wup_bf[:, d_expert + lo:d_expert + hi], preferred_element_type=_F32)
                 + bup_ref[0, :, d_expert + lo:d_expert + hi])
            gate = jnp.minimum(g, SWIGLU_LIMIT)
            up = jnp.clip(u, -SWIGLU_LIMIT, SWIGLU_LIMIT)
            act = gate * jax.nn.sigmoid(SWIGLU_ALPHA * gate) * (up + 1.0)
            acc = acc + jnp.dot(act.astype(_BF16), wdn_bf[lo:hi, :], preferred_element_type=_F32)
        ys_ref[...] = ((acc + bdn_ref[0]) * row_gate).astype(_BF16).astype(_F32)

    @pl.when(jnp.logical_not(active))
    def _():
        ys_ref[...] = jnp.zeros_like(ys_ref)


def _experts(xs, plan, w_up, b_up, w_down, b_down):
    n_tiles, local_rows, width = xs.shape
    dm = width - LANES
    n_blocks = plan["n_rows"] // MOE_ROWS
    n_exp, _, two_de = w_up.shape
    d_expert = two_de // 2
    grid_spec = pltpu.PrefetchScalarGridSpec(
        num_scalar_prefetch=9,
        grid=(n_blocks,),
        in_specs=[
            pl.BlockSpec(memory_space=pl.ANY),
            pl.BlockSpec(memory_space=pl.ANY),
            pl.BlockSpec((1, 1, two_de), lambda i, be, *_: (be[i], 0, 0)),
            pl.BlockSpec(memory_space=pl.ANY),
            pl.BlockSpec((1, 1, dm), lambda i, be, *_: (be[i], 0, 0)),
        ],
        out_specs=pl.BlockSpec((MOE_ROWS, dm), lambda i, *_: (i, 0)),
        scratch_shapes=[pltpu.VMEM((2, MOE_ROWS, width), _F32), pltpu.SemaphoreType.DMA((2,)),
                        pltpu.SMEM((2,), jnp.int32),
                        pltpu.VMEM((2, dm, two_de), _F32), pltpu.VMEM((2, d_expert, dm), _F32),
                        pltpu.SemaphoreType.DMA((2,)),
                        pltpu.VMEM((dm, two_de), _BF16), pltpu.VMEM((d_expert, dm), _BF16)],
    )
    return pl.pallas_call(
        functools.partial(_expert_kernel, d_expert=d_expert),
        grid_spec=grid_spec,
        out_shape=jax.ShapeDtypeStruct((plan["n_rows"], dm), _F32),
        compiler_params=_params("arbitrary"),
        name="moe_experts",
    )(plan["blk_e"], plan["n_used"], plan["blk_tlo"], plan["blk_thi"], plan["cnt"], plan["base"], plan["off"],
      plan["blk_ord"], plan["blk_next"], xs.reshape(n_tiles * local_rows, width), w_up, b_up.reshape(n_exp, 1, two_de), w_down,
      b_down.reshape(n_exp, 1, dm))


def _combine_kernel(cnt_ref, base_ref, off_ref, rows_ref, lpos_ref, h_ref, g2_ref, b2_ref, ys_hbm, o_ref,
                    local, sems, *, tm, alpha):
    i = pl.program_id(0)
    n_tiles = pl.num_programs(0)
    rows = local.shape[1]

    def group_copy(buf, src_row, dst_row, n_rows):
        n_rows = pl.multiple_of(n_rows, ROW_ALIGN)
        return pltpu.make_async_copy(ys_hbm.at[pl.ds(pl.multiple_of(src_row, ROW_ALIGN), n_rows)],
                                     local.at[buf, pl.ds(pl.multiple_of(dst_row, ROW_ALIGN), n_rows)],
                                     sems.at[buf])

    def fetch(tile):
        for e in range(N_EXPERTS):
            j = tile * N_EXPERTS + e
            n_rows = _group_rows(cnt_ref[j])

            @pl.when(n_rows > 0)
            def _():
                group_copy(tile % 2, base_ref[j], off_ref[j], n_rows).start()

    @pl.when(i == 0)
    def _():
        local[...] = jnp.zeros_like(local)
        fetch(i)

    @pl.when(i + 1 < n_tiles)
    def _():
        fetch(i + 1)

    slot = lax.broadcasted_iota(jnp.int32, (tm, rows), 1)
    pick = jnp.zeros((tm, rows), _F32)
    for k in range(TOP_K):
        pick = jnp.where(slot == lpos_ref[:, k:k + 1], 1.0, pick)

    group_copy(i % 2, 0, 0, rows_ref[i]).wait()
    y = jnp.dot(pick.astype(_BF16), local[i % 2].astype(_BF16), preferred_element_type=_F32)
    o_ref[...] = _layer_norm(alpha * h_ref[...] + y, g2_ref[...], b2_ref[...])


def _combine(h, ys, lpos, plan, g2, b2, alpha):
    t, dm = h.shape
    tm = MOE_TM
    grid_spec = pltpu.PrefetchScalarGridSpec(
        num_scalar_prefetch=4,
        grid=(t // tm,),
        in_specs=[
            pl.BlockSpec((tm, TOP_K), lambda i, *_: (i, 0)),
            pl.BlockSpec((tm, dm), lambda i, *_: (i, 0)),
            pl.BlockSpec((1, dm), lambda i, *_: (0, 0)),
            pl.BlockSpec((1, dm), lambda i, *_: (0, 0)),
            pl.BlockSpec(memory_space=pl.ANY),
        ],
        out_specs=pl.BlockSpec((tm, dm), lambda i, *_: (i, 0)),
        scratch_shapes=[pltpu.VMEM((2, LOCAL_ROWS, dm), _F32), pltpu.SemaphoreType.DMA((2,))],
    )
    return pl.pallas_call(
        functools.partial(_combine_kernel, tm=tm, alpha=alpha),
        grid_spec=grid_spec,
        out_shape=jax.ShapeDtypeStruct((t, dm), _F32),
        compiler_params=_params("arbitrary"),
        name="moe_combine",
    )(plan["cnt"], plan["base"], plan["off"], plan["tile_rows"], lpos.T, h, g2.reshape(1, dm),
      b2.reshape(1, dm), ys)


def _round_up(x, m):
    return (x + m - 1) // m * m


def _moe_plan(tile_cnt, n_assign):
    n_tiles, n_exp = tile_cnt.shape
    cnt = tile_cnt.astype(jnp.int32)
    grp = _round_up(cnt, ROW_ALIGN)
    tot = jnp.sum(grp, axis=0)
    padded = _round_up(tot, MOE_ROWS)
    pend = jnp.cumsum(padded)
    pstart = pend - padded
    base = pstart[None, :] + jnp.cumsum(grp, axis=0) - grp
    off = jnp.cumsum(grp, axis=1) - grp

    n_rows = _round_up(n_assign + n_tiles * n_exp * (ROW_ALIGN - 1) + n_exp * (MOE_ROWS - 1), MOE_ROWS)
    n_blocks = n_rows // MOE_ROWS
    blk_start = jnp.arange(n_blocks, dtype=jnp.int32) * MOE_ROWS
    blk_e = jnp.sum((pend[None, :] <= blk_start[:, None]).astype(jnp.int32), axis=1)
    blk_e = jnp.minimum(blk_e, n_exp - 1)
    n_used = (pend[-1] // MOE_ROWS).astype(jnp.int32)
    blk_e = blk_e[jnp.minimum(jnp.arange(n_blocks, dtype=jnp.int32), n_used - 1)]
    base_b = jnp.take(base, blk_e, axis=1)
    end_b = base_b + jnp.take(grp, blk_e, axis=1)
    blk_tlo = jnp.sum((end_b <= blk_start[None, :]).astype(jnp.int32), axis=0)
    blk_thi = jnp.sum((base_b < blk_start[None, :] + MOE_ROWS).astype(jnp.int32), axis=0) - 1
    nonempty = padded > 0
    expert_ord = jnp.cumsum(nonempty.astype(jnp.int32)) - nonempty.astype(jnp.int32)
    ids = jnp.arange(n_exp, dtype=jnp.int32)
    later = (ids[None, :] > ids[:, None]) & nonempty[None, :]
    expert_next = jnp.min(jnp.where(later, ids[None, :], n_exp), axis=1)
    expert_next = jnp.where(expert_next == n_exp, -1, expert_next)
    return dict(
        cnt=cnt.reshape(-1), base=base.reshape(-1).astype(jnp.int32), off=off.reshape(-1).astype(jnp.int32),
        tile_rows=jnp.sum(grp, axis=1).astype(jnp.int32),
        blk_e=blk_e.astype(jnp.int32), blk_tlo=blk_tlo, blk_thi=blk_thi,
        blk_ord=expert_ord[blk_e].astype(jnp.int32), blk_next=expert_next[blk_e].astype(jnp.int32),
        n_used=n_used.reshape(1), n_rows=n_rows)


def kernel(x, w_in, b_in, lambda_q1, lambda_k1, lambda_q2, lambda_k2, subln_g, w_proj_a, w_proj_b, w_out, b_out, ln1_g, ln1_b, w_router, b_router, w_up, b_up, w_down, b_down, ln2_g, ln2_b):
    bsz, seq, dm = x.shape
    depth = w_in.shape[0]
    alpha = (2.0 * depth) ** 0.25
    t = bsz * seq
    for layer in range(depth):
        lambda_init = 0.8 - 0.6 * math.exp(-0.3 * layer)
        x2d = x.reshape(t, dm)
        n_a, n_b = 3 * WIDTH_A, 3 * WIDTH_B
        w_l, b_l = w_in[layer], b_in[layer]
        query_scale = jnp.full((WIDTH_A,), QUERY_SCALE, _F32)
        ones = functools.partial(jnp.ones, dtype=_F32)
        qkv_a = _in_proj_a(x, w_l[:, :n_a].astype(_BF16), b_l[:n_a],
                           jnp.concatenate([query_scale, ones((2 * WIDTH_A,))]))
        w_rest = jnp.concatenate([w_l[:, n_a + n_b:], w_l[:, n_a:n_a + n_b]], axis=1).astype(_BF16)
        b_rest = jnp.concatenate([b_l[n_a + n_b:], b_l[n_a:n_a + n_b]])
        scale_rest = jnp.concatenate([ones((2 * dm,)), query_scale, ones((2 * WIDTH_B,))])
        proj2d = _in_proj(x2d, w_rest, b_rest, scale_rest)

        dil = [_dilated_pass(a, d) for a, (_, d) in zip(qkv_a, DILATED_PATTERNS)]

        lam_vecs = jnp.stack([lambda_q1[layer], lambda_k1[layer], lambda_q2[layer], lambda_k2[layer]]).astype(_F32)
        o_b = _diff_attention(proj2d.reshape(bsz, seq, -1), 2 * dm, lam_vecs, subln_g[layer],
                              lambda_init).reshape(t, WIDTH_B)

        h, lpos, cnt, xs = _merge_router(
            x2d, proj2d, 0, [o for o, _ in dil], [l for _, l in dil], o_b,
            w_proj_a[layer].astype(_BF16), w_proj_b[layer].astype(_BF16),
            w_out[layer].astype(_BF16), b_out[layer], ln1_g[layer], ln1_b[layer],
            w_router[layer], b_router[layer], alpha)

        plan = _moe_plan(cnt[:, :, 0], t * TOP_K)
        ys = _experts(xs, plan, w_up[layer], b_up[layer], w_down[layer], b_down[layer])
        out = _combine(h, ys, lpos, plan, ln2_g[layer], ln2_b[layer], alpha)
        x = out.reshape(bsz, seq, dm)
    return x
```
